```python
import math
import jax, jax.numpy as jnp
from jax import lax
import numpy as np

D_MODEL = 2048
BATCH = 8
SEQ = 2048
DEPTH = 4

CHUNK = 64
N_MIXERS = 3
N_POOL_LAYERS = (DEPTH + 2) // 3
N_SB_LAYERS = (DEPTH + 1) // 3
N_SSM_LAYERS = DEPTH // 3

POOL_WINDOWS = (2, 4, 8, 16)
N_POOL_GROUPS = len(POOL_WINDOWS)
POOL_GROUP_DIM = D_MODEL // N_POOL_GROUPS

SB_HEAD_DIM = 128
SB_HEADS = D_MODEL // SB_HEAD_DIM
Q_BLOCK = 128

SSM_GROUP_CH = 16
SSM_GROUPS = D_MODEL // SSM_GROUP_CH
SSM_STATE = 64
SSM_DT_MIN = 1e-3
SSM_DT_MAX = 1e-1

D_FF = 5632
CONV_WIDTH = 3

RMS_EPS = 1e-6

kernel_name = "hybrid_pool_stickbreak_s5_convffn_trunk"


def rms_norm(x, g):
    xf = x.astype(jnp.float32)
    y = xf * lax.rsqrt(jnp.mean(xf * xf, axis=-1, keepdims=True) + RMS_EPS)
    return (y * g.astype(jnp.float32)).astype(x.dtype)


def multiscale_pool_mixer(h, w, b, scale):
    bsz, seq, _ = h.shape
    hf = h.astype(jnp.float32).reshape(bsz, seq, N_POOL_GROUPS, POOL_GROUP_DIM)
    cs = jnp.cumsum(hf, axis=1)
    cs = jnp.concatenate([jnp.zeros_like(cs[:, :1]), cs], axis=1)
    t = jnp.arange(seq)[:, None]
    win = jnp.array(POOL_WINDOWS, dtype=jnp.int32)[None, :]
    lo = jnp.maximum(t + 1 - win, 0)
    cnt = (t + 1 - lo).astype(jnp.float32)
    grp = jnp.arange(N_POOL_GROUPS)[None, :]
    lower = cs[:, lo, grp]
    mean = (cs[:, 1:] - lower) / cnt[None, :, :, None]
    pooled = mean - hf
    y = jnp.einsum('bsgc,gcd->bsgd', pooled, w.astype(jnp.float32))
    y = y.reshape(bsz, seq, D_MODEL) + b.astype(jnp.float32)
    return (y * scale.astype(jnp.float32)).astype(h.dtype)


def stick_breaking_attention(h, w_qkv, q_gain, k_gain, w_o):
    bsz, seq, _ = h.shape
    qkv = (h @ w_qkv).reshape(bsz, seq, 3, SB_HEADS, SB_HEAD_DIM)
    q = rms_norm(qkv[:, :, 0], q_gain).astype(jnp.float32).transpose(0, 2, 1, 3)
    k = rms_norm(qkv[:, :, 1], k_gain).astype(jnp.float32).transpose(0, 2, 1, 3)
    v = qkv[:, :, 2].transpose(0, 2, 1, 3)
    inv_sqrt_d = 1.0 / math.sqrt(SB_HEAD_DIM)
    outs = []
    for blk in range(seq // Q_BLOCK):
        q0 = blk * Q_BLOCK
        kv_len = q0 + Q_BLOCK
        qb = q[:, :, q0:kv_len]
        kb = k[:, :, :kv_len]
        vb = v[:, :, :kv_len]
        z = jnp.einsum('bhqd,bhkd->bhqk', qb, kb) * inv_sqrt_d
        t_idx = q0 + jnp.arange(Q_BLOCK)[:, None]
        s_idx = jnp.arange(kv_len)[None, :]
        mask = s_idx < t_idx
        log_beta = jax.nn.log_sigmoid(z)
        log_1m_beta = jnp.where(mask, jax.nn.log_sigmoid(-z), 0.0)
        log_remain = lax.cumsum(log_1m_beta, axis=3, reverse=True) - log_1m_beta
        attn = jnp.where(mask, jnp.exp(log_beta + log_remain), 0.0)
        outs.append(jnp.einsum('bhqk,bhkd->bhqd', attn.astype(vb.dtype), vb))
    o = jnp.concatenate(outs, axis=2)
    o = o.transpose(0, 2, 1, 3).reshape(bsz, seq, D_MODEL)
    return o @ w_o


def _ssm_combine(e1, e2):
    a1r, a1i, b1r, b1i = e1
    a2r, a2i, b2r, b2i = e2
    return (a2r * a1r - a2i * a1i,
            a2r * a1i + a2i * a1r,
            a2r * b1r - a2i * b1i + b2r,
            a2r * b1i + a2i * b1r + b2i)


def s5_mixer(h, lam_re, lam_im, log_step, b_re, b_im, c_re, c_im, d_skip, w_glu, b_glu):
    bsz, seq, _ = h.shape
    u = h.astype(jnp.float32).reshape(bsz, seq, SSM_GROUPS, SSM_GROUP_CH)
    lr = lam_re.astype(jnp.float32)
    li = lam_im.astype(jnp.float32)
    step = jnp.exp(log_step.astype(jnp.float32))[:, None]
    mag = jnp.exp(lr * step)
    lb_re = mag * jnp.cos(li * step)
    lb_im = mag * jnp.sin(li * step)
    den = lr * lr + li * li
    f_re = ((lb_re - 1.0) * lr + lb_im * li) / den
    f_im = (lb_im * lr - (lb_re - 1.0) * li) / den
    br = b_re.astype(jnp.float32)
    bi = b_im.astype(jnp.float32)
    bb_re = f_re[..., None] * br - f_im[..., None] * bi
    bb_im = f_re[..., None] * bi + f_im[..., None] * br
    bu_re = jnp.einsum('bsgh,gph->bsgp', u, bb_re)
    bu_im = jnp.einsum('bsgh,gph->bsgp', u, bb_im)
    a_re = jnp.broadcast_to(lb_re, bu_re.shape)
    a_im = jnp.broadcast_to(lb_im, bu_im.shape)
    _, _, xs_re, xs_im = lax.associative_scan(_ssm_combine, (a_re, a_im, bu_re, bu_im), axis=1)
    y = (jnp.einsum('bsgp,ghp->bsgh', xs_re, c_re.astype(jnp.float32))
         - jnp.einsum('bsgp,ghp->bsgh', xs_im, c_im.astype(jnp.float32))
         + d_skip.astype(jnp.float32).reshape(SSM_GROUPS, SSM_GROUP_CH) * u)
    y = jax.nn.gelu(y.reshape(bsz, seq, D_MODEL)).astype(h.dtype)
    gv = y @ w_glu + b_glu
    val, gate = jnp.split(gv, 2, axis=-1)
    return val * jax.nn.sigmoid(gate)


def conv_ffn(h, w_up, conv_w, conv_b, w_down):
    seq = h.shape[1]
    up = h @ w_up
    padded = jnp.pad(up, ((0, 0), (CONV_WIDTH - 1, 0), (0, 0)))
    c = conv_b + sum(conv_w[j] * padded[:, j:j + seq] for j in range(CONV_WIDTH))
    val, gate = jnp.split(c, 2, axis=-1)
    return (jax.nn.silu(gate) * val) @ w_down


def _fwd_setup_inputs(seed: int = 0) -> dict:
    key = jax.random.key(seed)
    ks = jax.random.split(key, 26)
    f32 = jnp.float32
    nrm = lambda k, shape, s: jax.random.normal(k, shape, f32) * s
    lam_im_base = jnp.pi * jnp.arange(SSM_STATE, dtype=f32)
    return {
        "x": jax.random.normal(ks[0], (BATCH, SEQ, D_MODEL), f32),
        "norm_mix_g": 1.0 + nrm(ks[1], (DEPTH, D_MODEL), 0.02),
        "norm_ffn_g": 1.0 + nrm(ks[2], (DEPTH, D_MODEL), 0.02),
        "pool_w": nrm(ks[3], (N_POOL_LAYERS, N_POOL_GROUPS, POOL_GROUP_DIM, POOL_GROUP_DIM), POOL_GROUP_DIM ** -0.5),
        "pool_b": nrm(ks[4], (N_POOL_LAYERS, D_MODEL), 0.01),
        "pool_scale": 1.0 + nrm(ks[5], (N_POOL_LAYERS, D_MODEL), 0.02),
        "sb_w_qkv": nrm(ks[6], (N_SB_LAYERS, D_MODEL, 3 * D_MODEL), D_MODEL ** -0.5),
        "sb_q_gain": 1.0 + nrm(ks[7], (N_SB_LAYERS, SB_HEAD_DIM), 0.02),
        "sb_k_gain": 1.0 + nrm(ks[8], (N_SB_LAYERS, SB_HEAD_DIM), 0.02),
        "sb_w_o": nrm(ks[9], (N_SB_LAYERS, D_MODEL, D_MODEL), D_MODEL ** -0.5),
        "ssm_lam_re": -0.5 + nrm(ks[10], (N_SSM_LAYERS, SSM_GROUPS, SSM_STATE), 0.01),
        "ssm_lam_im": lam_im_base + nrm(ks[11], (N_SSM_LAYERS, SSM_GROUPS, SSM_STATE), 0.01),
        "ssm_log_step": jax.random.uniform(ks[12], (N_SSM_LAYERS, SSM_GROUPS), f32,
                                           math.log(SSM_DT_MIN), math.log(SSM_DT_MAX)),
        "ssm_b_re": nrm(ks[13], (N_SSM_LAYERS, SSM_GROUPS, SSM_STATE, SSM_GROUP_CH), (2 * SSM_GROUP_CH) ** -0.5),
        "ssm_b_im": nrm(ks[14], (N_SSM_LAYERS, SSM_GROUPS, SSM_STATE, SSM_GROUP_CH), (2 * SSM_GROUP_CH) ** -0.5),
        "ssm_c_re": nrm(ks[15], (N_SSM_LAYERS, SSM_GROUPS, SSM_GROUP_CH, SSM_STATE), (2 * SSM_STATE) ** -0.5),
        "ssm_c_im": nrm(ks[16], (N_SSM_LAYERS, SSM_GROUPS, SSM_GROUP_CH, SSM_STATE), (2 * SSM_STATE) ** -0.5),
        "ssm_d": nrm(ks[17], (N_SSM_LAYERS, D_MODEL), 1.0),
        "ssm_w_glu": nrm(ks[18], (N_SSM_LAYERS, D_MODEL, 2 * D_MODEL), D_MODEL ** -0.5),
        "ssm_b_glu": nrm(ks[19], (N_SSM_LAYERS, 2 * D_MODEL), 0.01),
        "ffn_w_up": nrm(ks[20], (DEPTH, D_MODEL, 2 * D_FF), D_MODEL ** -0.5),
        "ffn_conv_w": nrm(ks[21], (DEPTH, CONV_WIDTH, 2 * D_FF), CONV_WIDTH ** -0.5),
        "ffn_conv_b": nrm(ks[22], (DEPTH, 2 * D_FF), 0.01),
        "ffn_w_down": nrm(ks[23], (DEPTH, D_FF, D_MODEL), D_FF ** -0.5),
    }


def _fwd_reference(x, norm_mix_g, norm_ffn_g, pool_w, pool_b, pool_scale,
              sb_w_qkv, sb_q_gain, sb_k_gain, sb_w_o,
              ssm_lam_re, ssm_lam_im, ssm_log_step, ssm_b_re, ssm_b_im,
              ssm_c_re, ssm_c_im, ssm_d, ssm_w_glu, ssm_b_glu,
              ffn_w_up, ffn_conv_w, ffn_conv_b, ffn_w_down):
    for i in range(DEPTH):
        kind = i % N_MIXERS
        j = i // N_MIXERS
        h = rms_norm(x, norm_mix_g[i])
        if kind == 0:
            m = multiscale_pool_mixer(h, pool_w[j], pool_b[j], pool_scale[j])
        elif kind == 1:
            m = stick_breaking_attention(h, sb_w_qkv[j], sb_q_gain[j], sb_k_gain[j], sb_w_o[j])
        else:
            m = s5_mixer(h, ssm_lam_re[j], ssm_lam_im[j], ssm_log_step[j], ssm_b_re[j], ssm_b_im[j],
                         ssm_c_re[j], ssm_c_im[j], ssm_d[j], ssm_w_glu[j], ssm_b_glu[j])
        x = x + m
        x = x + conv_ffn(rms_norm(x, norm_ffn_g[i]), ffn_w_up[i], ffn_conv_w[i], ffn_conv_b[i], ffn_w_down[i])
    return x


import jax as _jax
import jax.numpy as _jnp

TWIN_FORMAT = 'train_step'
FWD_PARAMS = ['x', 'norm_mix_g', 'norm_ffn_g', 'pool_w', 'pool_b', 'pool_scale', 'sb_w_qkv', 'sb_q_gain', 'sb_k_gain', 'sb_w_o', 'ssm_lam_re', 'ssm_lam_im', 'ssm_log_step', 'ssm_b_re', 'ssm_b_im', 'ssm_c_re', 'ssm_c_im', 'ssm_d', 'ssm_w_glu', 'ssm_b_glu', 'ffn_w_up', 'ffn_conv_w', 'ffn_conv_b', 'ffn_w_down']
TWIN_WEIGHTS = ['norm_mix_g', 'norm_ffn_g', 'pool_w', 'pool_b', 'pool_scale', 'sb_w_qkv', 'sb_q_gain', 'sb_k_gain', 'sb_w_o', 'ssm_lam_re', 'ssm_lam_im', 'ssm_log_step', 'ssm_b_re', 'ssm_b_im', 'ssm_c_re', 'ssm_c_im', 'ssm_d', 'ssm_w_glu', 'ssm_b_glu', 'ffn_w_up', 'ffn_conv_w', 'ffn_conv_b', 'ffn_w_down']
TWIN_DIFF_INPUT = 'x'
TWIN_INPUTS = ['x', 'norm_mix_g', 'norm_ffn_g', 'pool_w', 'pool_b', 'pool_scale', 'sb_w_qkv', 'sb_q_gain', 'sb_k_gain', 'sb_w_o', 'ssm_lam_re', 'ssm_lam_im', 'ssm_log_step', 'ssm_b_re', 'ssm_b_im', 'ssm_c_re', 'ssm_c_im', 'ssm_d', 'ssm_w_glu', 'ssm_b_glu', 'ffn_w_up', 'ffn_conv_w', 'ffn_conv_b', 'ffn_w_down', 'loss_target', 'm_norm_mix_g', 'm_norm_ffn_g', 'm_pool_w', 'm_pool_b', 'm_pool_scale', 'm_sb_w_qkv', 'm_sb_q_gain', 'm_sb_k_gain', 'm_sb_w_o', 'm_ssm_lam_re', 'm_ssm_lam_im', 'm_ssm_log_step', 'm_ssm_b_re', 'm_ssm_b_im', 'm_ssm_c_re', 'm_ssm_c_im', 'm_ssm_d', 'm_ssm_w_glu', 'm_ssm_b_glu', 'm_ffn_w_up', 'm_ffn_conv_w', 'm_ffn_conv_b', 'm_ffn_w_down', 'v_norm_mix_g', 'v_norm_ffn_g', 'v_pool_w', 'v_pool_b', 'v_pool_scale', 'v_sb_w_qkv', 'v_sb_q_gain', 'v_sb_k_gain', 'v_sb_w_o', 'v_ssm_lam_re', 'v_ssm_lam_im', 'v_ssm_log_step', 'v_ssm_b_re', 'v_ssm_b_im', 'v_ssm_c_re', 'v_ssm_c_im', 'v_ssm_d', 'v_ssm_w_glu', 'v_ssm_b_glu', 'v_ffn_w_up', 'v_ffn_conv_w', 'v_ffn_conv_b', 'v_ffn_w_down']
TWIN_OUTPUTS = ['loss', 'grad_x', 'grad_norm_mix_g', 'grad_norm_ffn_g', 'grad_pool_w', 'grad_pool_b', 'grad_pool_scale', 'grad_sb_w_qkv', 'grad_sb_q_gain', 'grad_sb_k_gain', 'grad_sb_w_o', 'grad_ssm_lam_re', 'grad_ssm_lam_im', 'grad_ssm_log_step', 'grad_ssm_b_re', 'grad_ssm_b_im', 'grad_ssm_c_re', 'grad_ssm_c_im', 'grad_ssm_d', 'grad_ssm_w_glu', 'grad_ssm_b_glu', 'grad_ffn_w_up', 'grad_ffn_conv_w', 'grad_ffn_conv_b', 'grad_ffn_w_down', 'delta_norm_mix_g', 'delta_norm_ffn_g', 'delta_pool_w', 'delta_pool_b', 'delta_pool_scale', 'delta_sb_w_qkv', 'delta_sb_q_gain', 'delta_sb_k_gain', 'delta_sb_w_o', 'delta_ssm_lam_re', 'delta_ssm_lam_im', 'delta_ssm_log_step', 'delta_ssm_b_re', 'delta_ssm_b_im', 'delta_ssm_c_re', 'delta_ssm_c_im', 'delta_ssm_d', 'delta_ssm_w_glu', 'delta_ssm_b_glu', 'delta_ffn_w_up', 'delta_ffn_conv_w', 'delta_ffn_conv_b', 'delta_ffn_w_down', 'new_m_norm_mix_g', 'new_m_norm_ffn_g', 'new_m_pool_w', 'new_m_pool_b', 'new_m_pool_scale', 'new_m_sb_w_qkv', 'new_m_sb_q_gain', 'new_m_sb_k_gain', 'new_m_sb_w_o', 'new_m_ssm_lam_re', 'new_m_ssm_lam_im', 'new_m_ssm_log_step', 'new_m_ssm_b_re', 'new_m_ssm_b_im', 'new_m_ssm_c_re', 'new_m_ssm_c_im', 'new_m_ssm_d', 'new_m_ssm_w_glu', 'new_m_ssm_b_glu', 'new_m_ffn_w_up', 'new_m_ffn_conv_w', 'new_m_ffn_conv_b', 'new_m_ffn_w_down', 'new_v_norm_mix_g', 'new_v_norm_ffn_g', 'new_v_pool_w', 'new_v_pool_b', 'new_v_pool_scale', 'new_v_sb_w_qkv', 'new_v_sb_q_gain', 'new_v_sb_k_gain', 'new_v_sb_w_o', 'new_v_ssm_lam_re', 'new_v_ssm_lam_im', 'new_v_ssm_log_step', 'new_v_ssm_b_re', 'new_v_ssm_b_im', 'new_v_ssm_c_re', 'new_v_ssm_c_im', 'new_v_ssm_d', 'new_v_ssm_w_glu', 'new_v_ssm_b_glu', 'new_v_ffn_w_up', 'new_v_ffn_conv_w', 'new_v_ffn_conv_b', 'new_v_ffn_w_down']
TWIN_LEAF_KINDS = {'loss': 'loss', 'grad_x': 'grad_x', 'grad_norm_mix_g': 'grad_w', 'grad_norm_ffn_g': 'grad_w', 'grad_pool_w': 'grad_w', 'grad_pool_b': 'grad_w', 'grad_pool_scale': 'grad_w', 'grad_sb_w_qkv': 'grad_w', 'grad_sb_q_gain': 'grad_w', 'grad_sb_k_gain': 'grad_w', 'grad_sb_w_o': 'grad_w', 'grad_ssm_lam_re': 'grad_w', 'grad_ssm_lam_im': 'grad_w', 'grad_ssm_log_step': 'grad_w', 'grad_ssm_b_re': 'grad_w', 'grad_ssm_b_im': 'grad_w', 'grad_ssm_c_re': 'grad_w', 'grad_ssm_c_im': 'grad_w', 'grad_ssm_d': 'grad_w', 'grad_ssm_w_glu': 'grad_w', 'grad_ssm_b_glu': 'grad_w', 'grad_ffn_w_up': 'grad_w', 'grad_ffn_conv_w': 'grad_w', 'grad_ffn_conv_b': 'grad_w', 'grad_ffn_w_down': 'grad_w', 'delta_norm_mix_g': 'delta_w', 'delta_norm_ffn_g': 'delta_w', 'delta_pool_w': 'delta_w', 'delta_pool_b': 'delta_w', 'delta_pool_scale': 'delta_w', 'delta_sb_w_qkv': 'delta_w', 'delta_sb_q_gain': 'delta_w', 'delta_sb_k_gain': 'delta_w', 'delta_sb_w_o': 'delta_w', 'delta_ssm_lam_re': 'delta_w', 'delta_ssm_lam_im': 'delta_w', 'delta_ssm_log_step': 'delta_w', 'delta_ssm_b_re': 'delta_w', 'delta_ssm_b_im': 'delta_w', 'delta_ssm_c_re': 'delta_w', 'delta_ssm_c_im': 'delta_w', 'delta_ssm_d': 'delta_w', 'delta_ssm_w_glu': 'delta_w', 'delta_ssm_b_glu': 'delta_w', 'delta_ffn_w_up': 'delta_w', 'delta_ffn_conv_w': 'delta_w', 'delta_ffn_conv_b': 'delta_w', 'delta_ffn_w_down': 'delta_w', 'new_m_norm_mix_g': 'new_m', 'new_m_norm_ffn_g': 'new_m', 'new_m_pool_w': 'new_m', 'new_m_pool_b': 'new_m', 'new_m_pool_scale': 'new_m', 'new_m_sb_w_qkv': 'new_m', 'new_m_sb_q_gain': 'new_m', 'new_m_sb_k_gain': 'new_m', 'new_m_sb_w_o': 'new_m', 'new_m_ssm_lam_re': 'new_m', 'new_m_ssm_lam_im': 'new_m', 'new_m_ssm_log_step': 'new_m', 'new_m_ssm_b_re': 'new_m', 'new_m_ssm_b_im': 'new_m', 'new_m_ssm_c_re': 'new_m', 'new_m_ssm_c_im': 'new_m', 'new_m_ssm_d': 'new_m', 'new_m_ssm_w_glu': 'new_m', 'new_m_ssm_b_glu': 'new_m', 'new_m_ffn_w_up': 'new_m', 'new_m_ffn_conv_w': 'new_m', 'new_m_ffn_conv_b': 'new_m', 'new_m_ffn_w_down': 'new_m', 'new_v_norm_mix_g': 'new_v', 'new_v_norm_ffn_g': 'new_v', 'new_v_pool_w': 'new_v', 'new_v_pool_b': 'new_v', 'new_v_pool_scale': 'new_v', 'new_v_sb_w_qkv': 'new_v', 'new_v_sb_q_gain': 'new_v', 'new_v_sb_k_gain': 'new_v', 'new_v_sb_w_o': 'new_v', 'new_v_ssm_lam_re': 'new_v', 'new_v_ssm_lam_im': 'new_v', 'new_v_ssm_log_step': 'new_v', 'new_v_ssm_b_re': 'new_v', 'new_v_ssm_b_im': 'new_v', 'new_v_ssm_c_re': 'new_v', 'new_v_ssm_c_im': 'new_v', 'new_v_ssm_d': 'new_v', 'new_v_ssm_w_glu': 'new_v', 'new_v_ssm_b_glu': 'new_v', 'new_v_ffn_w_up': 'new_v', 'new_v_ffn_conv_w': 'new_v', 'new_v_ffn_conv_b': 'new_v', 'new_v_ffn_w_down': 'new_v'}


def _forward(args):
    return _fwd_reference(*[args[k] for k in FWD_PARAMS])


def _output_shape():
    out = _jax.eval_shape(lambda: _forward(_fwd_setup_inputs(0)))
    return out.shape, out.dtype

N_MICROBATCH = 1
ADAM_LR = 0.001
ADAM_B1 = 0.9
ADAM_B2 = 0.999
ADAM_EPS = 1e-08
ADAM_WD = 0.01
ADAM_STEP = 10
PER_EXAMPLE_BATCH_AXIS = {'x': 0, 'loss_target': 0}
SHARED_INPUTS = []
_WEIGHT_DTYPES = {'norm_mix_g': _jnp.float32, 'norm_ffn_g': _jnp.float32, 'pool_w': _jnp.float32, 'pool_b': _jnp.float32, 'pool_scale': _jnp.float32, 'sb_w_qkv': _jnp.float32, 'sb_q_gain': _jnp.float32, 'sb_k_gain': _jnp.float32, 'sb_w_o': _jnp.float32, 'ssm_lam_re': _jnp.float32, 'ssm_lam_im': _jnp.float32, 'ssm_log_step': _jnp.float32, 'ssm_b_re': _jnp.float32, 'ssm_b_im': _jnp.float32, 'ssm_c_re': _jnp.float32, 'ssm_c_im': _jnp.float32, 'ssm_d': _jnp.float32, 'ssm_w_glu': _jnp.float32, 'ssm_b_glu': _jnp.float32, 'ffn_w_up': _jnp.float32, 'ffn_conv_w': _jnp.float32, 'ffn_conv_b': _jnp.float32, 'ffn_w_down': _jnp.float32}
MOMENT_SCALE = {'norm_mix_g': 4.977611e+00, 'norm_ffn_g': 6.457460e+00, 'pool_w': 5.272219e-01, 'pool_b': 3.063322e+00, 'pool_scale': 6.556340e+00, 'sb_w_qkv': 1.396002e-01, 'sb_q_gain': 8.266110e+00, 'sb_k_gain': 8.237625e+00, 'sb_w_o': 1.906385e-01, 'ssm_lam_re': 4.646528e-03, 'ssm_lam_im': 4.447722e-03, 'ssm_log_step': 2.431547e+00, 'ssm_b_re': 3.291429e-03, 'ssm_b_im': 3.264445e-03, 'ssm_c_re': 6.578172e-03, 'ssm_c_im': 6.521743e-03, 'ssm_d': 1.178971e+00, 'ssm_w_glu': 2.873437e-01, 'ssm_b_glu': 9.006102e-01, 'ffn_w_up': 1.157538e-01, 'ffn_conv_w': 8.804415e-01, 'ffn_conv_b': 8.109240e-01, 'ffn_w_down': 1.569388e-01}


def _to_microbatches(a, axis):
    t = _jnp.moveaxis(a, axis, 0)
    t = t.reshape((N_MICROBATCH, t.shape[0] // N_MICROBATCH) + t.shape[1:])
    return _jnp.moveaxis(t, 1, axis + 1)


def setup_inputs(seed: int = 0) -> dict:
    inp = _fwd_setup_inputs(seed)
    key = _jax.random.fold_in(_jax.random.key(seed), 7919)
    shape, _ = _output_shape()
    out = dict(inp)
    out["loss_target"] = _jax.random.normal(_jax.random.fold_in(key, 0), shape, _jnp.float32)
    for i, name in enumerate(TWIN_WEIGHTS):
        w = inp[name].astype(_jnp.float32)
        if MOMENT_SCALE is None:
            s = _jnp.sqrt(_jnp.mean(_jnp.square(w)) + 1e-30)
        else:
            s = MOMENT_SCALE[name]
        km, kv = _jax.random.split(_jax.random.fold_in(key, i + 1))
        out[name] = w
        out["m_" + name] = s * _jax.random.normal(km, w.shape, _jnp.float32)
        out["v_" + name] = (s * s) * _jax.random.uniform(kv, w.shape, _jnp.float32, 0.5, 1.5)
    if N_MICROBATCH > 1:
        for name, axis in PER_EXAMPLE_BATCH_AXIS.items():
            out[name] = _to_microbatches(out[name], axis)
    return {'x': out['x'], 'norm_mix_g': out['norm_mix_g'], 'norm_ffn_g': out['norm_ffn_g'], 'pool_w': out['pool_w'], 'pool_b': out['pool_b'], 'pool_scale': out['pool_scale'], 'sb_w_qkv': out['sb_w_qkv'], 'sb_q_gain': out['sb_q_gain'], 'sb_k_gain': out['sb_k_gain'], 'sb_w_o': out['sb_w_o'], 'ssm_lam_re': out['ssm_lam_re'], 'ssm_lam_im': out['ssm_lam_im'], 'ssm_log_step': out['ssm_log_step'], 'ssm_b_re': out['ssm_b_re'], 'ssm_b_im': out['ssm_b_im'], 'ssm_c_re': out['ssm_c_re'], 'ssm_c_im': out['ssm_c_im'], 'ssm_d': out['ssm_d'], 'ssm_w_glu': out['ssm_w_glu'], 'ssm_b_glu': out['ssm_b_glu'], 'ffn_w_up': out['ffn_w_up'], 'ffn_conv_w': out['ffn_conv_w'], 'ffn_conv_b': out['ffn_conv_b'], 'ffn_w_down': out['ffn_w_down'], 'loss_target': out['loss_target'], 'm_norm_mix_g': out['m_norm_mix_g'], 'm_norm_ffn_g': out['m_norm_ffn_g'], 'm_pool_w': out['m_pool_w'], 'm_pool_b': out['m_pool_b'], 'm_pool_scale': out['m_pool_scale'], 'm_sb_w_qkv': out['m_sb_w_qkv'], 'm_sb_q_gain': out['m_sb_q_gain'], 'm_sb_k_gain': out['m_sb_k_gain'], 'm_sb_w_o': out['m_sb_w_o'], 'm_ssm_lam_re': out['m_ssm_lam_re'], 'm_ssm_lam_im': out['m_ssm_lam_im'], 'm_ssm_log_step': out['m_ssm_log_step'], 'm_ssm_b_re': out['m_ssm_b_re'], 'm_ssm_b_im': out['m_ssm_b_im'], 'm_ssm_c_re': out['m_ssm_c_re'], 'm_ssm_c_im': out['m_ssm_c_im'], 'm_ssm_d': out['m_ssm_d'], 'm_ssm_w_glu': out['m_ssm_w_glu'], 'm_ssm_b_glu': out['m_ssm_b_glu'], 'm_ffn_w_up': out['m_ffn_w_up'], 'm_ffn_conv_w': out['m_ffn_conv_w'], 'm_ffn_conv_b': out['m_ffn_conv_b'], 'm_ffn_w_down': out['m_ffn_w_down'], 'v_norm_mix_g': out['v_norm_mix_g'], 'v_norm_ffn_g': out['v_norm_ffn_g'], 'v_pool_w': out['v_pool_w'], 'v_pool_b': out['v_pool_b'], 'v_pool_scale': out['v_pool_scale'], 'v_sb_w_qkv': out['v_sb_w_qkv'], 'v_sb_q_gain': out['v_sb_q_gain'], 'v_sb_k_gain': out['v_sb_k_gain'], 'v_sb_w_o': out['v_sb_w_o'], 'v_ssm_lam_re': out['v_ssm_lam_re'], 'v_ssm_lam_im': out['v_ssm_lam_im'], 'v_ssm_log_step': out['v_ssm_log_step'], 'v_ssm_b_re': out['v_ssm_b_re'], 'v_ssm_b_im': out['v_ssm_b_im'], 'v_ssm_c_re': out['v_ssm_c_re'], 'v_ssm_c_im': out['v_ssm_c_im'], 'v_ssm_d': out['v_ssm_d'], 'v_ssm_w_glu': out['v_ssm_w_glu'], 'v_ssm_b_glu': out['v_ssm_b_glu'], 'v_ffn_w_up': out['v_ffn_w_up'], 'v_ffn_conv_w': out['v_ffn_conv_w'], 'v_ffn_conv_b': out['v_ffn_conv_b'], 'v_ffn_w_down': out['v_ffn_w_down']}


def _loss(weights, diff, rest, loss_target):
    with _jax.named_scope("forward"):
        args = {**rest, TWIN_DIFF_INPUT: diff, **{k: w.astype(_WEIGHT_DTYPES[k]) for k, w in weights.items()}}
        y = _forward(args)
    with _jax.named_scope("loss_head"):
        err = _jnp.square(y.astype(_jnp.float32) - loss_target)
        return 0.5 * _jnp.sum(_jnp.mean(err, axis=-1)) if err.ndim else 0.5 * err


def _adamw(w, g, m, v):
    m = ADAM_B1 * m + (1.0 - ADAM_B1) * g
    v = ADAM_B2 * v + (1.0 - ADAM_B2) * _jnp.square(g)
    m_hat = m / (1.0 - ADAM_B1 ** ADAM_STEP)
    v_hat = v / (1.0 - ADAM_B2 ** ADAM_STEP)
    delta = -ADAM_LR * (m_hat / (_jnp.sqrt(v_hat) + ADAM_EPS) + ADAM_WD * w)
    return delta, m, v


def reference(x, norm_mix_g, norm_ffn_g, pool_w, pool_b, pool_scale, sb_w_qkv, sb_q_gain, sb_k_gain, sb_w_o, ssm_lam_re, ssm_lam_im, ssm_log_step, ssm_b_re, ssm_b_im, ssm_c_re, ssm_c_im, ssm_d, ssm_w_glu, ssm_b_glu, ffn_w_up, ffn_conv_w, ffn_conv_b, ffn_w_down, loss_target, m_norm_mix_g, m_norm_ffn_g, m_pool_w, m_pool_b, m_pool_scale, m_sb_w_qkv, m_sb_q_gain, m_sb_k_gain, m_sb_w_o, m_ssm_lam_re, m_ssm_lam_im, m_ssm_log_step, m_ssm_b_re, m_ssm_b_im, m_ssm_c_re, m_ssm_c_im, m_ssm_d, m_ssm_w_glu, m_ssm_b_glu, m_ffn_w_up, m_ffn_conv_w, m_ffn_conv_b, m_ffn_w_down, v_norm_mix_g, v_norm_ffn_g, v_pool_w, v_pool_b, v_pool_scale, v_sb_w_qkv, v_sb_q_gain, v_sb_k_gain, v_sb_w_o, v_ssm_lam_re, v_ssm_lam_im, v_ssm_log_step, v_ssm_b_re, v_ssm_b_im, v_ssm_c_re, v_ssm_c_im, v_ssm_d, v_ssm_w_glu, v_ssm_b_glu, v_ffn_w_up, v_ffn_conv_w, v_ffn_conv_b, v_ffn_w_down):
    given = dict(x=x, norm_mix_g=norm_mix_g, norm_ffn_g=norm_ffn_g, pool_w=pool_w, pool_b=pool_b, pool_scale=pool_scale, sb_w_qkv=sb_w_qkv, sb_q_gain=sb_q_gain, sb_k_gain=sb_k_gain, sb_w_o=sb_w_o, ssm_lam_re=ssm_lam_re, ssm_lam_im=ssm_lam_im, ssm_log_step=ssm_log_step, ssm_b_re=ssm_b_re, ssm_b_im=ssm_b_im, ssm_c_re=ssm_c_re, ssm_c_im=ssm_c_im, ssm_d=ssm_d, ssm_w_glu=ssm_w_glu, ssm_b_glu=ssm_b_glu, ffn_w_up=ffn_w_up, ffn_conv_w=ffn_conv_w, ffn_conv_b=ffn_conv_b, ffn_w_down=ffn_w_down, loss_target=loss_target, m_norm_mix_g=m_norm_mix_g, m_norm_ffn_g=m_norm_ffn_g, m_pool_w=m_pool_w, m_pool_b=m_pool_b, m_pool_scale=m_pool_scale, m_sb_w_qkv=m_sb_w_qkv, m_sb_q_gain=m_sb_q_gain, m_sb_k_gain=m_sb_k_gain, m_sb_w_o=m_sb_w_o, m_ssm_lam_re=m_ssm_lam_re, m_ssm_lam_im=m_ssm_lam_im, m_ssm_log_step=m_ssm_log_step, m_ssm_b_re=m_ssm_b_re, m_ssm_b_im=m_ssm_b_im, m_ssm_c_re=m_ssm_c_re, m_ssm_c_im=m_ssm_c_im, m_ssm_d=m_ssm_d, m_ssm_w_glu=m_ssm_w_glu, m_ssm_b_glu=m_ssm_b_glu, m_ffn_w_up=m_ffn_w_up, m_ffn_conv_w=m_ffn_conv_w, m_ffn_conv_b=m_ffn_conv_b, m_ffn_w_down=m_ffn_w_down, v_norm_mix_g=v_norm_mix_g, v_norm_ffn_g=v_norm_ffn_g, v_pool_w=v_pool_w, v_pool_b=v_pool_b, v_pool_scale=v_pool_scale, v_sb_w_qkv=v_sb_w_qkv, v_sb_q_gain=v_sb_q_gain, v_sb_k_gain=v_sb_k_gain, v_sb_w_o=v_sb_w_o, v_ssm_lam_re=v_ssm_lam_re, v_ssm_lam_im=v_ssm_lam_im, v_ssm_log_step=v_ssm_log_step, v_ssm_b_re=v_ssm_b_re, v_ssm_b_im=v_ssm_b_im, v_ssm_c_re=v_ssm_c_re, v_ssm_c_im=v_ssm_c_im, v_ssm_d=v_ssm_d, v_ssm_w_glu=v_ssm_w_glu, v_ssm_b_glu=v_ssm_b_glu, v_ffn_w_up=v_ffn_w_up, v_ffn_conv_w=v_ffn_conv_w, v_ffn_conv_b=v_ffn_conv_b, v_ffn_w_down=v_ffn_w_down)
    weights = {n: given[n] for n in TWIN_WEIGHTS}
    shared = {n: given[n] for n in SHARED_INPUTS}
    per_example = {n: given[n] for n in ['x']}
    grad_fn = _jax.value_and_grad(_loss, argnums=(0, 1))

    def one_microbatch(ex, loss_target):
        ex = dict(ex)
        diff = ex.pop(TWIN_DIFF_INPUT)
        return grad_fn(weights, diff, {**shared, **ex}, loss_target)

    if N_MICROBATCH == 1:
        loss, (grad_w, grad_x) = one_microbatch(per_example, given["loss_target"])
    else:
        def body(carry, xs):
            loss_sum, grad_sum = carry
            l_k, (gw_k, gx_k) = one_microbatch(xs[0], xs[1])
            with _jax.named_scope("update"):
                return (loss_sum + l_k, _jax.tree.map(_jnp.add, grad_sum, gw_k)), gx_k

        init = (_jnp.zeros((), _jnp.float32), _jax.tree.map(_jnp.zeros_like, weights))
        (loss, grad_w), grad_x = _jax.lax.scan(body, init, (per_example, given["loss_target"]))
    with _jax.named_scope("update"):
        delta_w, new_m, new_v = {}, {}, {}
        for n in TWIN_WEIGHTS:
            delta_w[n], new_m[n], new_v[n] = _adamw(weights[n], grad_w[n], given["m_" + n], given["v_" + n])
    return (loss, grad_x, *[grad_w[n] for n in TWIN_WEIGHTS], *[delta_w[n] for n in TWIN_WEIGHTS],
            *[new_m[n] for n in TWIN_WEIGHTS], *[new_v[n] for n in TWIN_WEIGHTS])
```

```python
import functools
import math

import jax
import jax.numpy as jnp
from jax import lax
from jax.experimental import pallas as pl
from jax.experimental.pallas import tpu as pltpu

F32 = jnp.float32
BF16 = jnp.bfloat16

RMS_EPS = 1e-6
POOL_WINDOWS = (2, 4, 8, 16)
SB_HEAD_DIM = 128
SSM_GROUP_CH = 16
SSM_STATE = 64
SSM_BLOCK_GROUPS = 8
ADAM_LR = 0.001
ADAM_B1 = 0.9
ADAM_B2 = 0.999
ADAM_EPS = 1e-08
ADAM_WD = 0.01
ADAM_STEP = 10

V7X_VMEM_BYTES = 64 * 1024 * 1024
VMEM_LIMIT = V7X_VMEM_BYTES - 8 * 1024 * 1024
SUBLANES = 8
LANES = 128
MESH = pl.DeviceIdType.MESH
N_CHIPS = 4
N_DEV = 8


def _params(*sem):
    return pltpu.CompilerParams(dimension_semantics=tuple(sem) if sem else None, vmem_limit_bytes=VMEM_LIMIT)


def _tile(n, want):
    if n <= want:
        return n
    t = (want // LANES) * LANES
    while t > LANES and n % t:
        t -= LANES
    assert n % t == 0, (n, want)
    return t


def _spec(shape, imap, lead=None):
    if lead is None:
        return pl.BlockSpec(tuple(shape), imap)
    return pl.BlockSpec((None,) + tuple(shape), lambda *a: (lead,) + tuple(imap(*a)))


def _sigmoid(v):
    return 1.0 / (1.0 + jnp.exp(-v))


def _shift_down(v, k):
    return pltpu.roll(v, k, 0)


def _shift_up(v, k):
    return pltpu.roll(v, v.shape[0] - k, 0)


def rmsnorm_fwd(x, g, out_dtypes, name):
    t, d = x.shape
    tr = min(t, 128)

    def body(x_ref, g_ref, *o_refs):
        xv = x_ref[...]
        r = lax.rsqrt(jnp.mean(xv * xv, axis=-1, keepdims=True) + RMS_EPS)
        h = xv * r * g_ref[...]
        for o in o_refs:
            o[...] = h.astype(o.dtype)

    outs = pl.pallas_call(
        body, name=name, grid=(t // tr,),
        in_specs=[pl.BlockSpec((tr, d), lambda i: (i, 0)), pl.BlockSpec((1, d), lambda i: (0, 0))],
        out_specs=[pl.BlockSpec((tr, d), lambda i: (i, 0)) for _ in out_dtypes],
        out_shape=[jax.ShapeDtypeStruct((t, d), dt) for dt in out_dtypes],
        compiler_params=_params("parallel"),
    )(x, g)
    return outs


def rmsnorm_bwd(x, g, dh, dres, name):
    t, d = x.shape
    tr = min(t, 128)

    def body(x_ref, g_ref, dh_ref, dres_ref, dx_ref, dxb_ref, dg_ref):
        xv = x_ref[...]
        r = lax.rsqrt(jnp.mean(xv * xv, axis=-1, keepdims=True) + RMS_EPS)
        xhat = xv * r
        dhv = dh_ref[...]
        dxhat = dhv * g_ref[...]
        dx = dres_ref[...] + r * (dxhat - xhat * jnp.mean(dxhat * xhat, axis=-1, keepdims=True))
        dx_ref[...] = dx
        dxb_ref[...] = dx.astype(BF16)
        part = jnp.sum(dhv * xhat, axis=0, keepdims=True)

        @pl.when(pl.program_id(0) == 0)
        def _():
            dg_ref[...] = part

        @pl.when(pl.program_id(0) != 0)
        def _():
            dg_ref[...] += part

    row = pl.BlockSpec((tr, d), lambda i: (i, 0))
    vec = pl.BlockSpec((1, d), lambda i: (0, 0))
    return pl.pallas_call(
        body, name=name, grid=(t // tr,),
        in_specs=[row, vec, row, row],
        out_specs=[row, row, vec],
        out_shape=[jax.ShapeDtypeStruct((t, d), F32), jax.ShapeDtypeStruct((t, d), BF16),
                   jax.ShapeDtypeStruct((1, d), F32)],
        compiler_params=_params("arbitrary"),
    )(x, g, dh, dres)


def mm_cols(a, b, *, a_contract=1, b_nt=False, out_dtype, name, resid=None, tn=512, lead=None):
    m = a.shape[1 - a_contract]
    k = a.shape[a_contract]
    n = b.shape[-2] if b_nt else b.shape[-1]
    assert (b.shape[-1] if b_nt else b.shape[-2]) == k
    tn = _tile(n, tn)

    def body(a_ref, b_ref, *rest):
        o_ref = rest[-1]
        dn = (((a_contract,), (1 if b_nt else 0,)), ((), ()))
        acc = lax.dot_general(a_ref[...], b_ref[...], dn, preferred_element_type=F32)
        if resid is not None:
            acc = acc + rest[0][...]
        o_ref[...] = acc.astype(o_ref.dtype)

    in_specs = [pl.BlockSpec(a.shape, lambda j: (0, 0)),
                _spec((tn, k), lambda j: (j, 0), lead) if b_nt else _spec((k, tn), lambda j: (0, j), lead)]
    args = [a, b]
    if resid is not None:
        in_specs.append(pl.BlockSpec((m, tn), lambda j: (0, j)))
        args.append(resid)
    return pl.pallas_call(
        body, name=name, grid=(n // tn,), in_specs=in_specs,
        out_specs=pl.BlockSpec((m, tn), lambda j: (0, j)),
        out_shape=jax.ShapeDtypeStruct((m, n), out_dtype),
        compiler_params=_params("parallel"),
    )(*args)


def mm_rows(st, res, *, out_dtype, name, tn=512):
    k, m = st.shape
    n = res.shape[1]
    assert res.shape[0] == k
    tn = _tile(m, tn)

    def body(st_ref, res_ref, o_ref):
        o_ref[...] = lax.dot_general(st_ref[...], res_ref[...], (((0,), (0,)), ((), ())),
                                     preferred_element_type=F32).astype(o_ref.dtype)

    return pl.pallas_call(
        body, name=name, grid=(m // tn,),
        in_specs=[pl.BlockSpec((k, tn), lambda j: (0, j)), pl.BlockSpec((k, n), lambda j: (0, 0))],
        out_specs=pl.BlockSpec((tn, n), lambda j: (j, 0)),
        out_shape=jax.ShapeDtypeStruct((m, n), out_dtype),
        compiler_params=_params("parallel"),
    )(st, res)


def mm_k(a_list, b, *, b_nt, name, resid=None, tk=512, tnn=1024, lead=None):
    m = a_list[0].shape[0]
    ks = [a.shape[1] for a in a_list]
    ktot = sum(ks)
    n = b.shape[-2] if b_nt else b.shape[-1]
    assert (b.shape[-1] if b_nt else b.shape[-2]) == ktot
    tk = _tile(ks[0], tk)
    assert all(kk % tk == 0 for kk in ks)
    tnn = _tile(n, tnn)
    nks = [kk // tk for kk in ks]
    starts = [sum(nks[:i]) for i in range(len(nks))]
    nk = sum(nks)

    def body(*refs):
        a_refs = refs[:len(a_list)]
        b_ref = refs[len(a_list)]
        o_ref = refs[-1]
        kk = pl.program_id(1)

        @pl.when(kk == 0)
        def _():
            if resid is not None:
                o_ref[...] = refs[len(a_list) + 1][...]
            else:
                o_ref[...] = jnp.zeros_like(o_ref)

        dn = (((1,), (1 if b_nt else 0,)), ((), ()))
        for i, a_ref in enumerate(a_refs):
            @pl.when(jnp.logical_and(kk >= starts[i], kk < starts[i] + nks[i]))
            def _(a_ref=a_ref):
                o_ref[...] += lax.dot_general(a_ref[...], b_ref[...], dn, preferred_element_type=F32)

    def a_spec(i):
        return pl.BlockSpec((m, tk), lambda nn, kk: (0, jnp.clip(kk - starts[i], 0, nks[i] - 1)))

    in_specs = [a_spec(i) for i in range(len(a_list))]
    in_specs.append(_spec((tnn, tk), lambda nn, kk: (nn, kk), lead) if b_nt
                    else _spec((tk, tnn), lambda nn, kk: (kk, nn), lead))
    args = list(a_list) + [b]
    if resid is not None:
        in_specs.append(pl.BlockSpec((m, tnn), lambda nn, kk: (0, nn)))
        args.append(resid)
    return pl.pallas_call(
        body, name=name, grid=(n // tnn, nk), in_specs=in_specs,
        out_specs=pl.BlockSpec((m, tnn), lambda nn, kk: (0, nn)),
        out_shape=jax.ShapeDtypeStruct((m, n), F32),
        compiler_params=_params("parallel", "arbitrary"),
    )(*args)


HALO = SUBLANES
CHUNK_ROWS = 64


def _conv_taps(ext, r):
    return ext[HALO:], _shift_down(ext, 1)[HALO:], _shift_down(ext, 2)[HALO:]


def ffn_up_fused(h, w_up, conv_w, conv_b, name, lead=None):
    t, d = h.shape
    f = w_up.shape[-1] // 2
    tn = _tile(f, 256)
    nf = f // tn
    r = min(CHUNK_ROWS, t)

    def body(h_ref, wv_ref, wg_ref, cwv_ref, cwg_ref, cbv_ref, cbg_ref, uv_ref, ug_ref, act_ref, sv, sg):
        zero = jnp.zeros((HALO, tn), F32)
        sv[0:HALO, :] = zero
        sg[0:HALO, :] = zero
        hv = h_ref[...]
        sv[HALO:, :] = jnp.dot(hv, wv_ref[...], preferred_element_type=F32).astype(BF16).astype(F32)
        sg[HALO:, :] = jnp.dot(hv, wg_ref[...], preferred_element_type=F32).astype(BF16).astype(F32)
        cwv, cwg = cwv_ref[...], cwg_ref[...]
        cbv, cbg = cbv_ref[...], cbg_ref[...]

        def chunk(i, carry):
            r0 = pl.multiple_of(i * r, r)
            v0, v1, v2 = _conv_taps(sv[pl.ds(r0, r + HALO), :], r)
            g0, g1, g2 = _conv_taps(sg[pl.ds(r0, r + HALO), :], r)
            cval = cbv + cwv[2:3] * v0 + cwv[1:2] * v1 + cwv[0:1] * v2
            cgate = cbg + cwg[2:3] * g0 + cwg[1:2] * g1 + cwg[0:1] * g2
            uv_ref[pl.ds(r0, r), :] = v0.astype(BF16)
            ug_ref[pl.ds(r0, r), :] = g0.astype(BF16)
            act_ref[pl.ds(r0, r), :] = (cgate * _sigmoid(cgate) * cval).astype(BF16)
            return carry

        lax.fori_loop(0, t // r, chunk, 0)

    col = lambda off: _spec((d, tn), lambda j: (0, j + off), lead)
    cw = lambda off: pl.BlockSpec((3, tn), lambda j: (0, j + off))
    cb = lambda off: pl.BlockSpec((1, tn), lambda j: (0, j + off))
    out = pl.BlockSpec((t, tn), lambda j: (0, j))
    return pl.pallas_call(
        body, name=name, grid=(nf,),
        in_specs=[pl.BlockSpec((t, d), lambda j: (0, 0)), col(0), col(nf), cw(0), cw(nf), cb(0), cb(nf)],
        out_specs=[out, out, out],
        out_shape=[jax.ShapeDtypeStruct((t, f), BF16)] * 3,
        scratch_shapes=[pltpu.VMEM((t + HALO, tn), F32), pltpu.VMEM((t + HALO, tn), F32)],
        compiler_params=_params("parallel"),
    )(h, w_up, w_up, conv_w, conv_w, conv_b, conv_b)


def ffn_bwd_fused(dout, w_down, up_val, up_gate, conv_w, conv_b, name, lead=None):
    t, d = dout.shape
    f = w_down.shape[-2]
    tn = _tile(f, 256)
    nf = f // tn
    r = min(CHUNK_ROWS, t)

    def body(do_ref, wd_ref, uv_ref, ug_ref, cwv_ref, cwg_ref, cbv_ref, cbg_ref,
             dv_ref, dg_ref, dcwv_ref, dcwg_ref, dcbv_ref, dcbg_ref, da, sv, sg, ev, eg):
        zero = jnp.zeros((HALO, tn), F32)
        sv[0:HALO, :] = zero
        sg[0:HALO, :] = zero
        ev[t:, :] = zero
        eg[t:, :] = zero
        da[...] = lax.dot_general(do_ref[...], wd_ref[...], (((1,), (1,)), ((), ())), preferred_element_type=F32)
        sv[HALO:, :] = uv_ref[...].astype(F32)
        sg[HALO:, :] = ug_ref[...].astype(F32)
        cwv, cwg = cwv_ref[...], cwg_ref[...]
        cbv, cbg = cbv_ref[...], cbg_ref[...]

        def chunk(i, acc):
            r0 = pl.multiple_of(i * r, r)
            v = _conv_taps(sv[pl.ds(r0, r + HALO), :], r)
            g = _conv_taps(sg[pl.ds(r0, r + HALO), :], r)
            cval = cbv + cwv[2:3] * v[0] + cwv[1:2] * v[1] + cwv[0:1] * v[2]
            cgate = cbg + cwg[2:3] * g[0] + cwg[1:2] * g[1] + cwg[0:1] * g[2]
            s = _sigmoid(cgate)
            dav = da[pl.ds(r0, r), :]
            dval = dav * (cgate * s)
            dgate = dav * cval * (s * (1.0 + cgate * (1.0 - s)))
            ev[pl.ds(r0, r), :] = dval
            eg[pl.ds(r0, r), :] = dgate
            col = lambda z: jnp.sum(z, axis=0, keepdims=True)
            new = [acc[0] + col(dval), acc[1] + col(dgate)]
            new += [acc[2 + j] + col(dval * v[2 - j]) for j in range(3)]
            new += [acc[5 + j] + col(dgate * g[2 - j]) for j in range(3)]
            return tuple(new)

        z1 = jnp.zeros((1, tn), F32)
        acc = lax.fori_loop(0, t // r, chunk, (z1,) * 8)
        dcbv_ref[...] = acc[0]
        dcbg_ref[...] = acc[1]
        for j in range(3):
            dcwv_ref[j:j + 1, :] = acc[2 + j]
            dcwg_ref[j:j + 1, :] = acc[5 + j]

        def chunk2(i, carry):
            r0 = pl.multiple_of(i * r, r)
            for e_ref, cw_, o_ref in ((ev, cwv, dv_ref), (eg, cwg, dg_ref)):
                ext = e_ref[pl.ds(r0, r + HALO), :]
                d0, d1, d2 = ext[:r], _shift_up(ext, 1)[:r], _shift_up(ext, 2)[:r]
                o_ref[pl.ds(r0, r), :] = (cw_[2:3] * d0 + cw_[1:2] * d1 + cw_[0:1] * d2).astype(BF16)
            return carry

        lax.fori_loop(0, t // r, chunk2, 0)

    cw = lambda off: pl.BlockSpec((3, tn), lambda j: (0, j + off))
    cb = lambda off: pl.BlockSpec((1, tn), lambda j: (0, j + off))
    tile = pl.BlockSpec((t, tn), lambda j: (0, j))
    s = lambda rows, dt: jax.ShapeDtypeStruct((rows, f), dt)
    halo = pltpu.VMEM((t + HALO, tn), F32)
    return pl.pallas_call(
        body, name=name, grid=(nf,),
        in_specs=[pl.BlockSpec((t, d), lambda j: (0, 0)), _spec((tn, d), lambda j: (j, 0), lead),
                  tile, tile, cw(0), cw(nf), cb(0), cb(nf)],
        out_specs=[tile, tile, pl.BlockSpec((3, tn), lambda j: (0, j)), pl.BlockSpec((3, tn), lambda j: (0, j)),
                   pl.BlockSpec((1, tn), lambda j: (0, j)), pl.BlockSpec((1, tn), lambda j: (0, j))],
        out_shape=[s(t, BF16), s(t, BF16), s(3, F32), s(3, F32), s(1, F32), s(1, F32)],
        scratch_shapes=[pltpu.VMEM((t, tn), F32), halo, halo, halo, halo],
        compiler_params=_params("parallel"),
    )(dout, w_down, up_val, up_gate, conv_w, conv_w, conv_b, conv_b)


POOL_PAD = max(POOL_WINDOWS)


def _window_sum(ext, win, shift):
    assert POOL_WINDOWS == (2, 4, 8, 16)
    s2 = ext + shift(ext, 1)
    s4 = s2 + shift(s2, 2)
    s8 = s4 + shift(s4, 4)
    s16 = s8 + shift(s8, 8)
    return jnp.where(win == 2, s2, jnp.where(win == 4, s4, jnp.where(win == 8, s8, s16)))


def _pool_win_scalar(g):
    win = jnp.int32(POOL_WINDOWS[-1])
    for k in range(len(POOL_WINDOWS) - 2, -1, -1):
        win = jnp.where(g == k, jnp.int32(POOL_WINDOWS[k]), win)
    return win


def _pool_count(r0, r, win):
    rows = r0 + lax.broadcasted_iota(jnp.int32, (r, 1), 0)
    return jnp.minimum(rows + 1, win).astype(F32)


def _pooled_into(hp, pooled, h_ref, t, r, win):
    hp[0:POOL_PAD, :] = jnp.zeros((POOL_PAD, hp.shape[1]), F32)
    hp[POOL_PAD:, :] = h_ref[...]

    def chunk(i, carry):
        r0 = pl.multiple_of(i * r, r)
        ext = hp[pl.ds(r0, r + POOL_PAD), :]
        s = _window_sum(ext, win, _shift_down)[POOL_PAD:]
        pooled[pl.ds(r0, r), :] = (s / _pool_count(r0, r, win) - ext[POOL_PAD:]).astype(BF16)
        return carry

    lax.fori_loop(0, t // r, chunk, 0)


def pool_fwd(h, x, w, b, scale, name, lead=None):
    t, d = h.shape
    ng, dg = w.shape[-3], w.shape[-2]
    r = min(CHUNK_ROWS, t)

    def body(h_ref, x_ref, w_ref, b_ref, s_ref, o_ref, hp, pooled):
        win = _pool_win_scalar(pl.program_id(0))
        _pooled_into(hp, pooled, h_ref, t, r, win)
        y = jnp.dot(pooled[...], w_ref[...], preferred_element_type=F32)
        o_ref[...] = x_ref[...] + (y + b_ref[...]) * s_ref[...]

    col = pl.BlockSpec((t, dg), lambda g: (0, g))
    vec = pl.BlockSpec((1, dg), lambda g: (0, g))
    return pl.pallas_call(
        body, name=name, grid=(ng,),
        in_specs=[col, col, _spec((None, dg, dg), lambda g: (g, 0, 0), lead), vec, vec],
        out_specs=col, out_shape=jax.ShapeDtypeStruct((t, d), F32),
        scratch_shapes=[pltpu.VMEM((t + POOL_PAD, dg), F32), pltpu.VMEM((t, dg), BF16)],
        compiler_params=_params("parallel"),
    )(h, x, w, b, scale)


def pool_bwd(dm, h, w, b, scale, name, lead=None):
    t, d = h.shape
    ng, dg = w.shape[-3], w.shape[-2]
    r = min(CHUNK_ROWS, t)

    def body(dm_ref, h_ref, w_ref, b_ref, s_ref, dh_ref, dw_ref, db_ref, ds_ref, hp, pooled, q):
        win = _pool_win_scalar(pl.program_id(0))
        _pooled_into(hp, pooled, h_ref, t, r, win)
        wv = w_ref[...]
        y = jnp.dot(pooled[...], wv, preferred_element_type=F32)
        dmv = dm_ref[...]
        ds_ref[...] = jnp.sum(dmv * (y + b_ref[...]), axis=0, keepdims=True)
        dy = dmv * s_ref[...]
        db_ref[...] = jnp.sum(dy, axis=0, keepdims=True)
        dyb = dy.astype(BF16)
        dw_ref[...] = lax.dot_general(pooled[...], dyb, (((0,), (0,)), ((), ())),
                                      preferred_element_type=F32).astype(dw_ref.dtype)
        dp = lax.dot_general(dyb, wv, (((1,), (1,)), ((), ())), preferred_element_type=F32)
        q[t:, :] = jnp.zeros((POOL_PAD, dg), F32)
        q[0:t, :] = dp / _pool_count(0, t, win)
        dh_ref[...] = -dp

        def chunk(i, carry):
            r0 = pl.multiple_of(i * r, r)
            ext = q[pl.ds(r0, r + POOL_PAD), :]
            dh_ref[pl.ds(r0, r), :] += _window_sum(ext, win, _shift_up)[:r]
            return carry

        lax.fori_loop(0, t // r, chunk, 0)

    col = pl.BlockSpec((t, dg), lambda g: (0, g))
    vec = pl.BlockSpec((1, dg), lambda g: (0, g))
    wspec = pl.BlockSpec((None, dg, dg), lambda g: (g, 0, 0))
    return pl.pallas_call(
        body, name=name, grid=(ng,),
        in_specs=[col, col, _spec((None, dg, dg), lambda g: (g, 0, 0), lead), vec, vec],
        out_specs=[col, wspec, vec, vec],
        out_shape=[jax.ShapeDtypeStruct((t, d), F32), jax.ShapeDtypeStruct((ng, dg, dg), BF16),
                   jax.ShapeDtypeStruct((1, d), F32), jax.ShapeDtypeStruct((1, d), F32)],
        scratch_shapes=[pltpu.VMEM((t + POOL_PAD, dg), F32), pltpu.VMEM((t, dg), BF16),
                        pltpu.VMEM((t + POOL_PAD, dg), F32)],
        compiler_params=_params("parallel"),
    )(dm, h, w, b, scale)


SB_BLOCK = 256


def _tri_sum(v, tri):
    hi = v.astype(BF16)
    r1 = v - hi.astype(F32)
    mid = r1.astype(BF16)
    lo = (r1 - mid.astype(F32)).astype(BF16)
    dot = lambda p: jnp.dot(p, tri, preferred_element_type=F32)
    return dot(hi) + dot(mid) + dot(lo)


def _tri(bk, cmp):
    return cmp(lax.broadcasted_iota(jnp.int32, (bk, bk), 0), lax.broadcasted_iota(jnp.int32, (bk, bk), 1)).astype(BF16)


def _sb_logits(qblk, kblk, q0, k0, inv):
    bq, bk = qblk.shape[0], kblk.shape[0]
    z = lax.dot_general(qblk, kblk, (((1,), (1,)), ((), ())), preferred_element_type=F32) * inv
    qpos = q0 + lax.broadcasted_iota(jnp.int32, (bq, bk), 0)
    kpos = k0 + lax.broadcasted_iota(jnp.int32, (bq, bk), 1)
    mask = kpos < qpos
    lb = jnp.minimum(z, 0.0) - jnp.log(1.0 + jnp.exp(-jnp.abs(z)))
    lm = jnp.where(mask, lb - z, 0.0)
    return lb, lm, mask


def _head_norm(ref, gain):
    xv = ref[...].astype(F32)
    r = lax.rsqrt(jnp.mean(xv * xv, axis=-1, keepdims=True) + RMS_EPS)
    xhat = xv * r
    return xhat, r, (xhat * gain).astype(BF16)


def sb_attn_fwd(qkv, q_gain, k_gain, name):
    t = qkv.shape[0]
    d = qkv.shape[1] // 3
    dh = SB_HEAD_DIM
    nh = d // dh
    blk = min(SB_BLOCK, t)
    inv = 1.0 / math.sqrt(dh)

    def body(q_ref, k_ref, v_ref, qg_ref, kg_ref, o_ref, lt_ref, qn, kn):
        qn[...] = _head_norm(q_ref, qg_ref[...])[2]
        kn[...] = _head_norm(k_ref, kg_ref[...])[2]
        later = _tri(blk, lambda j, s: j > s)

        def q_loop(qb, carry):
            q0 = pl.multiple_of(qb * blk, blk)
            qblk = qn[pl.ds(q0, blk), :]

            def k_loop(i, st):
                c, acc = st
                k0 = pl.multiple_of((qb - i) * blk, blk)
                lb, lm, mask = _sb_logits(qblk, kn[pl.ds(k0, blk), :], q0, k0, inv)
                a = jnp.where(mask, jnp.exp(lb + _tri_sum(lm, later) + c), 0.0)
                acc = acc + jnp.dot(a.astype(BF16), v_ref[pl.ds(k0, blk), :], preferred_element_type=F32)
                return c + jnp.sum(lm, axis=1, keepdims=True), acc

            c, acc = lax.fori_loop(0, qb + 1, k_loop, (jnp.zeros((blk, 1), F32), jnp.zeros((blk, dh), F32)))
            o_ref[pl.ds(q0, blk), :] = acc.astype(BF16)
            lt_ref[pl.ds(q0, blk), :] = c
            return carry

        lax.fori_loop(0, t // blk, q_loop, 0)

    head = lambda off: pl.BlockSpec((t, dh), lambda h: (0, h + off))
    gain = pl.BlockSpec((1, dh), lambda h: (0, 0))
    return pl.pallas_call(
        body, name=name, grid=(nh,),
        in_specs=[head(0), head(nh), head(2 * nh), gain, gain],
        out_specs=[head(0), pl.BlockSpec((None, t, 1), lambda h: (h, 0, 0))],
        out_shape=[jax.ShapeDtypeStruct((t, d), BF16), jax.ShapeDtypeStruct((nh, t, 1), F32)],
        scratch_shapes=[pltpu.VMEM((t, dh), BF16), pltpu.VMEM((t, dh), BF16)],
        compiler_params=_params("parallel"),
    )(qkv, qkv, qkv, q_gain, k_gain)


def sb_attn_bwd(qkv, ltot, do, q_gain, k_gain, name):
    t = qkv.shape[0]
    d = qkv.shape[1] // 3
    dh = SB_HEAD_DIM
    nh = d // dh
    blk = min(SB_BLOCK, t)
    inv = 1.0 / math.sqrt(dh)
    tn_dims = (((0,), (0,)), ((), ()))

    def body(q_ref, k_ref, v_ref, lt_ref, do_ref, qg_ref, kg_ref, dq_ref, dk_ref, dv_ref, dqg_ref, dkg_ref,
             qn, kn, dqn, dkn, dvn):
        qg, kg = qg_ref[...], kg_ref[...]
        qhat, rq, qnb = _head_norm(q_ref, qg)
        khat, rk, knb = _head_norm(k_ref, kg)
        qn[...] = qnb
        kn[...] = knb
        dkn[...] = jnp.zeros_like(dkn)
        dvn[...] = jnp.zeros_like(dvn)
        upto = _tri(blk, lambda j, s: j <= s)
        before = _tri(blk, lambda j, s: j < s)

        def q_loop(qb, carry):
            q0 = pl.multiple_of(qb * blk, blk)
            qblk = qn[pl.ds(q0, blk), :]
            doblk = do_ref[pl.ds(q0, blk), :]
            ltv = lt_ref[pl.ds(q0, blk), :]

            def k_loop(kb, st):
                pl_, pg, dq = st
                k0 = pl.multiple_of(kb * blk, blk)
                kblk = kn[pl.ds(k0, blk), :]
                lb, lm, mask = _sb_logits(qblk, kblk, q0, k0, inv)
                a = jnp.where(mask, jnp.exp(lb + (ltv - pl_ - _tri_sum(lm, upto))), 0.0)
                da = lax.dot_general(doblk, v_ref[pl.ds(k0, blk), :], (((1,), (1,)), ((), ())),
                                     preferred_element_type=F32)
                g = da * a
                g_before = pg + _tri_sum(g, before)
                beta = jnp.exp(lb)
                dz = (jnp.where(mask, g * (1.0 - beta) - beta * g_before, 0.0) * inv).astype(BF16)
                dq = dq + jnp.dot(dz, kblk, preferred_element_type=F32)
                dkn[pl.ds(k0, blk), :] += lax.dot_general(dz, qblk, tn_dims, preferred_element_type=F32)
                dvn[pl.ds(k0, blk), :] += lax.dot_general(a.astype(BF16), doblk, tn_dims, preferred_element_type=F32)
                return pl_ + jnp.sum(lm, axis=1, keepdims=True), pg + jnp.sum(g, axis=1, keepdims=True), dq

            z1 = jnp.zeros((blk, 1), F32)
            _, _, dq = lax.fori_loop(0, qb + 1, k_loop, (z1, z1, jnp.zeros((blk, dh), F32)))
            dqn[pl.ds(q0, blk), :] = dq
            return carry

        lax.fori_loop(0, t // blk, q_loop, 0)

        first = pl.program_id(0) == 0
        for dn, xhat, r, gain, out_ref, dgain_ref in ((dqn, qhat, rq, qg, dq_ref, dqg_ref),
                                                      (dkn, khat, rk, kg, dk_ref, dkg_ref)):
            dnv = dn[...]
            dxhat = dnv * gain
            out_ref[...] = (r * (dxhat - xhat * jnp.mean(dxhat * xhat, axis=-1, keepdims=True))).astype(BF16)
            part = jnp.sum(dnv * xhat, axis=0, keepdims=True)

            @pl.when(first)
            def _(dgain_ref=dgain_ref, part=part):
                dgain_ref[...] = part

            @pl.when(jnp.logical_not(first))
            def _(dgain_ref=dgain_ref, part=part):
                dgain_ref[...] += part

        dv_ref[...] = dvn[...].astype(BF16)

    head = lambda off: pl.BlockSpec((t, dh), lambda h: (0, h + off))
    gain = pl.BlockSpec((1, dh), lambda h: (0, 0))
    big = jax.ShapeDtypeStruct((t, d), BF16)
    small = jax.ShapeDtypeStruct((1, dh), F32)
    return pl.pallas_call(
        body, name=name, grid=(nh,),
        in_specs=[head(0), head(nh), head(2 * nh), pl.BlockSpec((None, t, 1), lambda h: (h, 0, 0)), head(0),
                  gain, gain],
        out_specs=[head(0), head(0), head(0), gain, gain],
        out_shape=[big, big, big, small, small],
        scratch_shapes=[pltpu.VMEM((t, dh), BF16), pltpu.VMEM((t, dh), BF16),
                        pltpu.VMEM((t, dh), F32), pltpu.VMEM((t, dh), F32), pltpu.VMEM((t, dh), F32)],
        compiler_params=_params("arbitrary"),
    )(qkv, qkv, qkv, ltot, do, q_gain, k_gain)


GELU_C = math.sqrt(2.0 / math.pi)
GELU_A = 0.044715
SCAN_ROWS = SUBLANES


def _gelu(y):
    return 0.5 * y * (1.0 + jnp.tanh(GELU_C * (y + GELU_A * y * y * y)))


def _gelu_grad(y):
    th = jnp.tanh(GELU_C * (y + GELU_A * y * y * y))
    return 0.5 * (1.0 + th) + 0.5 * y * (1.0 - th * th) * GELU_C * (1.0 + 3.0 * GELU_A * y * y)


def _powers(ar, ai):
    out = [(ar, ai)]
    for _ in range(SCAN_ROWS - 1):
        pr, pi = out[-1]
        out.append((pr * ar - pi * ai, pr * ai + pi * ar))
    return out


def _rows(vals):
    c = vals[0].shape[1]
    row = lax.broadcasted_iota(jnp.int32, (SCAN_ROWS, c), 0)
    out = jnp.broadcast_to(vals[SCAN_ROWS - 1], (SCAN_ROWS, c))
    for j in range(SCAN_ROWS - 2, -1, -1):
        out = jnp.where(row == j, vals[j], out)
    return out


def _scan_forward(sr, si, off, t, ar, ai):
    c = ar.shape[1]
    p = _powers(ar, ai)
    pwr = _rows([q[0] for q in p])
    pwi = _rows([q[1] for q in p])
    row = lax.broadcasted_iota(jnp.int32, (SCAN_ROWS, c), 0)

    def tile(i, carry):
        cr, ci = carry
        r0 = pl.multiple_of(off + i * SCAN_ROWS, SCAN_ROWS)
        xr = sr[pl.ds(r0, SCAN_ROWS), :]
        xi = si[pl.ds(r0, SCAN_ROWS), :]
        for k in (1, 2, 4):
            pr, pi = p[k - 1]
            shr = jnp.where(row >= k, _shift_down(xr, k), 0.0)
            shi = jnp.where(row >= k, _shift_down(xi, k), 0.0)
            xr, xi = xr + pr * shr - pi * shi, xi + pr * shi + pi * shr
        xr, xi = xr + pwr * cr - pwi * ci, xi + pwr * ci + pwi * cr
        sr[pl.ds(r0, SCAN_ROWS), :] = xr
        si[pl.ds(r0, SCAN_ROWS), :] = xi
        return xr[SCAN_ROWS - 1:SCAN_ROWS], xi[SCAN_ROWS - 1:SCAN_ROWS]

    z = jnp.zeros((1, c), F32)
    lax.fori_loop(0, t // SCAN_ROWS, tile, (z, z))


def _scan_reverse(gr, gi, t, ar, ai, xr_ref, xi_ref):
    c = ar.shape[1]
    p = _powers(ar, ai)
    pwr = _rows([p[SCAN_ROWS - 1 - j][0] for j in range(SCAN_ROWS)])
    pwi = _rows([p[SCAN_ROWS - 1 - j][1] for j in range(SCAN_ROWS)])
    row = lax.broadcasted_iota(jnp.int32, (SCAN_ROWS, c), 0)
    n = t // SCAN_ROWS

    def tile(ii, carry):
        cr, ci, dar, dai = carry
        r0 = pl.multiple_of((n - 1 - ii) * SCAN_ROWS, SCAN_ROWS)
        xr = gr[pl.ds(r0, SCAN_ROWS), :]
        xi = gi[pl.ds(r0, SCAN_ROWS), :]
        for k in (1, 2, 4):
            pr, pi = p[k - 1]
            shr = jnp.where(row < SCAN_ROWS - k, _shift_up(xr, k), 0.0)
            shi = jnp.where(row < SCAN_ROWS - k, _shift_up(xi, k), 0.0)
            xr, xi = xr + pr * shr + pi * shi, xi + pr * shi - pi * shr
        xr, xi = xr + pwr * cr + pwi * ci, xi + pwr * ci - pwi * cr
        gr[pl.ds(r0, SCAN_ROWS), :] = xr
        gi[pl.ds(r0, SCAN_ROWS), :] = xi
        xpr = _shift_down(xr_ref[pl.ds(r0, 2 * SCAN_ROWS), :], 1)[SCAN_ROWS:]
        xpi = _shift_down(xi_ref[pl.ds(r0, 2 * SCAN_ROWS), :], 1)[SCAN_ROWS:]
        return xr[0:1], xi[0:1], dar + xr * xpr + xi * xpi, dai + xi * xpr - xr * xpi

    z = jnp.zeros((1, c), F32)
    z8 = jnp.zeros((SCAN_ROWS, c), F32)
    _, _, dar, dai = lax.fori_loop(0, n, tile, (z, z, z8, z8))
    return jnp.sum(dar, axis=0, keepdims=True), jnp.sum(dai, axis=0, keepdims=True)


def _ssm_specs(t, nb, ch, st):
    col = pl.BlockSpec((t, ch), lambda b: (0, b))
    vec = pl.BlockSpec((1, ch), lambda b: (0, b))
    bspec = pl.BlockSpec((None, ch, st), lambda b: (b, 0, 0))
    cspec = pl.BlockSpec((None, st, ch), lambda b: (b, 0, 0))
    aspec = pl.BlockSpec((None, 1, st), lambda b: (b, 0, 0))
    return col, vec, bspec, cspec, aspec


def ssm_core_fwd(u, bre, bim, cre, cim, a_re, a_im, dskip, name):
    t, d = u.shape
    nb, ch, st = bre.shape

    def body(u_ref, bre_ref, bim_ref, cre_ref, cim_ref, ar_ref, ai_ref, d_ref, y_ref, yg_ref, sr, si):
        uv = u_ref[...]
        ub = uv.astype(BF16)
        sr[...] = jnp.dot(ub, bre_ref[...], preferred_element_type=F32)
        si[...] = jnp.dot(ub, bim_ref[...], preferred_element_type=F32)
        _scan_forward(sr, si, 0, t, ar_ref[...], ai_ref[...])
        y = (jnp.dot(sr[...].astype(BF16), cre_ref[...], preferred_element_type=F32)
             - jnp.dot(si[...].astype(BF16), cim_ref[...], preferred_element_type=F32) + d_ref[...] * uv)
        y_ref[...] = y
        yg_ref[...] = _gelu(y).astype(BF16)

    col, vec, bspec, cspec, aspec = _ssm_specs(t, nb, ch, st)
    return pl.pallas_call(
        body, name=name, grid=(nb,),
        in_specs=[col, bspec, bspec, cspec, cspec, aspec, aspec, vec],
        out_specs=[col, col],
        out_shape=[jax.ShapeDtypeStruct((t, d), F32), jax.ShapeDtypeStruct((t, d), BF16)],
        scratch_shapes=[pltpu.VMEM((t, st), F32), pltpu.VMEM((t, st), F32)],
        compiler_params=_params("parallel"),
    )(u, bre, bim, cre, cim, a_re, a_im, dskip)


def ssm_core_bwd(u, y, dyg, bre, bim, cre, cim, a_re, a_im, dskip, name):
    t, d = u.shape
    nb, ch, st = bre.shape
    tn_dims = (((0,), (0,)), ((), ()))
    nt_dims = (((1,), (1,)), ((), ()))

    def body(u_ref, y_ref, dyg_ref, bre_ref, bim_ref, cre_ref, cim_ref, ar_ref, ai_ref, d_ref,
             du_ref, dd_ref, dbre_ref, dbim_ref, dcre_ref, dcim_ref, dar_ref, dai_ref, xr, xi, gr, gi):
        uv = u_ref[...]
        ub = uv.astype(BF16)
        ar, ai = ar_ref[...], ai_ref[...]
        dy = dyg_ref[...] * _gelu_grad(y_ref[...])
        dd_ref[...] = jnp.sum(dy * uv, axis=0, keepdims=True)
        zero = jnp.zeros((HALO, st), F32)
        xr[0:HALO, :] = zero
        xi[0:HALO, :] = zero
        xr[HALO:, :] = jnp.dot(ub, bre_ref[...], preferred_element_type=F32)
        xi[HALO:, :] = jnp.dot(ub, bim_ref[...], preferred_element_type=F32)
        _scan_forward(xr, xi, HALO, t, ar, ai)
        dyb = dy.astype(BF16)
        dcre_ref[...] = lax.dot_general(xr[HALO:, :].astype(BF16), dyb, tn_dims, preferred_element_type=F32)
        dcim_ref[...] = -lax.dot_general(xi[HALO:, :].astype(BF16), dyb, tn_dims, preferred_element_type=F32)
        gr[...] = lax.dot_general(dyb, cre_ref[...], nt_dims, preferred_element_type=F32)
        gi[...] = -lax.dot_general(dyb, cim_ref[...], nt_dims, preferred_element_type=F32)
        dar, dai = _scan_reverse(gr, gi, t, ar, ai, xr, xi)
        dar_ref[...] = dar
        dai_ref[...] = dai
        grb = gr[...].astype(BF16)
        gib = gi[...].astype(BF16)
        dbre_ref[...] = lax.dot_general(ub, grb, tn_dims, preferred_element_type=F32)
        dbim_ref[...] = lax.dot_general(ub, gib, tn_dims, preferred_element_type=F32)
        du_ref[...] = (d_ref[...] * dy + lax.dot_general(grb, bre_ref[...], nt_dims, preferred_element_type=F32)
                       + lax.dot_general(gib, bim_ref[...], nt_dims, preferred_element_type=F32))

    col, vec, bspec, cspec, aspec = _ssm_specs(t, nb, ch, st)
    sh = jax.ShapeDtypeStruct
    return pl.pallas_call(
        body, name=name, grid=(nb,),
        in_specs=[col, col, col, bspec, bspec, cspec, cspec, aspec, aspec, vec],
        out_specs=[col, vec, bspec, bspec, cspec, cspec, aspec, aspec],
        out_shape=[sh((t, d), F32), sh((1, d), F32), sh((nb, ch, st), F32), sh((nb, ch, st), F32),
                   sh((nb, st, ch), F32), sh((nb, st, ch), F32), sh((nb, 1, st), F32), sh((nb, 1, st), F32)],
        scratch_shapes=[pltpu.VMEM((t + HALO, st), F32), pltpu.VMEM((t + HALO, st), F32),
                        pltpu.VMEM((t, st), F32), pltpu.VMEM((t, st), F32)],
        compiler_params=_params("parallel"),
    )(u, y, dyg, bre, bim, cre, cim, a_re, a_im, dskip)


def glu_fwd(yg, w_glu, b_glu, x, name, lead=None):
    t, d = yg.shape
    tn = _tile(d, 256)
    nd = d // tn

    def body(yg_ref, wv_ref, wg_ref, bv_ref, bg_ref, x_ref, val_ref, gate_ref, o_ref):
        ygv = yg_ref[...]
        vb = (jnp.dot(ygv, wv_ref[...], preferred_element_type=F32) + bv_ref[...]).astype(BF16)
        gb = (jnp.dot(ygv, wg_ref[...], preferred_element_type=F32) + bg_ref[...]).astype(BF16)
        val_ref[...] = vb
        gate_ref[...] = gb
        o_ref[...] = x_ref[...] + vb.astype(F32) * _sigmoid(gb.astype(F32))

    col = lambda off: _spec((d, tn), lambda j: (0, j + off), lead)
    vec = lambda off: pl.BlockSpec((1, tn), lambda j: (0, j + off))
    tile = pl.BlockSpec((t, tn), lambda j: (0, j))
    return pl.pallas_call(
        body, name=name, grid=(nd,),
        in_specs=[pl.BlockSpec((t, d), lambda j: (0, 0)), col(0), col(nd), vec(0), vec(nd), tile],
        out_specs=[tile, tile, tile],
        out_shape=[jax.ShapeDtypeStruct((t, d), BF16), jax.ShapeDtypeStruct((t, d), BF16),
                   jax.ShapeDtypeStruct((t, d), F32)],
        compiler_params=_params("parallel"),
    )(yg, w_glu, w_glu, b_glu, b_glu, x)


def glu_bwd(dm, val, gate, name):
    t, d = dm.shape
    tn = _tile(d, 256)

    def body(dm_ref, val_ref, gate_ref, dv_ref, dg_ref, dbv_ref, dbg_ref):
        dmv = dm_ref[...]
        s = _sigmoid(gate_ref[...].astype(F32))
        dval = dmv * s
        dgate = dmv * val_ref[...].astype(F32) * s * (1.0 - s)
        dv_ref[...] = dval.astype(BF16)
        dg_ref[...] = dgate.astype(BF16)
        dbv_ref[...] = jnp.sum(dval, axis=0, keepdims=True)
        dbg_ref[...] = jnp.sum(dgate, axis=0, keepdims=True)

    tile = pl.BlockSpec((t, tn), lambda j: (0, j))
    vec = pl.BlockSpec((1, tn), lambda j: (0, j))
    return pl.pallas_call(
        body, name=name, grid=(d // tn,),
        in_specs=[tile, tile, tile], out_specs=[tile, tile, vec, vec],
        out_shape=[jax.ShapeDtypeStruct((t, d), BF16), jax.ShapeDtypeStruct((t, d), BF16),
                   jax.ShapeDtypeStruct((1, d), F32), jax.ShapeDtypeStruct((1, d), F32)],
        compiler_params=_params("parallel"),
    )(dm, val, gate)


def _block_diag(m, gb):
    g, a, b = m.shape
    eye = jnp.eye(gb, dtype=m.dtype)
    return jnp.einsum("ngab,gk->ngakb", m.reshape(g // gb, gb, a, b), eye).reshape(g // gb, gb * a, gb * b)


def ssm_prepare(lam_re, lam_im, log_step, b_re, b_im, c_re, c_im):
    gb = SSM_BLOCK_GROUPS
    g, p = lam_re.shape
    step = jnp.exp(log_step)[:, None]
    mag = jnp.exp(lam_re * step)
    lb_re = mag * jnp.cos(lam_im * step)
    lb_im = mag * jnp.sin(lam_im * step)
    den = lam_re * lam_re + lam_im * lam_im
    f_re = ((lb_re - 1.0) * lam_re + lb_im * lam_im) / den
    f_im = (lb_im * lam_re - (lb_re - 1.0) * lam_im) / den
    bb_re = f_re[..., None] * b_re - f_im[..., None] * b_im
    bb_im = f_re[..., None] * b_im + f_im[..., None] * b_re
    tr = lambda m: jnp.transpose(m, (0, 2, 1))
    return (_block_diag(tr(bb_re), gb), _block_diag(tr(bb_im), gb), _block_diag(tr(c_re), gb), _block_diag(tr(c_im), gb),
            lb_re.reshape(g // gb, 1, gb * p), lb_im.reshape(g // gb, 1, gb * p))


EW_BLOCK_BYTES = 2 * 1024 * 1024
BF16_ROWS = 16


def _row_tile(rows, cols):
    limit = max(BF16_ROWS, EW_BLOCK_BYTES // (cols * 4))
    best = None
    for tr in range(BF16_ROWS, min(rows, limit) + 1, BF16_ROWS):
        if rows % tr == 0:
            best = tr
    return best if best is not None else rows


def _as2d(a):
    return a.reshape(-1, a.shape[-1])


def ew(fn, ins, out_dtypes, name):
    rows, cols = ins[0].shape
    tr = _row_tile(rows, cols)
    n_in = len(ins)

    def body(*refs):
        outs = fn(*[r[...] for r in refs[:n_in]])
        for o_ref, v in zip(refs[n_in:], outs):
            o_ref[...] = v.astype(o_ref.dtype)

    spec = pl.BlockSpec((tr, cols), lambda i: (i, 0))
    return pl.pallas_call(
        body, name=name, grid=(rows // tr,), in_specs=[spec] * n_in, out_specs=[spec] * len(out_dtypes),
        out_shape=[jax.ShapeDtypeStruct((rows, cols), dt) for dt in out_dtypes],
        compiler_params=_params("parallel"),
    )(*ins)


def cast_bf16(a, name):
    return ew(lambda v: (v,), [_as2d(a)], [BF16], name)[0].reshape(a.shape)


def add_pair(a, b, name):
    out = ew(lambda u, v: (u.astype(F32) + v.astype(F32),), [_as2d(a), _as2d(b)], [BF16], name)[0]
    return out.reshape(a.shape)


def add_chips(lb, name):
    shp = lb.shape[1:]
    v = lb.reshape(N_CHIPS, -1, shp[-1])
    rows, cols = v.shape[1:]
    tr = _row_tile(rows, cols)

    def body(a0, a1, a2, a3, o_ref):
        o_ref[...] = ((a0[...].astype(F32) + a1[...].astype(F32)) + a2[...].astype(F32)) + a3[...].astype(F32)

    spec = lambda k: pl.BlockSpec((None, tr, cols), lambda i: (k, i, 0))
    out = pl.pallas_call(
        body, name=name, grid=(rows // tr,), in_specs=[spec(k) for k in range(N_CHIPS)],
        out_specs=pl.BlockSpec((tr, cols), lambda i: (i, 0)),
        out_shape=jax.ShapeDtypeStruct((rows, cols), F32),
        compiler_params=_params("parallel"),
    )(v, v, v, v)
    return out.reshape(shp)


def _adamw(w, g, m, v):
    m = ADAM_B1 * m + (1.0 - ADAM_B1) * g
    v = ADAM_B2 * v + (1.0 - ADAM_B2) * (g * g)
    m_hat = m / (1.0 - ADAM_B1 ** ADAM_STEP)
    v_hat = v / (1.0 - ADAM_B2 ** ADAM_STEP)
    delta = -ADAM_LR * (m_hat / (jnp.sqrt(v_hat) + ADAM_EPS) + ADAM_WD * w)
    return delta, m, v


def adamw(w, g, m, v, name):
    outs = ew(_adamw, [_as2d(w), _as2d(g), _as2d(m), _as2d(v)], [F32, F32, F32], name)
    return [o.reshape(w.shape) for o in outs]


def loss_head(y, target, name):
    t, d = y.shape
    tr = min(t, 128)
    n = t // tr

    def body(y_ref, t_ref, dy_ref, dyb_ref, loss_ref, acc):
        i = pl.program_id(0)
        err = y_ref[...] - t_ref[...]
        dy = err * (1.0 / d)
        dy_ref[...] = dy
        dyb_ref[...] = dy.astype(BF16)
        part = jnp.sum(err * err, axis=0, keepdims=True)

        @pl.when(i == 0)
        def _():
            acc[...] = part

        @pl.when(i != 0)
        def _():
            acc[...] += part

        @pl.when(i == n - 1)
        def _():
            loss_ref[...] = jnp.full((1, LANES), 0.5 / d, F32) * jnp.sum(acc[...])

    row = pl.BlockSpec((tr, d), lambda i: (i, 0))
    return pl.pallas_call(
        body, name=name, grid=(n,), in_specs=[row, row],
        out_specs=[row, row, pl.BlockSpec((1, LANES), lambda i: (0, 0))],
        out_shape=[jax.ShapeDtypeStruct((t, d), F32), jax.ShapeDtypeStruct((t, d), BF16),
                   jax.ShapeDtypeStruct((1, LANES), F32)],
        scratch_shapes=[pltpu.VMEM((1, d), F32)],
        compiler_params=_params("arbitrary"),
    )(y, target)


HBM_SPEC = pl.BlockSpec(memory_space=pltpu.HBM)
VMEM_SPEC = pl.BlockSpec(memory_space=pltpu.VMEM)


def _place():
    return lax.axis_index("x"), lax.axis_index("y"), lax.axis_index("c")


def _other_chips(x, y):
    return [(1 - x, y), (x, 1 - y), (1 - x, 1 - y)]


def _remote(src, dst, send_sem, recv_sem, dev):
    return pltpu.make_async_remote_copy(src_ref=src, dst_ref=dst, send_sem=send_sem, recv_sem=recv_sem,
                                        device_id=dev, device_id_type=MESH)


def _piece(refs, shard_shape, ax, j, half):
    w = shard_shape[ax]
    a, off = divmod(j * w, refs[0].shape[ax]) if isinstance(j, int) else (0, j * w)
    idx = [pl.ds(0, s) for s in shard_shape]
    idx[ax] = pl.ds(off, w)
    if half is not None:
        h0 = shard_shape[0] // 2
        idx[0] = pl.ds((off if ax == 0 else 0) + half * h0, h0)
    return refs[a].at[tuple(idx)]


def small_allreduce(v, name):
    n, r, l = v.shape
    assert n == N_DEV

    def body(v_ref, o_ref, recv, red, send1, recv1, send2, recv2):
        x, y, c = _place()
        me = 4 * x + 2 * y + c
        dev = lambda k: (k // 4, (k // 2) % 2, k % 2)
        firsts = []
        for o in range(1, N_DEV):
            tgt = (me + o) % N_DEV
            cp = _remote(v_ref.at[tgt], recv.at[me], send1.at[o], recv1.at[me], dev(tgt))
            cp.start()
            firsts.append(cp)
        recv[me] = v_ref[me]
        for o in range(1, N_DEV):
            src = (me + o) % N_DEV
            _remote(v_ref.at[src], recv.at[src], send1.at[o], recv1.at[src], dev(src)).wait_recv()
        acc = recv[0]
        for s in range(1, N_DEV):
            acc = acc + recv[s]
        red[...] = acc
        o_ref[me] = acc
        seconds = []
        for o in range(1, N_DEV):
            tgt = (me + o) % N_DEV
            cp = _remote(red, o_ref.at[me], send2.at[o], recv2.at[me], dev(tgt))
            cp.start()
            seconds.append(cp)
        for o in range(1, N_DEV):
            src = (me + o) % N_DEV
            _remote(red, o_ref.at[src], send2.at[o], recv2.at[src], dev(src)).wait_recv()
        for cp in firsts + seconds:
            cp.wait_send()

    sems = pltpu.SemaphoreType.DMA((N_DEV,))
    return pl.pallas_call(
        body, name=name, in_specs=[VMEM_SPEC], out_specs=VMEM_SPEC,
        out_shape=jax.ShapeDtypeStruct(v.shape, F32),
        scratch_shapes=[pltpu.VMEM((N_DEV, r, l), F32), pltpu.VMEM((r, l), F32), sems, sems, sems, sems],
        compiler_params=pltpu.CompilerParams(vmem_limit_bytes=VMEM_LIMIT),
    )(v)


def gather_weights(shards, axes, name):
    n = len(shards)
    items = [(p, l) for p in range(n) for l in range(shards[p].shape[0])]
    shapes = [tuple(s.shape[1:]) for s in shards]
    out_shape = []
    for s, ax in zip(shards, axes):
        shp = list(s.shape)
        shp[1 + ax] *= N_CHIPS
        out_shape.append(jax.ShapeDtypeStruct(tuple(shp), s.dtype))

    def body(*refs):
        s_refs, o_refs = refs[:n], refs[n:2 * n]
        local_sem, send_ici, recv_ici, send_d2d, recv_d2d = refs[2 * n:]
        x, y, c = _place()
        me = 2 * x + y
        chips = _other_chips(x, y)
        sibling = (x, y, 1 - c)
        dst = lambda p, l, j, half: _piece([o_refs[p].at[l]], shapes[p], axes[p], j, half)
        h0 = lambda p: shapes[p][0] // 2

        local, sent = [], []
        for k, (p, l) in enumerate(items):
            cp = pltpu.make_async_copy(s_refs[p].at[l], dst(p, l, me, None), local_sem.at[k])
            cp.start()
            local.append(cp)
        for k, (p, l) in enumerate(items):
            for q, chip in enumerate(chips):
                cp = _remote(s_refs[p].at[l, pl.ds(c * h0(p), h0(p))], dst(p, l, me, c),
                             send_ici.at[3 * k + q], recv_ici.at[3 * k + q], (chip[0], chip[1], c))
                cp.start()
                sent.append(cp)
        for k, (p, l) in enumerate(items):
            for q, chip in enumerate(chips):
                got = dst(p, l, 2 * chip[0] + chip[1], c)
                _remote(got, got, send_ici.at[3 * k + q], recv_ici.at[3 * k + q], (chip[0], chip[1], c)).wait_recv()
                cp = _remote(got, got, send_d2d.at[3 * k + q], recv_d2d.at[3 * k + q], sibling)
                cp.start()
                sent.append(cp)
        for k, (p, l) in enumerate(items):
            for q, chip in enumerate(chips):
                got = dst(p, l, 2 * chip[0] + chip[1], 1 - c)
                _remote(got, got, send_d2d.at[3 * k + q], recv_d2d.at[3 * k + q], sibling).wait_recv()
        for cp in sent:
            cp.wait_send()
        for cp in local:
            cp.wait()

    sems = pltpu.SemaphoreType.DMA((3 * len(items),))
    return pl.pallas_call(
        body, name=name, in_specs=[HBM_SPEC] * n, out_specs=[HBM_SPEC] * n, out_shape=out_shape,
        scratch_shapes=[pltpu.SemaphoreType.DMA((len(items),)), sems, sems, sems, sems],
    )(*shards)


def exchange_core_halves(grads, shapes, axes, name):
    n = len(grads)
    flat = [a for g in grads for a in g]
    starts = [sum(len(g) for g in grads[:k]) for k in range(n)]
    half_shape = lambda k: (N_CHIPS, shapes[k][0] // 2) + tuple(shapes[k][1:])
    out_shape = [jax.ShapeDtypeStruct(half_shape(k), BF16) for k in range(n)] * 2

    def body(*refs):
        g_refs = [refs[starts[k]:starts[k] + len(grads[k])] for k in range(n)]
        own = refs[len(flat):len(flat) + n]
        got = refs[len(flat) + n:len(flat) + 2 * n]
        local_sem, send_sem, recv_sem = refs[len(flat) + 2 * n:]
        x, y, c = _place()
        sibling = (x, y, 1 - c)
        local, sent = [], []
        for k in range(n):
            for j in range(N_CHIPS):
                s = N_CHIPS * k + j
                cp = pltpu.make_async_copy(_piece(g_refs[k], shapes[k], axes[k], j, c), own[k].at[j], local_sem.at[s])
                cp.start()
                local.append(cp)
                cp = _remote(_piece(g_refs[k], shapes[k], axes[k], j, 1 - c), got[k].at[j],
                             send_sem.at[s], recv_sem.at[s], sibling)
                cp.start()
                sent.append(cp)
        for k in range(n):
            for j in range(N_CHIPS):
                s = N_CHIPS * k + j
                _remote(got[k].at[j], got[k].at[j], send_sem.at[s], recv_sem.at[s], sibling).wait_recv()
        for cp in sent:
            cp.wait_send()
        for cp in local:
            cp.wait()

    sems = pltpu.SemaphoreType.DMA((N_CHIPS * n,))
    outs = pl.pallas_call(
        body, name=name, in_specs=[HBM_SPEC] * len(flat), out_specs=[HBM_SPEC] * (2 * n), out_shape=out_shape,
        scratch_shapes=[sems, sems, sems],
    )(*flat)
    return outs[:n], outs[n:]


def exchange_chip_shards(halves, name):
    n = len(halves)

    def body(*refs):
        h_refs, lb = refs[:n], refs[n:2 * n]
        local_sem, send_sem, recv_sem = refs[2 * n:]
        x, y, c = _place()
        me = 2 * x + y
        chips = _other_chips(x, y)
        local, sent = [], []
        for k in range(n):
            cp = pltpu.make_async_copy(h_refs[k].at[me], lb[k].at[me], local_sem.at[k])
            cp.start()
            local.append(cp)
            for q, chip in enumerate(chips):
                cp = _remote(h_refs[k].at[2 * chip[0] + chip[1]], lb[k].at[me],
                             send_sem.at[3 * k + q], recv_sem.at[3 * k + q], (chip[0], chip[1], c))
                cp.start()
                sent.append(cp)
        for k in range(n):
            for q, chip in enumerate(chips):
                got = lb[k].at[2 * chip[0] + chip[1]]
                _remote(got, got, send_sem.at[3 * k + q], recv_sem.at[3 * k + q], (chip[0], chip[1], c)).wait_recv()
        for cp in sent:
            cp.wait_send()
        for cp in local:
            cp.wait()

    sems = pltpu.SemaphoreType.DMA((3 * n,))
    return pl.pallas_call(
        body, name=name, in_specs=[HBM_SPEC] * n, out_specs=[HBM_SPEC] * n,
        out_shape=[jax.ShapeDtypeStruct(h.shape, h.dtype) for h in halves],
        scratch_shapes=[pltpu.SemaphoreType.DMA((n,)), sems, sems],
    )(*halves)


def share_reduced_halves(reduced, items, stacked_shapes, name):
    n = len(reduced)

    def body(*refs):
        r_refs, outs = refs[:n], refs[n:n + len(stacked_shapes)]
        local_sem, send_sem, recv_sem = refs[n + len(stacked_shapes):]
        x, y, c = _place()
        sibling = (x, y, 1 - c)
        local, sent = [], []

        def rows(k, half):
            p, l = items[k]
            h0 = stacked_shapes[p][1] // 2
            return outs[p].at[l, pl.ds(half * h0, h0)]

        for k in range(n):
            cp = pltpu.make_async_copy(r_refs[k], rows(k, c), local_sem.at[k])
            cp.start()
            local.append(cp)
            cp = _remote(r_refs[k], rows(k, c), send_sem.at[k], recv_sem.at[k], sibling)
            cp.start()
            sent.append(cp)
        for k in range(n):
            got = rows(k, 1 - c)
            _remote(got, got, send_sem.at[k], recv_sem.at[k], sibling).wait_recv()
        for cp in sent:
            cp.wait_send()
        for cp in local:
            cp.wait()

    sems = pltpu.SemaphoreType.DMA((n,))
    return pl.pallas_call(
        body, name=name, in_specs=[HBM_SPEC] * n, out_specs=[HBM_SPEC] * len(stacked_shapes),
        out_shape=[jax.ShapeDtypeStruct(s, F32) for s in stacked_shapes],
        scratch_shapes=[sems, sems, sems],
    )(*reduced)


WEIGHTS = ["norm_mix_g", "norm_ffn_g", "pool_w", "pool_b", "pool_scale", "sb_w_qkv", "sb_q_gain", "sb_k_gain",
           "sb_w_o", "ssm_lam_re", "ssm_lam_im", "ssm_log_step", "ssm_b_re", "ssm_b_im", "ssm_c_re", "ssm_c_im",
           "ssm_d", "ssm_w_glu", "ssm_b_glu", "ffn_w_up", "ffn_conv_w", "ffn_conv_b", "ffn_w_down"]
BIG = {"pool_w": 1, "sb_w_qkv": 1, "sb_w_o": 0, "ssm_w_glu": 1, "ffn_w_up": 1, "ffn_w_down": 0}
SMALL_SHARDED = {"pool_b": 1, "pool_scale": 1, "ssm_d": 1, "ssm_b_glu": 1, "ffn_conv_w": 2}
SMALL = [n for n in WEIGHTS if n not in BIG]
SMALL_PAD = N_DEV * SUBLANES * LANES
N_MIXERS = 3


def _pack(arrays):
    flat = jnp.concatenate([a.reshape(-1).astype(F32) for a in arrays])
    total = -(-flat.shape[0] // SMALL_PAD) * SMALL_PAD
    flat = jnp.pad(flat, (0, total - flat.shape[0]))
    return flat.reshape(N_DEV, -1, LANES)


def _unpack(packed, like):
    flat = packed.reshape(-1)
    out, off = [], 0
    for a in like:
        out.append(flat[off:off + a.size].reshape(a.shape))
        off += a.size
    return out


def kernel(x, norm_mix_g, norm_ffn_g, pool_w, pool_b, pool_scale, sb_w_qkv, sb_q_gain, sb_k_gain, sb_w_o, ssm_lam_re, ssm_lam_im, ssm_log_step, ssm_b_re, ssm_b_im, ssm_c_re, ssm_c_im, ssm_d, ssm_w_glu, ssm_b_glu, ffn_w_up, ffn_conv_w, ffn_conv_b, ffn_w_down, loss_target, m_norm_mix_g, m_norm_ffn_g, m_pool_w, m_pool_b, m_pool_scale, m_sb_w_qkv, m_sb_q_gain, m_sb_k_gain, m_sb_w_o, m_ssm_lam_re, m_ssm_lam_im, m_ssm_log_step, m_ssm_b_re, m_ssm_b_im, m_ssm_c_re, m_ssm_c_im, m_ssm_d, m_ssm_w_glu, m_ssm_b_glu, m_ffn_w_up, m_ffn_conv_w, m_ffn_conv_b, m_ffn_w_down, v_norm_mix_g, v_norm_ffn_g, v_pool_w, v_pool_b, v_pool_scale, v_sb_w_qkv, v_sb_q_gain, v_sb_k_gain, v_sb_w_o, v_ssm_lam_re, v_ssm_lam_im, v_ssm_log_step, v_ssm_b_re, v_ssm_b_im, v_ssm_c_re, v_ssm_c_im, v_ssm_d, v_ssm_w_glu, v_ssm_b_glu, v_ffn_w_up, v_ffn_conv_w, v_ffn_conv_b, v_ffn_w_down):
    given = dict(locals())
    w = {n: given[n] for n in WEIGHTS}
    mom = {n: given["m_" + n] for n in WEIGHTS}
    var = {n: given["v_" + n] for n in WEIGHTS}
    xi, yi, ci = _place()
    me = 2 * xi + yi
    depth = norm_mix_g.shape[0]
    x_in = x[0]
    t, d = x_in.shape

    def placed(a, ax):
        shp = list(a.shape)
        shp[ax] *= N_CHIPS
        full = lax.dynamic_update_slice_in_dim(jnp.zeros(shp, F32), a, me * a.shape[ax], ax)
        return jnp.where(ci == 0, full, 0.0)

    sharded_full = [placed(w[n], ax) for n, ax in SMALL_SHARDED.items()]
    whole = dict(zip(SMALL_SHARDED, _unpack(small_allreduce(_pack(sharded_full), "gather_vectors"), sharded_full)))

    big = list(BIG)
    gathered = dict(zip(big, gather_weights([cast_bf16(w[n], "cast_" + n) for n in big], [BIG[n] for n in big],
                                            "gather_weights")))
    vec = lambda a, i: a[i:i + 1]

    saved = []
    xc = x_in
    for i in range(depth):
        kind, j = i % N_MIXERS, i // N_MIXERS
        s = {"x_in": xc}
        g_mix = vec(norm_mix_g, i)
        if kind == 0:
            (h,) = rmsnorm_fwd(xc, g_mix, [F32], f"norm_mix{i}")
            x_mid = pool_fwd(h, xc, gathered["pool_w"], vec(whole["pool_b"], j), vec(whole["pool_scale"], j),
                             f"pool_fwd{i}", lead=j)
        elif kind == 1:
            (h,) = rmsnorm_fwd(xc, g_mix, [BF16], f"norm_mix{i}")
            s["qkv"] = mm_cols(h, gathered["sb_w_qkv"], out_dtype=BF16, name=f"sb_qkv{i}", lead=j)
            s["o"], s["ltot"] = sb_attn_fwd(s["qkv"], vec(sb_q_gain, j), vec(sb_k_gain, j), f"sb_attn_fwd{i}")
            x_mid = mm_cols(s["o"], gathered["sb_w_o"], out_dtype=F32, name=f"sb_out{i}", resid=xc, lead=j)
        else:
            (h,) = rmsnorm_fwd(xc, g_mix, [F32], f"norm_mix{i}")
            prm = tuple(w[n][j] for n in ("ssm_lam_re", "ssm_lam_im", "ssm_log_step", "ssm_b_re", "ssm_b_im",
                                          "ssm_c_re", "ssm_c_im"))
            prep, s["prep_vjp"] = jax.vjp(ssm_prepare, *prm)
            s["prep"] = tuple(a.astype(BF16) for a in prep[:4]) + tuple(prep[4:])
            s["y"], s["yg"] = ssm_core_fwd(h, *s["prep"], vec(whole["ssm_d"], j), f"ssm_fwd{i}")
            s["val"], s["gate"], x_mid = glu_fwd(s["yg"], gathered["ssm_w_glu"], vec(whole["ssm_b_glu"], j), xc,
                                                 f"ssm_glu{i}", lead=j)
        s["x_mid"] = x_mid
        (h2,) = rmsnorm_fwd(x_mid, vec(norm_ffn_g, i), [BF16], f"norm_ffn{i}")
        s["up_val"], s["up_gate"], s["act"] = ffn_up_fused(h2, gathered["ffn_w_up"], whole["ffn_conv_w"][i],
                                                           vec(ffn_conv_b, i), f"ffn_up{i}", lead=i)
        xc = mm_k([s["act"]], gathered["ffn_w_down"], b_nt=False, name=f"ffn_down{i}", resid=x_mid, lead=i)
        saved.append(s)

    dx, dxb, loss_part = loss_head(xc, loss_target[0], "loss_head")
    loss = lax.psum(loss_part[0, 0], ("x", "y", "c"))

    small = {n: [None] * w[n].shape[0] for n in SMALL}
    big_g = {}
    for i in reversed(range(depth)):
        kind, j = i % N_MIXERS, i // N_MIXERS
        s = saved[i]
        g_ffn, g_mix = vec(norm_ffn_g, i), vec(norm_mix_g, i)
        cw, cb = whole["ffn_conv_w"][i], vec(ffn_conv_b, i)
        (h2,) = rmsnorm_fwd(s["x_mid"], g_ffn, [BF16], f"norm_ffn_re{i}")
        dupv, dupg, dcwv, dcwg, dcbv, dcbg = ffn_bwd_fused(dxb, gathered["ffn_w_down"], s["up_val"], s["up_gate"], cw, cb,
                                                           f"ffn_bwd{i}", lead=i)
        big_g["ffn_w_down", i] = [mm_rows(s["act"], dxb, out_dtype=BF16, name=f"ffn_dwdown{i}")]
        big_g["ffn_w_up", i] = [mm_cols(h2, dupv, a_contract=0, out_dtype=BF16, name=f"ffn_dwup_val{i}"),
                                mm_cols(h2, dupg, a_contract=0, out_dtype=BF16, name=f"ffn_dwup_gate{i}")]
        dh2 = mm_k([dupv, dupg], gathered["ffn_w_up"], b_nt=True, name=f"ffn_dh{i}", lead=i)
        dx_mid, dxb_mid, small["norm_ffn_g"][i] = rmsnorm_bwd(s["x_mid"], g_ffn, dh2, dx, f"norm_ffn_bwd{i}")
        small["ffn_conv_w"][i] = jnp.concatenate([dcwv, dcwg], axis=1)[None]
        small["ffn_conv_b"][i] = jnp.concatenate([dcbv, dcbg], axis=1)

        if kind == 0:
            (h,) = rmsnorm_fwd(s["x_in"], g_mix, [F32], f"norm_mix_re{i}")
            dh, dwp, small["pool_b"][j], small["pool_scale"][j] = pool_bwd(
                dx_mid, h, gathered["pool_w"], vec(whole["pool_b"], j), vec(whole["pool_scale"], j), f"pool_bwd{i}", lead=j)
            big_g["pool_w", j] = [dwp]
        elif kind == 1:
            (h,) = rmsnorm_fwd(s["x_in"], g_mix, [BF16], f"norm_mix_re{i}")
            do = mm_cols(dxb_mid, gathered["sb_w_o"], b_nt=True, out_dtype=BF16, name=f"sb_do{i}", lead=j)
            big_g["sb_w_o", j] = [mm_cols(s["o"], dxb_mid, a_contract=0, out_dtype=BF16, name=f"sb_dwo{i}")]
            dq, dk, dv, small["sb_q_gain"][j], small["sb_k_gain"][j] = sb_attn_bwd(
                s["qkv"], s["ltot"], do, vec(sb_q_gain, j), vec(sb_k_gain, j), f"sb_attn_bwd{i}")
            dqkv = jnp.concatenate([dq, dk, dv], axis=1)
            big_g["sb_w_qkv", j] = [mm_cols(h, dqkv, a_contract=0, out_dtype=BF16, name=f"sb_dwqkv{i}")]
            dh = mm_k([dqkv], gathered["sb_w_qkv"], b_nt=True, name=f"sb_dh{i}", lead=j)
        else:
            (h,) = rmsnorm_fwd(s["x_in"], g_mix, [F32], f"norm_mix_re{i}")
            dval, dgate, dbv, dbg = glu_bwd(dx_mid, s["val"], s["gate"], f"ssm_glu_bwd{i}")
            small["ssm_b_glu"][j] = jnp.concatenate([dbv, dbg], axis=1)
            big_g["ssm_w_glu", j] = [mm_cols(s["yg"], dval, a_contract=0, out_dtype=BF16, name=f"ssm_dwglu_val{i}"),
                                     mm_cols(s["yg"], dgate, a_contract=0, out_dtype=BF16, name=f"ssm_dwglu_gate{i}")]
            dyg = mm_k([dval, dgate], gathered["ssm_w_glu"], b_nt=True, name=f"ssm_dyg{i}", lead=j)
            dh, small["ssm_d"][j], *dprep = ssm_core_bwd(h, s["y"], dyg, *s["prep"], vec(whole["ssm_d"], j), f"ssm_bwd{i}")
            dprm = s["prep_vjp"](tuple(dprep))
            for n, g in zip(("ssm_lam_re", "ssm_lam_im", "ssm_log_step", "ssm_b_re", "ssm_b_im", "ssm_c_re", "ssm_c_im"),
                            dprm):
                small[n][j] = g[None]
        dx, dxb, small["norm_mix_g"][i] = rmsnorm_bwd(s["x_in"], g_mix, dh, dx_mid, f"norm_mix_bwd{i}")

    small_full = [jnp.concatenate(small[n], axis=0) for n in SMALL]
    small_sum = dict(zip(SMALL, _unpack(small_allreduce(_pack(small_full), "reduce_vectors"), small_full)))
    grads = {}
    for n in SMALL:
        g = small_sum[n]
        if n in SMALL_SHARDED:
            ax = SMALL_SHARDED[n]
            g = lax.dynamic_slice_in_dim(g, me * w[n].shape[ax], w[n].shape[ax], ax)
        grads[n] = g

    items = [(p, l) for p, n in enumerate(big) for l in range(w[n].shape[0])]
    shapes = [tuple(w[big[p]].shape[1:]) for p, _ in items]
    axes = [BIG[big[p]] for p, _ in items]
    own, got = exchange_core_halves([big_g[big[p], l] for p, l in items], shapes, axes, "grads_core_exchange")
    chip_sum = [add_pair(a, b, f"grads_core_add{k}") for k, (a, b) in enumerate(zip(own, got))]
    landed = exchange_chip_shards(chip_sum, "grads_chip_exchange")
    reduced = [add_chips(lb, f"grads_chip_add{k}") for k, lb in enumerate(landed)]
    for n, g in zip(big, share_reduced_halves(reduced, items, [tuple(w[n].shape) for n in big], "grads_core_share")):
        grads[n] = g

    delta, new_m, new_v = {}, {}, {}
    for n in big:
        delta[n], new_m[n], new_v[n] = adamw(w[n], grads[n], mom[n], var[n], "adamw_" + n)
    like = [w[n] for n in SMALL]
    packed = [_pack([src[n] for n in SMALL]).reshape(-1, LANES) for src in (w, grads, mom, var)]
    for dst, out in zip((delta, new_m, new_v), adamw(*packed, "adamw_vectors")):
        dst.update(zip(SMALL, _unpack(out, like)))

    return (loss, dx[None], *[grads[n] for n in WEIGHTS], *[delta[n] for n in WEIGHTS],
            *[new_m[n] for n in WEIGHTS], *[new_v[n] for n in WEIGHTS])
```

```python
import functools
import math

import jax
import jax.numpy as jnp
from jax import lax
from jax.experimental import pallas as pl
from jax.experimental.pallas import tpu as pltpu

F32 = jnp.float32
BF16 = jnp.bfloat16

RMS_EPS = 1e-6
POOL_WINDOWS = (2, 4, 8, 16)
SB_HEAD_DIM = 128
SSM_GROUP_CH = 16
SSM_STATE = 64
SSM_BLOCK_GROUPS = 8
ADAM_LR = 0.001
ADAM_B1 = 0.9
ADAM_B2 = 0.999
ADAM_EPS = 1e-08
ADAM_WD = 0.01
ADAM_STEP = 10

V7X_VMEM_BYTES = 64 * 1024 * 1024
VMEM_LIMIT = V7X_VMEM_BYTES - 8 * 1024 * 1024
SUBLANES = 8
LANES = 128
MESH = pl.DeviceIdType.MESH
N_CHIPS = 4
N_DEV = 8


def _params(*sem):
    return pltpu.CompilerParams(dimension_semantics=tuple(sem) if sem else None, vmem_limit_bytes=VMEM_LIMIT)


def _tile(n, want):
    if n <= want:
        return n
    t = (want // LANES) * LANES
    while t > LANES and n % t:
        t -= LANES
    assert n % t == 0, (n, want)
    return t


def _spec(shape, imap, lead=None):
    if lead is None:
        return pl.BlockSpec(tuple(shape), imap)
    return pl.BlockSpec((None,) + tuple(shape), lambda *a: (lead,) + tuple(imap(*a)))


def _sigmoid(v):
    return 1.0 / (1.0 + jnp.exp(-v))


def _shift_down(v, k):
    return pltpu.roll(v, k, 0)


def _shift_up(v, k):
    return pltpu.roll(v, v.shape[0] - k, 0)


def rmsnorm_fwd(x, g, out_dtypes, name):
    t, d = x.shape
    tr = min(t, 128)

    def body(x_ref, g_ref, *o_refs):
        xv = x_ref[...]
        r = lax.rsqrt(jnp.mean(xv * xv, axis=-1, keepdims=True) + RMS_EPS)
        h = xv * r * g_ref[...]
        for o in o_refs:
            o[...] = h.astype(o.dtype)

    outs = pl.pallas_call(
        body, name=name, grid=(t // tr,),
        in_specs=[pl.BlockSpec((tr, d), lambda i: (i, 0)), pl.BlockSpec((1, d), lambda i: (0, 0))],
        out_specs=[pl.BlockSpec((tr, d), lambda i: (i, 0)) for _ in out_dtypes],
        out_shape=[jax.ShapeDtypeStruct((t, d), dt) for dt in out_dtypes],
        compiler_params=_params("parallel"),
    )(x, g)
    return outs


def rmsnorm_bwd(x, g, dh, dres, name):
    t, d = x.shape
    tr = min(t, 128)

    def body(x_ref, g_ref, dh_ref, dres_ref, dx_ref, dxb_ref, dg_ref):
        xv = x_ref[...]
        r = lax.rsqrt(jnp.mean(xv * xv, axis=-1, keepdims=True) + RMS_EPS)
        xhat = xv * r
        dhv = dh_ref[...]
        dxhat = dhv * g_ref[...]
        dx = dres_ref[...] + r * (dxhat - xhat * jnp.mean(dxhat * xhat, axis=-1, keepdims=True))
        dx_ref[...] = dx
        dxb_ref[...] = dx.astype(BF16)
        part = jnp.sum(dhv * xhat, axis=0, keepdims=True)

        @pl.when(pl.program_id(0) == 0)
        def _():
            dg_ref[...] = part

        @pl.when(pl.program_id(0) != 0)
        def _():
            dg_ref[...] += part

    row = pl.BlockSpec((tr, d), lambda i: (i, 0))
    vec = pl.BlockSpec((1, d), lambda i: (0, 0))
    return pl.pallas_call(
        body, name=name, grid=(t // tr,),
        in_specs=[row, vec, row, row],
        out_specs=[row, row, vec],
        out_shape=[jax.ShapeDtypeStruct((t, d), F32), jax.ShapeDtypeStruct((t, d), BF16),
                   jax.ShapeDtypeStruct((1, d), F32)],
        compiler_params=_params("arbitrary"),
    )(x, g, dh, dres)


def mm_cols(a, b, *, a_contract=1, b_nt=False, out_dtype, name, resid=None, tn=512, lead=None):
    m = a.shape[1 - a_contract]
    k = a.shape[a_contract]
    n = b.shape[-2] if b_nt else b.shape[-1]
    assert (b.shape[-1] if b_nt else b.shape[-2]) == k
    tn = _tile(n, tn)

    def body(a_ref, b_ref, *rest):
        o_ref = rest[-1]
        dn = (((a_contract,), (1 if b_nt else 0,)), ((), ()))
        acc = lax.dot_general(a_ref[...], b_ref[...], dn, preferred_element_type=F32)
        if resid is not None:
            acc = acc + rest[0][...]
        o_ref[...] = acc.astype(o_ref.dtype)

    in_specs = [pl.BlockSpec(a.shape, lambda j: (0, 0)),
                _spec((tn, k), lambda j: (j, 0), lead) if b_nt else _spec((k, tn), lambda j: (0, j), lead)]
    args = [a, b]
    if resid is not None:
        in_specs.append(pl.BlockSpec((m, tn), lambda j: (0, j)))
        args.append(resid)
    return pl.pallas_call(
        body, name=name, grid=(n // tn,), in_specs=in_specs,
        out_specs=pl.BlockSpec((m, tn), lambda j: (0, j)),
        out_shape=jax.ShapeDtypeStruct((m, n), out_dtype),
        compiler_params=_params("parallel"),
    )(*args)


def mm_rows(st, res, *, out_dtype, name, tn=512):
    k, m = st.shape
    n = res.shape[1]
    assert res.shape[0] == k
    tn = _tile(m, tn)

    def body(st_ref, res_ref, o_ref):
        o_ref[...] = lax.dot_general(st_ref[...], res_ref[...], (((0,), (0,)), ((), ())),
                                     preferred_element_type=F32).astype(o_ref.dtype)

    return pl.pallas_call(
        body, name=name, grid=(m // tn,),
        in_specs=[pl.BlockSpec((k, tn), lambda j: (0, j)), pl.BlockSpec((k, n), lambda j: (0, 0))],
        out_specs=pl.BlockSpec((tn, n), lambda j: (j, 0)),
        out_shape=jax.ShapeDtypeStruct((m, n), out_dtype),
        compiler_params=_params("parallel"),
    )(st, res)


def mm_k(a_list, b, *, b_nt, name, resid=None, tk=512, tnn=1024, lead=None):
    m = a_list[0].shape[0]
    ks = [a.shape[1] for a in a_list]
    ktot = sum(ks)
    n = b.shape[-2] if b_nt else b.shape[-1]
    assert (b.shape[-1] if b_nt else b.shape[-2]) == ktot
    tk = _tile(ks[0], tk)
    assert all(kk % tk == 0 for kk in ks)
    tnn = _tile(n, tnn)
    nks = [kk // tk for kk in ks]
    starts = [sum(nks[:i]) for i in range(len(nks))]
    nk = sum(nks)

    def body(*refs):
        a_refs = refs[:len(a_list)]
        b_ref = refs[len(a_list)]
        o_ref = refs[-1]
        kk = pl.program_id(1)

        @pl.when(kk == 0)
        def _():
            if resid is not None:
                o_ref[...] = refs[len(a_list) + 1][...]
            else:
                o_ref[...] = jnp.zeros_like(o_ref)

        dn = (((1,), (1 if b_nt else 0,)), ((), ()))
        for i, a_ref in enumerate(a_refs):
            @pl.when(jnp.logical_and(kk >= starts[i], kk < starts[i] + nks[i]))
            def _(a_ref=a_ref):
                o_ref[...] += lax.dot_general(a_ref[...], b_ref[...], dn, preferred_element_type=F32)

    def a_spec(i):
        return pl.BlockSpec((m, tk), lambda nn, kk: (0, jnp.clip(kk - starts[i], 0, nks[i] - 1)))

    in_specs = [a_spec(i) for i in range(len(a_list))]
    in_specs.append(_spec((tnn, tk), lambda nn, kk: (nn, kk), lead) if b_nt
                    else _spec((tk, tnn), lambda nn, kk: (kk, nn), lead))
    args = list(a_list) + [b]
    if resid is not None:
        in_specs.append(pl.BlockSpec((m, tnn), lambda nn, kk: (0, nn)))
        args.append(resid)
    return pl.pallas_call(
        body, name=name, grid=(n // tnn, nk), in_specs=in_specs,
        out_specs=pl.BlockSpec((m, tnn), lambda nn, kk: (0, nn)),
        out_shape=jax.ShapeDtypeStruct((m, n), F32),
        compiler_params=_params("parallel", "arbitrary"),
    )(*args)


HALO = SUBLANES
CHUNK_ROWS = 64


def _conv_taps(ext, r):
    return ext[HALO:], _shift_down(ext, 1)[HALO:], _shift_down(ext, 2)[HALO:]


def ffn_up_fused(h, w_up, conv_w, conv_b, name, lead=None):
    t, d = h.shape
    f = w_up.shape[-1] // 2
    tn = _tile(f, 256)
    nf = f // tn
    r = min(CHUNK_ROWS, t)

    def body(h_ref, wv_ref, wg_ref, cwv_ref, cwg_ref, cbv_ref, cbg_ref, uv_ref, ug_ref, act_ref, sv, sg):
        zero = jnp.zeros((HALO, tn), F32)
        sv[0:HALO, :] = zero
        sg[0:HALO, :] = zero
        hv = h_ref[...]
        sv[HALO:, :] = jnp.dot(hv, wv_ref[...], preferred_element_type=F32).astype(BF16).astype(F32)
        sg[HALO:, :] = jnp.dot(hv, wg_ref[...], preferred_element_type=F32).astype(BF16).astype(F32)
        cwv, cwg = cwv_ref[...], cwg_ref[...]
        cbv, cbg = cbv_ref[...], cbg_ref[...]

        def chunk(i, carry):
            r0 = pl.multiple_of(i * r, r)
            v0, v1, v2 = _conv_taps(sv[pl.ds(r0, r + HALO), :], r)
            g0, g1, g2 = _conv_taps(sg[pl.ds(r0, r + HALO), :], r)
            cval = cbv + cwv[2:3] * v0 + cwv[1:2] * v1 + cwv[0:1] * v2
            cgate = cbg + cwg[2:3] * g0 + cwg[1:2] * g1 + cwg[0:1] * g2
            uv_ref[pl.ds(r0, r), :] = v0.astype(BF16)
            ug_ref[pl.ds(r0, r), :] = g0.astype(BF16)
            act_ref[pl.ds(r0, r), :] = (cgate * _sigmoid(cgate) * cval).astype(BF16)
            return carry

        lax.fori_loop(0, t // r, chunk, 0)

    col = lambda off: _spec((d, tn), lambda j: (0, j + off), lead)
    cw = lambda off: pl.BlockSpec((3, tn), lambda j: (0, j + off))
    cb = lambda off: pl.BlockSpec((1, tn), lambda j: (0, j + off))
    out = pl.BlockSpec((t, tn), lambda j: (0, j))
    return pl.pallas_call(
        body, name=name, grid=(nf,),
        in_specs=[pl.BlockSpec((t, d), lambda j: (0, 0)), col(0), col(nf), cw(0), cw(nf), cb(0), cb(nf)],
        out_specs=[out, out, out],
        out_shape=[jax.ShapeDtypeStruct((t, f), BF16)] * 3,
        scratch_shapes=[pltpu.VMEM((t + HALO, tn), F32), pltpu.VMEM((t + HALO, tn), F32)],
        compiler_params=_params("parallel"),
    )(h, w_up, w_up, conv_w, conv_w, conv_b, conv_b)


def ffn_bwd_fused(dout, w_down, up_val, up_gate, conv_w, conv_b, name, lead=None):
    t, d = dout.shape
    f = w_down.shape[-2]
    tn = _tile(f, 256)
    nf = f // tn
    r = min(CHUNK_ROWS, t)

    def body(do_ref, wd_ref, uv_ref, ug_ref, cwv_ref, cwg_ref, cbv_ref, cbg_ref,
             dv_ref, dg_ref, dcwv_ref, dcwg_ref, dcbv_ref, dcbg_ref, da, sv, sg, ev, eg):
        zero = jnp.zeros((HALO, tn), F32)
        sv[0:HALO, :] = zero
        sg[0:HALO, :] = zero
        ev[t:, :] = zero
        eg[t:, :] = zero
        da[...] = lax.dot_general(do_ref[...], wd_ref[...], (((1,), (1,)), ((), ())), preferred_element_type=F32)
        sv[HALO:, :] = uv_ref[...].astype(F32)
        sg[HALO:, :] = ug_ref[...].astype(F32)
        cwv, cwg = cwv_ref[...], cwg_ref[...]
        cbv, cbg = cbv_ref[...], cbg_ref[...]

        def chunk(i, acc):
            r0 = pl.multiple_of(i * r, r)
            v = _conv_taps(sv[pl.ds(r0, r + HALO), :], r)
            g = _conv_taps(sg[pl.ds(r0, r + HALO), :], r)
            cval = cbv + cwv[2:3] * v[0] + cwv[1:2] * v[1] + cwv[0:1] * v[2]
            cgate = cbg + cwg[2:3] * g[0] + cwg[1:2] * g[1] + cwg[0:1] * g[2]
            s = _sigmoid(cgate)
            dav = da[pl.ds(r0, r), :]
            dval = dav * (cgate * s)
            dgate = dav * cval * (s * (1.0 + cgate * (1.0 - s)))
            ev[pl.ds(r0, r), :] = dval
            eg[pl.ds(r0, r), :] = dgate
            col = lambda z: jnp.sum(z, axis=0, keepdims=True)
            new = [acc[0] + col(dval), acc[1] + col(dgate)]
            new += [acc[2 + j] + col(dval * v[2 - j]) for j in range(3)]
            new += [acc[5 + j] + col(dgate * g[2 - j]) for j in range(3)]
            return tuple(new)

        z1 = jnp.zeros((1, tn), F32)
        acc = lax.fori_loop(0, t // r, chunk, (z1,) * 8)
        dcbv_ref[...] = acc[0]
        dcbg_ref[...] = acc[1]
        for j in range(3):
            dcwv_ref[j:j + 1, :] = acc[2 + j]
            dcwg_ref[j:j + 1, :] = acc[5 + j]

        def chunk2(i, carry):
            r0 = pl.multiple_of(i * r, r)
            for e_ref, cw_, o_ref in ((ev, cwv, dv_ref), (eg, cwg, dg_ref)):
                ext = e_ref[pl.ds(r0, r + HALO), :]
                d0, d1, d2 = ext[:r], _shift_up(ext, 1)[:r], _shift_up(ext, 2)[:r]
                o_ref[pl.ds(r0, r), :] = (cw_[2:3] * d0 + cw_[1:2] * d1 + cw_[0:1] * d2).astype(BF16)
            return carry

        lax.fori_loop(0, t // r, chunk2, 0)

    cw = lambda off: pl.BlockSpec((3, tn), lambda j: (0, j + off))
    cb = lambda off: pl.BlockSpec((1, tn), lambda j: (0, j + off))
    tile = pl.BlockSpec((t, tn), lambda j: (0, j))
    s = lambda rows, dt: jax.ShapeDtypeStruct((rows, f), dt)
    halo = pltpu.VMEM((t + HALO, tn), F32)
    return pl.pallas_call(
        body, name=name, grid=(nf,),
        in_specs=[pl.BlockSpec((t, d), lambda j: (0, 0)), _spec((tn, d), lambda j: (j, 0), lead),
                  tile, tile, cw(0), cw(nf), cb(0), cb(nf)],
        out_specs=[tile, tile, pl.BlockSpec((3, tn), lambda j: (0, j)), pl.BlockSpec((3, tn), lambda j: (0, j)),
                   pl.BlockSpec((1, tn), lambda j: (0, j)), pl.BlockSpec((1, tn), lambda j: (0, j))],
        out_shape=[s(t, BF16), s(t, BF16), s(3, F32), s(3, F32), s(1, F32), s(1, F32)],
        scratch_shapes=[pltpu.VMEM((t, tn), F32), halo, halo, halo, halo],
        compiler_params=_params("parallel"),
    )(dout, w_down, up_val, up_gate, conv_w, conv_w, conv_b, conv_b)


POOL_PAD = max(POOL_WINDOWS)


def _window_sum(ext, win, shift):
    assert POOL_WINDOWS == (2, 4, 8, 16)
    s2 = ext + shift(ext, 1)
    s4 = s2 + shift(s2, 2)
    s8 = s4 + shift(s4, 4)
    s16 = s8 + shift(s8, 8)
    return jnp.where(win == 2, s2, jnp.where(win == 4, s4, jnp.where(win == 8, s8, s16)))


def _pool_win_scalar(g):
    win = jnp.int32(POOL_WINDOWS[-1])
    for k in range(len(POOL_WINDOWS) - 2, -1, -1):
        win = jnp.where(g == k, jnp.int32(POOL_WINDOWS[k]), win)
    return win


def _pool_count(r0, r, win):
    rows = r0 + lax.broadcasted_iota(jnp.int32, (r, 1), 0)
    return jnp.minimum(rows + 1, win).astype(F32)


def _pooled_into(hp, pooled, h_ref, t, r, win):
    hp[0:POOL_PAD, :] = jnp.zeros((POOL_PAD, hp.shape[1]), F32)
    hp[POOL_PAD:, :] = h_ref[...]

    def chunk(i, carry):
        r0 = pl.multiple_of(i * r, r)
        ext = hp[pl.ds(r0, r + POOL_PAD), :]
        s = _window_sum(ext, win, _shift_down)[POOL_PAD:]
        pooled[pl.ds(r0, r), :] = (s / _pool_count(r0, r, win) - ext[POOL_PAD:]).astype(BF16)
        return carry

    lax.fori_loop(0, t // r, chunk, 0)


def pool_fwd(h, x, w, b, scale, name, lead=None):
    t, d = h.shape
    ng, dg = w.shape[-3], w.shape[-2]
    r = min(CHUNK_ROWS, t)

    def body(h_ref, x_ref, w_ref, b_ref, s_ref, o_ref, hp, pooled):
        win = _pool_win_scalar(pl.program_id(0))
        _pooled_into(hp, pooled, h_ref, t, r, win)
        y = jnp.dot(pooled[...], w_ref[...], preferred_element_type=F32)
        o_ref[...] = x_ref[...] + (y + b_ref[...]) * s_ref[...]

    col = pl.BlockSpec((t, dg), lambda g: (0, g))
    vec = pl.BlockSpec((1, dg), lambda g: (0, g))
    return pl.pallas_call(
        body, name=name, grid=(ng,),
        in_specs=[col, col, _spec((None, dg, dg), lambda g: (g, 0, 0), lead), vec, vec],
        out_specs=col, out_shape=jax.ShapeDtypeStruct((t, d), F32),
        scratch_shapes=[pltpu.VMEM((t + POOL_PAD, dg), F32), pltpu.VMEM((t, dg), BF16)],
        compiler_params=_params("parallel"),
    )(h, x, w, b, scale)


def pool_bwd(dm, h, w, b, scale, name, lead=None):
    t, d = h.shape
    ng, dg = w.shape[-3], w.shape[-2]
    r = min(CHUNK_ROWS, t)

    def body(dm_ref, h_ref, w_ref, b_ref, s_ref, dh_ref, dw_ref, db_ref, ds_ref, hp, pooled, q):
        win = _pool_win_scalar(pl.program_id(0))
        _pooled_into(hp, pooled, h_ref, t, r, win)
        wv = w_ref[...]
        y = jnp.dot(pooled[...], wv, preferred_element_type=F32)
        dmv = dm_ref[...]
        ds_ref[...] = jnp.sum(dmv * (y + b_ref[...]), axis=0, keepdims=True)
        dy = dmv * s_ref[...]
        db_ref[...] = jnp.sum(dy, axis=0, keepdims=True)
        dyb = dy.astype(BF16)
        dw_ref[...] = lax.dot_general(pooled[...], dyb, (((0,), (0,)), ((), ())),
                                      preferred_element_type=F32).astype(dw_ref.dtype)
        dp = lax.dot_general(dyb, wv, (((1,), (1,)), ((), ())), preferred_element_type=F32)
        q[t:, :] = jnp.zeros((POOL_PAD, dg), F32)
        q[0:t, :] = dp / _pool_count(0, t, win)
        dh_ref[...] = -dp

        def chunk(i, carry):
            r0 = pl.multiple_of(i * r, r)
            ext = q[pl.ds(r0, r + POOL_PAD), :]
            dh_ref[pl.ds(r0, r), :] += _window_sum(ext, win, _shift_up)[:r]
            return carry

        lax.fori_loop(0, t // r, chunk, 0)

    col = pl.BlockSpec((t, dg), lambda g: (0, g))
    vec = pl.BlockSpec((1, dg), lambda g: (0, g))
    wspec = pl.BlockSpec((None, dg, dg), lambda g: (g, 0, 0))
    return pl.pallas_call(
        body, name=name, grid=(ng,),
        in_specs=[col, col, _spec((None, dg, dg), lambda g: (g, 0, 0), lead), vec, vec],
        out_specs=[col, wspec, vec, vec],
        out_shape=[jax.ShapeDtypeStruct((t, d), F32), jax.ShapeDtypeStruct((ng, dg, dg), F32),
                   jax.ShapeDtypeStruct((1, d), F32), jax.ShapeDtypeStruct((1, d), F32)],
        scratch_shapes=[pltpu.VMEM((t + POOL_PAD, dg), F32), pltpu.VMEM((t, dg), BF16),
                        pltpu.VMEM((t + POOL_PAD, dg), F32)],
        compiler_params=_params("parallel"),
    )(dm, h, w, b, scale)


SB_BLOCK = 256


def _tri_sum(v, tri):
    hi = v.astype(BF16)
    r1 = v - hi.astype(F32)
    mid = r1.astype(BF16)
    lo = (r1 - mid.astype(F32)).astype(BF16)
    dot = lambda p: jnp.dot(p, tri, preferred_element_type=F32)
    return dot(hi) + dot(mid) + dot(lo)


def _tri(bk, cmp):
    return cmp(lax.broadcasted_iota(jnp.int32, (bk, bk), 0), lax.broadcasted_iota(jnp.int32, (bk, bk), 1)).astype(BF16)


def _sb_logits(qblk, kblk, q0, k0, inv):
    bq, bk = qblk.shape[0], kblk.shape[0]
    z = lax.dot_general(qblk, kblk, (((1,), (1,)), ((), ())), preferred_element_type=F32) * inv
    qpos = q0 + lax.broadcasted_iota(jnp.int32, (bq, bk), 0)
    kpos = k0 + lax.broadcasted_iota(jnp.int32, (bq, bk), 1)
    mask = kpos < qpos
    lb = jnp.minimum(z, 0.0) - jnp.log(1.0 + jnp.exp(-jnp.abs(z)))
    lm = jnp.where(mask, lb - z, 0.0)
    return lb, lm, mask


def _head_norm(ref, gain):
    xv = ref[...].astype(F32)
    r = lax.rsqrt(jnp.mean(xv * xv, axis=-1, keepdims=True) + RMS_EPS)
    xhat = xv * r
    return xhat, r, (xhat * gain).astype(BF16)


def sb_attn_fwd(qkv, q_gain, k_gain, name):
    t = qkv.shape[0]
    d = qkv.shape[1] // 3
    dh = SB_HEAD_DIM
    nh = d // dh
    blk = min(SB_BLOCK, t)
    inv = 1.0 / math.sqrt(dh)

    def body(q_ref, k_ref, v_ref, qg_ref, kg_ref, o_ref, lt_ref, qn, kn):
        qn[...] = _head_norm(q_ref, qg_ref[...])[2]
        kn[...] = _head_norm(k_ref, kg_ref[...])[2]
        later = _tri(blk, lambda j, s: j > s)

        def q_loop(qb, carry):
            q0 = pl.multiple_of(qb * blk, blk)
            qblk = qn[pl.ds(q0, blk), :]

            def k_loop(i, st):
                c, acc = st
                k0 = pl.multiple_of((qb - i) * blk, blk)
                lb, lm, mask = _sb_logits(qblk, kn[pl.ds(k0, blk), :], q0, k0, inv)
                a = jnp.where(mask, jnp.exp(lb + _tri_sum(lm, later) + c), 0.0)
                acc = acc + jnp.dot(a.astype(BF16), v_ref[pl.ds(k0, blk), :], preferred_element_type=F32)
                return c + jnp.sum(lm, axis=1, keepdims=True), acc

            c, acc = lax.fori_loop(0, qb + 1, k_loop, (jnp.zeros((blk, 1), F32), jnp.zeros((blk, dh), F32)))
            o_ref[pl.ds(q0, blk), :] = acc.astype(BF16)
            lt_ref[pl.ds(q0, blk), :] = c
            return carry

        lax.fori_loop(0, t // blk, q_loop, 0)

    head = lambda off: pl.BlockSpec((t, dh), lambda h: (0, h + off))
    gain = pl.BlockSpec((1, dh), lambda h: (0, 0))
    return pl.pallas_call(
        body, name=name, grid=(nh,),
        in_specs=[head(0), head(nh), head(2 * nh), gain, gain],
        out_specs=[head(0), pl.BlockSpec((None, t, 1), lambda h: (h, 0, 0))],
        out_shape=[jax.ShapeDtypeStruct((t, d), BF16), jax.ShapeDtypeStruct((nh, t, 1), F32)],
        scratch_shapes=[pltpu.VMEM((t, dh), BF16), pltpu.VMEM((t, dh), BF16)],
        compiler_params=_params("parallel"),
    )(qkv, qkv, qkv, q_gain, k_gain)


def sb_attn_bwd(qkv, ltot, do, q_gain, k_gain, name):
    t = qkv.shape[0]
    d = qkv.shape[1] // 3
    dh = SB_HEAD_DIM
    nh = d // dh
    blk = min(SB_BLOCK, t)
    inv = 1.0 / math.sqrt(dh)
    tn_dims = (((0,), (0,)), ((), ()))

    def body(q_ref, k_ref, v_ref, lt_ref, do_ref, qg_ref, kg_ref, dq_ref, dk_ref, dv_ref, dqg_ref, dkg_ref,
             qn, kn, dqn, dkn, dvn):
        qg, kg = qg_ref[...], kg_ref[...]
        qhat, rq, qnb = _head_norm(q_ref, qg)
        khat, rk, knb = _head_norm(k_ref, kg)
        qn[...] = qnb
        kn[...] = knb
        dkn[...] = jnp.zeros_like(dkn)
        dvn[...] = jnp.zeros_like(dvn)
        upto = _tri(blk, lambda j, s: j <= s)
        before = _tri(blk, lambda j, s: j < s)

        def q_loop(qb, carry):
            q0 = pl.multiple_of(qb * blk, blk)
            qblk = qn[pl.ds(q0, blk), :]
            doblk = do_ref[pl.ds(q0, blk), :]
            ltv = lt_ref[pl.ds(q0, blk), :]

            def k_loop(kb, st):
                pl_, pg, dq = st
                k0 = pl.multiple_of(kb * blk, blk)
                kblk = kn[pl.ds(k0, blk), :]
                lb, lm, mask = _sb_logits(qblk, kblk, q0, k0, inv)
                a = jnp.where(mask, jnp.exp(lb + (ltv - pl_ - _tri_sum(lm, upto))), 0.0)
                da = lax.dot_general(doblk, v_ref[pl.ds(k0, blk), :], (((1,), (1,)), ((), ())),
                                     preferred_element_type=F32)
                g = da * a
                g_before = pg + _tri_sum(g, before)
                beta = jnp.exp(lb)
                dz = (jnp.where(mask, g * (1.0 - beta) - beta * g_before, 0.0) * inv).astype(BF16)
                dq = dq + jnp.dot(dz, kblk, preferred_element_type=F32)
                dkn[pl.ds(k0, blk), :] += lax.dot_general(dz, qblk, tn_dims, preferred_element_type=F32)
                dvn[pl.ds(k0, blk), :] += lax.dot_general(a.astype(BF16), doblk, tn_dims, preferred_element_type=F32)
                return pl_ + jnp.sum(lm, axis=1, keepdims=True), pg + jnp.sum(g, axis=1, keepdims=True), dq

            z1 = jnp.zeros((blk, 1), F32)
            _, _, dq = lax.fori_loop(0, qb + 1, k_loop, (z1, z1, jnp.zeros((blk, dh), F32)))
            dqn[pl.ds(q0, blk), :] = dq
            return carry

        lax.fori_loop(0, t // blk, q_loop, 0)

        first = pl.program_id(0) == 0
        for dn, xhat, r, gain, out_ref, dgain_ref in ((dqn, qhat, rq, qg, dq_ref, dqg_ref),
                                                      (dkn, khat, rk, kg, dk_ref, dkg_ref)):
            dnv = dn[...]
            dxhat = dnv * gain
            out_ref[...] = (r * (dxhat - xhat * jnp.mean(dxhat * xhat, axis=-1, keepdims=True))).astype(BF16)
            part = jnp.sum(dnv * xhat, axis=0, keepdims=True)

            @pl.when(first)
            def _(dgain_ref=dgain_ref, part=part):
                dgain_ref[...] = part

            @pl.when(jnp.logical_not(first))
            def _(dgain_ref=dgain_ref, part=part):
                dgain_ref[...] += part

        dv_ref[...] = dvn[...].astype(BF16)

    head = lambda off: pl.BlockSpec((t, dh), lambda h: (0, h + off))
    gain = pl.BlockSpec((1, dh), lambda h: (0, 0))
    big = jax.ShapeDtypeStruct((t, d), BF16)
    small = jax.ShapeDtypeStruct((1, dh), F32)
    return pl.pallas_call(
        body, name=name, grid=(nh,),
        in_specs=[head(0), head(nh), head(2 * nh), pl.BlockSpec((None, t, 1), lambda h: (h, 0, 0)), head(0),
                  gain, gain],
        out_specs=[head(0), head(0), head(0), gain, gain],
        out_shape=[big, big, big, small, small],
        scratch_shapes=[pltpu.VMEM((t, dh), BF16), pltpu.VMEM((t, dh), BF16),
                        pltpu.VMEM((t, dh), F32), pltpu.VMEM((t, dh), F32), pltpu.VMEM((t, dh), F32)],
        compiler_params=_params("arbitrary"),
    )(qkv, qkv, qkv, ltot, do, q_gain, k_gain)


GELU_C = math.sqrt(2.0 / math.pi)
GELU_A = 0.044715
SCAN_ROWS = SUBLANES


def _gelu(y):
    return 0.5 * y * (1.0 + jnp.tanh(GELU_C * (y + GELU_A * y * y * y)))


def _gelu_grad(y):
    th = jnp.tanh(GELU_C * (y + GELU_A * y * y * y))
    return 0.5 * (1.0 + th) + 0.5 * y * (1.0 - th * th) * GELU_C * (1.0 + 3.0 * GELU_A * y * y)


def _powers(ar, ai):
    out = [(ar, ai)]
    for _ in range(SCAN_ROWS - 1):
        pr, pi = out[-1]
        out.append((pr * ar - pi * ai, pr * ai + pi * ar))
    return out


def _rows(vals):
    c = vals[0].shape[1]
    row = lax.broadcasted_iota(jnp.int32, (SCAN_ROWS, c), 0)
    out = jnp.broadcast_to(vals[SCAN_ROWS - 1], (SCAN_ROWS, c))
    for j in range(SCAN_ROWS - 2, -1, -1):
        out = jnp.where(row == j, vals[j], out)
    return out


def _scan_forward(sr, si, off, t, ar, ai):
    c = ar.shape[1]
    p = _powers(ar, ai)
    pwr = _rows([q[0] for q in p])
    pwi = _rows([q[1] for q in p])
    row = lax.broadcasted_iota(jnp.int32, (SCAN_ROWS, c), 0)

    def tile(i, carry):
        cr, ci = carry
        r0 = pl.multiple_of(off + i * SCAN_ROWS, SCAN_ROWS)
        xr = sr[pl.ds(r0, SCAN_ROWS), :]
        xi = si[pl.ds(r0, SCAN_ROWS), :]
        for k in (1, 2, 4):
            pr, pi = p[k - 1]
            shr = jnp.where(row >= k, _shift_down(xr, k), 0.0)
            shi = jnp.where(row >= k, _shift_down(xi, k), 0.0)
            xr, xi = xr + pr * shr - pi * shi, xi + pr * shi + pi * shr
        xr, xi = xr + pwr * cr - pwi * ci, xi + pwr * ci + pwi * cr
        sr[pl.ds(r0, SCAN_ROWS), :] = xr
        si[pl.ds(r0, SCAN_ROWS), :] = xi
        return xr[SCAN_ROWS - 1:SCAN_ROWS], xi[SCAN_ROWS - 1:SCAN_ROWS]

    z = jnp.zeros((1, c), F32)
    lax.fori_loop(0, t // SCAN_ROWS, tile, (z, z))


def _scan_reverse(gr, gi, t, ar, ai, xr_ref, xi_ref):
    c = ar.shape[1]
    p = _powers(ar, ai)
    pwr = _rows([p[SCAN_ROWS - 1 - j][0] for j in range(SCAN_ROWS)])
    pwi = _rows([p[SCAN_ROWS - 1 - j][1] for j in range(SCAN_ROWS)])
    row = lax.broadcasted_iota(jnp.int32, (SCAN_ROWS, c), 0)
    n = t // SCAN_ROWS

    def tile(ii, carry):
        cr, ci, dar, dai = carry
        r0 = pl.multiple_of((n - 1 - ii) * SCAN_ROWS, SCAN_ROWS)
        xr = gr[pl.ds(r0, SCAN_ROWS), :]
        xi = gi[pl.ds(r0, SCAN_ROWS), :]
        for k in (1, 2, 4):
            pr, pi = p[k - 1]
            shr = jnp.where(row < SCAN_ROWS - k, _shift_up(xr, k), 0.0)
            shi = jnp.where(row < SCAN_ROWS - k, _shift_up(xi, k), 0.0)
            xr, xi = xr + pr * shr + pi * shi, xi + pr * shi - pi * shr
        xr, xi = xr + pwr * cr + pwi * ci, xi + pwr * ci - pwi * cr
        gr[pl.ds(r0, SCAN_ROWS), :] = xr
        gi[pl.ds(r0, SCAN_ROWS), :] = xi
        xpr = _shift_down(xr_ref[pl.ds(r0, 2 * SCAN_ROWS), :], 1)[SCAN_ROWS:]
        xpi = _shift_down(xi_ref[pl.ds(r0, 2 * SCAN_ROWS), :], 1)[SCAN_ROWS:]
        return xr[0:1], xi[0:1], dar + xr * xpr + xi * xpi, dai + xi * xpr - xr * xpi

    z = jnp.zeros((1, c), F32)
    z8 = jnp.zeros((SCAN_ROWS, c), F32)
    _, _, dar, dai = lax.fori_loop(0, n, tile, (z, z, z8, z8))
    return jnp.sum(dar, axis=0, keepdims=True), jnp.sum(dai, axis=0, keepdims=True)


def _ssm_specs(t, nb, ch, st):
    col = pl.BlockSpec((t, ch), lambda b: (0, b))
    vec = pl.BlockSpec((1, ch), lambda b: (0, b))
    bspec = pl.BlockSpec((None, ch, st), lambda b: (b, 0, 0))
    cspec = pl.BlockSpec((None, st, ch), lambda b: (b, 0, 0))
    aspec = pl.BlockSpec((None, 1, st), lambda b: (b, 0, 0))
    return col, vec, bspec, cspec, aspec


def ssm_core_fwd(u, bre, bim, cre, cim, a_re, a_im, dskip, name):
    t, d = u.shape
    nb, ch, st = bre.shape

    def body(u_ref, bre_ref, bim_ref, cre_ref, cim_ref, ar_ref, ai_ref, d_ref, y_ref, yg_ref, sr, si):
        uv = u_ref[...]
        ub = uv.astype(BF16)
        sr[...] = jnp.dot(ub, bre_ref[...], preferred_element_type=F32)
        si[...] = jnp.dot(ub, bim_ref[...], preferred_element_type=F32)
        _scan_forward(sr, si, 0, t, ar_ref[...], ai_ref[...])
        y = (jnp.dot(sr[...].astype(BF16), cre_ref[...], preferred_element_type=F32)
             - jnp.dot(si[...].astype(BF16), cim_ref[...], preferred_element_type=F32) + d_ref[...] * uv)
        y_ref[...] = y
        yg_ref[...] = _gelu(y).astype(BF16)

    col, vec, bspec, cspec, aspec = _ssm_specs(t, nb, ch, st)
    return pl.pallas_call(
        body, name=name, grid=(nb,),
        in_specs=[col, bspec, bspec, cspec, cspec, aspec, aspec, vec],
        out_specs=[col, col],
        out_shape=[jax.ShapeDtypeStruct((t, d), F32), jax.ShapeDtypeStruct((t, d), BF16)],
        scratch_shapes=[pltpu.VMEM((t, st), F32), pltpu.VMEM((t, st), F32)],
        compiler_params=_params("parallel"),
    )(u, bre, bim, cre, cim, a_re, a_im, dskip)


def ssm_core_bwd(u, y, dyg, bre, bim, cre, cim, a_re, a_im, dskip, name):
    t, d = u.shape
    nb, ch, st = bre.shape
    tn_dims = (((0,), (0,)), ((), ()))
    nt_dims = (((1,), (1,)), ((), ()))

    def body(u_ref, y_ref, dyg_ref, bre_ref, bim_ref, cre_ref, cim_ref, ar_ref, ai_ref, d_ref,
             du_ref, dd_ref, dbre_ref, dbim_ref, dcre_ref, dcim_ref, dar_ref, dai_ref, xr, xi, gr, gi):
        uv = u_ref[...]
        ub = uv.astype(BF16)
        ar, ai = ar_ref[...], ai_ref[...]
        dy = dyg_ref[...] * _gelu_grad(y_ref[...])
        dd_ref[...] = jnp.sum(dy * uv, axis=0, keepdims=True)
        zero = jnp.zeros((HALO, st), F32)
        xr[0:HALO, :] = zero
        xi[0:HALO, :] = zero
        xr[HALO:, :] = jnp.dot(ub, bre_ref[...], preferred_element_type=F32)
        xi[HALO:, :] = jnp.dot(ub, bim_ref[...], preferred_element_type=F32)
        _scan_forward(xr, xi, HALO, t, ar, ai)
        dyb = dy.astype(BF16)
        dcre_ref[...] = lax.dot_general(xr[HALO:, :].astype(BF16), dyb, tn_dims, preferred_element_type=F32)
        dcim_ref[...] = -lax.dot_general(xi[HALO:, :].astype(BF16), dyb, tn_dims, preferred_element_type=F32)
        gr[...] = lax.dot_general(dyb, cre_ref[...], nt_dims, preferred_element_type=F32)
        gi[...] = -lax.dot_general(dyb, cim_ref[...], nt_dims, preferred_element_type=F32)
        dar, dai = _scan_reverse(gr, gi, t, ar, ai, xr, xi)
        dar_ref[...] = dar
        dai_ref[...] = dai
        grb = gr[...].astype(BF16)
        gib = gi[...].astype(BF16)
        dbre_ref[...] = lax.dot_general(ub, grb, tn_dims, preferred_element_type=F32)
        dbim_ref[...] = lax.dot_general(ub, gib, tn_dims, preferred_element_type=F32)
        du_ref[...] = (d_ref[...] * dy + lax.dot_general(grb, bre_ref[...], nt_dims, preferred_element_type=F32)
                       + lax.dot_general(gib, bim_ref[...], nt_dims, preferred_element_type=F32))

    col, vec, bspec, cspec, aspec = _ssm_specs(t, nb, ch, st)
    sh = jax.ShapeDtypeStruct
    return pl.pallas_call(
        body, name=name, grid=(nb,),
        in_specs=[col, col, col, bspec, bspec, cspec, cspec, aspec, aspec, vec],
        out_specs=[col, vec, bspec, bspec, cspec, cspec, aspec, aspec],
        out_shape=[sh((t, d), F32), sh((1, d), F32), sh((nb, ch, st), F32), sh((nb, ch, st), F32),
                   sh((nb, st, ch), F32), sh((nb, st, ch), F32), sh((nb, 1, st), F32), sh((nb, 1, st), F32)],
        scratch_shapes=[pltpu.VMEM((t + HALO, st), F32), pltpu.VMEM((t + HALO, st), F32),
                        pltpu.VMEM((t, st), F32), pltpu.VMEM((t, st), F32)],
        compiler_params=_params("parallel"),
    )(u, y, dyg, bre, bim, cre, cim, a_re, a_im, dskip)


def glu_fwd(yg, w_glu, b_glu, x, name, lead=None):
    t, d = yg.shape
    tn = _tile(d, 256)
    nd = d // tn

    def body(yg_ref, wv_ref, wg_ref, bv_ref, bg_ref, x_ref, val_ref, gate_ref, o_ref):
        ygv = yg_ref[...]
        vb = (jnp.dot(ygv, wv_ref[...], preferred_element_type=F32) + bv_ref[...]).astype(BF16)
        gb = (jnp.dot(ygv, wg_ref[...], preferred_element_type=F32) + bg_ref[...]).astype(BF16)
        val_ref[...] = vb
        gate_ref[...] = gb
        o_ref[...] = x_ref[...] + vb.astype(F32) * _sigmoid(gb.astype(F32))

    col = lambda off: _spec((d, tn), lambda j: (0, j + off), lead)
    vec = lambda off: pl.BlockSpec((1, tn), lambda j: (0, j + off))
    tile = pl.BlockSpec((t, tn), lambda j: (0, j))
    return pl.pallas_call(
        body, name=name, grid=(nd,),
        in_specs=[pl.BlockSpec((t, d), lambda j: (0, 0)), col(0), col(nd), vec(0), vec(nd), tile],
        out_specs=[tile, tile, tile],
        out_shape=[jax.ShapeDtypeStruct((t, d), BF16), jax.ShapeDtypeStruct((t, d), BF16),
                   jax.ShapeDtypeStruct((t, d), F32)],
        compiler_params=_params("parallel"),
    )(yg, w_glu, w_glu, b_glu, b_glu, x)


def glu_bwd(dm, val, gate, name):
    t, d = dm.shape
    tn = _tile(d, 256)

    def body(dm_ref, val_ref, gate_ref, dv_ref, dg_ref, dbv_ref, dbg_ref):
        dmv = dm_ref[...]
        s = _sigmoid(gate_ref[...].astype(F32))
        dval = dmv * s
        dgate = dmv * val_ref[...].astype(F32) * s * (1.0 - s)
        dv_ref[...] = dval.astype(BF16)
        dg_ref[...] = dgate.astype(BF16)
        dbv_ref[...] = jnp.sum(dval, axis=0, keepdims=True)
        dbg_ref[...] = jnp.sum(dgate, axis=0, keepdims=True)

    tile = pl.BlockSpec((t, tn), lambda j: (0, j))
    vec = pl.BlockSpec((1, tn), lambda j: (0, j))
    return pl.pallas_call(
        body, name=name, grid=(d // tn,),
        in_specs=[tile, tile, tile], out_specs=[tile, tile, vec, vec],
        out_shape=[jax.ShapeDtypeStruct((t, d), BF16), jax.ShapeDtypeStruct((t, d), BF16),
                   jax.ShapeDtypeStruct((1, d), F32), jax.ShapeDtypeStruct((1, d), F32)],
        compiler_params=_params("parallel"),
    )(dm, val, gate)


def _block_diag(m, gb):
    g, a, b = m.shape
    eye = jnp.eye(gb, dtype=m.dtype)
    return jnp.einsum("ngab,gk->ngakb", m.reshape(g // gb, gb, a, b), eye).reshape(g // gb, gb * a, gb * b)


def ssm_prepare(lam_re, lam_im, log_step, b_re, b_im, c_re, c_im):
    gb = SSM_BLOCK_GROUPS
    g, p = lam_re.shape
    step = jnp.exp(log_step)[:, None]
    mag = jnp.exp(lam_re * step)
    lb_re = mag * jnp.cos(lam_im * step)
    lb_im = mag * jnp.sin(lam_im * step)
    den = lam_re * lam_re + lam_im * lam_im
    f_re = ((lb_re - 1.0) * lam_re + lb_im * lam_im) / den
    f_im = (lb_im * lam_re - (lb_re - 1.0) * lam_im) / den
    bb_re = f_re[..., None] * b_re - f_im[..., None] * b_im
    bb_im = f_re[..., None] * b_im + f_im[..., None] * b_re
    tr = lambda m: jnp.transpose(m, (0, 2, 1))
    return (_block_diag(tr(bb_re), gb), _block_diag(tr(bb_im), gb), _block_diag(tr(c_re), gb), _block_diag(tr(c_im), gb),
            lb_re.reshape(g // gb, 1, gb * p), lb_im.reshape(g // gb, 1, gb * p))


EW_BLOCK_BYTES = 2 * 1024 * 1024
BF16_ROWS = 16


def _row_tile(rows, cols):
    limit = max(BF16_ROWS, EW_BLOCK_BYTES // (cols * 4))
    best = None
    for tr in range(BF16_ROWS, min(rows, limit) + 1, BF16_ROWS):
        if rows % tr == 0:
            best = tr
    return best if best is not None else rows


def _as2d(a):
    return a.reshape(-1, a.shape[-1])


def ew(fn, ins, out_dtypes, name):
    rows, cols = ins[0].shape
    tr = _row_tile(rows, cols)
    n_in = len(ins)

    def body(*refs):
        outs = fn(*[r[...] for r in refs[:n_in]])
        for o_ref, v in zip(refs[n_in:], outs):
            o_ref[...] = v.astype(o_ref.dtype)

    spec = pl.BlockSpec((tr, cols), lambda i: (i, 0))
    return pl.pallas_call(
        body, name=name, grid=(rows // tr,), in_specs=[spec] * n_in, out_specs=[spec] * len(out_dtypes),
        out_shape=[jax.ShapeDtypeStruct((rows, cols), dt) for dt in out_dtypes],
        compiler_params=_params("parallel"),
    )(*ins)


def _adamw(w, g, m, v):
    m = ADAM_B1 * m + (1.0 - ADAM_B1) * g
    v = ADAM_B2 * v + (1.0 - ADAM_B2) * (g * g)
    m_hat = m / (1.0 - ADAM_B1 ** ADAM_STEP)
    v_hat = v / (1.0 - ADAM_B2 ** ADAM_STEP)
    delta = -ADAM_LR * (m_hat / (jnp.sqrt(v_hat) + ADAM_EPS) + ADAM_WD * w)
    return delta, m, v


def adamw(w, g, m, v, name):
    outs = ew(_adamw, [_as2d(w), _as2d(g), _as2d(m), _as2d(v)], [F32, F32, F32], name)
    return [o.reshape(w.shape) for o in outs]


def loss_head(y, target, name):
    t, d = y.shape
    tr = min(t, 128)
    n = t // tr

    def body(y_ref, t_ref, dy_ref, dyb_ref, loss_ref, acc):
        i = pl.program_id(0)
        err = y_ref[...] - t_ref[...]
        dy = err * (1.0 / d)
        dy_ref[...] = dy
        dyb_ref[...] = dy.astype(BF16)
        part = jnp.sum(err * err, axis=0, keepdims=True)

        @pl.when(i == 0)
        def _():
            acc[...] = part

        @pl.when(i != 0)
        def _():
            acc[...] += part

        @pl.when(i == n - 1)
        def _():
            loss_ref[...] = jnp.full((1, LANES), 0.5 / d, F32) * jnp.sum(acc[...])

    row = pl.BlockSpec((tr, d), lambda i: (i, 0))
    return pl.pallas_call(
        body, name=name, grid=(n,), in_specs=[row, row],
        out_specs=[row, row, pl.BlockSpec((1, LANES), lambda i: (0, 0))],
        out_shape=[jax.ShapeDtypeStruct((t, d), F32), jax.ShapeDtypeStruct((t, d), BF16),
                   jax.ShapeDtypeStruct((1, LANES), F32)],
        scratch_shapes=[pltpu.VMEM((1, d), F32)],
        compiler_params=_params("arbitrary"),
    )(y, target)


HBM_SPEC = pl.BlockSpec(memory_space=pltpu.HBM)
VMEM_SPEC = pl.BlockSpec(memory_space=pltpu.VMEM)


def _place():
    return lax.axis_index("x"), lax.axis_index("y"), lax.axis_index("c")


def _other_chips(x, y):
    return [(1 - x, y), (x, 1 - y), (1 - x, 1 - y)]


def _remote(src, dst, send_sem, recv_sem, dev):
    return pltpu.make_async_remote_copy(src_ref=src, dst_ref=dst, send_sem=send_sem, recv_sem=recv_sem,
                                        device_id=dev, device_id_type=MESH)


def _piece(refs, shard_shape, ax, j, half):
    w = shard_shape[ax]
    a, off = divmod(j * w, refs[0].shape[ax]) if isinstance(j, int) else (0, j * w)
    idx = [pl.ds(0, s) for s in shard_shape]
    idx[ax] = pl.ds(off, w)
    if half is not None:
        h0 = shard_shape[0] // 2
        idx[0] = pl.ds((off if ax == 0 else 0) + half * h0, h0)
    return refs[a].at[tuple(idx)]


def small_allreduce(v, name):
    n, r, l = v.shape
    assert n == N_DEV

    def body(v_ref, o_ref, recv, red, send1, recv1, send2, recv2):
        x, y, c = _place()
        me = 4 * x + 2 * y + c
        dev = lambda k: (k // 4, (k // 2) % 2, k % 2)
        firsts = []
        for o in range(1, N_DEV):
            tgt = (me + o) % N_DEV
            cp = _remote(v_ref.at[tgt], recv.at[me], send1.at[o], recv1.at[me], dev(tgt))
            cp.start()
            firsts.append(cp)
        recv[me] = v_ref[me]
        for o in range(1, N_DEV):
            src = (me + o) % N_DEV
            _remote(v_ref.at[src], recv.at[src], send1.at[o], recv1.at[src], dev(src)).wait_recv()
        acc = recv[0]
        for s in range(1, N_DEV):
            acc = acc + recv[s]
        red[...] = acc
        o_ref[me] = acc
        seconds = []
        for o in range(1, N_DEV):
            tgt = (me + o) % N_DEV
            cp = _remote(red, o_ref.at[me], send2.at[o], recv2.at[me], dev(tgt))
            cp.start()
            seconds.append(cp)
        for o in range(1, N_DEV):
            src = (me + o) % N_DEV
            _remote(red, o_ref.at[src], send2.at[o], recv2.at[src], dev(src)).wait_recv()
        for cp in firsts + seconds:
            cp.wait_send()

    sems = pltpu.SemaphoreType.DMA((N_DEV,))
    return pl.pallas_call(
        body, name=name, in_specs=[VMEM_SPEC], out_specs=VMEM_SPEC,
        out_shape=jax.ShapeDtypeStruct(v.shape, F32),
        scratch_shapes=[pltpu.VMEM((N_DEV, r, l), F32), pltpu.VMEM((r, l), F32), sems, sems, sems, sems],
        compiler_params=pltpu.CompilerParams(vmem_limit_bytes=VMEM_LIMIT),
    )(v)


def _me_scalar():
    return (2 * lax.axis_index("x") + lax.axis_index("y")).astype(jnp.int32).reshape(1)


def cast_into_gathered(wf, axis, me1, name):
    nl, r, cw = wf.shape
    tr = _row_tile(r, cw)
    nrb = r // tr
    full = (nl, r * N_CHIPS, cw) if axis == 0 else (nl, r, cw * N_CHIPS)
    omap = (lambda l, i, me: (l, me[0] * nrb + i, 0)) if axis == 0 else (lambda l, i, me: (l, i, me[0]))

    def body(me_ref, w_ref, o_ref):
        o_ref[...] = w_ref[...].astype(BF16)

    return pl.pallas_call(
        body, name=name,
        grid_spec=pltpu.PrefetchScalarGridSpec(
            num_scalar_prefetch=1, grid=(nl, nrb),
            in_specs=[pl.BlockSpec((None, tr, cw), lambda l, i, me: (l, i, 0))],
            out_specs=pl.BlockSpec((None, tr, cw), omap)),
        out_shape=jax.ShapeDtypeStruct(full, BF16),
        compiler_params=_params("parallel", "parallel"),
    )(me1, wf)


def gather_send(bufs, shapes, axes, name):
    n = len(bufs)
    items = [(p, l) for p in range(n) for l in range(bufs[p].shape[0])]

    def body(*refs):
        o_refs = refs[n:2 * n]
        send_sem, recv_sem = refs[2 * n:]
        x, y, c = _place()
        me = 2 * x + y
        chips = _other_chips(x, y)
        place = lambda p, l, j: _piece([o_refs[p].at[l]], shapes[p], axes[p], j, c)
        sent = []
        for k, (p, l) in enumerate(items):
            for q, chip in enumerate(chips):
                mine = place(p, l, me)
                cp = _remote(mine, mine, send_sem.at[3 * k + q], recv_sem.at[3 * k + q], (chip[0], chip[1], c))
                cp.start()
                sent.append(cp)
        for k, (p, l) in enumerate(items):
            for q, chip in enumerate(chips):
                got = place(p, l, 2 * chip[0] + chip[1])
                _remote(got, got, send_sem.at[3 * k + q], recv_sem.at[3 * k + q], (chip[0], chip[1], c)).wait_recv()
        for cp in sent:
            cp.wait_send()

    sems = pltpu.SemaphoreType.DMA((3 * len(items),))
    return pl.pallas_call(
        body, name=name, in_specs=[HBM_SPEC] * n, out_specs=[HBM_SPEC] * n,
        out_shape=[jax.ShapeDtypeStruct(b.shape, b.dtype) for b in bufs],
        input_output_aliases={p: p for p in range(n)},
        scratch_shapes=[sems, sems],
    )(*bufs)


SIBLING_SLOTS = 2


def _row_step(nrb):
    s = pl.program_id(0)
    for ax in range(1, len(nrb)):
        s = s * nrb[ax] + pl.program_id(ax)
    return s


def gather_forward(buf, axis, r, cw, me1, name):
    nl = buf.shape[0]
    h0 = r // 2
    tr = _row_tile(h0, cw)
    nrb = h0 // tr
    peer = lambda q, me: (me[0] + q + 1) % N_CHIPS
    if axis == 1:
        view = buf.reshape(nl, 1, 2, h0, N_CHIPS * cw)
        spec = pl.BlockSpec((1, 1, 2, tr, cw), lambda l, q, i, me: (l, 0, 0, i, peer(q, me)))
    else:
        view = buf.reshape(nl, N_CHIPS, 2, h0, cw)
        spec = pl.BlockSpec((1, 1, 2, tr, cw), lambda l, q, i, me: (l, peer(q, me), 0, i, 0))

    def body(me_ref, in_ref, o_ref, rbuf, send_sem, recv_sem):
        x, y, c = _place()
        slot = _row_step((nl, N_CHIPS - 1, nrb)) % SIBLING_SLOTS
        cp = _remote(in_ref.at[0, 0, c], rbuf.at[slot], send_sem.at[slot], recv_sem.at[slot], (x, y, 1 - c))
        cp.start()
        o_ref[0, 0, c] = in_ref[0, 0, c]
        cp.wait_recv()
        o_ref[0, 0, 1 - c] = rbuf[slot]
        cp.wait_send()

    out = pl.pallas_call(
        body, name=name,
        grid_spec=pltpu.PrefetchScalarGridSpec(
            num_scalar_prefetch=1, grid=(nl, N_CHIPS - 1, nrb), in_specs=[spec], out_specs=spec,
            scratch_shapes=[pltpu.VMEM((SIBLING_SLOTS, tr, cw), buf.dtype),
                            pltpu.SemaphoreType.DMA((SIBLING_SLOTS,)), pltpu.SemaphoreType.DMA((SIBLING_SLOTS,))]),
        out_shape=jax.ShapeDtypeStruct(view.shape, view.dtype),
        input_output_aliases={1: 0},
        compiler_params=_params("arbitrary", "arbitrary", "arbitrary"),
    )(me1, view)
    return out.reshape(buf.shape)


def pair_reduce(g, axis, r, cw, name, into=None, first_slot=0):
    h0 = r // 2
    tr = _row_tile(h0, cw)
    nrb = h0 // tr
    if axis == 1:
        n_sh = g.shape[1] // cw
        view = g.reshape(1, 2, h0, n_sh * cw)
        spec = pl.BlockSpec((1, 2, tr, cw), lambda j, i: (0, 0, i, j))
    else:
        n_sh = g.shape[0] // r
        view = g.reshape(n_sh, 2, h0, cw)
        spec = pl.BlockSpec((1, 2, tr, cw), lambda j, i: (j, 0, i, 0))

    def body(g_ref, *rest):
        o_ref, rbuf, send_sem, recv_sem = rest[-4:]
        x, y, c = _place()
        slot = _row_step((n_sh, nrb)) % SIBLING_SLOTS
        cp = _remote(g_ref.at[0, 1 - c], rbuf.at[slot], send_sem.at[slot], recv_sem.at[slot], (x, y, 1 - c))
        cp.start()
        mine = g_ref[0, c].astype(F32)
        cp.wait_recv()
        o_ref[0] = (mine + rbuf[slot].astype(F32)).astype(BF16)
        cp.wait_send()

    args, in_specs, aliases = [view], [spec], {}
    if into is not None:
        args.append(into)
        in_specs.append(pl.BlockSpec(memory_space=pl.ANY))
        aliases = {1: 0}
    return pl.pallas_call(
        body, name=name, grid=(n_sh, nrb), in_specs=in_specs,
        out_specs=pl.BlockSpec((1, tr, cw), lambda j, i: (j + first_slot, i, 0)),
        out_shape=jax.ShapeDtypeStruct((N_CHIPS, h0, cw), BF16),
        input_output_aliases=aliases,
        scratch_shapes=[pltpu.VMEM((SIBLING_SLOTS, tr, cw), BF16),
                        pltpu.SemaphoreType.DMA((SIBLING_SLOTS,)), pltpu.SemaphoreType.DMA((SIBLING_SLOTS,))],
        compiler_params=_params("arbitrary", "arbitrary"),
    )(*args)


def chip_exchange(halves, name):
    n = len(halves)

    def body(*refs):
        h_refs, lb = refs[:n], refs[n:2 * n]
        send_sem, recv_sem = refs[2 * n:]
        x, y, c = _place()
        chips = _other_chips(x, y)
        sent = []
        for k in range(n):
            for q, chip in enumerate(chips):
                cp = _remote(h_refs[k].at[2 * chip[0] + chip[1]], lb[k].at[q],
                             send_sem.at[3 * k + q], recv_sem.at[3 * k + q], (chip[0], chip[1], c))
                cp.start()
                sent.append(cp)
        for k in range(n):
            for q, chip in enumerate(chips):
                _remote(lb[k].at[q], lb[k].at[q], send_sem.at[3 * k + q], recv_sem.at[3 * k + q],
                        (chip[0], chip[1], c)).wait_recv()
        for cp in sent:
            cp.wait_send()

    sems = pltpu.SemaphoreType.DMA((3 * n,))
    return pl.pallas_call(
        body, name=name, in_specs=[HBM_SPEC] * n, out_specs=[HBM_SPEC] * n,
        out_shape=[jax.ShapeDtypeStruct((N_CHIPS - 1,) + h.shape[1:], h.dtype) for h in halves],
        scratch_shapes=[sems, sems],
    )(*halves)


def reduce_share(half, landed, me1, layer, n_layers, name, into=None):
    _, h0, cw = half.shape
    tr = _row_tile(h0, cw)
    nrb = h0 // tr

    def body(me_ref, h_ref, l0, l1, l2, *rest):
        o_ref, sbuf, rbuf, send_sem, recv_sem = rest[-5:]
        x, y, c = _place()
        slot = pl.program_id(0) % SIBLING_SLOTS
        total = ((h_ref[...].astype(F32) + l0[...].astype(F32)) + l1[...].astype(F32)) + l2[...].astype(F32)
        sbuf[slot] = total
        cp = _remote(sbuf.at[slot], rbuf.at[slot], send_sem.at[slot], recv_sem.at[slot], (x, y, 1 - c))
        cp.start()
        o_ref[0, c] = total
        cp.wait_recv()
        o_ref[0, 1 - c] = rbuf[slot]
        cp.wait_send()

    landed_spec = lambda q: pl.BlockSpec((None, tr, cw), lambda i, me: (q, i, 0))
    args = [me1, half, landed, landed, landed]
    in_specs = [pl.BlockSpec((None, tr, cw), lambda i, me: (me[0], i, 0))] + [landed_spec(q) for q in range(N_CHIPS - 1)]
    aliases = {}
    if into is not None:
        args.append(into.reshape(n_layers, 2, h0, cw))
        in_specs.append(pl.BlockSpec(memory_space=pl.ANY))
        aliases = {5: 0}
    out = pl.pallas_call(
        body, name=name,
        grid_spec=pltpu.PrefetchScalarGridSpec(
            num_scalar_prefetch=1, grid=(nrb,), in_specs=in_specs,
            out_specs=pl.BlockSpec((1, 2, tr, cw), lambda i, me: (layer, 0, i, 0)),
            scratch_shapes=[pltpu.VMEM((SIBLING_SLOTS, tr, cw), F32), pltpu.VMEM((SIBLING_SLOTS, tr, cw), F32),
                            pltpu.SemaphoreType.DMA((SIBLING_SLOTS,)), pltpu.SemaphoreType.DMA((SIBLING_SLOTS,))]),
        out_shape=jax.ShapeDtypeStruct((n_layers, 2, h0, cw), F32),
        input_output_aliases=aliases,
        compiler_params=_params("arbitrary"),
    )(*args)
    return out.reshape(n_layers, 2 * h0, cw)


WEIGHTS = ["norm_mix_g", "norm_ffn_g", "pool_w", "pool_b", "pool_scale", "sb_w_qkv", "sb_q_gain", "sb_k_gain",
           "sb_w_o", "ssm_lam_re", "ssm_lam_im", "ssm_log_step", "ssm_b_re", "ssm_b_im", "ssm_c_re", "ssm_c_im",
           "ssm_d", "ssm_w_glu", "ssm_b_glu", "ffn_w_up", "ffn_conv_w", "ffn_conv_b", "ffn_w_down"]
BIG = {"sb_w_qkv": 1, "sb_w_o": 0, "ssm_w_glu": 1, "ffn_w_up": 1, "ffn_w_down": 0}
SMALL_SHARDED = {"pool_w": 2, "pool_b": 1, "pool_scale": 1, "ssm_d": 1, "ssm_b_glu": 1, "ffn_conv_w": 2}
SMALL = [n for n in WEIGHTS if n not in BIG]
SMALL_PAD = N_DEV * SUBLANES * LANES
N_MIXERS = 3


def _pack(arrays):
    flat = jnp.concatenate([a.reshape(-1).astype(F32) for a in arrays])
    total = -(-flat.shape[0] // SMALL_PAD) * SMALL_PAD
    flat = jnp.pad(flat, (0, total - flat.shape[0]))
    return flat.reshape(N_DEV, -1, LANES)


def _unpack(packed, like):
    flat = packed.reshape(-1)
    out, off = [], 0
    for a in like:
        out.append(flat[off:off + a.size].reshape(a.shape))
        off += a.size
    return out


def kernel(x, norm_mix_g, norm_ffn_g, pool_w, pool_b, pool_scale, sb_w_qkv, sb_q_gain, sb_k_gain, sb_w_o, ssm_lam_re, ssm_lam_im, ssm_log_step, ssm_b_re, ssm_b_im, ssm_c_re, ssm_c_im, ssm_d, ssm_w_glu, ssm_b_glu, ffn_w_up, ffn_conv_w, ffn_conv_b, ffn_w_down, loss_target, m_norm_mix_g, m_norm_ffn_g, m_pool_w, m_pool_b, m_pool_scale, m_sb_w_qkv, m_sb_q_gain, m_sb_k_gain, m_sb_w_o, m_ssm_lam_re, m_ssm_lam_im, m_ssm_log_step, m_ssm_b_re, m_ssm_b_im, m_ssm_c_re, m_ssm_c_im, m_ssm_d, m_ssm_w_glu, m_ssm_b_glu, m_ffn_w_up, m_ffn_conv_w, m_ffn_conv_b, m_ffn_w_down, v_norm_mix_g, v_norm_ffn_g, v_pool_w, v_pool_b, v_pool_scale, v_sb_w_qkv, v_sb_q_gain, v_sb_k_gain, v_sb_w_o, v_ssm_lam_re, v_ssm_lam_im, v_ssm_log_step, v_ssm_b_re, v_ssm_b_im, v_ssm_c_re, v_ssm_c_im, v_ssm_d, v_ssm_w_glu, v_ssm_b_glu, v_ffn_w_up, v_ffn_conv_w, v_ffn_conv_b, v_ffn_w_down):
    given = dict(locals())
    w = {n: given[n] for n in WEIGHTS}
    mom = {n: given["m_" + n] for n in WEIGHTS}
    var = {n: given["v_" + n] for n in WEIGHTS}
    xi, yi, ci = _place()
    me = 2 * xi + yi
    depth = norm_mix_g.shape[0]
    x_in = x[0]
    t, d = x_in.shape

    def placed(a, ax):
        shp = list(a.shape)
        shp[ax] *= N_CHIPS
        full = lax.dynamic_update_slice_in_dim(jnp.zeros(shp, F32), a, me * a.shape[ax], ax)
        return jnp.where(ci == 0, full, 0.0)

    sharded_full = [placed(w[n], ax) for n, ax in SMALL_SHARDED.items()]
    whole = dict(zip(SMALL_SHARDED, _unpack(small_allreduce(_pack(sharded_full), "gather_vectors"), sharded_full)))

    big = list(BIG)
    me1 = _me_scalar()
    shard = {n: tuple(w[n].shape[1:]) for n in big}
    bufs = gather_send([cast_into_gathered(w[n], BIG[n], me1, "cast_" + n) for n in big],
                       [shard[n] for n in big], [BIG[n] for n in big], "gather_send")
    gathered = {n: gather_forward(b, BIG[n], *shard[n], me1, "gather_forward_" + n) for n, b in zip(big, bufs)}
    gathered["pool_w"] = whole["pool_w"].astype(BF16)
    vec = lambda a, i: a[i:i + 1]

    saved = []
    xc = x_in
    for i in range(depth):
        kind, j = i % N_MIXERS, i // N_MIXERS
        s = {"x_in": xc}
        g_mix = vec(norm_mix_g, i)
        if kind == 0:
            (h,) = rmsnorm_fwd(xc, g_mix, [F32], f"norm_mix{i}")
            x_mid = pool_fwd(h, xc, gathered["pool_w"], vec(whole["pool_b"], j), vec(whole["pool_scale"], j),
                             f"pool_fwd{i}", lead=j)
        elif kind == 1:
            (h,) = rmsnorm_fwd(xc, g_mix, [BF16], f"norm_mix{i}")
            s["qkv"] = mm_cols(h, gathered["sb_w_qkv"], out_dtype=BF16, name=f"sb_qkv{i}", lead=j)
            s["o"], s["ltot"] = sb_attn_fwd(s["qkv"], vec(sb_q_gain, j), vec(sb_k_gain, j), f"sb_attn_fwd{i}")
            x_mid = mm_cols(s["o"], gathered["sb_w_o"], out_dtype=F32, name=f"sb_out{i}", resid=xc, lead=j)
        else:
            (h,) = rmsnorm_fwd(xc, g_mix, [F32], f"norm_mix{i}")
            prm = tuple(w[n][j] for n in ("ssm_lam_re", "ssm_lam_im", "ssm_log_step", "ssm_b_re", "ssm_b_im",
                                          "ssm_c_re", "ssm_c_im"))
            prep, s["prep_vjp"] = jax.vjp(ssm_prepare, *prm)
            s["prep"] = tuple(a.astype(BF16) for a in prep[:4]) + tuple(prep[4:])
            s["y"], s["yg"] = ssm_core_fwd(h, *s["prep"], vec(whole["ssm_d"], j), f"ssm_fwd{i}")
            s["val"], s["gate"], x_mid = glu_fwd(s["yg"], gathered["ssm_w_glu"], vec(whole["ssm_b_glu"], j), xc,
                                                 f"ssm_glu{i}", lead=j)
        s["x_mid"] = x_mid
        (h2,) = rmsnorm_fwd(x_mid, vec(norm_ffn_g, i), [BF16], f"norm_ffn{i}")
        s["up_val"], s["up_gate"], s["act"] = ffn_up_fused(h2, gathered["ffn_w_up"], whole["ffn_conv_w"][i],
                                                           vec(ffn_conv_b, i), f"ffn_up{i}", lead=i)
        xc = mm_k([s["act"]], gathered["ffn_w_down"], b_nt=False, name=f"ffn_down{i}", resid=x_mid, lead=i)
        saved.append(s)

    dx, dxb, loss_part = loss_head(xc, loss_target[0], "loss_head")
    loss = lax.psum(loss_part[0, 0], ("x", "y", "c"))

    small = {n: [None] * w[n].shape[0] for n in SMALL}
    big_g = {}
    for i in reversed(range(depth)):
        kind, j = i % N_MIXERS, i // N_MIXERS
        s = saved[i]
        g_ffn, g_mix = vec(norm_ffn_g, i), vec(norm_mix_g, i)
        cw, cb = whole["ffn_conv_w"][i], vec(ffn_conv_b, i)
        (h2,) = rmsnorm_fwd(s["x_mid"], g_ffn, [BF16], f"norm_ffn_re{i}")
        dupv, dupg, dcwv, dcwg, dcbv, dcbg = ffn_bwd_fused(dxb, gathered["ffn_w_down"], s["up_val"], s["up_gate"], cw, cb,
                                                           f"ffn_bwd{i}", lead=i)
        big_g["ffn_w_down", i] = [mm_rows(s["act"], dxb, out_dtype=BF16, name=f"ffn_dwdown{i}")]
        big_g["ffn_w_up", i] = [mm_cols(h2, dupv, a_contract=0, out_dtype=BF16, name=f"ffn_dwup_val{i}"),
                                mm_cols(h2, dupg, a_contract=0, out_dtype=BF16, name=f"ffn_dwup_gate{i}")]
        dh2 = mm_k([dupv, dupg], gathered["ffn_w_up"], b_nt=True, name=f"ffn_dh{i}", lead=i)
        dx_mid, dxb_mid, small["norm_ffn_g"][i] = rmsnorm_bwd(s["x_mid"], g_ffn, dh2, dx, f"norm_ffn_bwd{i}")
        small["ffn_conv_w"][i] = jnp.concatenate([dcwv, dcwg], axis=1)[None]
        small["ffn_conv_b"][i] = jnp.concatenate([dcbv, dcbg], axis=1)

        if kind == 0:
            (h,) = rmsnorm_fwd(s["x_in"], g_mix, [F32], f"norm_mix_re{i}")
            dh, dwp, small["pool_b"][j], small["pool_scale"][j] = pool_bwd(
                dx_mid, h, gathered["pool_w"], vec(whole["pool_b"], j), vec(whole["pool_scale"], j), f"pool_bwd{i}", lead=j)
            small["pool_w"][j] = dwp[None]
        elif kind == 1:
            (h,) = rmsnorm_fwd(s["x_in"], g_mix, [BF16], f"norm_mix_re{i}")
            do = mm_cols(dxb_mid, gathered["sb_w_o"], b_nt=True, out_dtype=BF16, name=f"sb_do{i}", lead=j)
            big_g["sb_w_o", j] = [mm_cols(s["o"], dxb_mid, a_contract=0, out_dtype=BF16, name=f"sb_dwo{i}")]
            dq, dk, dv, small["sb_q_gain"][j], small["sb_k_gain"][j] = sb_attn_bwd(
                s["qkv"], s["ltot"], do, vec(sb_q_gain, j), vec(sb_k_gain, j), f"sb_attn_bwd{i}")
            dqkv = jnp.concatenate([dq, dk, dv], axis=1)
            big_g["sb_w_qkv", j] = [mm_cols(h, dqkv, a_contract=0, out_dtype=BF16, name=f"sb_dwqkv{i}")]
            dh = mm_k([dqkv], gathered["sb_w_qkv"], b_nt=True, name=f"sb_dh{i}", lead=j)
        else:
            (h,) = rmsnorm_fwd(s["x_in"], g_mix, [F32], f"norm_mix_re{i}")
            dval, dgate, dbv, dbg = glu_bwd(dx_mid, s["val"], s["gate"], f"ssm_glu_bwd{i}")
            small["ssm_b_glu"][j] = jnp.concatenate([dbv, dbg], axis=1)
            big_g["ssm_w_glu", j] = [mm_cols(s["yg"], dval, a_contract=0, out_dtype=BF16, name=f"ssm_dwglu_val{i}"),
                                     mm_cols(s["yg"], dgate, a_contract=0, out_dtype=BF16, name=f"ssm_dwglu_gate{i}")]
            dyg = mm_k([dval, dgate], gathered["ssm_w_glu"], b_nt=True, name=f"ssm_dyg{i}", lead=j)
            dh, small["ssm_d"][j], *dprep = ssm_core_bwd(h, s["y"], dyg, *s["prep"], vec(whole["ssm_d"], j), f"ssm_bwd{i}")
            dprm = s["prep_vjp"](tuple(dprep))
            for n, g in zip(("ssm_lam_re", "ssm_lam_im", "ssm_log_step", "ssm_b_re", "ssm_b_im", "ssm_c_re", "ssm_c_im"),
                            dprm):
                small[n][j] = g[None]
        dx, dxb, small["norm_mix_g"][i] = rmsnorm_bwd(s["x_in"], g_mix, dh, dx_mid, f"norm_mix_bwd{i}")

    small_full = [jnp.concatenate(small[n], axis=0) for n in SMALL]
    small_sum = dict(zip(SMALL, _unpack(small_allreduce(_pack(small_full), "reduce_vectors"), small_full)))
    grads = {}
    for n in SMALL:
        g = small_sum[n]
        if n in SMALL_SHARDED:
            ax = SMALL_SHARDED[n]
            g = lax.dynamic_slice_in_dim(g, me * w[n].shape[ax], w[n].shape[ax], ax)
        grads[n] = g

    items = [(n, l) for n in big for l in range(w[n].shape[0])]
    halves = []
    for n, l in items:
        h = None
        for a, g in enumerate(big_g[n, l]):
            h = pair_reduce(g, BIG[n], *shard[n], f"grads_pair_{n}{l}_{a}", into=h,
                            first_slot=a * (N_CHIPS // len(big_g[n, l])))
        halves.append(h)
    landed = chip_exchange(halves, "grads_chip_exchange")
    for (n, l), h, lb in zip(items, halves, landed):
        grads[n] = reduce_share(h, lb, me1, l, w[n].shape[0], f"grads_share_{n}{l}", into=grads.get(n))

    delta, new_m, new_v = {}, {}, {}
    for n in big:
        delta[n], new_m[n], new_v[n] = adamw(w[n], grads[n], mom[n], var[n], "adamw_" + n)
    like = [w[n] for n in SMALL]
    packed = [_pack([src[n] for n in SMALL]).reshape(-1, LANES) for src in (w, grads, mom, var)]
    for dst, out in zip((delta, new_m, new_v), adamw(*packed, "adamw_vectors")):
        dst.update(zip(SMALL, _unpack(out, like)))

    return (loss, dx[None], *[grads[n] for n in WEIGHTS], *[delta[n] for n in WEIGHTS],
            *[new_m[n] for n in WEIGHTS], *[new_v[n] for n in WEIGHTS])
```

```python
import functools
import math

import jax
import jax.numpy as jnp
from jax import lax
from jax.experimental import pallas as pl
from jax.experimental.pallas import tpu as pltpu

F32 = jnp.float32
BF16 = jnp.bfloat16

RMS_EPS = 1e-6
POOL_WINDOWS = (2, 4, 8, 16)
SB_HEAD_DIM = 128
SSM_GROUP_CH = 16
SSM_STATE = 64
SSM_BLOCK_GROUPS = 8
ADAM_LR = 0.001
ADAM_B1 = 0.9
ADAM_B2 = 0.999
ADAM_EPS = 1e-08
ADAM_WD = 0.01
ADAM_STEP = 10

V7X_VMEM_BYTES = 64 * 1024 * 1024
VMEM_LIMIT = V7X_VMEM_BYTES - 8 * 1024 * 1024
SUBLANES = 8
LANES = 128
MESH = pl.DeviceIdType.MESH
N_CHIPS = 4
N_DEV = 8


def _params(*sem):
    return pltpu.CompilerParams(dimension_semantics=tuple(sem) if sem else None, vmem_limit_bytes=VMEM_LIMIT)


def _tile(n, want):
    if n <= want:
        return n
    t = (want // LANES) * LANES
    while t > LANES and n % t:
        t -= LANES
    assert n % t == 0, (n, want)
    return t


def _spec(shape, imap, lead=None):
    if lead is None:
        return pl.BlockSpec(tuple(shape), imap)
    return pl.BlockSpec((None,) + tuple(shape), lambda *a: (lead,) + tuple(imap(*a)))


def _sigmoid(v):
    return 1.0 / (1.0 + jnp.exp(-v))


def _shift_down(v, k):
    return pltpu.roll(v, k, 0)


def _shift_up(v, k):
    return pltpu.roll(v, v.shape[0] - k, 0)


def rmsnorm_fwd(x, g, out_dtypes, name):
    t, d = x.shape
    tr = min(t, 128)

    def body(x_ref, g_ref, *o_refs):
        xv = x_ref[...]
        r = lax.rsqrt(jnp.mean(xv * xv, axis=-1, keepdims=True) + RMS_EPS)
        h = xv * r * g_ref[...]
        for o in o_refs:
            o[...] = h.astype(o.dtype)

    outs = pl.pallas_call(
        body, name=name, grid=(t // tr,),
        in_specs=[pl.BlockSpec((tr, d), lambda i: (i, 0)), pl.BlockSpec((1, d), lambda i: (0, 0))],
        out_specs=[pl.BlockSpec((tr, d), lambda i: (i, 0)) for _ in out_dtypes],
        out_shape=[jax.ShapeDtypeStruct((t, d), dt) for dt in out_dtypes],
        compiler_params=_params("parallel"),
    )(x, g)
    return outs


def rmsnorm_bwd(x, g, dh, dres, name):
    t, d = x.shape
    tr = min(t, 128)

    def body(x_ref, g_ref, dh_ref, dres_ref, dx_ref, dxb_ref, dg_ref):
        xv = x_ref[...]
        r = lax.rsqrt(jnp.mean(xv * xv, axis=-1, keepdims=True) + RMS_EPS)
        xhat = xv * r
        dhv = dh_ref[...]
        dxhat = dhv * g_ref[...]
        dx = dres_ref[...] + r * (dxhat - xhat * jnp.mean(dxhat * xhat, axis=-1, keepdims=True))
        dx_ref[...] = dx
        dxb_ref[...] = dx.astype(BF16)
        part = jnp.sum(dhv * xhat, axis=0, keepdims=True)

        @pl.when(pl.program_id(0) == 0)
        def _():
            dg_ref[...] = part

        @pl.when(pl.program_id(0) != 0)
        def _():
            dg_ref[...] += part

    row = pl.BlockSpec((tr, d), lambda i: (i, 0))
    vec = pl.BlockSpec((1, d), lambda i: (0, 0))
    return pl.pallas_call(
        body, name=name, grid=(t // tr,),
        in_specs=[row, vec, row, row],
        out_specs=[row, row, vec],
        out_shape=[jax.ShapeDtypeStruct((t, d), F32), jax.ShapeDtypeStruct((t, d), BF16),
                   jax.ShapeDtypeStruct((1, d), F32)],
        compiler_params=_params("arbitrary"),
    )(x, g, dh, dres)


def mm_cols(a, b, *, a_contract=1, b_nt=False, out_dtype, name, resid=None, tn=512, lead=None):
    m = a.shape[1 - a_contract]
    k = a.shape[a_contract]
    n = b.shape[-2] if b_nt else b.shape[-1]
    assert (b.shape[-1] if b_nt else b.shape[-2]) == k
    tn = _tile(n, tn)

    def body(a_ref, b_ref, *rest):
        o_ref = rest[-1]
        dn = (((a_contract,), (1 if b_nt else 0,)), ((), ()))
        acc = lax.dot_general(a_ref[...], b_ref[...], dn, preferred_element_type=F32)
        if resid is not None:
            acc = acc + rest[0][...]
        o_ref[...] = acc.astype(o_ref.dtype)

    in_specs = [pl.BlockSpec(a.shape, lambda j: (0, 0)),
                _spec((tn, k), lambda j: (j, 0), lead) if b_nt else _spec((k, tn), lambda j: (0, j), lead)]
    args = [a, b]
    if resid is not None:
        in_specs.append(pl.BlockSpec((m, tn), lambda j: (0, j)))
        args.append(resid)
    return pl.pallas_call(
        body, name=name, grid=(n // tn,), in_specs=in_specs,
        out_specs=pl.BlockSpec((m, tn), lambda j: (0, j)),
        out_shape=jax.ShapeDtypeStruct((m, n), out_dtype),
        compiler_params=_params("parallel"),
    )(*args)


def mm_rows(st, res, *, out_dtype, name, tn=512):
    k, m = st.shape
    n = res.shape[1]
    assert res.shape[0] == k
    tn = _tile(m, tn)

    def body(st_ref, res_ref, o_ref):
        o_ref[...] = lax.dot_general(st_ref[...], res_ref[...], (((0,), (0,)), ((), ())),
                                     preferred_element_type=F32).astype(o_ref.dtype)

    return pl.pallas_call(
        body, name=name, grid=(m // tn,),
        in_specs=[pl.BlockSpec((k, tn), lambda j: (0, j)), pl.BlockSpec((k, n), lambda j: (0, 0))],
        out_specs=pl.BlockSpec((tn, n), lambda j: (j, 0)),
        out_shape=jax.ShapeDtypeStruct((m, n), out_dtype),
        compiler_params=_params("parallel"),
    )(st, res)


def mm_k(a_list, b, *, b_nt, name, resid=None, tk=512, tnn=1024, lead=None):
    m = a_list[0].shape[0]
    ks = [a.shape[1] for a in a_list]
    ktot = sum(ks)
    n = b.shape[-2] if b_nt else b.shape[-1]
    assert (b.shape[-1] if b_nt else b.shape[-2]) == ktot
    tk = _tile(ks[0], tk)
    assert all(kk % tk == 0 for kk in ks)
    tnn = _tile(n, tnn)
    nks = [kk // tk for kk in ks]
    starts = [sum(nks[:i]) for i in range(len(nks))]
    nk = sum(nks)

    def body(*refs):
        a_refs = refs[:len(a_list)]
        b_ref = refs[len(a_list)]
        o_ref = refs[-1]
        kk = pl.program_id(1)

        @pl.when(kk == 0)
        def _():
            if resid is not None:
                o_ref[...] = refs[len(a_list) + 1][...]
            else:
                o_ref[...] = jnp.zeros_like(o_ref)

        dn = (((1,), (1 if b_nt else 0,)), ((), ()))
        for i, a_ref in enumerate(a_refs):
            @pl.when(jnp.logical_and(kk >= starts[i], kk < starts[i] + nks[i]))
            def _(a_ref=a_ref):
                o_ref[...] += lax.dot_general(a_ref[...], b_ref[...], dn, preferred_element_type=F32)

    def a_spec(i):
        return pl.BlockSpec((m, tk), lambda nn, kk: (0, jnp.clip(kk - starts[i], 0, nks[i] - 1)))

    in_specs = [a_spec(i) for i in range(len(a_list))]
    in_specs.append(_spec((tnn, tk), lambda nn, kk: (nn, kk), lead) if b_nt
                    else _spec((tk, tnn), lambda nn, kk: (kk, nn), lead))
    args = list(a_list) + [b]
    if resid is not None:
        in_specs.append(pl.BlockSpec((m, tnn), lambda nn, kk: (0, nn)))
        args.append(resid)
    return pl.pallas_call(
        body, name=name, grid=(n // tnn, nk), in_specs=in_specs,
        out_specs=pl.BlockSpec((m, tnn), lambda nn, kk: (0, nn)),
        out_shape=jax.ShapeDtypeStruct((m, n), F32),
        compiler_params=_params("parallel", "arbitrary"),
    )(*args)


HALO = SUBLANES
CHUNK_ROWS = 64


def _conv_taps(ext, r):
    return ext[HALO:], _shift_down(ext, 1)[HALO:], _shift_down(ext, 2)[HALO:]


def ffn_up_fused(h, w_up, conv_w, conv_b, name, lead=None):
    t, d = h.shape
    f = w_up.shape[-1] // 2
    tn = _tile(f, 256)
    nf = f // tn
    r = min(CHUNK_ROWS, t)

    def body(h_ref, wv_ref, wg_ref, cwv_ref, cwg_ref, cbv_ref, cbg_ref, uv_ref, ug_ref, act_ref, sv, sg):
        zero = jnp.zeros((HALO, tn), F32)
        sv[0:HALO, :] = zero
        sg[0:HALO, :] = zero
        hv = h_ref[...]
        sv[HALO:, :] = jnp.dot(hv, wv_ref[...], preferred_element_type=F32).astype(BF16).astype(F32)
        sg[HALO:, :] = jnp.dot(hv, wg_ref[...], preferred_element_type=F32).astype(BF16).astype(F32)
        cwv, cwg = cwv_ref[...], cwg_ref[...]
        cbv, cbg = cbv_ref[...], cbg_ref[...]

        def chunk(i, carry):
            r0 = pl.multiple_of(i * r, r)
            v0, v1, v2 = _conv_taps(sv[pl.ds(r0, r + HALO), :], r)
            g0, g1, g2 = _conv_taps(sg[pl.ds(r0, r + HALO), :], r)
            cval = cbv + cwv[2:3] * v0 + cwv[1:2] * v1 + cwv[0:1] * v2
            cgate = cbg + cwg[2:3] * g0 + cwg[1:2] * g1 + cwg[0:1] * g2
            uv_ref[pl.ds(r0, r), :] = v0.astype(BF16)
            ug_ref[pl.ds(r0, r), :] = g0.astype(BF16)
            act_ref[pl.ds(r0, r), :] = (cgate * _sigmoid(cgate) * cval).astype(BF16)
            return carry

        lax.fori_loop(0, t // r, chunk, 0)

    col = lambda off: _spec((d, tn), lambda j: (0, j + off), lead)
    cw = lambda off: pl.BlockSpec((3, tn), lambda j: (0, j + off))
    cb = lambda off: pl.BlockSpec((1, tn), lambda j: (0, j + off))
    out = pl.BlockSpec((t, tn), lambda j: (0, j))
    return pl.pallas_call(
        body, name=name, grid=(nf,),
        in_specs=[pl.BlockSpec((t, d), lambda j: (0, 0)), col(0), col(nf), cw(0), cw(nf), cb(0), cb(nf)],
        out_specs=[out, out, out],
        out_shape=[jax.ShapeDtypeStruct((t, f), BF16)] * 3,
        scratch_shapes=[pltpu.VMEM((t + HALO, tn), F32), pltpu.VMEM((t + HALO, tn), F32)],
        compiler_params=_params("parallel"),
    )(h, w_up, w_up, conv_w, conv_w, conv_b, conv_b)


def ffn_bwd_fused(dout, w_down, up_val, up_gate, conv_w, conv_b, name, lead=None):
    t, d = dout.shape
    f = w_down.shape[-2]
    tn = _tile(f, 256)
    nf = f // tn
    r = min(CHUNK_ROWS, t)

    def body(do_ref, wd_ref, uv_ref, ug_ref, cwv_ref, cwg_ref, cbv_ref, cbg_ref,
             dv_ref, dg_ref, dcwv_ref, dcwg_ref, dcbv_ref, dcbg_ref, da, sv, sg, ev, eg):
        zero = jnp.zeros((HALO, tn), F32)
        sv[0:HALO, :] = zero
        sg[0:HALO, :] = zero
        ev[t:, :] = zero
        eg[t:, :] = zero
        da[...] = lax.dot_general(do_ref[...], wd_ref[...], (((1,), (1,)), ((), ())), preferred_element_type=F32)
        sv[HALO:, :] = uv_ref[...].astype(F32)
        sg[HALO:, :] = ug_ref[...].astype(F32)
        cwv, cwg = cwv_ref[...], cwg_ref[...]
        cbv, cbg = cbv_ref[...], cbg_ref[...]

        def chunk(i, acc):
            r0 = pl.multiple_of(i * r, r)
            v = _conv_taps(sv[pl.ds(r0, r + HALO), :], r)
            g = _conv_taps(sg[pl.ds(r0, r + HALO), :], r)
            cval = cbv + cwv[2:3] * v[0] + cwv[1:2] * v[1] + cwv[0:1] * v[2]
            cgate = cbg + cwg[2:3] * g[0] + cwg[1:2] * g[1] + cwg[0:1] * g[2]
            s = _sigmoid(cgate)
            dav = da[pl.ds(r0, r), :]
            dval = dav * (cgate * s)
            dgate = dav * cval * (s * (1.0 + cgate * (1.0 - s)))
            ev[pl.ds(r0, r), :] = dval
            eg[pl.ds(r0, r), :] = dgate
            col = lambda z: jnp.sum(z, axis=0, keepdims=True)
            new = [acc[0] + col(dval), acc[1] + col(dgate)]
            new += [acc[2 + j] + col(dval * v[2 - j]) for j in range(3)]
            new += [acc[5 + j] + col(dgate * g[2 - j]) for j in range(3)]
            return tuple(new)

        z1 = jnp.zeros((1, tn), F32)
        acc = lax.fori_loop(0, t // r, chunk, (z1,) * 8)
        dcbv_ref[...] = acc[0]
        dcbg_ref[...] = acc[1]
        for j in range(3):
            dcwv_ref[j:j + 1, :] = acc[2 + j]
            dcwg_ref[j:j + 1, :] = acc[5 + j]

        def chunk2(i, carry):
            r0 = pl.multiple_of(i * r, r)
            for e_ref, cw_, o_ref in ((ev, cwv, dv_ref), (eg, cwg, dg_ref)):
                ext = e_ref[pl.ds(r0, r + HALO), :]
                d0, d1, d2 = ext[:r], _shift_up(ext, 1)[:r], _shift_up(ext, 2)[:r]
                o_ref[pl.ds(r0, r), :] = (cw_[2:3] * d0 + cw_[1:2] * d1 + cw_[0:1] * d2).astype(BF16)
            return carry

        lax.fori_loop(0, t // r, chunk2, 0)

    cw = lambda off: pl.BlockSpec((3, tn), lambda j: (0, j + off))
    cb = lambda off: pl.BlockSpec((1, tn), lambda j: (0, j + off))
    tile = pl.BlockSpec((t, tn), lambda j: (0, j))
    s = lambda rows, dt: jax.ShapeDtypeStruct((rows, f), dt)
    halo = pltpu.VMEM((t + HALO, tn), F32)
    return pl.pallas_call(
        body, name=name, grid=(nf,),
        in_specs=[pl.BlockSpec((t, d), lambda j: (0, 0)), _spec((tn, d), lambda j: (j, 0), lead),
                  tile, tile, cw(0), cw(nf), cb(0), cb(nf)],
        out_specs=[tile, tile, pl.BlockSpec((3, tn), lambda j: (0, j)), pl.BlockSpec((3, tn), lambda j: (0, j)),
                   pl.BlockSpec((1, tn), lambda j: (0, j)), pl.BlockSpec((1, tn), lambda j: (0, j))],
        out_shape=[s(t, BF16), s(t, BF16), s(3, F32), s(3, F32), s(1, F32), s(1, F32)],
        scratch_shapes=[pltpu.VMEM((t, tn), F32), halo, halo, halo, halo],
        compiler_params=_params("parallel"),
    )(dout, w_down, up_val, up_gate, conv_w, conv_w, conv_b, conv_b)


POOL_PAD = max(POOL_WINDOWS)


def _window_sum(ext, win, shift):
    assert POOL_WINDOWS == (2, 4, 8, 16)
    s2 = ext + shift(ext, 1)
    s4 = s2 + shift(s2, 2)
    s8 = s4 + shift(s4, 4)
    s16 = s8 + shift(s8, 8)
    return jnp.where(win == 2, s2, jnp.where(win == 4, s4, jnp.where(win == 8, s8, s16)))


def _pool_win_scalar(g):
    win = jnp.int32(POOL_WINDOWS[-1])
    for k in range(len(POOL_WINDOWS) - 2, -1, -1):
        win = jnp.where(g == k, jnp.int32(POOL_WINDOWS[k]), win)
    return win


def _pool_count(r0, r, win):
    rows = r0 + lax.broadcasted_iota(jnp.int32, (r, 1), 0)
    return jnp.minimum(rows + 1, win).astype(F32)


def _pooled_into(hp, pooled, h_ref, t, r, win):
    hp[0:POOL_PAD, :] = jnp.zeros((POOL_PAD, hp.shape[1]), F32)
    hp[POOL_PAD:, :] = h_ref[...]

    def chunk(i, carry):
        r0 = pl.multiple_of(i * r, r)
        ext = hp[pl.ds(r0, r + POOL_PAD), :]
        s = _window_sum(ext, win, _shift_down)[POOL_PAD:]
        pooled[pl.ds(r0, r), :] = (s / _pool_count(r0, r, win) - ext[POOL_PAD:]).astype(BF16)
        return carry

    lax.fori_loop(0, t // r, chunk, 0)


def pool_fwd(h, x, w, b, scale, name, lead=None):
    t, d = h.shape
    ng, dg = w.shape[-3], w.shape[-2]
    r = min(CHUNK_ROWS, t)

    def body(h_ref, x_ref, w_ref, b_ref, s_ref, o_ref, hp, pooled):
        win = _pool_win_scalar(pl.program_id(0))
        _pooled_into(hp, pooled, h_ref, t, r, win)
        y = jnp.dot(pooled[...], w_ref[...], preferred_element_type=F32)
        o_ref[...] = x_ref[...] + (y + b_ref[...]) * s_ref[...]

    col = pl.BlockSpec((t, dg), lambda g: (0, g))
    vec = pl.BlockSpec((1, dg), lambda g: (0, g))
    return pl.pallas_call(
        body, name=name, grid=(ng,),
        in_specs=[col, col, _spec((None, dg, dg), lambda g: (g, 0, 0), lead), vec, vec],
        out_specs=col, out_shape=jax.ShapeDtypeStruct((t, d), F32),
        scratch_shapes=[pltpu.VMEM((t + POOL_PAD, dg), F32), pltpu.VMEM((t, dg), BF16)],
        compiler_params=_params("parallel"),
    )(h, x, w, b, scale)


def pool_bwd(dm, h, w, b, scale, name, lead=None):
    t, d = h.shape
    ng, dg = w.shape[-3], w.shape[-2]
    r = min(CHUNK_ROWS, t)

    def body(dm_ref, h_ref, w_ref, b_ref, s_ref, dh_ref, dw_ref, db_ref, ds_ref, hp, pooled, q):
        win = _pool_win_scalar(pl.program_id(0))
        _pooled_into(hp, pooled, h_ref, t, r, win)
        wv = w_ref[...]
        y = jnp.dot(pooled[...], wv, preferred_element_type=F32)
        dmv = dm_ref[...]
        ds_ref[...] = jnp.sum(dmv * (y + b_ref[...]), axis=0, keepdims=True)
        dy = dmv * s_ref[...]
        db_ref[...] = jnp.sum(dy, axis=0, keepdims=True)
        dyb = dy.astype(BF16)
        dw_ref[...] = lax.dot_general(pooled[...], dyb, (((0,), (0,)), ((), ())),
                                      preferred_element_type=F32).astype(dw_ref.dtype)
        dp = lax.dot_general(dyb, wv, (((1,), (1,)), ((), ())), preferred_element_type=F32)
        q[t:, :] = jnp.zeros((POOL_PAD, dg), F32)
        q[0:t, :] = dp / _pool_count(0, t, win)
        dh_ref[...] = -dp

        def chunk(i, carry):
            r0 = pl.multiple_of(i * r, r)
            ext = q[pl.ds(r0, r + POOL_PAD), :]
            dh_ref[pl.ds(r0, r), :] += _window_sum(ext, win, _shift_up)[:r]
            return carry

        lax.fori_loop(0, t // r, chunk, 0)

    col = pl.BlockSpec((t, dg), lambda g: (0, g))
    vec = pl.BlockSpec((1, dg), lambda g: (0, g))
    wspec = pl.BlockSpec((None, dg, dg), lambda g: (g, 0, 0))
    return pl.pallas_call(
        body, name=name, grid=(ng,),
        in_specs=[col, col, _spec((None, dg, dg), lambda g: (g, 0, 0), lead), vec, vec],
        out_specs=[col, wspec, vec, vec],
        out_shape=[jax.ShapeDtypeStruct((t, d), F32), jax.ShapeDtypeStruct((ng, dg, dg), F32),
                   jax.ShapeDtypeStruct((1, d), F32), jax.ShapeDtypeStruct((1, d), F32)],
        scratch_shapes=[pltpu.VMEM((t + POOL_PAD, dg), F32), pltpu.VMEM((t, dg), BF16),
                        pltpu.VMEM((t + POOL_PAD, dg), F32)],
        compiler_params=_params("parallel"),
    )(dm, h, w, b, scale)


SB_BLOCK = 256


def _tri_sum(v, tri):
    hi = v.astype(BF16)
    r1 = v - hi.astype(F32)
    mid = r1.astype(BF16)
    lo = (r1 - mid.astype(F32)).astype(BF16)
    dot = lambda p: jnp.dot(p, tri, preferred_element_type=F32)
    return dot(hi) + dot(mid) + dot(lo)


def _tri(bk, cmp):
    return cmp(lax.broadcasted_iota(jnp.int32, (bk, bk), 0), lax.broadcasted_iota(jnp.int32, (bk, bk), 1)).astype(BF16)


def _sb_logits(qblk, kblk, q0, k0, inv):
    bq, bk = qblk.shape[0], kblk.shape[0]
    z = lax.dot_general(qblk, kblk, (((1,), (1,)), ((), ())), preferred_element_type=F32) * inv
    qpos = q0 + lax.broadcasted_iota(jnp.int32, (bq, bk), 0)
    kpos = k0 + lax.broadcasted_iota(jnp.int32, (bq, bk), 1)
    mask = kpos < qpos
    lb = jnp.minimum(z, 0.0) - jnp.log(1.0 + jnp.exp(-jnp.abs(z)))
    lm = jnp.where(mask, lb - z, 0.0)
    return lb, lm, mask


def _head_norm(ref, gain):
    xv = ref[...].astype(F32)
    r = lax.rsqrt(jnp.mean(xv * xv, axis=-1, keepdims=True) + RMS_EPS)
    xhat = xv * r
    return xhat, r, (xhat * gain).astype(BF16)


def sb_attn_fwd(qkv, q_gain, k_gain, name):
    t = qkv.shape[0]
    d = qkv.shape[1] // 3
    dh = SB_HEAD_DIM
    nh = d // dh
    blk = min(SB_BLOCK, t)
    inv = 1.0 / math.sqrt(dh)

    def body(q_ref, k_ref, v_ref, qg_ref, kg_ref, o_ref, lt_ref, qn, kn):
        qn[...] = _head_norm(q_ref, qg_ref[...])[2]
        kn[...] = _head_norm(k_ref, kg_ref[...])[2]
        later = _tri(blk, lambda j, s: j > s)

        def q_loop(qb, carry):
            q0 = pl.multiple_of(qb * blk, blk)
            qblk = qn[pl.ds(q0, blk), :]

            def k_loop(i, st):
                c, acc = st
                k0 = pl.multiple_of((qb - i) * blk, blk)
                lb, lm, mask = _sb_logits(qblk, kn[pl.ds(k0, blk), :], q0, k0, inv)
                a = jnp.where(mask, jnp.exp(lb + _tri_sum(lm, later) + c), 0.0)
                acc = acc + jnp.dot(a.astype(BF16), v_ref[pl.ds(k0, blk), :], preferred_element_type=F32)
                return c + jnp.sum(lm, axis=1, keepdims=True), acc

            c, acc = lax.fori_loop(0, qb + 1, k_loop, (jnp.zeros((blk, 1), F32), jnp.zeros((blk, dh), F32)))
            o_ref[pl.ds(q0, blk), :] = acc.astype(BF16)
            lt_ref[pl.ds(q0, blk), :] = c
            return carry

        lax.fori_loop(0, t // blk, q_loop, 0)

    head = lambda off: pl.BlockSpec((t, dh), lambda h: (0, h + off))
    gain = pl.BlockSpec((1, dh), lambda h: (0, 0))
    return pl.pallas_call(
        body, name=name, grid=(nh,),
        in_specs=[head(0), head(nh), head(2 * nh), gain, gain],
        out_specs=[head(0), pl.BlockSpec((None, t, 1), lambda h: (h, 0, 0))],
        out_shape=[jax.ShapeDtypeStruct((t, d), BF16), jax.ShapeDtypeStruct((nh, t, 1), F32)],
        scratch_shapes=[pltpu.VMEM((t, dh), BF16), pltpu.VMEM((t, dh), BF16)],
        compiler_params=_params("parallel"),
    )(qkv, qkv, qkv, q_gain, k_gain)


def sb_attn_bwd(qkv, ltot, do, q_gain, k_gain, name):
    t = qkv.shape[0]
    d = qkv.shape[1] // 3
    dh = SB_HEAD_DIM
    nh = d // dh
    blk = min(SB_BLOCK, t)
    inv = 1.0 / math.sqrt(dh)
    tn_dims = (((0,), (0,)), ((), ()))

    def body(q_ref, k_ref, v_ref, lt_ref, do_ref, qg_ref, kg_ref, dq_ref, dk_ref, dv_ref, dqg_ref, dkg_ref,
             qn, kn, dqn, dkn, dvn):
        qg, kg = qg_ref[...], kg_ref[...]
        qhat, rq, qnb = _head_norm(q_ref, qg)
        khat, rk, knb = _head_norm(k_ref, kg)
        qn[...] = qnb
        kn[...] = knb
        dkn[...] = jnp.zeros_like(dkn)
        dvn[...] = jnp.zeros_like(dvn)
        upto = _tri(blk, lambda j, s: j <= s)
        before = _tri(blk, lambda j, s: j < s)

        def q_loop(qb, carry):
            q0 = pl.multiple_of(qb * blk, blk)
            qblk = qn[pl.ds(q0, blk), :]
            doblk = do_ref[pl.ds(q0, blk), :]
            ltv = lt_ref[pl.ds(q0, blk), :]

            def k_loop(kb, st):
                pl_, pg, dq = st
                k0 = pl.multiple_of(kb * blk, blk)
                kblk = kn[pl.ds(k0, blk), :]
                lb, lm, mask = _sb_logits(qblk, kblk, q0, k0, inv)
                a = jnp.where(mask, jnp.exp(lb + (ltv - pl_ - _tri_sum(lm, upto))), 0.0)
                da = lax.dot_general(doblk, v_ref[pl.ds(k0, blk), :], (((1,), (1,)), ((), ())),
                                     preferred_element_type=F32)
                g = da * a
                g_before = pg + _tri_sum(g, before)
                beta = jnp.exp(lb)
                dz = (jnp.where(mask, g * (1.0 - beta) - beta * g_before, 0.0) * inv).astype(BF16)
                dq = dq + jnp.dot(dz, kblk, preferred_element_type=F32)
                dkn[pl.ds(k0, blk), :] += lax.dot_general(dz, qblk, tn_dims, preferred_element_type=F32)
                dvn[pl.ds(k0, blk), :] += lax.dot_general(a.astype(BF16), doblk, tn_dims, preferred_element_type=F32)
                return pl_ + jnp.sum(lm, axis=1, keepdims=True), pg + jnp.sum(g, axis=1, keepdims=True), dq

            z1 = jnp.zeros((blk, 1), F32)
            _, _, dq = lax.fori_loop(0, qb + 1, k_loop, (z1, z1, jnp.zeros((blk, dh), F32)))
            dqn[pl.ds(q0, blk), :] = dq
            return carry

        lax.fori_loop(0, t // blk, q_loop, 0)

        first = pl.program_id(0) == 0
        for dn, xhat, r, gain, out_ref, dgain_ref in ((dqn, qhat, rq, qg, dq_ref, dqg_ref),
                                                      (dkn, khat, rk, kg, dk_ref, dkg_ref)):
            dnv = dn[...]
            dxhat = dnv * gain
            out_ref[...] = (r * (dxhat - xhat * jnp.mean(dxhat * xhat, axis=-1, keepdims=True))).astype(BF16)
            part = jnp.sum(dnv * xhat, axis=0, keepdims=True)

            @pl.when(first)
            def _(dgain_ref=dgain_ref, part=part):
                dgain_ref[...] = part

            @pl.when(jnp.logical_not(first))
            def _(dgain_ref=dgain_ref, part=part):
                dgain_ref[...] += part

        dv_ref[...] = dvn[...].astype(BF16)

    head = lambda off: pl.BlockSpec((t, dh), lambda h: (0, h + off))
    gain = pl.BlockSpec((1, dh), lambda h: (0, 0))
    big = jax.ShapeDtypeStruct((t, d), BF16)
    small = jax.ShapeDtypeStruct((1, dh), F32)
    return pl.pallas_call(
        body, name=name, grid=(nh,),
        in_specs=[head(0), head(nh), head(2 * nh), pl.BlockSpec((None, t, 1), lambda h: (h, 0, 0)), head(0),
                  gain, gain],
        out_specs=[head(0), head(0), head(0), gain, gain],
        out_shape=[big, big, big, small, small],
        scratch_shapes=[pltpu.VMEM((t, dh), BF16), pltpu.VMEM((t, dh), BF16),
                        pltpu.VMEM((t, dh), F32), pltpu.VMEM((t, dh), F32), pltpu.VMEM((t, dh), F32)],
        compiler_params=_params("arbitrary"),
    )(qkv, qkv, qkv, ltot, do, q_gain, k_gain)


GELU_C = math.sqrt(2.0 / math.pi)
GELU_A = 0.044715
SCAN_ROWS = SUBLANES


def _gelu(y):
    return 0.5 * y * (1.0 + jnp.tanh(GELU_C * (y + GELU_A * y * y * y)))


def _gelu_grad(y):
    th = jnp.tanh(GELU_C * (y + GELU_A * y * y * y))
    return 0.5 * (1.0 + th) + 0.5 * y * (1.0 - th * th) * GELU_C * (1.0 + 3.0 * GELU_A * y * y)


def _powers(ar, ai):
    out = [(ar, ai)]
    for _ in range(SCAN_ROWS - 1):
        pr, pi = out[-1]
        out.append((pr * ar - pi * ai, pr * ai + pi * ar))
    return out


def _rows(vals):
    c = vals[0].shape[1]
    row = lax.broadcasted_iota(jnp.int32, (SCAN_ROWS, c), 0)
    out = jnp.broadcast_to(vals[SCAN_ROWS - 1], (SCAN_ROWS, c))
    for j in range(SCAN_ROWS - 2, -1, -1):
        out = jnp.where(row == j, vals[j], out)
    return out


def _scan_forward(sr, si, off, t, ar, ai):
    c = ar.shape[1]
    p = _powers(ar, ai)
    pwr = _rows([q[0] for q in p])
    pwi = _rows([q[1] for q in p])
    row = lax.broadcasted_iota(jnp.int32, (SCAN_ROWS, c), 0)

    def tile(i, carry):
        cr, ci = carry
        r0 = pl.multiple_of(off + i * SCAN_ROWS, SCAN_ROWS)
        xr = sr[pl.ds(r0, SCAN_ROWS), :]
        xi = si[pl.ds(r0, SCAN_ROWS), :]
        for k in (1, 2, 4):
            pr, pi = p[k - 1]
            shr = jnp.where(row >= k, _shift_down(xr, k), 0.0)
            shi = jnp.where(row >= k, _shift_down(xi, k), 0.0)
            xr, xi = xr + pr * shr - pi * shi, xi + pr * shi + pi * shr
        xr, xi = xr + pwr * cr - pwi * ci, xi + pwr * ci + pwi * cr
        sr[pl.ds(r0, SCAN_ROWS), :] = xr
        si[pl.ds(r0, SCAN_ROWS), :] = xi
        return xr[SCAN_ROWS - 1:SCAN_ROWS], xi[SCAN_ROWS - 1:SCAN_ROWS]

    z = jnp.zeros((1, c), F32)
    lax.fori_loop(0, t // SCAN_ROWS, tile, (z, z))


def _scan_reverse(gr, gi, t, ar, ai, xr_ref, xi_ref):
    c = ar.shape[1]
    p = _powers(ar, ai)
    pwr = _rows([p[SCAN_ROWS - 1 - j][0] for j in range(SCAN_ROWS)])
    pwi = _rows([p[SCAN_ROWS - 1 - j][1] for j in range(SCAN_ROWS)])
    row = lax.broadcasted_iota(jnp.int32, (SCAN_ROWS, c), 0)
    n = t // SCAN_ROWS

    def tile(ii, carry):
        cr, ci, dar, dai = carry
        r0 = pl.multiple_of((n - 1 - ii) * SCAN_ROWS, SCAN_ROWS)
        xr = gr[pl.ds(r0, SCAN_ROWS), :]
        xi = gi[pl.ds(r0, SCAN_ROWS), :]
        for k in (1, 2, 4):
            pr, pi = p[k - 1]
            shr = jnp.where(row < SCAN_ROWS - k, _shift_up(xr, k), 0.0)
            shi = jnp.where(row < SCAN_ROWS - k, _shift_up(xi, k), 0.0)
            xr, xi = xr + pr * shr + pi * shi, xi + pr * shi - pi * shr
        xr, xi = xr + pwr * cr + pwi * ci, xi + pwr * ci - pwi * cr
        gr[pl.ds(r0, SCAN_ROWS), :] = xr
        gi[pl.ds(r0, SCAN_ROWS), :] = xi
        xpr = _shift_down(xr_ref[pl.ds(r0, 2 * SCAN_ROWS), :], 1)[SCAN_ROWS:]
        xpi = _shift_down(xi_ref[pl.ds(r0, 2 * SCAN_ROWS), :], 1)[SCAN_ROWS:]
        return xr[0:1], xi[0:1], dar + xr * xpr + xi * xpi, dai + xi * xpr - xr * xpi

    z = jnp.zeros((1, c), F32)
    z8 = jnp.zeros((SCAN_ROWS, c), F32)
    _, _, dar, dai = lax.fori_loop(0, n, tile, (z, z, z8, z8))
    return jnp.sum(dar, axis=0, keepdims=True), jnp.sum(dai, axis=0, keepdims=True)


def _ssm_specs(t, nb, ch, st):
    col = pl.BlockSpec((t, ch), lambda b: (0, b))
    vec = pl.BlockSpec((1, ch), lambda b: (0, b))
    bspec = pl.BlockSpec((None, ch, st), lambda b: (b, 0, 0))
    cspec = pl.BlockSpec((None, st, ch), lambda b: (b, 0, 0))
    aspec = pl.BlockSpec((None, 1, st), lambda b: (b, 0, 0))
    return col, vec, bspec, cspec, aspec


def ssm_core_fwd(u, bre, bim, cre, cim, a_re, a_im, dskip, name):
    t, d = u.shape
    nb, ch, st = bre.shape

    def body(u_ref, bre_ref, bim_ref, cre_ref, cim_ref, ar_ref, ai_ref, d_ref, y_ref, yg_ref, sr, si):
        uv = u_ref[...]
        ub = uv.astype(BF16)
        sr[...] = jnp.dot(ub, bre_ref[...], preferred_element_type=F32)
        si[...] = jnp.dot(ub, bim_ref[...], preferred_element_type=F32)
        _scan_forward(sr, si, 0, t, ar_ref[...], ai_ref[...])
        y = (jnp.dot(sr[...].astype(BF16), cre_ref[...], preferred_element_type=F32)
             - jnp.dot(si[...].astype(BF16), cim_ref[...], preferred_element_type=F32) + d_ref[...] * uv)
        y_ref[...] = y
        yg_ref[...] = _gelu(y).astype(BF16)

    col, vec, bspec, cspec, aspec = _ssm_specs(t, nb, ch, st)
    return pl.pallas_call(
        body, name=name, grid=(nb,),
        in_specs=[col, bspec, bspec, cspec, cspec, aspec, aspec, vec],
        out_specs=[col, col],
        out_shape=[jax.ShapeDtypeStruct((t, d), F32), jax.ShapeDtypeStruct((t, d), BF16)],
        scratch_shapes=[pltpu.VMEM((t, st), F32), pltpu.VMEM((t, st), F32)],
        compiler_params=_params("parallel"),
    )(u, bre, bim, cre, cim, a_re, a_im, dskip)


def ssm_core_bwd(u, y, dyg, bre, bim, cre, cim, a_re, a_im, dskip, name):
    t, d = u.shape
    nb, ch, st = bre.shape
    tn_dims = (((0,), (0,)), ((), ()))
    nt_dims = (((1,), (1,)), ((), ()))

    def body(u_ref, y_ref, dyg_ref, bre_ref, bim_ref, cre_ref, cim_ref, ar_ref, ai_ref, d_ref,
             du_ref, dd_ref, dbre_ref, dbim_ref, dcre_ref, dcim_ref, dar_ref, dai_ref, xr, xi, gr, gi):
        uv = u_ref[...]
        ub = uv.astype(BF16)
        ar, ai = ar_ref[...], ai_ref[...]
        dy = dyg_ref[...] * _gelu_grad(y_ref[...])
        dd_ref[...] = jnp.sum(dy * uv, axis=0, keepdims=True)
        zero = jnp.zeros((HALO, st), F32)
        xr[0:HALO, :] = zero
        xi[0:HALO, :] = zero
        xr[HALO:, :] = jnp.dot(ub, bre_ref[...], preferred_element_type=F32)
        xi[HALO:, :] = jnp.dot(ub, bim_ref[...], preferred_element_type=F32)
        _scan_forward(xr, xi, HALO, t, ar, ai)
        dyb = dy.astype(BF16)
        dcre_ref[...] = lax.dot_general(xr[HALO:, :].astype(BF16), dyb, tn_dims, preferred_element_type=F32)
        dcim_ref[...] = -lax.dot_general(xi[HALO:, :].astype(BF16), dyb, tn_dims, preferred_element_type=F32)
        gr[...] = lax.dot_general(dyb, cre_ref[...], nt_dims, preferred_element_type=F32)
        gi[...] = -lax.dot_general(dyb, cim_ref[...], nt_dims, preferred_element_type=F32)
        dar, dai = _scan_reverse(gr, gi, t, ar, ai, xr, xi)
        dar_ref[...] = dar
        dai_ref[...] = dai
        grb = gr[...].astype(BF16)
        gib = gi[...].astype(BF16)
        dbre_ref[...] = lax.dot_general(ub, grb, tn_dims, preferred_element_type=F32)
        dbim_ref[...] = lax.dot_general(ub, gib, tn_dims, preferred_element_type=F32)
        du_ref[...] = (d_ref[...] * dy + lax.dot_general(grb, bre_ref[...], nt_dims, preferred_element_type=F32)
                       + lax.dot_general(gib, bim_ref[...], nt_dims, preferred_element_type=F32))

    col, vec, bspec, cspec, aspec = _ssm_specs(t, nb, ch, st)
    sh = jax.ShapeDtypeStruct
    return pl.pallas_call(
        body, name=name, grid=(nb,),
        in_specs=[col, col, col, bspec, bspec, cspec, cspec, aspec, aspec, vec],
        out_specs=[col, vec, bspec, bspec, cspec, cspec, aspec, aspec],
        out_shape=[sh((t, d), F32), sh((1, d), F32), sh((nb, ch, st), F32), sh((nb, ch, st), F32),
                   sh((nb, st, ch), F32), sh((nb, st, ch), F32), sh((nb, 1, st), F32), sh((nb, 1, st), F32)],
        scratch_shapes=[pltpu.VMEM((t + HALO, st), F32), pltpu.VMEM((t + HALO, st), F32),
                        pltpu.VMEM((t, st), F32), pltpu.VMEM((t, st), F32)],
        compiler_params=_params("parallel"),
    )(u, y, dyg, bre, bim, cre, cim, a_re, a_im, dskip)


def glu_fwd(yg, w_glu, b_glu, x, name, lead=None):
    t, d = yg.shape
    tn = _tile(d, 256)
    nd = d // tn

    def body(yg_ref, wv_ref, wg_ref, bv_ref, bg_ref, x_ref, val_ref, gate_ref, o_ref):
        ygv = yg_ref[...]
        vb = (jnp.dot(ygv, wv_ref[...], preferred_element_type=F32) + bv_ref[...]).astype(BF16)
        gb = (jnp.dot(ygv, wg_ref[...], preferred_element_type=F32) + bg_ref[...]).astype(BF16)
        val_ref[...] = vb
        gate_ref[...] = gb
        o_ref[...] = x_ref[...] + vb.astype(F32) * _sigmoid(gb.astype(F32))

    col = lambda off: _spec((d, tn), lambda j: (0, j + off), lead)
    vec = lambda off: pl.BlockSpec((1, tn), lambda j: (0, j + off))
    tile = pl.BlockSpec((t, tn), lambda j: (0, j))
    return pl.pallas_call(
        body, name=name, grid=(nd,),
        in_specs=[pl.BlockSpec((t, d), lambda j: (0, 0)), col(0), col(nd), vec(0), vec(nd), tile],
        out_specs=[tile, tile, tile],
        out_shape=[jax.ShapeDtypeStruct((t, d), BF16), jax.ShapeDtypeStruct((t, d), BF16),
                   jax.ShapeDtypeStruct((t, d), F32)],
        compiler_params=_params("parallel"),
    )(yg, w_glu, w_glu, b_glu, b_glu, x)


def glu_bwd(dm, val, gate, name):
    t, d = dm.shape
    tn = _tile(d, 256)

    def body(dm_ref, val_ref, gate_ref, dv_ref, dg_ref, dbv_ref, dbg_ref):
        dmv = dm_ref[...]
        s = _sigmoid(gate_ref[...].astype(F32))
        dval = dmv * s
        dgate = dmv * val_ref[...].astype(F32) * s * (1.0 - s)
        dv_ref[...] = dval.astype(BF16)
        dg_ref[...] = dgate.astype(BF16)
        dbv_ref[...] = jnp.sum(dval, axis=0, keepdims=True)
        dbg_ref[...] = jnp.sum(dgate, axis=0, keepdims=True)

    tile = pl.BlockSpec((t, tn), lambda j: (0, j))
    vec = pl.BlockSpec((1, tn), lambda j: (0, j))
    return pl.pallas_call(
        body, name=name, grid=(d // tn,),
        in_specs=[tile, tile, tile], out_specs=[tile, tile, vec, vec],
        out_shape=[jax.ShapeDtypeStruct((t, d), BF16), jax.ShapeDtypeStruct((t, d), BF16),
                   jax.ShapeDtypeStruct((1, d), F32), jax.ShapeDtypeStruct((1, d), F32)],
        compiler_params=_params("parallel"),
    )(dm, val, gate)


def _block_diag(m, gb):
    g, a, b = m.shape
    eye = jnp.eye(gb, dtype=m.dtype)
    return jnp.einsum("ngab,gk->ngakb", m.reshape(g // gb, gb, a, b), eye).reshape(g // gb, gb * a, gb * b)


def ssm_prepare(lam_re, lam_im, log_step, b_re, b_im, c_re, c_im):
    gb = SSM_BLOCK_GROUPS
    g, p = lam_re.shape
    step = jnp.exp(log_step)[:, None]
    mag = jnp.exp(lam_re * step)
    lb_re = mag * jnp.cos(lam_im * step)
    lb_im = mag * jnp.sin(lam_im * step)
    den = lam_re * lam_re + lam_im * lam_im
    f_re = ((lb_re - 1.0) * lam_re + lb_im * lam_im) / den
    f_im = (lb_im * lam_re - (lb_re - 1.0) * lam_im) / den
    bb_re = f_re[..., None] * b_re - f_im[..., None] * b_im
    bb_im = f_re[..., None] * b_im + f_im[..., None] * b_re
    tr = lambda m: jnp.transpose(m, (0, 2, 1))
    return (_block_diag(tr(bb_re), gb), _block_diag(tr(bb_im), gb), _block_diag(tr(c_re), gb), _block_diag(tr(c_im), gb),
            lb_re.reshape(g // gb, 1, gb * p), lb_im.reshape(g // gb, 1, gb * p))


EW_BLOCK_BYTES = 2 * 1024 * 1024
BF16_ROWS = 16


def _row_tile(rows, cols):
    limit = max(BF16_ROWS, EW_BLOCK_BYTES // (cols * 4))
    best = None
    for tr in range(BF16_ROWS, min(rows, limit) + 1, BF16_ROWS):
        if rows % tr == 0:
            best = tr
    return best if best is not None else rows


def _as2d(a):
    return a.reshape(-1, a.shape[-1])


def ew(fn, ins, out_dtypes, name):
    rows, cols = ins[0].shape
    tr = _row_tile(rows, cols)
    n_in = len(ins)

    def body(*refs):
        outs = fn(*[r[...] for r in refs[:n_in]])
        for o_ref, v in zip(refs[n_in:], outs):
            o_ref[...] = v.astype(o_ref.dtype)

    spec = pl.BlockSpec((tr, cols), lambda i: (i, 0))
    return pl.pallas_call(
        body, name=name, grid=(rows // tr,), in_specs=[spec] * n_in, out_specs=[spec] * len(out_dtypes),
        out_shape=[jax.ShapeDtypeStruct((rows, cols), dt) for dt in out_dtypes],
        compiler_params=_params("parallel"),
    )(*ins)


def _adamw(w, g, m, v):
    m = ADAM_B1 * m + (1.0 - ADAM_B1) * g
    v = ADAM_B2 * v + (1.0 - ADAM_B2) * (g * g)
    m_hat = m / (1.0 - ADAM_B1 ** ADAM_STEP)
    v_hat = v / (1.0 - ADAM_B2 ** ADAM_STEP)
    delta = -ADAM_LR * (m_hat / (jnp.sqrt(v_hat) + ADAM_EPS) + ADAM_WD * w)
    return delta, m, v


def adamw(w, g, m, v, name):
    outs = ew(_adamw, [_as2d(w), _as2d(g), _as2d(m), _as2d(v)], [F32, F32, F32], name)
    return [o.reshape(w.shape) for o in outs]


def loss_head(y, target, name):
    t, d = y.shape
    tr = min(t, 128)
    n = t // tr

    def body(y_ref, t_ref, dy_ref, dyb_ref, loss_ref, acc):
        i = pl.program_id(0)
        err = y_ref[...] - t_ref[...]
        dy = err * (1.0 / d)
        dy_ref[...] = dy
        dyb_ref[...] = dy.astype(BF16)
        part = jnp.sum(err * err, axis=0, keepdims=True)

        @pl.when(i == 0)
        def _():
            acc[...] = part

        @pl.when(i != 0)
        def _():
            acc[...] += part

        @pl.when(i == n - 1)
        def _():
            loss_ref[...] = jnp.full((1, LANES), 0.5 / d, F32) * jnp.sum(acc[...])

    row = pl.BlockSpec((tr, d), lambda i: (i, 0))
    return pl.pallas_call(
        body, name=name, grid=(n,), in_specs=[row, row],
        out_specs=[row, row, pl.BlockSpec((1, LANES), lambda i: (0, 0))],
        out_shape=[jax.ShapeDtypeStruct((t, d), F32), jax.ShapeDtypeStruct((t, d), BF16),
                   jax.ShapeDtypeStruct((1, LANES), F32)],
        scratch_shapes=[pltpu.VMEM((1, d), F32)],
        compiler_params=_params("arbitrary"),
    )(y, target)


HBM_SPEC = pl.BlockSpec(memory_space=pltpu.HBM)
VMEM_SPEC = pl.BlockSpec(memory_space=pltpu.VMEM)


def _place():
    return lax.axis_index("x"), lax.axis_index("y"), lax.axis_index("c")


def _other_chips(x, y):
    return [(1 - x, y), (x, 1 - y), (1 - x, 1 - y)]


def _remote(src, dst, send_sem, recv_sem, dev):
    return pltpu.make_async_remote_copy(src_ref=src, dst_ref=dst, send_sem=send_sem, recv_sem=recv_sem,
                                        device_id=dev, device_id_type=MESH)


def _piece(refs, shard_shape, ax, j, half):
    w = shard_shape[ax]
    a, off = divmod(j * w, refs[0].shape[ax]) if isinstance(j, int) else (0, j * w)
    idx = [pl.ds(0, s) for s in shard_shape]
    idx[ax] = pl.ds(off, w)
    if half is not None:
        h0 = shard_shape[0] // 2
        idx[0] = pl.ds((off if ax == 0 else 0) + half * h0, h0)
    return refs[a].at[tuple(idx)]


def small_allreduce(v, name):
    n, r, l = v.shape
    assert n == N_DEV

    def body(v_ref, o_ref, recv, red, send1, recv1, send2, recv2):
        x, y, c = _place()
        me = 4 * x + 2 * y + c
        dev = lambda k: (k // 4, (k // 2) % 2, k % 2)
        firsts = []
        for o in range(1, N_DEV):
            tgt = (me + o) % N_DEV
            cp = _remote(v_ref.at[tgt], recv.at[me], send1.at[o], recv1.at[me], dev(tgt))
            cp.start()
            firsts.append(cp)
        recv[me] = v_ref[me]
        for o in range(1, N_DEV):
            src = (me + o) % N_DEV
            _remote(v_ref.at[src], recv.at[src], send1.at[o], recv1.at[src], dev(src)).wait_recv()
        acc = recv[0]
        for s in range(1, N_DEV):
            acc = acc + recv[s]
        red[...] = acc
        o_ref[me] = acc
        seconds = []
        for o in range(1, N_DEV):
            tgt = (me + o) % N_DEV
            cp = _remote(red, o_ref.at[me], send2.at[o], recv2.at[me], dev(tgt))
            cp.start()
            seconds.append(cp)
        for o in range(1, N_DEV):
            src = (me + o) % N_DEV
            _remote(red, o_ref.at[src], send2.at[o], recv2.at[src], dev(src)).wait_recv()
        for cp in firsts + seconds:
            cp.wait_send()

    sems = pltpu.SemaphoreType.DMA((N_DEV,))
    return pl.pallas_call(
        body, name=name, in_specs=[VMEM_SPEC], out_specs=VMEM_SPEC,
        out_shape=jax.ShapeDtypeStruct(v.shape, F32),
        scratch_shapes=[pltpu.VMEM((N_DEV, r, l), F32), pltpu.VMEM((r, l), F32), sems, sems, sems, sems],
        compiler_params=pltpu.CompilerParams(vmem_limit_bytes=VMEM_LIMIT),
    )(v)


def _me_scalar():
    return (2 * lax.axis_index("x") + lax.axis_index("y")).astype(jnp.int32).reshape(1)


def cast_into_gathered(wf, layer, axis, me1, name):
    _, r, cw = wf.shape
    tr = _row_tile(r, cw)
    nrb = r // tr
    full = (1, r * N_CHIPS, cw) if axis == 0 else (1, r, cw * N_CHIPS)
    omap = (lambda i, me: (0, me[0] * nrb + i, 0)) if axis == 0 else (lambda i, me: (0, i, me[0]))

    def body(me_ref, w_ref, o_ref):
        o_ref[...] = w_ref[...].astype(BF16)

    return pl.pallas_call(
        body, name=name,
        grid_spec=pltpu.PrefetchScalarGridSpec(
            num_scalar_prefetch=1, grid=(nrb,),
            in_specs=[pl.BlockSpec((None, tr, cw), lambda i, me: (layer, i, 0))],
            out_specs=pl.BlockSpec((None, tr, cw), omap)),
        out_shape=jax.ShapeDtypeStruct(full, BF16),
        compiler_params=_params("parallel"),
    )(me1, wf)


SEM_SPEC = pl.BlockSpec(memory_space=pltpu.SEMAPHORE)
SPLIT_COPY_PARAMS = pltpu.CompilerParams(has_side_effects=pltpu.SideEffectType.DATAFLOW_SIDE_EFFECTING)
TOKEN_SHAPE = (SUBLANES, LANES)


def _hbm(a):
    return pltpu.with_memory_space_constraint(a, pltpu.HBM)


def _gather_copies(refs, shapes, axes, send_sem, recv_sem):
    x, y, c = _place()
    me = 2 * x + y
    out = []
    for p, ref in enumerate(refs):
        place = lambda j: _piece([ref.at[0]], shapes[p], axes[p], j, c)
        for q, chip in enumerate(_other_chips(x, y)):
            dev = (chip[0], chip[1], c)
            sems = (send_sem.at[3 * p + q], recv_sem.at[3 * p + q])
            theirs = place(2 * chip[0] + chip[1])
            out.append((_remote(place(me), place(me), *sems, dev), _remote(theirs, theirs, *sems, dev)))
    return out


def gather_start(bufs, shapes, axes, after, name):
    n = len(bufs)

    def body(*refs):
        send_sem, recv_sem = refs[n + 1:n + 3]
        o_refs = refs[n + 3:2 * n + 3]
        token = refs[-1]
        for mine, _ in _gather_copies(o_refs, shapes, axes, send_sem, recv_sem):
            mine.start()
        token[...] = jnp.zeros(TOKEN_SHAPE, F32)

    sems = pltpu.SemaphoreType.DMA((3 * n,))
    outs = pl.pallas_call(
        body, name=name,
        in_specs=[HBM_SPEC] * n + [pl.BlockSpec(memory_space=pl.ANY)],
        out_specs=[SEM_SPEC, SEM_SPEC] + [HBM_SPEC] * n + [VMEM_SPEC],
        out_shape=[sems, sems] + [pltpu.HBM(b.shape, b.dtype) for b in bufs] + [jax.ShapeDtypeStruct(TOKEN_SHAPE, F32)],
        input_output_aliases={p: p + 2 for p in range(n)},
        compiler_params=SPLIT_COPY_PARAMS,
    )(*[_hbm(b) for b in bufs], after)
    return outs[0], outs[1], list(outs[2:2 + n]), outs[-1]


def gather_wait(send_sem, recv_sem, bufs, shapes, axes, after, name):
    n = len(bufs)

    def body(*refs):
        s_sem, r_sem = refs[n:n + 2]
        o_refs = refs[n + 3:]
        for mine, theirs in _gather_copies(o_refs, shapes, axes, s_sem, r_sem):
            mine.wait_send()
            theirs.wait_recv()

    return pl.pallas_call(
        body, name=name,
        in_specs=[HBM_SPEC] * n + [SEM_SPEC, SEM_SPEC, pl.BlockSpec(memory_space=pl.ANY)],
        out_specs=[HBM_SPEC] * n,
        out_shape=[pltpu.HBM(b.shape, b.dtype) for b in bufs],
        input_output_aliases={p: p for p in range(n)},
        compiler_params=SPLIT_COPY_PARAMS,
    )(*bufs, send_sem, recv_sem, after)


SIBLING_SLOTS = 2


def _row_step(nrb):
    s = pl.program_id(0)
    for ax in range(1, len(nrb)):
        s = s * nrb[ax] + pl.program_id(ax)
    return s


def gather_forward(buf, axis, r, cw, me1, name):
    nl = buf.shape[0]
    h0 = r // 2
    tr = _row_tile(h0, cw)
    nrb = h0 // tr
    peer = lambda q, me: (me[0] + q + 1) % N_CHIPS
    if axis == 1:
        view = buf.reshape(nl, 1, 2, h0, N_CHIPS * cw)
        spec = pl.BlockSpec((1, 1, 2, tr, cw), lambda l, q, i, me: (l, 0, 0, i, peer(q, me)))
    else:
        view = buf.reshape(nl, N_CHIPS, 2, h0, cw)
        spec = pl.BlockSpec((1, 1, 2, tr, cw), lambda l, q, i, me: (l, peer(q, me), 0, i, 0))

    def body(me_ref, in_ref, o_ref, rbuf, send_sem, recv_sem):
        x, y, c = _place()
        slot = _row_step((nl, N_CHIPS - 1, nrb)) % SIBLING_SLOTS
        cp = _remote(in_ref.at[0, 0, c], rbuf.at[slot], send_sem.at[slot], recv_sem.at[slot], (x, y, 1 - c))
        cp.start()
        o_ref[0, 0, c] = in_ref[0, 0, c]
        cp.wait_recv()
        o_ref[0, 0, 1 - c] = rbuf[slot]
        cp.wait_send()

    out = pl.pallas_call(
        body, name=name,
        grid_spec=pltpu.PrefetchScalarGridSpec(
            num_scalar_prefetch=1, grid=(nl, N_CHIPS - 1, nrb), in_specs=[spec], out_specs=spec,
            scratch_shapes=[pltpu.VMEM((SIBLING_SLOTS, tr, cw), buf.dtype),
                            pltpu.SemaphoreType.DMA((SIBLING_SLOTS,)), pltpu.SemaphoreType.DMA((SIBLING_SLOTS,))]),
        out_shape=jax.ShapeDtypeStruct(view.shape, view.dtype),
        input_output_aliases={1: 0},
        compiler_params=_params("arbitrary", "arbitrary", "arbitrary"),
    )(me1, view)
    return out.reshape(buf.shape)


def pair_reduce(g, axis, r, cw, name, into=None, first_slot=0):
    h0 = r // 2
    tr = _row_tile(h0, cw)
    nrb = h0 // tr
    if axis == 1:
        n_sh = g.shape[1] // cw
        view = g.reshape(1, 2, h0, n_sh * cw)
        spec = pl.BlockSpec((1, 2, tr, cw), lambda j, i: (0, 0, i, j))
    else:
        n_sh = g.shape[0] // r
        view = g.reshape(n_sh, 2, h0, cw)
        spec = pl.BlockSpec((1, 2, tr, cw), lambda j, i: (j, 0, i, 0))

    def body(g_ref, *rest):
        o_ref, rbuf, send_sem, recv_sem = rest[-4:]
        x, y, c = _place()
        slot = _row_step((n_sh, nrb)) % SIBLING_SLOTS
        cp = _remote(g_ref.at[0, 1 - c], rbuf.at[slot], send_sem.at[slot], recv_sem.at[slot], (x, y, 1 - c))
        cp.start()
        mine = g_ref[0, c].astype(F32)
        cp.wait_recv()
        o_ref[0] = (mine + rbuf[slot].astype(F32)).astype(BF16)
        cp.wait_send()

    args, in_specs, aliases = [view], [spec], {}
    if into is not None:
        args.append(into)
        in_specs.append(pl.BlockSpec(memory_space=pl.ANY))
        aliases = {1: 0}
    return pl.pallas_call(
        body, name=name, grid=(n_sh, nrb), in_specs=in_specs,
        out_specs=pl.BlockSpec((1, tr, cw), lambda j, i: (j + first_slot, i, 0)),
        out_shape=jax.ShapeDtypeStruct((N_CHIPS, h0, cw), BF16),
        input_output_aliases=aliases,
        scratch_shapes=[pltpu.VMEM((SIBLING_SLOTS, tr, cw), BF16),
                        pltpu.SemaphoreType.DMA((SIBLING_SLOTS,)), pltpu.SemaphoreType.DMA((SIBLING_SLOTS,))],
        compiler_params=_params("arbitrary", "arbitrary"),
    )(*args)


def _chip_copies(h_refs, lb_refs, send_sem, recv_sem):
    x, y, c = _place()
    out = []
    for k, (h, lb) in enumerate(zip(h_refs, lb_refs)):
        for q, chip in enumerate(_other_chips(x, y)):
            out.append(_remote(h.at[2 * chip[0] + chip[1]], lb.at[q], send_sem.at[3 * k + q], recv_sem.at[3 * k + q],
                               (chip[0], chip[1], c)))
    return out


def chip_start(halves, name):
    n = len(halves)
    landed = [lax.empty((N_CHIPS - 1,) + h.shape[1:], h.dtype) for h in halves]

    def body(*refs):
        send_sem, recv_sem = refs[2 * n:2 * n + 2]
        h_refs, lb_refs = refs[2 * n + 2:3 * n + 2], refs[3 * n + 2:4 * n + 2]
        for cp in _chip_copies(h_refs, lb_refs, send_sem, recv_sem):
            cp.start()
        refs[-1][...] = jnp.zeros(TOKEN_SHAPE, F32)

    sems = pltpu.SemaphoreType.DMA((3 * n,))
    outs = pl.pallas_call(
        body, name=name,
        in_specs=[HBM_SPEC] * (2 * n),
        out_specs=[SEM_SPEC, SEM_SPEC] + [HBM_SPEC] * (2 * n) + [VMEM_SPEC],
        out_shape=[sems, sems] + [pltpu.HBM(a.shape, a.dtype) for a in halves + landed]
        + [jax.ShapeDtypeStruct(TOKEN_SHAPE, F32)],
        input_output_aliases={p: p + 2 for p in range(2 * n)},
        compiler_params=SPLIT_COPY_PARAMS,
    )(*[_hbm(a) for a in halves + landed])
    return outs[0], outs[1], list(outs[2:2 + n]), list(outs[2 + n:2 + 2 * n]), outs[-1]


def chip_wait(send_sem, recv_sem, halves, landed, after, name):
    n = len(halves)

    def body(*refs):
        s_sem, r_sem = refs[2 * n:2 * n + 2]
        h_refs, lb_refs = refs[2 * n + 3:3 * n + 3], refs[3 * n + 3:]
        for cp in _chip_copies(h_refs, lb_refs, s_sem, r_sem):
            cp.wait_send()
            cp.wait_recv()

    outs = pl.pallas_call(
        body, name=name,
        in_specs=[HBM_SPEC] * (2 * n) + [SEM_SPEC, SEM_SPEC, pl.BlockSpec(memory_space=pl.ANY)],
        out_specs=[HBM_SPEC] * (2 * n),
        out_shape=[pltpu.HBM(a.shape, a.dtype) for a in halves + landed],
        input_output_aliases={p: p for p in range(2 * n)},
        compiler_params=SPLIT_COPY_PARAMS,
    )(*halves, *landed, send_sem, recv_sem, after)
    return list(outs[:n]), list(outs[n:])


def reduce_share(half, landed, me1, layer, n_layers, name, into=None):
    _, h0, cw = half.shape
    tr = _row_tile(h0, cw)
    nrb = h0 // tr

    def body(me_ref, h_ref, l0, l1, l2, *rest):
        o_ref, sbuf, rbuf, send_sem, recv_sem = rest[-5:]
        x, y, c = _place()
        slot = pl.program_id(0) % SIBLING_SLOTS
        total = ((h_ref[...].astype(F32) + l0[...].astype(F32)) + l1[...].astype(F32)) + l2[...].astype(F32)
        sbuf[slot] = total
        cp = _remote(sbuf.at[slot], rbuf.at[slot], send_sem.at[slot], recv_sem.at[slot], (x, y, 1 - c))
        cp.start()
        o_ref[0, c] = total
        cp.wait_recv()
        o_ref[0, 1 - c] = rbuf[slot]
        cp.wait_send()

    landed_spec = lambda q: pl.BlockSpec((None, tr, cw), lambda i, me: (q, i, 0))
    args = [me1, half, landed, landed, landed]
    in_specs = [pl.BlockSpec((None, tr, cw), lambda i, me: (me[0], i, 0))] + [landed_spec(q) for q in range(N_CHIPS - 1)]
    aliases = {}
    if into is not None:
        args.append(into.reshape(n_layers, 2, h0, cw))
        in_specs.append(pl.BlockSpec(memory_space=pl.ANY))
        aliases = {5: 0}
    out = pl.pallas_call(
        body, name=name,
        grid_spec=pltpu.PrefetchScalarGridSpec(
            num_scalar_prefetch=1, grid=(nrb,), in_specs=in_specs,
            out_specs=pl.BlockSpec((1, 2, tr, cw), lambda i, me: (layer, 0, i, 0)),
            scratch_shapes=[pltpu.VMEM((SIBLING_SLOTS, tr, cw), F32), pltpu.VMEM((SIBLING_SLOTS, tr, cw), F32),
                            pltpu.SemaphoreType.DMA((SIBLING_SLOTS,)), pltpu.SemaphoreType.DMA((SIBLING_SLOTS,))]),
        out_shape=jax.ShapeDtypeStruct((n_layers, 2, h0, cw), F32),
        input_output_aliases=aliases,
        compiler_params=_params("arbitrary"),
    )(*args)
    return out.reshape(n_layers, 2 * h0, cw)


WEIGHTS = ["norm_mix_g", "norm_ffn_g", "pool_w", "pool_b", "pool_scale", "sb_w_qkv", "sb_q_gain", "sb_k_gain",
           "sb_w_o", "ssm_lam_re", "ssm_lam_im", "ssm_log_step", "ssm_b_re", "ssm_b_im", "ssm_c_re", "ssm_c_im",
           "ssm_d", "ssm_w_glu", "ssm_b_glu", "ffn_w_up", "ffn_conv_w", "ffn_conv_b", "ffn_w_down"]
BIG = {"sb_w_qkv": 1, "sb_w_o": 0, "ssm_w_glu": 1, "ffn_w_up": 1, "ffn_w_down": 0}
SMALL_SHARDED = {"pool_w": 2, "pool_b": 1, "pool_scale": 1, "ssm_d": 1, "ssm_b_glu": 1, "ffn_conv_w": 2}
SMALL = [n for n in WEIGHTS if n not in BIG]
SMALL_PAD = N_DEV * SUBLANES * LANES
N_MIXERS = 3


def _pack(arrays):
    flat = jnp.concatenate([a.reshape(-1).astype(F32) for a in arrays])
    total = -(-flat.shape[0] // SMALL_PAD) * SMALL_PAD
    flat = jnp.pad(flat, (0, total - flat.shape[0]))
    return flat.reshape(N_DEV, -1, LANES)


def _unpack(packed, like):
    flat = packed.reshape(-1)
    out, off = [], 0
    for a in like:
        out.append(flat[off:off + a.size].reshape(a.shape))
        off += a.size
    return out


def kernel(x, norm_mix_g, norm_ffn_g, pool_w, pool_b, pool_scale, sb_w_qkv, sb_q_gain, sb_k_gain, sb_w_o, ssm_lam_re, ssm_lam_im, ssm_log_step, ssm_b_re, ssm_b_im, ssm_c_re, ssm_c_im, ssm_d, ssm_w_glu, ssm_b_glu, ffn_w_up, ffn_conv_w, ffn_conv_b, ffn_w_down, loss_target, m_norm_mix_g, m_norm_ffn_g, m_pool_w, m_pool_b, m_pool_scale, m_sb_w_qkv, m_sb_q_gain, m_sb_k_gain, m_sb_w_o, m_ssm_lam_re, m_ssm_lam_im, m_ssm_log_step, m_ssm_b_re, m_ssm_b_im, m_ssm_c_re, m_ssm_c_im, m_ssm_d, m_ssm_w_glu, m_ssm_b_glu, m_ffn_w_up, m_ffn_conv_w, m_ffn_conv_b, m_ffn_w_down, v_norm_mix_g, v_norm_ffn_g, v_pool_w, v_pool_b, v_pool_scale, v_sb_w_qkv, v_sb_q_gain, v_sb_k_gain, v_sb_w_o, v_ssm_lam_re, v_ssm_lam_im, v_ssm_log_step, v_ssm_b_re, v_ssm_b_im, v_ssm_c_re, v_ssm_c_im, v_ssm_d, v_ssm_w_glu, v_ssm_b_glu, v_ffn_w_up, v_ffn_conv_w, v_ffn_conv_b, v_ffn_w_down):
    given = dict(locals())
    w = {n: given[n] for n in WEIGHTS}
    mom = {n: given["m_" + n] for n in WEIGHTS}
    var = {n: given["v_" + n] for n in WEIGHTS}
    xi, yi, ci = _place()
    me = 2 * xi + yi
    depth = norm_mix_g.shape[0]
    x_in = x[0]
    t, d = x_in.shape

    def placed(a, ax):
        shp = list(a.shape)
        shp[ax] *= N_CHIPS
        full = lax.dynamic_update_slice_in_dim(jnp.zeros(shp, F32), a, me * a.shape[ax], ax)
        return jnp.where(ci == 0, full, 0.0)

    sharded_full = [placed(w[n], ax) for n, ax in SMALL_SHARDED.items()]
    whole = dict(zip(SMALL_SHARDED, _unpack(small_allreduce(_pack(sharded_full), "gather_vectors"), sharded_full)))

    big = list(BIG)
    me1 = _me_scalar()
    shard = {n: tuple(w[n].shape[1:]) for n in big}
    vec = lambda a, i: a[i:i + 1]
    tie = lambda v, token: v + token[0, 0]

    def layer_weights(i):
        kind, j = i % N_MIXERS, i // N_MIXERS
        mixer = {0: [], 1: [("sb_w_qkv", j), ("sb_w_o", j)], 2: [("ssm_w_glu", j)]}[kind]
        return mixer + [("ffn_w_up", i), ("ffn_w_down", i)]

    started, token = [], x_in
    for i in range(depth):
        keys = layer_weights(i)
        bufs = [cast_into_gathered(w[n], l, BIG[n], me1, f"cast_{n}{l}") for n, l in keys]
        send_sem, recv_sem, bufs, token = gather_start(bufs, [shard[n] for n, _ in keys], [BIG[n] for n, _ in keys],
                                                       token, f"gather_start{i}")
        started.append((keys, send_sem, recv_sem, bufs))
    gathered = {}
    pool_w_bf = whole["pool_w"].astype(BF16)

    saved = []
    xc = x_in
    for i in range(depth):
        kind, j = i % N_MIXERS, i // N_MIXERS
        keys, send_sem, recv_sem, bufs = started[i]
        bufs = gather_wait(send_sem, recv_sem, bufs, [shard[n] for n, _ in keys], [BIG[n] for n, _ in keys],
                           token if i == 0 else xc, f"gather_wait{i}")
        for (n, l), b in zip(keys, bufs):
            gathered[n, l] = gather_forward(b, BIG[n], *shard[n], me1, f"gather_forward_{n}{l}")
        s = {"x_in": xc}
        g_mix = vec(norm_mix_g, i)
        if kind == 0:
            (h,) = rmsnorm_fwd(xc, g_mix, [F32], f"norm_mix{i}")
            x_mid = pool_fwd(h, xc, pool_w_bf, vec(whole["pool_b"], j), vec(whole["pool_scale"], j),
                             f"pool_fwd{i}", lead=j)
        elif kind == 1:
            (h,) = rmsnorm_fwd(xc, g_mix, [BF16], f"norm_mix{i}")
            s["qkv"] = mm_cols(h, gathered["sb_w_qkv", j], out_dtype=BF16, name=f"sb_qkv{i}", lead=0)
            s["o"], s["ltot"] = sb_attn_fwd(s["qkv"], vec(sb_q_gain, j), vec(sb_k_gain, j), f"sb_attn_fwd{i}")
            x_mid = mm_cols(s["o"], gathered["sb_w_o", j], out_dtype=F32, name=f"sb_out{i}", resid=xc, lead=0)
        else:
            (h,) = rmsnorm_fwd(xc, g_mix, [F32], f"norm_mix{i}")
            prm = tuple(w[n][j] for n in ("ssm_lam_re", "ssm_lam_im", "ssm_log_step", "ssm_b_re", "ssm_b_im",
                                          "ssm_c_re", "ssm_c_im"))
            prep, s["prep_vjp"] = jax.vjp(ssm_prepare, *prm)
            s["prep"] = tuple(a.astype(BF16) for a in prep[:4]) + tuple(prep[4:])
            s["y"], s["yg"] = ssm_core_fwd(h, *s["prep"], vec(whole["ssm_d"], j), f"ssm_fwd{i}")
            s["val"], s["gate"], x_mid = glu_fwd(s["yg"], gathered["ssm_w_glu", j], vec(whole["ssm_b_glu"], j), xc,
                                                 f"ssm_glu{i}", lead=0)
        s["x_mid"] = x_mid
        (h2,) = rmsnorm_fwd(x_mid, vec(norm_ffn_g, i), [BF16], f"norm_ffn{i}")
        s["up_val"], s["up_gate"], s["act"] = ffn_up_fused(h2, gathered["ffn_w_up", i], whole["ffn_conv_w"][i],
                                                           vec(ffn_conv_b, i), f"ffn_up{i}", lead=0)
        xc = mm_k([s["act"]], gathered["ffn_w_down", i], b_nt=False, name=f"ffn_down{i}", resid=x_mid, lead=0)
        saved.append(s)

    dx, dxb, loss_part = loss_head(xc, loss_target[0], "loss_head")
    loss = lax.psum(loss_part[0, 0], ("x", "y", "c"))

    small = {n: [None] * w[n].shape[0] for n in SMALL}
    big_g, grads = {}, {}

    def finish_reduction(pending, after):
        layer, keys, send_sem, recv_sem, halves, landed, _ = pending
        halves, landed = chip_wait(send_sem, recv_sem, halves, landed, after, f"grads_chip_wait{layer}")
        for (n, l), h, lb in zip(keys, halves, landed):
            grads[n] = reduce_share(h, lb, me1, l, w[n].shape[0], f"grads_share_{n}{l}", into=grads.get(n))

    pending = None
    for i in reversed(range(depth)):
        kind, j = i % N_MIXERS, i // N_MIXERS
        s = saved[i]
        g_ffn, g_mix = vec(norm_ffn_g, i), vec(norm_mix_g, i)
        if pending is not None:
            g_ffn = tie(g_ffn, pending[-1])
        cw, cb = whole["ffn_conv_w"][i], vec(ffn_conv_b, i)
        (h2,) = rmsnorm_fwd(s["x_mid"], g_ffn, [BF16], f"norm_ffn_re{i}")
        dupv, dupg, dcwv, dcwg, dcbv, dcbg = ffn_bwd_fused(dxb, gathered["ffn_w_down", i], s["up_val"], s["up_gate"], cw, cb,
                                                           f"ffn_bwd{i}", lead=0)
        big_g["ffn_w_down", i] = [mm_rows(s["act"], dxb, out_dtype=BF16, name=f"ffn_dwdown{i}")]
        big_g["ffn_w_up", i] = [mm_cols(h2, dupv, a_contract=0, out_dtype=BF16, name=f"ffn_dwup_val{i}"),
                                mm_cols(h2, dupg, a_contract=0, out_dtype=BF16, name=f"ffn_dwup_gate{i}")]
        dh2 = mm_k([dupv, dupg], gathered["ffn_w_up", i], b_nt=True, name=f"ffn_dh{i}", lead=0)
        dx_mid, dxb_mid, small["norm_ffn_g"][i] = rmsnorm_bwd(s["x_mid"], g_ffn, dh2, dx, f"norm_ffn_bwd{i}")
        small["ffn_conv_w"][i] = jnp.concatenate([dcwv, dcwg], axis=1)[None]
        small["ffn_conv_b"][i] = jnp.concatenate([dcbv, dcbg], axis=1)

        if kind == 0:
            (h,) = rmsnorm_fwd(s["x_in"], g_mix, [F32], f"norm_mix_re{i}")
            dh, dwp, small["pool_b"][j], small["pool_scale"][j] = pool_bwd(
                dx_mid, h, pool_w_bf, vec(whole["pool_b"], j), vec(whole["pool_scale"], j), f"pool_bwd{i}", lead=j)
            small["pool_w"][j] = dwp[None]
        elif kind == 1:
            (h,) = rmsnorm_fwd(s["x_in"], g_mix, [BF16], f"norm_mix_re{i}")
            do = mm_cols(dxb_mid, gathered["sb_w_o", j], b_nt=True, out_dtype=BF16, name=f"sb_do{i}", lead=0)
            big_g["sb_w_o", j] = [mm_cols(s["o"], dxb_mid, a_contract=0, out_dtype=BF16, name=f"sb_dwo{i}")]
            dq, dk, dv, small["sb_q_gain"][j], small["sb_k_gain"][j] = sb_attn_bwd(
                s["qkv"], s["ltot"], do, vec(sb_q_gain, j), vec(sb_k_gain, j), f"sb_attn_bwd{i}")
            dqkv = jnp.concatenate([dq, dk, dv], axis=1)
            big_g["sb_w_qkv", j] = [mm_cols(h, dqkv, a_contract=0, out_dtype=BF16, name=f"sb_dwqkv{i}")]
            dh = mm_k([dqkv], gathered["sb_w_qkv", j], b_nt=True, name=f"sb_dh{i}", lead=0)
        else:
            (h,) = rmsnorm_fwd(s["x_in"], g_mix, [F32], f"norm_mix_re{i}")
            dval, dgate, dbv, dbg = glu_bwd(dx_mid, s["val"], s["gate"], f"ssm_glu_bwd{i}")
            small["ssm_b_glu"][j] = jnp.concatenate([dbv, dbg], axis=1)
            big_g["ssm_w_glu", j] = [mm_cols(s["yg"], dval, a_contract=0, out_dtype=BF16, name=f"ssm_dwglu_val{i}"),
                                     mm_cols(s["yg"], dgate, a_contract=0, out_dtype=BF16, name=f"ssm_dwglu_gate{i}")]
            dyg = mm_k([dval, dgate], gathered["ssm_w_glu", j], b_nt=True, name=f"ssm_dyg{i}", lead=0)
            dh, small["ssm_d"][j], *dprep = ssm_core_bwd(h, s["y"], dyg, *s["prep"], vec(whole["ssm_d"], j), f"ssm_bwd{i}")
            dprm = s["prep_vjp"](tuple(dprep))
            for n, g in zip(("ssm_lam_re", "ssm_lam_im", "ssm_log_step", "ssm_b_re", "ssm_b_im", "ssm_c_re", "ssm_c_im"),
                            dprm):
                small[n][j] = g[None]
        dx, dxb, small["norm_mix_g"][i] = rmsnorm_bwd(s["x_in"], g_mix, dh, dx_mid, f"norm_mix_bwd{i}")

        keys, halves = layer_weights(i), []
        for n, l in keys:
            h = None
            for a, g in enumerate(big_g[n, l]):
                h = pair_reduce(g, BIG[n], *shard[n], f"grads_pair_{n}{l}_{a}", into=h,
                                first_slot=a * (N_CHIPS // len(big_g[n, l])))
            halves.append(h)
        started_copies = chip_start(halves, f"grads_chip_start{i}")
        if pending is not None:
            finish_reduction(pending, dx)
        pending = (i, keys) + tuple(started_copies)
    finish_reduction(pending, dx)

    small_full = [jnp.concatenate(small[n], axis=0) for n in SMALL]
    small_sum = dict(zip(SMALL, _unpack(small_allreduce(_pack(small_full), "reduce_vectors"), small_full)))
    for n in SMALL:
        g = small_sum[n]
        if n in SMALL_SHARDED:
            ax = SMALL_SHARDED[n]
            g = lax.dynamic_slice_in_dim(g, me * w[n].shape[ax], w[n].shape[ax], ax)
        grads[n] = g

    delta, new_m, new_v = {}, {}, {}
    for n in big:
        delta[n], new_m[n], new_v[n] = adamw(w[n], grads[n], mom[n], var[n], "adamw_" + n)
    like = [w[n] for n in SMALL]
    packed = [_pack([src[n] for n in SMALL]).reshape(-1, LANES) for src in (w, grads, mom, var)]
    for dst, out in zip((delta, new_m, new_v), adamw(*packed, "adamw_vectors")):
        dst.update(zip(SMALL, _unpack(out, like)))

    return (loss, dx[None], *[grads[n] for n in WEIGHTS], *[delta[n] for n in WEIGHTS],
            *[new_m[n] for n in WEIGHTS], *[new_v[n] for n in WEIGHTS])
```

```python
import functools
import math

import jax
import jax.numpy as jnp
from jax import lax
from jax.experimental import pallas as pl
from jax.experimental.pallas import tpu as pltpu

F32 = jnp.float32
BF16 = jnp.bfloat16

RMS_EPS = 1e-6
POOL_WINDOWS = (2, 4, 8, 16)
SB_HEAD_DIM = 128
SSM_GROUP_CH = 16
SSM_STATE = 64
SSM_BLOCK_GROUPS = 8
ADAM_LR = 0.001
ADAM_B1 = 0.9
ADAM_B2 = 0.999
ADAM_EPS = 1e-08
ADAM_WD = 0.01
ADAM_STEP = 10

V7X_VMEM_BYTES = 64 * 1024 * 1024
VMEM_LIMIT = V7X_VMEM_BYTES - 8 * 1024 * 1024
SUBLANES = 8
LANES = 128
MESH = pl.DeviceIdType.MESH
N_CHIPS = 4
N_DEV = 8


def _params(*sem):
    return pltpu.CompilerParams(dimension_semantics=tuple(sem) if sem else None, vmem_limit_bytes=VMEM_LIMIT)


def _tile(n, want):
    if n <= want:
        return n
    t = (want // LANES) * LANES
    while t > LANES and n % t:
        t -= LANES
    assert n % t == 0, (n, want)
    return t


def _spec(shape, imap, lead=None):
    if lead is None:
        return pl.BlockSpec(tuple(shape), imap)
    return pl.BlockSpec((None,) + tuple(shape), lambda *a: (lead,) + tuple(imap(*a)))


def _sigmoid(v):
    return 1.0 / (1.0 + jnp.exp(-v))


def _shift_down(v, k):
    return pltpu.roll(v, k, 0)


def _shift_up(v, k):
    return pltpu.roll(v, v.shape[0] - k, 0)


def rmsnorm_fwd(x, g, out_dtypes, name):
    t, d = x.shape
    tr = min(t, 128)

    def body(x_ref, g_ref, *o_refs):
        xv = x_ref[...]
        r = lax.rsqrt(jnp.mean(xv * xv, axis=-1, keepdims=True) + RMS_EPS)
        h = xv * r * g_ref[...]
        for o in o_refs:
            o[...] = h.astype(o.dtype)

    outs = pl.pallas_call(
        body, name=name, grid=(t // tr,),
        in_specs=[pl.BlockSpec((tr, d), lambda i: (i, 0)), pl.BlockSpec((1, d), lambda i: (0, 0))],
        out_specs=[pl.BlockSpec((tr, d), lambda i: (i, 0)) for _ in out_dtypes],
        out_shape=[jax.ShapeDtypeStruct((t, d), dt) for dt in out_dtypes],
        compiler_params=_params("parallel"),
    )(x, g)
    return outs


def rmsnorm_bwd(x, g, dh, dres, name):
    t, d = x.shape
    tr = min(t, 128)

    def body(x_ref, g_ref, dh_ref, dres_ref, dx_ref, dxb_ref, dg_ref):
        xv = x_ref[...]
        r = lax.rsqrt(jnp.mean(xv * xv, axis=-1, keepdims=True) + RMS_EPS)
        xhat = xv * r
        dhv = dh_ref[...]
        dxhat = dhv * g_ref[...]
        dx = dres_ref[...] + r * (dxhat - xhat * jnp.mean(dxhat * xhat, axis=-1, keepdims=True))
        dx_ref[...] = dx
        dxb_ref[...] = dx.astype(BF16)
        part = jnp.sum(dhv * xhat, axis=0, keepdims=True)

        @pl.when(pl.program_id(0) == 0)
        def _():
            dg_ref[...] = part

        @pl.when(pl.program_id(0) != 0)
        def _():
            dg_ref[...] += part

    row = pl.BlockSpec((tr, d), lambda i: (i, 0))
    vec = pl.BlockSpec((1, d), lambda i: (0, 0))
    return pl.pallas_call(
        body, name=name, grid=(t // tr,),
        in_specs=[row, vec, row, row],
        out_specs=[row, row, vec],
        out_shape=[jax.ShapeDtypeStruct((t, d), F32), jax.ShapeDtypeStruct((t, d), BF16),
                   jax.ShapeDtypeStruct((1, d), F32)],
        compiler_params=_params("arbitrary"),
    )(x, g, dh, dres)


def mm_cols(a, b, *, a_contract=1, b_nt=False, out_dtype, name, resid=None, tn=512, lead=None):
    m = a.shape[1 - a_contract]
    k = a.shape[a_contract]
    n = b.shape[-2] if b_nt else b.shape[-1]
    assert (b.shape[-1] if b_nt else b.shape[-2]) == k
    tn = _tile(n, tn)

    def body(a_ref, b_ref, *rest):
        o_ref = rest[-1]
        dn = (((a_contract,), (1 if b_nt else 0,)), ((), ()))
        acc = lax.dot_general(a_ref[...], b_ref[...], dn, preferred_element_type=F32)
        if resid is not None:
            acc = acc + rest[0][...]
        o_ref[...] = acc.astype(o_ref.dtype)

    in_specs = [pl.BlockSpec(a.shape, lambda j: (0, 0)),
                _spec((tn, k), lambda j: (j, 0), lead) if b_nt else _spec((k, tn), lambda j: (0, j), lead)]
    args = [a, b]
    if resid is not None:
        in_specs.append(pl.BlockSpec((m, tn), lambda j: (0, j)))
        args.append(resid)
    return pl.pallas_call(
        body, name=name, grid=(n // tn,), in_specs=in_specs,
        out_specs=pl.BlockSpec((m, tn), lambda j: (0, j)),
        out_shape=jax.ShapeDtypeStruct((m, n), out_dtype),
        compiler_params=_params("parallel"),
    )(*args)


def mm_rows(st, res, *, out_dtype, name, tn=512):
    k, m = st.shape
    n = res.shape[1]
    assert res.shape[0] == k
    tn = _tile(m, tn)

    def body(st_ref, res_ref, o_ref):
        o_ref[...] = lax.dot_general(st_ref[...], res_ref[...], (((0,), (0,)), ((), ())),
                                     preferred_element_type=F32).astype(o_ref.dtype)

    return pl.pallas_call(
        body, name=name, grid=(m // tn,),
        in_specs=[pl.BlockSpec((k, tn), lambda j: (0, j)), pl.BlockSpec((k, n), lambda j: (0, 0))],
        out_specs=pl.BlockSpec((tn, n), lambda j: (j, 0)),
        out_shape=jax.ShapeDtypeStruct((m, n), out_dtype),
        compiler_params=_params("parallel"),
    )(st, res)


def mm_k(a_list, b, *, b_nt, name, resid=None, tk=512, tnn=1024, lead=None):
    m = a_list[0].shape[0]
    ks = [a.shape[1] for a in a_list]
    ktot = sum(ks)
    n = b.shape[-2] if b_nt else b.shape[-1]
    assert (b.shape[-1] if b_nt else b.shape[-2]) == ktot
    tk = _tile(ks[0], tk)
    assert all(kk % tk == 0 for kk in ks)
    tnn = _tile(n, tnn)
    nks = [kk // tk for kk in ks]
    starts = [sum(nks[:i]) for i in range(len(nks))]
    nk = sum(nks)

    def body(*refs):
        a_refs = refs[:len(a_list)]
        b_ref = refs[len(a_list)]
        o_ref = refs[-1]
        kk = pl.program_id(1)

        @pl.when(kk == 0)
        def _():
            if resid is not None:
                o_ref[...] = refs[len(a_list) + 1][...]
            else:
                o_ref[...] = jnp.zeros_like(o_ref)

        dn = (((1,), (1 if b_nt else 0,)), ((), ()))
        for i, a_ref in enumerate(a_refs):
            @pl.when(jnp.logical_and(kk >= starts[i], kk < starts[i] + nks[i]))
            def _(a_ref=a_ref):
                o_ref[...] += lax.dot_general(a_ref[...], b_ref[...], dn, preferred_element_type=F32)

    def a_spec(i):
        return pl.BlockSpec((m, tk), lambda nn, kk: (0, jnp.clip(kk - starts[i], 0, nks[i] - 1)))

    in_specs = [a_spec(i) for i in range(len(a_list))]
    in_specs.append(_spec((tnn, tk), lambda nn, kk: (nn, kk), lead) if b_nt
                    else _spec((tk, tnn), lambda nn, kk: (kk, nn), lead))
    args = list(a_list) + [b]
    if resid is not None:
        in_specs.append(pl.BlockSpec((m, tnn), lambda nn, kk: (0, nn)))
        args.append(resid)
    return pl.pallas_call(
        body, name=name, grid=(n // tnn, nk), in_specs=in_specs,
        out_specs=pl.BlockSpec((m, tnn), lambda nn, kk: (0, nn)),
        out_shape=jax.ShapeDtypeStruct((m, n), F32),
        compiler_params=_params("parallel", "arbitrary"),
    )(*args)


HALO = SUBLANES
CHUNK_ROWS = 64


def _conv_taps(ext, r):
    return ext[HALO:], _shift_down(ext, 1)[HALO:], _shift_down(ext, 2)[HALO:]


def ffn_up_fused(h, w_up, conv_w, conv_b, name, lead=None):
    t, d = h.shape
    f = w_up.shape[-1] // 2
    tn = _tile(f, 256)
    nf = f // tn
    r = min(CHUNK_ROWS, t)

    def body(h_ref, wv_ref, wg_ref, cwv_ref, cwg_ref, cbv_ref, cbg_ref, uv_ref, ug_ref, act_ref, sv, sg):
        zero = jnp.zeros((HALO, tn), F32)
        sv[0:HALO, :] = zero
        sg[0:HALO, :] = zero
        hv = h_ref[...]
        sv[HALO:, :] = jnp.dot(hv, wv_ref[...], preferred_element_type=F32).astype(BF16).astype(F32)
        sg[HALO:, :] = jnp.dot(hv, wg_ref[...], preferred_element_type=F32).astype(BF16).astype(F32)
        cwv, cwg = cwv_ref[...], cwg_ref[...]
        cbv, cbg = cbv_ref[...], cbg_ref[...]

        def chunk(i, carry):
            r0 = pl.multiple_of(i * r, r)
            v0, v1, v2 = _conv_taps(sv[pl.ds(r0, r + HALO), :], r)
            g0, g1, g2 = _conv_taps(sg[pl.ds(r0, r + HALO), :], r)
            cval = cbv + cwv[2:3] * v0 + cwv[1:2] * v1 + cwv[0:1] * v2
            cgate = cbg + cwg[2:3] * g0 + cwg[1:2] * g1 + cwg[0:1] * g2
            uv_ref[pl.ds(r0, r), :] = v0.astype(BF16)
            ug_ref[pl.ds(r0, r), :] = g0.astype(BF16)
            act_ref[pl.ds(r0, r), :] = (cgate * _sigmoid(cgate) * cval).astype(BF16)
            return carry

        lax.fori_loop(0, t // r, chunk, 0)

    col = lambda off: _spec((d, tn), lambda j: (0, j + off), lead)
    cw = lambda off: pl.BlockSpec((3, tn), lambda j: (0, j + off))
    cb = lambda off: pl.BlockSpec((1, tn), lambda j: (0, j + off))
    out = pl.BlockSpec((t, tn), lambda j: (0, j))
    return pl.pallas_call(
        body, name=name, grid=(nf,),
        in_specs=[pl.BlockSpec((t, d), lambda j: (0, 0)), col(0), col(nf), cw(0), cw(nf), cb(0), cb(nf)],
        out_specs=[out, out, out],
        out_shape=[jax.ShapeDtypeStruct((t, f), BF16)] * 3,
        scratch_shapes=[pltpu.VMEM((t + HALO, tn), F32), pltpu.VMEM((t + HALO, tn), F32)],
        compiler_params=_params("parallel"),
    )(h, w_up, w_up, conv_w, conv_w, conv_b, conv_b)


def ffn_bwd_fused(dout, w_down, up_val, up_gate, conv_w, conv_b, name, lead=None):
    t, d = dout.shape
    f = w_down.shape[-2]
    tn = _tile(f, 256)
    nf = f // tn
    r = min(CHUNK_ROWS, t)

    def body(do_ref, wd_ref, uv_ref, ug_ref, cwv_ref, cwg_ref, cbv_ref, cbg_ref,
             dv_ref, dg_ref, dcwv_ref, dcwg_ref, dcbv_ref, dcbg_ref, da, sv, sg, ev, eg):
        zero = jnp.zeros((HALO, tn), F32)
        sv[0:HALO, :] = zero
        sg[0:HALO, :] = zero
        ev[t:, :] = zero
        eg[t:, :] = zero
        da[...] = lax.dot_general(do_ref[...], wd_ref[...], (((1,), (1,)), ((), ())), preferred_element_type=F32)
        sv[HALO:, :] = uv_ref[...].astype(F32)
        sg[HALO:, :] = ug_ref[...].astype(F32)
        cwv, cwg = cwv_ref[...], cwg_ref[...]
        cbv, cbg = cbv_ref[...], cbg_ref[...]

        def chunk(i, acc):
            r0 = pl.multiple_of(i * r, r)
            v = _conv_taps(sv[pl.ds(r0, r + HALO), :], r)
            g = _conv_taps(sg[pl.ds(r0, r + HALO), :], r)
            cval = cbv + cwv[2:3] * v[0] + cwv[1:2] * v[1] + cwv[0:1] * v[2]
            cgate = cbg + cwg[2:3] * g[0] + cwg[1:2] * g[1] + cwg[0:1] * g[2]
            s = _sigmoid(cgate)
            dav = da[pl.ds(r0, r), :]
            dval = dav * (cgate * s)
            dgate = dav * cval * (s * (1.0 + cgate * (1.0 - s)))
            ev[pl.ds(r0, r), :] = dval
            eg[pl.ds(r0, r), :] = dgate
            col = lambda z: jnp.sum(z, axis=0, keepdims=True)
            new = [acc[0] + col(dval), acc[1] + col(dgate)]
            new += [acc[2 + j] + col(dval * v[2 - j]) for j in range(3)]
            new += [acc[5 + j] + col(dgate * g[2 - j]) for j in range(3)]
            return tuple(new)

        z1 = jnp.zeros((1, tn), F32)
        acc = lax.fori_loop(0, t // r, chunk, (z1,) * 8)
        dcbv_ref[...] = acc[0]
        dcbg_ref[...] = acc[1]
        for j in range(3):
            dcwv_ref[j:j + 1, :] = acc[2 + j]
            dcwg_ref[j:j + 1, :] = acc[5 + j]

        def chunk2(i, carry):
            r0 = pl.multiple_of(i * r, r)
            for e_ref, cw_, o_ref in ((ev, cwv, dv_ref), (eg, cwg, dg_ref)):
                ext = e_ref[pl.ds(r0, r + HALO), :]
                d0, d1, d2 = ext[:r], _shift_up(ext, 1)[:r], _shift_up(ext, 2)[:r]
                o_ref[pl.ds(r0, r), :] = (cw_[2:3] * d0 + cw_[1:2] * d1 + cw_[0:1] * d2).astype(BF16)
            return carry

        lax.fori_loop(0, t // r, chunk2, 0)

    cw = lambda off: pl.BlockSpec((3, tn), lambda j: (0, j + off))
    cb = lambda off: pl.BlockSpec((1, tn), lambda j: (0, j + off))
    tile = pl.BlockSpec((t, tn), lambda j: (0, j))
    s = lambda rows, dt: jax.ShapeDtypeStruct((rows, f), dt)
    halo = pltpu.VMEM((t + HALO, tn), F32)
    return pl.pallas_call(
        body, name=name, grid=(nf,),
        in_specs=[pl.BlockSpec((t, d), lambda j: (0, 0)), _spec((tn, d), lambda j: (j, 0), lead),
                  tile, tile, cw(0), cw(nf), cb(0), cb(nf)],
        out_specs=[tile, tile, pl.BlockSpec((3, tn), lambda j: (0, j)), pl.BlockSpec((3, tn), lambda j: (0, j)),
                   pl.BlockSpec((1, tn), lambda j: (0, j)), pl.BlockSpec((1, tn), lambda j: (0, j))],
        out_shape=[s(t, BF16), s(t, BF16), s(3, F32), s(3, F32), s(1, F32), s(1, F32)],
        scratch_shapes=[pltpu.VMEM((t, tn), F32), halo, halo, halo, halo],
        compiler_params=_params("parallel"),
    )(dout, w_down, up_val, up_gate, conv_w, conv_w, conv_b, conv_b)


POOL_PAD = max(POOL_WINDOWS)


def _window_sum(ext, win, shift):
    assert POOL_WINDOWS == (2, 4, 8, 16)
    s2 = ext + shift(ext, 1)
    s4 = s2 + shift(s2, 2)
    s8 = s4 + shift(s4, 4)
    s16 = s8 + shift(s8, 8)
    return jnp.where(win == 2, s2, jnp.where(win == 4, s4, jnp.where(win == 8, s8, s16)))


def _pool_win_scalar(g):
    win = jnp.int32(POOL_WINDOWS[-1])
    for k in range(len(POOL_WINDOWS) - 2, -1, -1):
        win = jnp.where(g == k, jnp.int32(POOL_WINDOWS[k]), win)
    return win


def _pool_count(r0, r, win):
    rows = r0 + lax.broadcasted_iota(jnp.int32, (r, 1), 0)
    return jnp.minimum(rows + 1, win).astype(F32)


def _pooled_into(hp, pooled, h_ref, t, r, win):
    hp[0:POOL_PAD, :] = jnp.zeros((POOL_PAD, hp.shape[1]), F32)
    hp[POOL_PAD:, :] = h_ref[...]

    def chunk(i, carry):
        r0 = pl.multiple_of(i * r, r)
        ext = hp[pl.ds(r0, r + POOL_PAD), :]
        s = _window_sum(ext, win, _shift_down)[POOL_PAD:]
        pooled[pl.ds(r0, r), :] = (s / _pool_count(r0, r, win) - ext[POOL_PAD:]).astype(BF16)
        return carry

    lax.fori_loop(0, t // r, chunk, 0)


def pool_fwd(h, x, w, b, scale, name, lead=None):
    t, d = h.shape
    ng, dg = w.shape[-3], w.shape[-2]
    r = min(CHUNK_ROWS, t)

    def body(h_ref, x_ref, w_ref, b_ref, s_ref, o_ref, hp, pooled):
        win = _pool_win_scalar(pl.program_id(0))
        _pooled_into(hp, pooled, h_ref, t, r, win)
        y = jnp.dot(pooled[...], w_ref[...], preferred_element_type=F32)
        o_ref[...] = x_ref[...] + (y + b_ref[...]) * s_ref[...]

    col = pl.BlockSpec((t, dg), lambda g: (0, g))
    vec = pl.BlockSpec((1, dg), lambda g: (0, g))
    return pl.pallas_call(
        body, name=name, grid=(ng,),
        in_specs=[col, col, _spec((None, dg, dg), lambda g: (g, 0, 0), lead), vec, vec],
        out_specs=col, out_shape=jax.ShapeDtypeStruct((t, d), F32),
        scratch_shapes=[pltpu.VMEM((t + POOL_PAD, dg), F32), pltpu.VMEM((t, dg), BF16)],
        compiler_params=_params("parallel"),
    )(h, x, w, b, scale)


def pool_bwd(dm, h, w, b, scale, name, lead=None):
    t, d = h.shape
    ng, dg = w.shape[-3], w.shape[-2]
    r = min(CHUNK_ROWS, t)

    def body(dm_ref, h_ref, w_ref, b_ref, s_ref, dh_ref, dw_ref, db_ref, ds_ref, hp, pooled, q):
        win = _pool_win_scalar(pl.program_id(0))
        _pooled_into(hp, pooled, h_ref, t, r, win)
        wv = w_ref[...]
        y = jnp.dot(pooled[...], wv, preferred_element_type=F32)
        dmv = dm_ref[...]
        ds_ref[...] = jnp.sum(dmv * (y + b_ref[...]), axis=0, keepdims=True)
        dy = dmv * s_ref[...]
        db_ref[...] = jnp.sum(dy, axis=0, keepdims=True)
        dyb = dy.astype(BF16)
        dw_ref[...] = lax.dot_general(pooled[...], dyb, (((0,), (0,)), ((), ())),
                                      preferred_element_type=F32).astype(dw_ref.dtype)
        dp = lax.dot_general(dyb, wv, (((1,), (1,)), ((), ())), preferred_element_type=F32)
        q[t:, :] = jnp.zeros((POOL_PAD, dg), F32)
        q[0:t, :] = dp / _pool_count(0, t, win)
        dh_ref[...] = -dp

        def chunk(i, carry):
            r0 = pl.multiple_of(i * r, r)
            ext = q[pl.ds(r0, r + POOL_PAD), :]
            dh_ref[pl.ds(r0, r), :] += _window_sum(ext, win, _shift_up)[:r]
            return carry

        lax.fori_loop(0, t // r, chunk, 0)

    col = pl.BlockSpec((t, dg), lambda g: (0, g))
    vec = pl.BlockSpec((1, dg), lambda g: (0, g))
    wspec = pl.BlockSpec((None, dg, dg), lambda g: (g, 0, 0))
    return pl.pallas_call(
        body, name=name, grid=(ng,),
        in_specs=[col, col, _spec((None, dg, dg), lambda g: (g, 0, 0), lead), vec, vec],
        out_specs=[col, wspec, vec, vec],
        out_shape=[jax.ShapeDtypeStruct((t, d), F32), jax.ShapeDtypeStruct((ng, dg, dg), F32),
                   jax.ShapeDtypeStruct((1, d), F32), jax.ShapeDtypeStruct((1, d), F32)],
        scratch_shapes=[pltpu.VMEM((t + POOL_PAD, dg), F32), pltpu.VMEM((t, dg), BF16),
                        pltpu.VMEM((t + POOL_PAD, dg), F32)],
        compiler_params=_params("parallel"),
    )(dm, h, w, b, scale)


SB_BLOCK = 256


def _tri_sum(v, tri):
    hi = v.astype(BF16)
    r1 = v - hi.astype(F32)
    mid = r1.astype(BF16)
    lo = (r1 - mid.astype(F32)).astype(BF16)
    dot = lambda p: jnp.dot(p, tri, preferred_element_type=F32)
    return dot(hi) + dot(mid) + dot(lo)


def _tri(bk, cmp):
    return cmp(lax.broadcasted_iota(jnp.int32, (bk, bk), 0), lax.broadcasted_iota(jnp.int32, (bk, bk), 1)).astype(BF16)


def _sb_logits(qblk, kblk, q0, k0, inv):
    bq, bk = qblk.shape[0], kblk.shape[0]
    z = lax.dot_general(qblk, kblk, (((1,), (1,)), ((), ())), preferred_element_type=F32) * inv
    qpos = q0 + lax.broadcasted_iota(jnp.int32, (bq, bk), 0)
    kpos = k0 + lax.broadcasted_iota(jnp.int32, (bq, bk), 1)
    mask = kpos < qpos
    lb = jnp.minimum(z, 0.0) - jnp.log(1.0 + jnp.exp(-jnp.abs(z)))
    lm = jnp.where(mask, lb - z, 0.0)
    return lb, lm, mask


def _head_norm(ref, gain):
    xv = ref[...].astype(F32)
    r = lax.rsqrt(jnp.mean(xv * xv, axis=-1, keepdims=True) + RMS_EPS)
    xhat = xv * r
    return xhat, r, (xhat * gain).astype(BF16)


def sb_attn_fwd(qkv, q_gain, k_gain, name):
    t = qkv.shape[0]
    d = qkv.shape[1] // 3
    dh = SB_HEAD_DIM
    nh = d // dh
    blk = min(SB_BLOCK, t)
    inv = 1.0 / math.sqrt(dh)

    def body(q_ref, k_ref, v_ref, qg_ref, kg_ref, o_ref, lt_ref, qn, kn):
        qn[...] = _head_norm(q_ref, qg_ref[...])[2]
        kn[...] = _head_norm(k_ref, kg_ref[...])[2]
        later = _tri(blk, lambda j, s: j > s)

        def q_loop(qb, carry):
            q0 = pl.multiple_of(qb * blk, blk)
            qblk = qn[pl.ds(q0, blk), :]

            def k_loop(i, st):
                c, acc = st
                k0 = pl.multiple_of((qb - i) * blk, blk)
                lb, lm, mask = _sb_logits(qblk, kn[pl.ds(k0, blk), :], q0, k0, inv)
                a = jnp.where(mask, jnp.exp(lb + _tri_sum(lm, later) + c), 0.0)
                acc = acc + jnp.dot(a.astype(BF16), v_ref[pl.ds(k0, blk), :], preferred_element_type=F32)
                return c + jnp.sum(lm, axis=1, keepdims=True), acc

            c, acc = lax.fori_loop(0, qb + 1, k_loop, (jnp.zeros((blk, 1), F32), jnp.zeros((blk, dh), F32)))
            o_ref[pl.ds(q0, blk), :] = acc.astype(BF16)
            lt_ref[pl.ds(q0, blk), :] = c
            return carry

        lax.fori_loop(0, t // blk, q_loop, 0)

    head = lambda off: pl.BlockSpec((t, dh), lambda h: (0, h + off))
    gain = pl.BlockSpec((1, dh), lambda h: (0, 0))
    return pl.pallas_call(
        body, name=name, grid=(nh,),
        in_specs=[head(0), head(nh), head(2 * nh), gain, gain],
        out_specs=[head(0), pl.BlockSpec((None, t, 1), lambda h: (h, 0, 0))],
        out_shape=[jax.ShapeDtypeStruct((t, d), BF16), jax.ShapeDtypeStruct((nh, t, 1), F32)],
        scratch_shapes=[pltpu.VMEM((t, dh), BF16), pltpu.VMEM((t, dh), BF16)],
        compiler_params=_params("parallel"),
    )(qkv, qkv, qkv, q_gain, k_gain)


def sb_attn_bwd(qkv, ltot, do, q_gain, k_gain, name):
    t = qkv.shape[0]
    d = qkv.shape[1] // 3
    dh = SB_HEAD_DIM
    nh = d // dh
    blk = min(SB_BLOCK, t)
    inv = 1.0 / math.sqrt(dh)
    tn_dims = (((0,), (0,)), ((), ()))

    def body(q_ref, k_ref, v_ref, lt_ref, do_ref, qg_ref, kg_ref, dq_ref, dk_ref, dv_ref, dqg_ref, dkg_ref,
             qn, kn, dqn, dkn, dvn):
        qg, kg = qg_ref[...], kg_ref[...]
        qhat, rq, qnb = _head_norm(q_ref, qg)
        khat, rk, knb = _head_norm(k_ref, kg)
        qn[...] = qnb
        kn[...] = knb
        dkn[...] = jnp.zeros_like(dkn)
        dvn[...] = jnp.zeros_like(dvn)
        upto = _tri(blk, lambda j, s: j <= s)
        before = _tri(blk, lambda j, s: j < s)

        def q_loop(qb, carry):
            q0 = pl.multiple_of(qb * blk, blk)
            qblk = qn[pl.ds(q0, blk), :]
            doblk = do_ref[pl.ds(q0, blk), :]
            ltv = lt_ref[pl.ds(q0, blk), :]

            def k_loop(kb, st):
                pl_, pg, dq = st
                k0 = pl.multiple_of(kb * blk, blk)
                kblk = kn[pl.ds(k0, blk), :]
                lb, lm, mask = _sb_logits(qblk, kblk, q0, k0, inv)
                a = jnp.where(mask, jnp.exp(lb + (ltv - pl_ - _tri_sum(lm, upto))), 0.0)
                da = lax.dot_general(doblk, v_ref[pl.ds(k0, blk), :], (((1,), (1,)), ((), ())),
                                     preferred_element_type=F32)
                g = da * a
                g_before = pg + _tri_sum(g, before)
                beta = jnp.exp(lb)
                dz = (jnp.where(mask, g * (1.0 - beta) - beta * g_before, 0.0) * inv).astype(BF16)
                dq = dq + jnp.dot(dz, kblk, preferred_element_type=F32)
                dkn[pl.ds(k0, blk), :] += lax.dot_general(dz, qblk, tn_dims, preferred_element_type=F32)
                dvn[pl.ds(k0, blk), :] += lax.dot_general(a.astype(BF16), doblk, tn_dims, preferred_element_type=F32)
                return pl_ + jnp.sum(lm, axis=1, keepdims=True), pg + jnp.sum(g, axis=1, keepdims=True), dq

            z1 = jnp.zeros((blk, 1), F32)
            _, _, dq = lax.fori_loop(0, qb + 1, k_loop, (z1, z1, jnp.zeros((blk, dh), F32)))
            dqn[pl.ds(q0, blk), :] = dq
            return carry

        lax.fori_loop(0, t // blk, q_loop, 0)

        first = pl.program_id(0) == 0
        for dn, xhat, r, gain, out_ref, dgain_ref in ((dqn, qhat, rq, qg, dq_ref, dqg_ref),
                                                      (dkn, khat, rk, kg, dk_ref, dkg_ref)):
            dnv = dn[...]
            dxhat = dnv * gain
            out_ref[...] = (r * (dxhat - xhat * jnp.mean(dxhat * xhat, axis=-1, keepdims=True))).astype(BF16)
            part = jnp.sum(dnv * xhat, axis=0, keepdims=True)

            @pl.when(first)
            def _(dgain_ref=dgain_ref, part=part):
                dgain_ref[...] = part

            @pl.when(jnp.logical_not(first))
            def _(dgain_ref=dgain_ref, part=part):
                dgain_ref[...] += part

        dv_ref[...] = dvn[...].astype(BF16)

    head = lambda off: pl.BlockSpec((t, dh), lambda h: (0, h + off))
    gain = pl.BlockSpec((1, dh), lambda h: (0, 0))
    big = jax.ShapeDtypeStruct((t, d), BF16)
    small = jax.ShapeDtypeStruct((1, dh), F32)
    return pl.pallas_call(
        body, name=name, grid=(nh,),
        in_specs=[head(0), head(nh), head(2 * nh), pl.BlockSpec((None, t, 1), lambda h: (h, 0, 0)), head(0),
                  gain, gain],
        out_specs=[head(0), head(0), head(0), gain, gain],
        out_shape=[big, big, big, small, small],
        scratch_shapes=[pltpu.VMEM((t, dh), BF16), pltpu.VMEM((t, dh), BF16),
                        pltpu.VMEM((t, dh), F32), pltpu.VMEM((t, dh), F32), pltpu.VMEM((t, dh), F32)],
        compiler_params=_params("arbitrary"),
    )(qkv, qkv, qkv, ltot, do, q_gain, k_gain)


GELU_C = math.sqrt(2.0 / math.pi)
GELU_A = 0.044715
SCAN_ROWS = SUBLANES


def _gelu(y):
    return 0.5 * y * (1.0 + jnp.tanh(GELU_C * (y + GELU_A * y * y * y)))


def _gelu_grad(y):
    th = jnp.tanh(GELU_C * (y + GELU_A * y * y * y))
    return 0.5 * (1.0 + th) + 0.5 * y * (1.0 - th * th) * GELU_C * (1.0 + 3.0 * GELU_A * y * y)


def _powers(ar, ai):
    out = [(ar, ai)]
    for _ in range(SCAN_ROWS - 1):
        pr, pi = out[-1]
        out.append((pr * ar - pi * ai, pr * ai + pi * ar))
    return out


def _rows(vals):
    c = vals[0].shape[1]
    row = lax.broadcasted_iota(jnp.int32, (SCAN_ROWS, c), 0)
    out = jnp.broadcast_to(vals[SCAN_ROWS - 1], (SCAN_ROWS, c))
    for j in range(SCAN_ROWS - 2, -1, -1):
        out = jnp.where(row == j, vals[j], out)
    return out


def _scan_forward(sr, si, off, t, ar, ai):
    c = ar.shape[1]
    p = _powers(ar, ai)
    pwr = _rows([q[0] for q in p])
    pwi = _rows([q[1] for q in p])
    row = lax.broadcasted_iota(jnp.int32, (SCAN_ROWS, c), 0)

    def tile(i, carry):
        cr, ci = carry
        r0 = pl.multiple_of(off + i * SCAN_ROWS, SCAN_ROWS)
        xr = sr[pl.ds(r0, SCAN_ROWS), :]
        xi = si[pl.ds(r0, SCAN_ROWS), :]
        for k in (1, 2, 4):
            pr, pi = p[k - 1]
            shr = jnp.where(row >= k, _shift_down(xr, k), 0.0)
            shi = jnp.where(row >= k, _shift_down(xi, k), 0.0)
            xr, xi = xr + pr * shr - pi * shi, xi + pr * shi + pi * shr
        xr, xi = xr + pwr * cr - pwi * ci, xi + pwr * ci + pwi * cr
        sr[pl.ds(r0, SCAN_ROWS), :] = xr
        si[pl.ds(r0, SCAN_ROWS), :] = xi
        return xr[SCAN_ROWS - 1:SCAN_ROWS], xi[SCAN_ROWS - 1:SCAN_ROWS]

    z = jnp.zeros((1, c), F32)
    lax.fori_loop(0, t // SCAN_ROWS, tile, (z, z))


def _scan_reverse(gr, gi, t, ar, ai, xr_ref, xi_ref):
    c = ar.shape[1]
    p = _powers(ar, ai)
    pwr = _rows([p[SCAN_ROWS - 1 - j][0] for j in range(SCAN_ROWS)])
    pwi = _rows([p[SCAN_ROWS - 1 - j][1] for j in range(SCAN_ROWS)])
    row = lax.broadcasted_iota(jnp.int32, (SCAN_ROWS, c), 0)
    n = t // SCAN_ROWS

    def tile(ii, carry):
        cr, ci, dar, dai = carry
        r0 = pl.multiple_of((n - 1 - ii) * SCAN_ROWS, SCAN_ROWS)
        xr = gr[pl.ds(r0, SCAN_ROWS), :]
        xi = gi[pl.ds(r0, SCAN_ROWS), :]
        for k in (1, 2, 4):
            pr, pi = p[k - 1]
            shr = jnp.where(row < SCAN_ROWS - k, _shift_up(xr, k), 0.0)
            shi = jnp.where(row < SCAN_ROWS - k, _shift_up(xi, k), 0.0)
            xr, xi = xr + pr * shr + pi * shi, xi + pr * shi - pi * shr
        xr, xi = xr + pwr * cr + pwi * ci, xi + pwr * ci - pwi * cr
        gr[pl.ds(r0, SCAN_ROWS), :] = xr
        gi[pl.ds(r0, SCAN_ROWS), :] = xi
        xpr = _shift_down(xr_ref[pl.ds(r0, 2 * SCAN_ROWS), :], 1)[SCAN_ROWS:]
        xpi = _shift_down(xi_ref[pl.ds(r0, 2 * SCAN_ROWS), :], 1)[SCAN_ROWS:]
        return xr[0:1], xi[0:1], dar + xr * xpr + xi * xpi, dai + xi * xpr - xr * xpi

    z = jnp.zeros((1, c), F32)
    z8 = jnp.zeros((SCAN_ROWS, c), F32)
    _, _, dar, dai = lax.fori_loop(0, n, tile, (z, z, z8, z8))
    return jnp.sum(dar, axis=0, keepdims=True), jnp.sum(dai, axis=0, keepdims=True)


def _ssm_specs(t, nb, ch, st):
    col = pl.BlockSpec((t, ch), lambda b: (0, b))
    vec = pl.BlockSpec((1, ch), lambda b: (0, b))
    bspec = pl.BlockSpec((None, ch, st), lambda b: (b, 0, 0))
    cspec = pl.BlockSpec((None, st, ch), lambda b: (b, 0, 0))
    aspec = pl.BlockSpec((None, 1, st), lambda b: (b, 0, 0))
    return col, vec, bspec, cspec, aspec


def ssm_core_fwd(u, bre, bim, cre, cim, a_re, a_im, dskip, name):
    t, d = u.shape
    nb, ch, st = bre.shape

    def body(u_ref, bre_ref, bim_ref, cre_ref, cim_ref, ar_ref, ai_ref, d_ref, y_ref, yg_ref, sr, si):
        uv = u_ref[...]
        ub = uv.astype(BF16)
        sr[...] = jnp.dot(ub, bre_ref[...], preferred_element_type=F32)
        si[...] = jnp.dot(ub, bim_ref[...], preferred_element_type=F32)
        _scan_forward(sr, si, 0, t, ar_ref[...], ai_ref[...])
        y = (jnp.dot(sr[...].astype(BF16), cre_ref[...], preferred_element_type=F32)
             - jnp.dot(si[...].astype(BF16), cim_ref[...], preferred_element_type=F32) + d_ref[...] * uv)
        y_ref[...] = y
        yg_ref[...] = _gelu(y).astype(BF16)

    col, vec, bspec, cspec, aspec = _ssm_specs(t, nb, ch, st)
    return pl.pallas_call(
        body, name=name, grid=(nb,),
        in_specs=[col, bspec, bspec, cspec, cspec, aspec, aspec, vec],
        out_specs=[col, col],
        out_shape=[jax.ShapeDtypeStruct((t, d), F32), jax.ShapeDtypeStruct((t, d), BF16)],
        scratch_shapes=[pltpu.VMEM((t, st), F32), pltpu.VMEM((t, st), F32)],
        compiler_params=_params("parallel"),
    )(u, bre, bim, cre, cim, a_re, a_im, dskip)


def ssm_core_bwd(u, y, dyg, bre, bim, cre, cim, a_re, a_im, dskip, name):
    t, d = u.shape
    nb, ch, st = bre.shape
    tn_dims = (((0,), (0,)), ((), ()))
    nt_dims = (((1,), (1,)), ((), ()))

    def body(u_ref, y_ref, dyg_ref, bre_ref, bim_ref, cre_ref, cim_ref, ar_ref, ai_ref, d_ref,
             du_ref, dd_ref, dbre_ref, dbim_ref, dcre_ref, dcim_ref, dar_ref, dai_ref, xr, xi, gr, gi):
        uv = u_ref[...]
        ub = uv.astype(BF16)
        ar, ai = ar_ref[...], ai_ref[...]
        dy = dyg_ref[...] * _gelu_grad(y_ref[...])
        dd_ref[...] = jnp.sum(dy * uv, axis=0, keepdims=True)
        zero = jnp.zeros((HALO, st), F32)
        xr[0:HALO, :] = zero
        xi[0:HALO, :] = zero
        xr[HALO:, :] = jnp.dot(ub, bre_ref[...], preferred_element_type=F32)
        xi[HALO:, :] = jnp.dot(ub, bim_ref[...], preferred_element_type=F32)
        _scan_forward(xr, xi, HALO, t, ar, ai)
        dyb = dy.astype(BF16)
        dcre_ref[...] = lax.dot_general(xr[HALO:, :].astype(BF16), dyb, tn_dims, preferred_element_type=F32)
        dcim_ref[...] = -lax.dot_general(xi[HALO:, :].astype(BF16), dyb, tn_dims, preferred_element_type=F32)
        gr[...] = lax.dot_general(dyb, cre_ref[...], nt_dims, preferred_element_type=F32)
        gi[...] = -lax.dot_general(dyb, cim_ref[...], nt_dims, preferred_element_type=F32)
        dar, dai = _scan_reverse(gr, gi, t, ar, ai, xr, xi)
        dar_ref[...] = dar
        dai_ref[...] = dai
        grb = gr[...].astype(BF16)
        gib = gi[...].astype(BF16)
        dbre_ref[...] = lax.dot_general(ub, grb, tn_dims, preferred_element_type=F32)
        dbim_ref[...] = lax.dot_general(ub, gib, tn_dims, preferred_element_type=F32)
        du_ref[...] = (d_ref[...] * dy + lax.dot_general(grb, bre_ref[...], nt_dims, preferred_element_type=F32)
                       + lax.dot_general(gib, bim_ref[...], nt_dims, preferred_element_type=F32))

    col, vec, bspec, cspec, aspec = _ssm_specs(t, nb, ch, st)
    sh = jax.ShapeDtypeStruct
    return pl.pallas_call(
        body, name=name, grid=(nb,),
        in_specs=[col, col, col, bspec, bspec, cspec, cspec, aspec, aspec, vec],
        out_specs=[col, vec, bspec, bspec, cspec, cspec, aspec, aspec],
        out_shape=[sh((t, d), F32), sh((1, d), F32), sh((nb, ch, st), F32), sh((nb, ch, st), F32),
                   sh((nb, st, ch), F32), sh((nb, st, ch), F32), sh((nb, 1, st), F32), sh((nb, 1, st), F32)],
        scratch_shapes=[pltpu.VMEM((t + HALO, st), F32), pltpu.VMEM((t + HALO, st), F32),
                        pltpu.VMEM((t, st), F32), pltpu.VMEM((t, st), F32)],
        compiler_params=_params("parallel"),
    )(u, y, dyg, bre, bim, cre, cim, a_re, a_im, dskip)


def glu_fwd(yg, w_glu, b_glu, x, name, lead=None):
    t, d = yg.shape
    tn = _tile(d, 256)
    nd = d // tn

    def body(yg_ref, wv_ref, wg_ref, bv_ref, bg_ref, x_ref, val_ref, gate_ref, o_ref):
        ygv = yg_ref[...]
        vb = (jnp.dot(ygv, wv_ref[...], preferred_element_type=F32) + bv_ref[...]).astype(BF16)
        gb = (jnp.dot(ygv, wg_ref[...], preferred_element_type=F32) + bg_ref[...]).astype(BF16)
        val_ref[...] = vb
        gate_ref[...] = gb
        o_ref[...] = x_ref[...] + vb.astype(F32) * _sigmoid(gb.astype(F32))

    col = lambda off: _spec((d, tn), lambda j: (0, j + off), lead)
    vec = lambda off: pl.BlockSpec((1, tn), lambda j: (0, j + off))
    tile = pl.BlockSpec((t, tn), lambda j: (0, j))
    return pl.pallas_call(
        body, name=name, grid=(nd,),
        in_specs=[pl.BlockSpec((t, d), lambda j: (0, 0)), col(0), col(nd), vec(0), vec(nd), tile],
        out_specs=[tile, tile, tile],
        out_shape=[jax.ShapeDtypeStruct((t, d), BF16), jax.ShapeDtypeStruct((t, d), BF16),
                   jax.ShapeDtypeStruct((t, d), F32)],
        compiler_params=_params("parallel"),
    )(yg, w_glu, w_glu, b_glu, b_glu, x)


def glu_bwd(dm, val, gate, name):
    t, d = dm.shape
    tn = _tile(d, 256)

    def body(dm_ref, val_ref, gate_ref, dv_ref, dg_ref, dbv_ref, dbg_ref):
        dmv = dm_ref[...]
        s = _sigmoid(gate_ref[...].astype(F32))
        dval = dmv * s
        dgate = dmv * val_ref[...].astype(F32) * s * (1.0 - s)
        dv_ref[...] = dval.astype(BF16)
        dg_ref[...] = dgate.astype(BF16)
        dbv_ref[...] = jnp.sum(dval, axis=0, keepdims=True)
        dbg_ref[...] = jnp.sum(dgate, axis=0, keepdims=True)

    tile = pl.BlockSpec((t, tn), lambda j: (0, j))
    vec = pl.BlockSpec((1, tn), lambda j: (0, j))
    return pl.pallas_call(
        body, name=name, grid=(d // tn,),
        in_specs=[tile, tile, tile], out_specs=[tile, tile, vec, vec],
        out_shape=[jax.ShapeDtypeStruct((t, d), BF16), jax.ShapeDtypeStruct((t, d), BF16),
                   jax.ShapeDtypeStruct((1, d), F32), jax.ShapeDtypeStruct((1, d), F32)],
        compiler_params=_params("parallel"),
    )(dm, val, gate)


def _block_diag(m, gb):
    g, a, b = m.shape
    eye = jnp.eye(gb, dtype=m.dtype)
    return jnp.einsum("ngab,gk->ngakb", m.reshape(g // gb, gb, a, b), eye).reshape(g // gb, gb * a, gb * b)


def ssm_prepare(lam_re, lam_im, log_step, b_re, b_im, c_re, c_im):
    gb = SSM_BLOCK_GROUPS
    g, p = lam_re.shape
    step = jnp.exp(log_step)[:, None]
    mag = jnp.exp(lam_re * step)
    lb_re = mag * jnp.cos(lam_im * step)
    lb_im = mag * jnp.sin(lam_im * step)
    den = lam_re * lam_re + lam_im * lam_im
    f_re = ((lb_re - 1.0) * lam_re + lb_im * lam_im) / den
    f_im = (lb_im * lam_re - (lb_re - 1.0) * lam_im) / den
    bb_re = f_re[..., None] * b_re - f_im[..., None] * b_im
    bb_im = f_re[..., None] * b_im + f_im[..., None] * b_re
    tr = lambda m: jnp.transpose(m, (0, 2, 1))
    return (_block_diag(tr(bb_re), gb), _block_diag(tr(bb_im), gb), _block_diag(tr(c_re), gb), _block_diag(tr(c_im), gb),
            lb_re.reshape(g // gb, 1, gb * p), lb_im.reshape(g // gb, 1, gb * p))


EW_BLOCK_BYTES = 2 * 1024 * 1024
BF16_ROWS = 16


def _row_tile(rows, cols, block_bytes=EW_BLOCK_BYTES):
    limit = max(BF16_ROWS, block_bytes // (cols * 4))
    best = None
    for tr in range(BF16_ROWS, min(rows, limit) + 1, BF16_ROWS):
        if rows % tr == 0:
            best = tr
    return best if best is not None else rows


def _as2d(a):
    return a.reshape(-1, a.shape[-1])


def ew(fn, ins, out_dtypes, name):
    rows, cols = ins[0].shape
    tr = _row_tile(rows, cols)
    n_in = len(ins)

    def body(*refs):
        outs = fn(*[r[...] for r in refs[:n_in]])
        for o_ref, v in zip(refs[n_in:], outs):
            o_ref[...] = v.astype(o_ref.dtype)

    spec = pl.BlockSpec((tr, cols), lambda i: (i, 0))
    return pl.pallas_call(
        body, name=name, grid=(rows // tr,), in_specs=[spec] * n_in, out_specs=[spec] * len(out_dtypes),
        out_shape=[jax.ShapeDtypeStruct((rows, cols), dt) for dt in out_dtypes],
        compiler_params=_params("parallel"),
    )(*ins)


def _adamw(w, g, m, v):
    m = ADAM_B1 * m + (1.0 - ADAM_B1) * g
    v = ADAM_B2 * v + (1.0 - ADAM_B2) * (g * g)
    m_hat = m / (1.0 - ADAM_B1 ** ADAM_STEP)
    v_hat = v / (1.0 - ADAM_B2 ** ADAM_STEP)
    delta = -ADAM_LR * (m_hat / (jnp.sqrt(v_hat) + ADAM_EPS) + ADAM_WD * w)
    return delta, m, v


def adamw(w, g, m, v, name):
    outs = ew(_adamw, [_as2d(w), _as2d(g), _as2d(m), _as2d(v)], [F32, F32, F32], name)
    return [o.reshape(w.shape) for o in outs]


def adamw_layer(w, g, m, v, layer, name, into=None):
    nl, r, cw = w.shape
    tr = _row_tile(r, cw)

    def body(w_ref, g_ref, m_ref, v_ref, *rest):
        g_out, d_out, m_out, v_out = rest[-4:]
        gv = g_ref[...]
        d_out[...], m_out[...], v_out[...] = _adamw(w_ref[...], gv, m_ref[...], v_ref[...])
        g_out[...] = gv

    lay = pl.BlockSpec((None, tr, cw), lambda i: (layer, i, 0))
    args = [w, g, m, v]
    in_specs = [lay, pl.BlockSpec((None, tr, cw), lambda i: (0, i, 0)), lay, lay]
    aliases = {}
    if into is not None:
        args += list(into)
        in_specs += [pl.BlockSpec(memory_space=pl.ANY)] * 4
        aliases = {4 + k: k for k in range(4)}
    return pl.pallas_call(
        body, name=name, grid=(r // tr,), in_specs=in_specs, out_specs=[lay] * 4,
        out_shape=[jax.ShapeDtypeStruct((nl, r, cw), F32)] * 4,
        input_output_aliases=aliases,
        compiler_params=_params("parallel"),
    )(*args)


def loss_head(y, target, name):
    t, d = y.shape
    tr = min(t, 128)
    n = t // tr

    def body(y_ref, t_ref, dy_ref, dyb_ref, loss_ref, acc):
        i = pl.program_id(0)
        err = y_ref[...] - t_ref[...]
        dy = err * (1.0 / d)
        dy_ref[...] = dy
        dyb_ref[...] = dy.astype(BF16)
        part = jnp.sum(err * err, axis=0, keepdims=True)

        @pl.when(i == 0)
        def _():
            acc[...] = part

        @pl.when(i != 0)
        def _():
            acc[...] += part

        @pl.when(i == n - 1)
        def _():
            loss_ref[...] = jnp.full((1, LANES), 0.5 / d, F32) * jnp.sum(acc[...])

    row = pl.BlockSpec((tr, d), lambda i: (i, 0))
    return pl.pallas_call(
        body, name=name, grid=(n,), in_specs=[row, row],
        out_specs=[row, row, pl.BlockSpec((1, LANES), lambda i: (0, 0))],
        out_shape=[jax.ShapeDtypeStruct((t, d), F32), jax.ShapeDtypeStruct((t, d), BF16),
                   jax.ShapeDtypeStruct((1, LANES), F32)],
        scratch_shapes=[pltpu.VMEM((1, d), F32)],
        compiler_params=_params("arbitrary"),
    )(y, target)


HBM_SPEC = pl.BlockSpec(memory_space=pltpu.HBM)
VMEM_SPEC = pl.BlockSpec(memory_space=pltpu.VMEM)


def _place():
    return lax.axis_index("x"), lax.axis_index("y"), lax.axis_index("c")


def _other_chips(x, y):
    return [(1 - x, y), (x, 1 - y), (1 - x, 1 - y)]


def _remote(src, dst, send_sem, recv_sem, dev):
    return pltpu.make_async_remote_copy(src_ref=src, dst_ref=dst, send_sem=send_sem, recv_sem=recv_sem,
                                        device_id=dev, device_id_type=MESH)


def _piece(refs, shard_shape, ax, j, half):
    w = shard_shape[ax]
    a, off = divmod(j * w, refs[0].shape[ax]) if isinstance(j, int) else (0, j * w)
    idx = [pl.ds(0, s) for s in shard_shape]
    idx[ax] = pl.ds(off, w)
    if half is not None:
        h0 = shard_shape[0] // 2
        idx[0] = pl.ds((off if ax == 0 else 0) + half * h0, h0)
    return refs[a].at[tuple(idx)]


def small_allreduce(v, name):
    n, r, l = v.shape
    assert n == N_DEV

    def body(v_ref, o_ref, recv, red, send1, recv1, send2, recv2):
        x, y, c = _place()
        me = 4 * x + 2 * y + c
        dev = lambda k: (k // 4, (k // 2) % 2, k % 2)
        firsts = []
        for o in range(1, N_DEV):
            tgt = (me + o) % N_DEV
            cp = _remote(v_ref.at[tgt], recv.at[me], send1.at[o], recv1.at[me], dev(tgt))
            cp.start()
            firsts.append(cp)
        recv[me] = v_ref[me]
        for o in range(1, N_DEV):
            src = (me + o) % N_DEV
            _remote(v_ref.at[src], recv.at[src], send1.at[o], recv1.at[src], dev(src)).wait_recv()
        acc = recv[0]
        for s in range(1, N_DEV):
            acc = acc + recv[s]
        red[...] = acc
        o_ref[me] = acc
        seconds = []
        for o in range(1, N_DEV):
            tgt = (me + o) % N_DEV
            cp = _remote(red, o_ref.at[me], send2.at[o], recv2.at[me], dev(tgt))
            cp.start()
            seconds.append(cp)
        for o in range(1, N_DEV):
            src = (me + o) % N_DEV
            _remote(red, o_ref.at[src], send2.at[o], recv2.at[src], dev(src)).wait_recv()
        for cp in firsts + seconds:
            cp.wait_send()

    sems = pltpu.SemaphoreType.DMA((N_DEV,))
    return pl.pallas_call(
        body, name=name, in_specs=[VMEM_SPEC], out_specs=VMEM_SPEC,
        out_shape=jax.ShapeDtypeStruct(v.shape, F32),
        scratch_shapes=[pltpu.VMEM((N_DEV, r, l), F32), pltpu.VMEM((r, l), F32), sems, sems, sems, sems],
        compiler_params=pltpu.CompilerParams(vmem_limit_bytes=VMEM_LIMIT),
    )(v)


def _me_scalar():
    return (2 * lax.axis_index("x") + lax.axis_index("y")).astype(jnp.int32).reshape(1)


def cast_into_gathered(wf, layer, axis, me1, name):
    _, r, cw = wf.shape
    tr = _row_tile(r, cw)
    nrb = r // tr
    full = (1, r * N_CHIPS, cw) if axis == 0 else (1, r, cw * N_CHIPS)
    omap = (lambda i, me: (0, me[0] * nrb + i, 0)) if axis == 0 else (lambda i, me: (0, i, me[0]))

    def body(me_ref, w_ref, o_ref):
        o_ref[...] = w_ref[...].astype(BF16)

    return pl.pallas_call(
        body, name=name,
        grid_spec=pltpu.PrefetchScalarGridSpec(
            num_scalar_prefetch=1, grid=(nrb,),
            in_specs=[pl.BlockSpec((None, tr, cw), lambda i, me: (layer, i, 0))],
            out_specs=pl.BlockSpec((None, tr, cw), omap)),
        out_shape=jax.ShapeDtypeStruct(full, BF16),
        compiler_params=_params("parallel"),
    )(me1, wf)


SEM_SPEC = pl.BlockSpec(memory_space=pltpu.SEMAPHORE)
SPLIT_COPY_PARAMS = pltpu.CompilerParams(has_side_effects=pltpu.SideEffectType.DATAFLOW_SIDE_EFFECTING)
TOKEN_SHAPE = (SUBLANES, LANES)


def _hbm(a):
    return pltpu.with_memory_space_constraint(a, pltpu.HBM)


def _gather_copies(refs, shapes, axes, send_sem, recv_sem):
    x, y, c = _place()
    me = 2 * x + y
    out = []
    for p, ref in enumerate(refs):
        place = lambda j: _piece([ref.at[0]], shapes[p], axes[p], j, c)
        for q, chip in enumerate(_other_chips(x, y)):
            dev = (chip[0], chip[1], c)
            sems = (send_sem.at[3 * p + q], recv_sem.at[3 * p + q])
            theirs = place(2 * chip[0] + chip[1])
            out.append((_remote(place(me), place(me), *sems, dev), _remote(theirs, theirs, *sems, dev)))
    return out


def gather_start(bufs, shapes, axes, after, name):
    n = len(bufs)

    def body(*refs):
        send_sem, recv_sem = refs[n + 1:n + 3]
        o_refs = refs[n + 3:2 * n + 3]
        token = refs[-1]
        for mine, _ in _gather_copies(o_refs, shapes, axes, send_sem, recv_sem):
            mine.start()
        token[...] = jnp.zeros(TOKEN_SHAPE, F32)

    sems = pltpu.SemaphoreType.DMA((3 * n,))
    outs = pl.pallas_call(
        body, name=name,
        in_specs=[HBM_SPEC] * n + [pl.BlockSpec(memory_space=pl.ANY)],
        out_specs=[SEM_SPEC, SEM_SPEC] + [HBM_SPEC] * n + [VMEM_SPEC],
        out_shape=[sems, sems] + [pltpu.HBM(b.shape, b.dtype) for b in bufs] + [jax.ShapeDtypeStruct(TOKEN_SHAPE, F32)],
        input_output_aliases={p: p + 2 for p in range(n)},
        compiler_params=SPLIT_COPY_PARAMS,
    )(*[_hbm(b) for b in bufs], after)
    return outs[0], outs[1], list(outs[2:2 + n]), outs[-1]


def gather_wait(send_sem, recv_sem, bufs, shapes, axes, after, name):
    n = len(bufs)

    def body(*refs):
        s_sem, r_sem = refs[n:n + 2]
        o_refs = refs[n + 3:]
        for mine, theirs in _gather_copies(o_refs, shapes, axes, s_sem, r_sem):
            mine.wait_send()
            theirs.wait_recv()

    return pl.pallas_call(
        body, name=name,
        in_specs=[HBM_SPEC] * n + [SEM_SPEC, SEM_SPEC, pl.BlockSpec(memory_space=pl.ANY)],
        out_specs=[HBM_SPEC] * n,
        out_shape=[pltpu.HBM(b.shape, b.dtype) for b in bufs],
        input_output_aliases={p: p for p in range(n)},
        compiler_params=SPLIT_COPY_PARAMS,
    )(*bufs, send_sem, recv_sem, after)


SIBLING_SLOTS = 2
SIBLING_BLOCK_BYTES = 8 * 1024 * 1024
SHARE_BLOCK_BYTES = 4 * 1024 * 1024


def _row_step(nrb):
    s = pl.program_id(0)
    for ax in range(1, len(nrb)):
        s = s * nrb[ax] + pl.program_id(ax)
    return s


def gather_forward(buf, axis, r, cw, me1, name):
    nl = buf.shape[0]
    h0 = r // 2
    tr = _row_tile(h0, cw, SIBLING_BLOCK_BYTES)
    nrb = h0 // tr
    peer = lambda q, me: (me[0] + q + 1) % N_CHIPS
    if axis == 1:
        view = buf.reshape(nl, 1, 2, h0, N_CHIPS * cw)
        spec = pl.BlockSpec((1, 1, 2, tr, cw), lambda l, q, i, me: (l, 0, 0, i, peer(q, me)))
    else:
        view = buf.reshape(nl, N_CHIPS, 2, h0, cw)
        spec = pl.BlockSpec((1, 1, 2, tr, cw), lambda l, q, i, me: (l, peer(q, me), 0, i, 0))

    def body(me_ref, in_ref, o_ref, rbuf, send_sem, recv_sem):
        x, y, c = _place()
        slot = _row_step((nl, N_CHIPS - 1, nrb)) % SIBLING_SLOTS
        cp = _remote(in_ref.at[0, 0, c], rbuf.at[slot], send_sem.at[slot], recv_sem.at[slot], (x, y, 1 - c))
        cp.start()
        o_ref[0, 0, c] = in_ref[0, 0, c]
        cp.wait_recv()
        o_ref[0, 0, 1 - c] = rbuf[slot]
        cp.wait_send()

    out = pl.pallas_call(
        body, name=name,
        grid_spec=pltpu.PrefetchScalarGridSpec(
            num_scalar_prefetch=1, grid=(nl, N_CHIPS - 1, nrb), in_specs=[spec], out_specs=spec,
            scratch_shapes=[pltpu.VMEM((SIBLING_SLOTS, tr, cw), buf.dtype),
                            pltpu.SemaphoreType.DMA((SIBLING_SLOTS,)), pltpu.SemaphoreType.DMA((SIBLING_SLOTS,))]),
        out_shape=jax.ShapeDtypeStruct(view.shape, view.dtype),
        input_output_aliases={1: 0},
        compiler_params=_params("arbitrary", "arbitrary", "arbitrary"),
    )(me1, view)
    return out.reshape(buf.shape)


def pair_reduce(g, axis, r, cw, name, into=None, first_slot=0):
    h0 = r // 2
    tr = _row_tile(h0, cw, SIBLING_BLOCK_BYTES)
    nrb = h0 // tr
    if axis == 1:
        n_sh = g.shape[1] // cw
        view = g.reshape(1, 2, h0, n_sh * cw)
        spec = pl.BlockSpec((1, 2, tr, cw), lambda j, i: (0, 0, i, j))
    else:
        n_sh = g.shape[0] // r
        view = g.reshape(n_sh, 2, h0, cw)
        spec = pl.BlockSpec((1, 2, tr, cw), lambda j, i: (j, 0, i, 0))

    def body(g_ref, *rest):
        o_ref, rbuf, send_sem, recv_sem = rest[-4:]
        x, y, c = _place()
        slot = _row_step((n_sh, nrb)) % SIBLING_SLOTS
        cp = _remote(g_ref.at[0, 1 - c], rbuf.at[slot], send_sem.at[slot], recv_sem.at[slot], (x, y, 1 - c))
        cp.start()
        mine = g_ref[0, c].astype(F32)
        cp.wait_recv()
        o_ref[0] = (mine + rbuf[slot].astype(F32)).astype(BF16)
        cp.wait_send()

    args, in_specs, aliases = [view], [spec], {}
    if into is not None:
        args.append(into)
        in_specs.append(pl.BlockSpec(memory_space=pl.ANY))
        aliases = {1: 0}
    return pl.pallas_call(
        body, name=name, grid=(n_sh, nrb), in_specs=in_specs,
        out_specs=pl.BlockSpec((1, tr, cw), lambda j, i: (j + first_slot, i, 0)),
        out_shape=jax.ShapeDtypeStruct((N_CHIPS, h0, cw), BF16),
        input_output_aliases=aliases,
        scratch_shapes=[pltpu.VMEM((SIBLING_SLOTS, tr, cw), BF16),
                        pltpu.SemaphoreType.DMA((SIBLING_SLOTS,)), pltpu.SemaphoreType.DMA((SIBLING_SLOTS,))],
        compiler_params=_params("arbitrary", "arbitrary"),
    )(*args)


def _chip_copies(h_refs, lb_refs, send_sem, recv_sem):
    x, y, c = _place()
    out = []
    for k, (h, lb) in enumerate(zip(h_refs, lb_refs)):
        for q, chip in enumerate(_other_chips(x, y)):
            out.append(_remote(h.at[2 * chip[0] + chip[1]], lb.at[q], send_sem.at[3 * k + q], recv_sem.at[3 * k + q],
                               (chip[0], chip[1], c)))
    return out


def chip_start(halves, name):
    n = len(halves)
    landed = [lax.empty((N_CHIPS - 1,) + h.shape[1:], h.dtype) for h in halves]

    def body(*refs):
        send_sem, recv_sem = refs[2 * n:2 * n + 2]
        h_refs, lb_refs = refs[2 * n + 2:3 * n + 2], refs[3 * n + 2:4 * n + 2]
        for cp in _chip_copies(h_refs, lb_refs, send_sem, recv_sem):
            cp.start()
        refs[-1][...] = jnp.zeros(TOKEN_SHAPE, F32)

    sems = pltpu.SemaphoreType.DMA((3 * n,))
    outs = pl.pallas_call(
        body, name=name,
        in_specs=[HBM_SPEC] * (2 * n),
        out_specs=[SEM_SPEC, SEM_SPEC] + [HBM_SPEC] * (2 * n) + [VMEM_SPEC],
        out_shape=[sems, sems] + [pltpu.HBM(a.shape, a.dtype) for a in halves + landed]
        + [jax.ShapeDtypeStruct(TOKEN_SHAPE, F32)],
        input_output_aliases={p: p + 2 for p in range(2 * n)},
        compiler_params=SPLIT_COPY_PARAMS,
    )(*[_hbm(a) for a in halves + landed])
    return outs[0], outs[1], list(outs[2:2 + n]), list(outs[2 + n:2 + 2 * n]), outs[-1]


def chip_wait(send_sem, recv_sem, halves, landed, after, name):
    n = len(halves)

    def body(*refs):
        s_sem, r_sem = refs[2 * n:2 * n + 2]
        h_refs, lb_refs = refs[2 * n + 3:3 * n + 3], refs[3 * n + 3:]
        for cp in _chip_copies(h_refs, lb_refs, s_sem, r_sem):
            cp.wait_send()
            cp.wait_recv()

    outs = pl.pallas_call(
        body, name=name,
        in_specs=[HBM_SPEC] * (2 * n) + [SEM_SPEC, SEM_SPEC, pl.BlockSpec(memory_space=pl.ANY)],
        out_specs=[HBM_SPEC] * (2 * n),
        out_shape=[pltpu.HBM(a.shape, a.dtype) for a in halves + landed],
        input_output_aliases={p: p for p in range(2 * n)},
        compiler_params=SPLIT_COPY_PARAMS,
    )(*halves, *landed, send_sem, recv_sem, after)
    return list(outs[:n]), list(outs[n:])


def reduce_share(half, landed, me1, name):
    _, h0, cw = half.shape
    tr = _row_tile(h0, cw, SHARE_BLOCK_BYTES)
    nrb = h0 // tr

    def body(me_ref, h_ref, l0, l1, l2, o_ref, sbuf, rbuf, send_sem, recv_sem):
        x, y, c = _place()
        slot = pl.program_id(0) % SIBLING_SLOTS
        total = ((h_ref[...].astype(F32) + l0[...].astype(F32)) + l1[...].astype(F32)) + l2[...].astype(F32)
        sbuf[slot] = total
        cp = _remote(sbuf.at[slot], rbuf.at[slot], send_sem.at[slot], recv_sem.at[slot], (x, y, 1 - c))
        cp.start()
        o_ref[0, c] = total
        cp.wait_recv()
        o_ref[0, 1 - c] = rbuf[slot]
        cp.wait_send()

    landed_spec = lambda q: pl.BlockSpec((None, tr, cw), lambda i, me: (q, i, 0))
    args = [me1, half, landed, landed, landed]
    in_specs = [pl.BlockSpec((None, tr, cw), lambda i, me: (me[0], i, 0))] + [landed_spec(q) for q in range(N_CHIPS - 1)]
    out = pl.pallas_call(
        body, name=name,
        grid_spec=pltpu.PrefetchScalarGridSpec(
            num_scalar_prefetch=1, grid=(nrb,), in_specs=in_specs,
            out_specs=pl.BlockSpec((1, 2, tr, cw), lambda i, me: (0, 0, i, 0)),
            scratch_shapes=[pltpu.VMEM((SIBLING_SLOTS, tr, cw), F32), pltpu.VMEM((SIBLING_SLOTS, tr, cw), F32),
                            pltpu.SemaphoreType.DMA((SIBLING_SLOTS,)), pltpu.SemaphoreType.DMA((SIBLING_SLOTS,))]),
        out_shape=jax.ShapeDtypeStruct((1, 2, h0, cw), F32),
        compiler_params=_params("arbitrary"),
    )(*args)
    return out.reshape(1, 2 * h0, cw)


WEIGHTS = ["norm_mix_g", "norm_ffn_g", "pool_w", "pool_b", "pool_scale", "sb_w_qkv", "sb_q_gain", "sb_k_gain",
           "sb_w_o", "ssm_lam_re", "ssm_lam_im", "ssm_log_step", "ssm_b_re", "ssm_b_im", "ssm_c_re", "ssm_c_im",
           "ssm_d", "ssm_w_glu", "ssm_b_glu", "ffn_w_up", "ffn_conv_w", "ffn_conv_b", "ffn_w_down"]
BIG = {"sb_w_qkv": 1, "sb_w_o": 0, "ssm_w_glu": 1, "ffn_w_up": 1, "ffn_w_down": 0}
SMALL_SHARDED = {"pool_w": 2, "pool_b": 1, "pool_scale": 1, "ssm_d": 1, "ssm_b_glu": 1, "ffn_conv_w": 2}
SMALL = [n for n in WEIGHTS if n not in BIG]
SMALL_PAD = N_DEV * SUBLANES * LANES
N_MIXERS = 3


def _pack(arrays):
    flat = jnp.concatenate([a.reshape(-1).astype(F32) for a in arrays])
    total = -(-flat.shape[0] // SMALL_PAD) * SMALL_PAD
    flat = jnp.pad(flat, (0, total - flat.shape[0]))
    return flat.reshape(N_DEV, -1, LANES)


def _unpack(packed, like):
    flat = packed.reshape(-1)
    out, off = [], 0
    for a in like:
        out.append(flat[off:off + a.size].reshape(a.shape))
        off += a.size
    return out


def kernel(x, norm_mix_g, norm_ffn_g, pool_w, pool_b, pool_scale, sb_w_qkv, sb_q_gain, sb_k_gain, sb_w_o, ssm_lam_re, ssm_lam_im, ssm_log_step, ssm_b_re, ssm_b_im, ssm_c_re, ssm_c_im, ssm_d, ssm_w_glu, ssm_b_glu, ffn_w_up, ffn_conv_w, ffn_conv_b, ffn_w_down, loss_target, m_norm_mix_g, m_norm_ffn_g, m_pool_w, m_pool_b, m_pool_scale, m_sb_w_qkv, m_sb_q_gain, m_sb_k_gain, m_sb_w_o, m_ssm_lam_re, m_ssm_lam_im, m_ssm_log_step, m_ssm_b_re, m_ssm_b_im, m_ssm_c_re, m_ssm_c_im, m_ssm_d, m_ssm_w_glu, m_ssm_b_glu, m_ffn_w_up, m_ffn_conv_w, m_ffn_conv_b, m_ffn_w_down, v_norm_mix_g, v_norm_ffn_g, v_pool_w, v_pool_b, v_pool_scale, v_sb_w_qkv, v_sb_q_gain, v_sb_k_gain, v_sb_w_o, v_ssm_lam_re, v_ssm_lam_im, v_ssm_log_step, v_ssm_b_re, v_ssm_b_im, v_ssm_c_re, v_ssm_c_im, v_ssm_d, v_ssm_w_glu, v_ssm_b_glu, v_ffn_w_up, v_ffn_conv_w, v_ffn_conv_b, v_ffn_w_down):
    given = dict(locals())
    w = {n: given[n] for n in WEIGHTS}
    mom = {n: given["m_" + n] for n in WEIGHTS}
    var = {n: given["v_" + n] for n in WEIGHTS}
    xi, yi, ci = _place()
    me = 2 * xi + yi
    depth = norm_mix_g.shape[0]
    x_in = x[0]
    t, d = x_in.shape

    def placed(a, ax):
        shp = list(a.shape)
        shp[ax] *= N_CHIPS
        full = lax.dynamic_update_slice_in_dim(jnp.zeros(shp, F32), a, me * a.shape[ax], ax)
        return jnp.where(ci == 0, full, 0.0)

    sharded_full = [placed(w[n], ax) for n, ax in SMALL_SHARDED.items()]
    whole_packed = small_allreduce(_pack(sharded_full), "gather_vectors")
    whole = dict(zip(SMALL_SHARDED, _unpack(whole_packed, sharded_full)))

    big = list(BIG)
    me1 = _me_scalar()
    shard = {n: tuple(w[n].shape[1:]) for n in big}
    vec = lambda a, i: a[i:i + 1]
    tie = lambda v, token: v + token[0, 0]

    def layer_weights(i):
        kind, j = i % N_MIXERS, i // N_MIXERS
        mixer = {0: [], 1: [("sb_w_qkv", j), ("sb_w_o", j)], 2: [("ssm_w_glu", j)]}[kind]
        return mixer + [("ffn_w_up", i), ("ffn_w_down", i)]

    started, token = [], whole_packed
    for i in range(depth):
        keys = layer_weights(i)
        bufs = [cast_into_gathered(w[n], l, BIG[n], me1, f"cast_{n}{l}") for n, l in keys]
        send_sem, recv_sem, bufs, token = gather_start(bufs, [shard[n] for n, _ in keys], [BIG[n] for n, _ in keys],
                                                       token, f"gather_start{i}")
        started.append((keys, send_sem, recv_sem, bufs))
    gathered = {}
    pool_w_bf = whole["pool_w"].astype(BF16)

    saved = []
    xc = x_in
    for i in range(depth):
        kind, j = i % N_MIXERS, i // N_MIXERS
        keys, send_sem, recv_sem, bufs = started[i]
        bufs = gather_wait(send_sem, recv_sem, bufs, [shard[n] for n, _ in keys], [BIG[n] for n, _ in keys],
                           token if i == 0 else xc, f"gather_wait{i}")
        for (n, l), b in zip(keys, bufs):
            gathered[n, l] = gather_forward(b, BIG[n], *shard[n], me1, f"gather_forward_{n}{l}")
        s = {"x_in": xc}
        g_mix = vec(norm_mix_g, i)
        if kind == 0:
            (h,) = rmsnorm_fwd(xc, g_mix, [F32], f"norm_mix{i}")
            x_mid = pool_fwd(h, xc, pool_w_bf, vec(whole["pool_b"], j), vec(whole["pool_scale"], j),
                             f"pool_fwd{i}", lead=j)
        elif kind == 1:
            (h,) = rmsnorm_fwd(xc, g_mix, [BF16], f"norm_mix{i}")
            s["qkv"] = mm_cols(h, gathered["sb_w_qkv", j], out_dtype=BF16, name=f"sb_qkv{i}", lead=0)
            s["o"], s["ltot"] = sb_attn_fwd(s["qkv"], vec(sb_q_gain, j), vec(sb_k_gain, j), f"sb_attn_fwd{i}")
            x_mid = mm_cols(s["o"], gathered["sb_w_o", j], out_dtype=F32, name=f"sb_out{i}", resid=xc, lead=0)
        else:
            (h,) = rmsnorm_fwd(xc, g_mix, [F32], f"norm_mix{i}")
            prm = tuple(w[n][j] for n in ("ssm_lam_re", "ssm_lam_im", "ssm_log_step", "ssm_b_re", "ssm_b_im",
                                          "ssm_c_re", "ssm_c_im"))
            prep, s["prep_vjp"] = jax.vjp(ssm_prepare, *prm)
            s["prep"] = tuple(a.astype(BF16) for a in prep[:4]) + tuple(prep[4:])
            s["y"], s["yg"] = ssm_core_fwd(h, *s["prep"], vec(whole["ssm_d"], j), f"ssm_fwd{i}")
            s["val"], s["gate"], x_mid = glu_fwd(s["yg"], gathered["ssm_w_glu", j], vec(whole["ssm_b_glu"], j), xc,
                                                 f"ssm_glu{i}", lead=0)
        s["x_mid"] = x_mid
        (h2,) = rmsnorm_fwd(x_mid, vec(norm_ffn_g, i), [BF16], f"norm_ffn{i}")
        s["up_val"], s["up_gate"], s["act"] = ffn_up_fused(h2, gathered["ffn_w_up", i], whole["ffn_conv_w"][i],
                                                           vec(ffn_conv_b, i), f"ffn_up{i}", lead=0)
        xc = mm_k([s["act"]], gathered["ffn_w_down", i], b_nt=False, name=f"ffn_down{i}", resid=x_mid, lead=0)
        saved.append(s)

    dx, dxb, loss_part = loss_head(xc, loss_target[0], "loss_head")
    loss = lax.psum(loss_part[0, 0], ("x", "y", "c"))

    small = {n: [None] * w[n].shape[0] for n in SMALL}
    big_g, updated = {}, {}

    def finish_reduction(pending, after):
        layer, keys, send_sem, recv_sem, halves, landed, _ = pending
        halves, landed = chip_wait(send_sem, recv_sem, halves, landed, after, f"grads_chip_wait{layer}")
        for (n, l), h, lb in zip(keys, halves, landed):
            g = reduce_share(h, lb, me1, f"grads_share_{n}{l}")
            updated[n] = adamw_layer(w[n], g, mom[n], var[n], l, f"adamw_{n}{l}", into=updated.get(n))

    pending = None
    for i in reversed(range(depth)):
        kind, j = i % N_MIXERS, i // N_MIXERS
        s = saved[i]
        g_ffn, g_mix = vec(norm_ffn_g, i), vec(norm_mix_g, i)
        if pending is not None:
            g_ffn = tie(g_ffn, pending[-1])
        cw, cb = whole["ffn_conv_w"][i], vec(ffn_conv_b, i)
        (h2,) = rmsnorm_fwd(s["x_mid"], g_ffn, [BF16], f"norm_ffn_re{i}")
        dupv, dupg, dcwv, dcwg, dcbv, dcbg = ffn_bwd_fused(dxb, gathered["ffn_w_down", i], s["up_val"], s["up_gate"], cw, cb,
                                                           f"ffn_bwd{i}", lead=0)
        big_g["ffn_w_down", i] = [mm_rows(s["act"], dxb, out_dtype=BF16, name=f"ffn_dwdown{i}")]
        big_g["ffn_w_up", i] = [mm_cols(h2, dupv, a_contract=0, out_dtype=BF16, name=f"ffn_dwup_val{i}"),
                                mm_cols(h2, dupg, a_contract=0, out_dtype=BF16, name=f"ffn_dwup_gate{i}")]
        dh2 = mm_k([dupv, dupg], gathered["ffn_w_up", i], b_nt=True, name=f"ffn_dh{i}", lead=0)
        dx_mid, dxb_mid, small["norm_ffn_g"][i] = rmsnorm_bwd(s["x_mid"], g_ffn, dh2, dx, f"norm_ffn_bwd{i}")
        small["ffn_conv_w"][i] = jnp.concatenate([dcwv, dcwg], axis=1)[None]
        small["ffn_conv_b"][i] = jnp.concatenate([dcbv, dcbg], axis=1)

        if kind == 0:
            (h,) = rmsnorm_fwd(s["x_in"], g_mix, [F32], f"norm_mix_re{i}")
            dh, dwp, small["pool_b"][j], small["pool_scale"][j] = pool_bwd(
                dx_mid, h, pool_w_bf, vec(whole["pool_b"], j), vec(whole["pool_scale"], j), f"pool_bwd{i}", lead=j)
            small["pool_w"][j] = dwp[None]
        elif kind == 1:
            (h,) = rmsnorm_fwd(s["x_in"], g_mix, [BF16], f"norm_mix_re{i}")
            do = mm_cols(dxb_mid, gathered["sb_w_o", j], b_nt=True, out_dtype=BF16, name=f"sb_do{i}", lead=0)
            big_g["sb_w_o", j] = [mm_cols(s["o"], dxb_mid, a_contract=0, out_dtype=BF16, name=f"sb_dwo{i}")]
            dq, dk, dv, small["sb_q_gain"][j], small["sb_k_gain"][j] = sb_attn_bwd(
                s["qkv"], s["ltot"], do, vec(sb_q_gain, j), vec(sb_k_gain, j), f"sb_attn_bwd{i}")
            dqkv = jnp.concatenate([dq, dk, dv], axis=1)
            big_g["sb_w_qkv", j] = [mm_cols(h, dqkv, a_contract=0, out_dtype=BF16, name=f"sb_dwqkv{i}")]
            dh = mm_k([dqkv], gathered["sb_w_qkv", j], b_nt=True, name=f"sb_dh{i}", lead=0)
        else:
            (h,) = rmsnorm_fwd(s["x_in"], g_mix, [F32], f"norm_mix_re{i}")
            dval, dgate, dbv, dbg = glu_bwd(dx_mid, s["val"], s["gate"], f"ssm_glu_bwd{i}")
            small["ssm_b_glu"][j] = jnp.concatenate([dbv, dbg], axis=1)
            big_g["ssm_w_glu", j] = [mm_cols(s["yg"], dval, a_contract=0, out_dtype=BF16, name=f"ssm_dwglu_val{i}"),
                                     mm_cols(s["yg"], dgate, a_contract=0, out_dtype=BF16, name=f"ssm_dwglu_gate{i}")]
            dyg = mm_k([dval, dgate], gathered["ssm_w_glu", j], b_nt=True, name=f"ssm_dyg{i}", lead=0)
            dh, small["ssm_d"][j], *dprep = ssm_core_bwd(h, s["y"], dyg, *s["prep"], vec(whole["ssm_d"], j), f"ssm_bwd{i}")
            dprm = s["prep_vjp"](tuple(dprep))
            for n, g in zip(("ssm_lam_re", "ssm_lam_im", "ssm_log_step", "ssm_b_re", "ssm_b_im", "ssm_c_re", "ssm_c_im"),
                            dprm):
                small[n][j] = g[None]
        dx, dxb, small["norm_mix_g"][i] = rmsnorm_bwd(s["x_in"], g_mix, dh, dx_mid, f"norm_mix_bwd{i}")

        keys, halves = layer_weights(i), []
        for n, l in keys:
            h = None
            for a, g in enumerate(big_g[n, l]):
                h = pair_reduce(g, BIG[n], *shard[n], f"grads_pair_{n}{l}_{a}", into=h,
                                first_slot=a * (N_CHIPS // len(big_g[n, l])))
            halves.append(h)
        started_copies = chip_start(halves, f"grads_chip_start{i}")
        if pending is not None:
            finish_reduction(pending, dx)
        pending = (i, keys) + tuple(started_copies)

    small_full = [jnp.concatenate(small[n], axis=0) for n in SMALL]
    small_sum = dict(zip(SMALL, _unpack(small_allreduce(_pack(small_full), "reduce_vectors"), small_full)))
    grads = {}
    for n in SMALL:
        g = small_sum[n]
        if n in SMALL_SHARDED:
            ax = SMALL_SHARDED[n]
            g = lax.dynamic_slice_in_dim(g, me * w[n].shape[ax], w[n].shape[ax], ax)
        grads[n] = g
    delta, new_m, new_v = {}, {}, {}
    like = [w[n] for n in SMALL]
    packed = [_pack([src[n] for n in SMALL]).reshape(-1, LANES) for src in (w, grads, mom, var)]
    vector_updates = adamw(*packed, "adamw_vectors")
    for dst, out in zip((delta, new_m, new_v), vector_updates):
        dst.update(zip(SMALL, _unpack(out, like)))

    finish_reduction(pending, vector_updates[0])
    for n in big:
        grads[n], delta[n], new_m[n], new_v[n] = updated[n]

    return (loss, dx[None], *[grads[n] for n in WEIGHTS], *[delta[n] for n in WEIGHTS],
            *[new_m[n] for n in WEIGHTS], *[new_v[n] for n in WEIGHTS])
```

```python
import functools
import math

import jax
import jax.numpy as jnp
from jax import lax
from jax.experimental import pallas as pl
from jax.experimental.pallas import tpu as pltpu

F32 = jnp.float32
BF16 = jnp.bfloat16

RMS_EPS = 1e-6
POOL_WINDOWS = (2, 4, 8, 16)
SB_HEAD_DIM = 128
SSM_GROUP_CH = 16
SSM_STATE = 64
SSM_BLOCK_GROUPS = 8
ADAM_LR = 0.001
ADAM_B1 = 0.9
ADAM_B2 = 0.999
ADAM_EPS = 1e-08
ADAM_WD = 0.01
ADAM_STEP = 10

V7X_VMEM_BYTES = 64 * 1024 * 1024
VMEM_LIMIT = V7X_VMEM_BYTES - 8 * 1024 * 1024
SUBLANES = 8
LANES = 128
NORM_ROWS = 256
MESH = pl.DeviceIdType.MESH
N_CHIPS = 4
N_DEV = 8


def _params(*sem):
    return pltpu.CompilerParams(dimension_semantics=tuple(sem) if sem else None, vmem_limit_bytes=VMEM_LIMIT)


def _tile(n, want):
    if n <= want:
        return n
    t = (want // LANES) * LANES
    while t > LANES and n % t:
        t -= LANES
    assert n % t == 0, (n, want)
    return t


def _spec(shape, imap, lead=None):
    if lead is None:
        return pl.BlockSpec(tuple(shape), imap)
    return pl.BlockSpec((None,) + tuple(shape), lambda *a: (lead,) + tuple(imap(*a)))


def _sigmoid(v):
    return 1.0 / (1.0 + jnp.exp(-v))


def _shift_down(v, k):
    return pltpu.roll(v, k, 0)


def _shift_up(v, k):
    return pltpu.roll(v, v.shape[0] - k, 0)


def rmsnorm_fwd(x, g, out_dtypes, name):
    t, d = x.shape
    tr = min(t, NORM_ROWS)

    def body(x_ref, g_ref, *o_refs):
        xv = x_ref[...]
        r = lax.rsqrt(jnp.mean(xv * xv, axis=-1, keepdims=True) + RMS_EPS)
        h = xv * r * g_ref[...]
        for o in o_refs:
            o[...] = h.astype(o.dtype)

    outs = pl.pallas_call(
        body, name=name, grid=(t // tr,),
        in_specs=[pl.BlockSpec((tr, d), lambda i: (i, 0)), pl.BlockSpec((1, d), lambda i: (0, 0))],
        out_specs=[pl.BlockSpec((tr, d), lambda i: (i, 0)) for _ in out_dtypes],
        out_shape=[jax.ShapeDtypeStruct((t, d), dt) for dt in out_dtypes],
        compiler_params=_params("parallel"),
    )(x, g)
    return outs


def rmsnorm_bwd(x, g, dh, dres, name):
    t, d = x.shape
    tr = min(t, NORM_ROWS)

    def body(x_ref, g_ref, dh_ref, dres_ref, dx_ref, dxb_ref, dg_ref):
        xv = x_ref[...]
        r = lax.rsqrt(jnp.mean(xv * xv, axis=-1, keepdims=True) + RMS_EPS)
        xhat = xv * r
        dhv = dh_ref[...]
        dxhat = dhv * g_ref[...]
        dx = dres_ref[...] + r * (dxhat - xhat * jnp.mean(dxhat * xhat, axis=-1, keepdims=True))
        dx_ref[...] = dx
        dxb_ref[...] = dx.astype(BF16)
        part = jnp.sum(dhv * xhat, axis=0, keepdims=True)

        @pl.when(pl.program_id(0) == 0)
        def _():
            dg_ref[...] = part

        @pl.when(pl.program_id(0) != 0)
        def _():
            dg_ref[...] += part

    row = pl.BlockSpec((tr, d), lambda i: (i, 0))
    vec = pl.BlockSpec((1, d), lambda i: (0, 0))
    return pl.pallas_call(
        body, name=name, grid=(t // tr,),
        in_specs=[row, vec, row, row],
        out_specs=[row, row, vec],
        out_shape=[jax.ShapeDtypeStruct((t, d), F32), jax.ShapeDtypeStruct((t, d), BF16),
                   jax.ShapeDtypeStruct((1, d), F32)],
        compiler_params=_params("arbitrary"),
    )(x, g, dh, dres)


def mm_cols(a, b, *, a_contract=1, b_nt=False, out_dtype, name, resid=None, tn=512, lead=None):
    m = a.shape[1 - a_contract]
    k = a.shape[a_contract]
    n = b.shape[-2] if b_nt else b.shape[-1]
    assert (b.shape[-1] if b_nt else b.shape[-2]) == k
    tn = _tile(n, tn)

    def body(a_ref, b_ref, *rest):
        o_ref = rest[-1]
        dn = (((a_contract,), (1 if b_nt else 0,)), ((), ()))
        acc = lax.dot_general(a_ref[...], b_ref[...], dn, preferred_element_type=F32)
        if resid is not None:
            acc = acc + rest[0][...]
        o_ref[...] = acc.astype(o_ref.dtype)

    in_specs = [pl.BlockSpec(a.shape, lambda j: (0, 0)),
                _spec((tn, k), lambda j: (j, 0), lead) if b_nt else _spec((k, tn), lambda j: (0, j), lead)]
    args = [a, b]
    if resid is not None:
        in_specs.append(pl.BlockSpec((m, tn), lambda j: (0, j)))
        args.append(resid)
    return pl.pallas_call(
        body, name=name, grid=(n // tn,), in_specs=in_specs,
        out_specs=pl.BlockSpec((m, tn), lambda j: (0, j)),
        out_shape=jax.ShapeDtypeStruct((m, n), out_dtype),
        compiler_params=_params("parallel"),
    )(*args)


def mm_rows(st, res, *, out_dtype, name, tn=512):
    k, m = st.shape
    n = res.shape[1]
    assert res.shape[0] == k
    tn = _tile(m, tn)

    def body(st_ref, res_ref, o_ref):
        o_ref[...] = lax.dot_general(st_ref[...], res_ref[...], (((0,), (0,)), ((), ())),
                                     preferred_element_type=F32).astype(o_ref.dtype)

    return pl.pallas_call(
        body, name=name, grid=(m // tn,),
        in_specs=[pl.BlockSpec((k, tn), lambda j: (0, j)), pl.BlockSpec((k, n), lambda j: (0, 0))],
        out_specs=pl.BlockSpec((tn, n), lambda j: (j, 0)),
        out_shape=jax.ShapeDtypeStruct((m, n), out_dtype),
        compiler_params=_params("parallel"),
    )(st, res)


def mm_k(a_list, b, *, b_nt, name, resid=None, tk=512, tnn=1024, lead=None):
    m = a_list[0].shape[0]
    ks = [a.shape[1] for a in a_list]
    ktot = sum(ks)
    n = b.shape[-2] if b_nt else b.shape[-1]
    assert (b.shape[-1] if b_nt else b.shape[-2]) == ktot
    tk = _tile(ks[0], tk)
    assert all(kk % tk == 0 for kk in ks)
    tnn = _tile(n, tnn)
    nks = [kk // tk for kk in ks]
    starts = [sum(nks[:i]) for i in range(len(nks))]
    nk = sum(nks)

    def body(*refs):
        a_refs = refs[:len(a_list)]
        b_ref = refs[len(a_list)]
        o_ref = refs[-1]
        kk = pl.program_id(1)

        @pl.when(kk == 0)
        def _():
            if resid is not None:
                o_ref[...] = refs[len(a_list) + 1][...]
            else:
                o_ref[...] = jnp.zeros_like(o_ref)

        dn = (((1,), (1 if b_nt else 0,)), ((), ()))
        for i, a_ref in enumerate(a_refs):
            @pl.when(jnp.logical_and(kk >= starts[i], kk < starts[i] + nks[i]))
            def _(a_ref=a_ref):
                o_ref[...] += lax.dot_general(a_ref[...], b_ref[...], dn, preferred_element_type=F32)

    def a_spec(i):
        return pl.BlockSpec((m, tk), lambda nn, kk: (0, jnp.clip(kk - starts[i], 0, nks[i] - 1)))

    in_specs = [a_spec(i) for i in range(len(a_list))]
    in_specs.append(_spec((tnn, tk), lambda nn, kk: (nn, kk), lead) if b_nt
                    else _spec((tk, tnn), lambda nn, kk: (kk, nn), lead))
    args = list(a_list) + [b]
    if resid is not None:
        in_specs.append(pl.BlockSpec((m, tnn), lambda nn, kk: (0, nn)))
        args.append(resid)
    return pl.pallas_call(
        body, name=name, grid=(n // tnn, nk), in_specs=in_specs,
        out_specs=pl.BlockSpec((m, tnn), lambda nn, kk: (0, nn)),
        out_shape=jax.ShapeDtypeStruct((m, n), F32),
        compiler_params=_params("parallel", "arbitrary"),
    )(*args)


HALO = SUBLANES
CHUNK_ROWS = 64


def _conv_taps(ext, r):
    return ext[HALO:], _shift_down(ext, 1)[HALO:], _shift_down(ext, 2)[HALO:]


def ffn_up_fused(h, w_up, conv_w, conv_b, name, lead=None):
    t, d = h.shape
    f = w_up.shape[-1] // 2
    tn = _tile(f, 256)
    nf = f // tn
    r = min(CHUNK_ROWS, t)

    def body(h_ref, wv_ref, wg_ref, cwv_ref, cwg_ref, cbv_ref, cbg_ref, uv_ref, ug_ref, act_ref, sv, sg):
        zero = jnp.zeros((HALO, tn), F32)
        sv[0:HALO, :] = zero
        sg[0:HALO, :] = zero
        hv = h_ref[...]
        sv[HALO:, :] = jnp.dot(hv, wv_ref[...], preferred_element_type=F32).astype(BF16).astype(F32)
        sg[HALO:, :] = jnp.dot(hv, wg_ref[...], preferred_element_type=F32).astype(BF16).astype(F32)
        cwv, cwg = cwv_ref[...], cwg_ref[...]
        cbv, cbg = cbv_ref[...], cbg_ref[...]

        def chunk(i, carry):
            r0 = pl.multiple_of(i * r, r)
            v0, v1, v2 = _conv_taps(sv[pl.ds(r0, r + HALO), :], r)
            g0, g1, g2 = _conv_taps(sg[pl.ds(r0, r + HALO), :], r)
            cval = cbv + cwv[2:3] * v0 + cwv[1:2] * v1 + cwv[0:1] * v2
            cgate = cbg + cwg[2:3] * g0 + cwg[1:2] * g1 + cwg[0:1] * g2
            uv_ref[pl.ds(r0, r), :] = v0.astype(BF16)
            ug_ref[pl.ds(r0, r), :] = g0.astype(BF16)
            act_ref[pl.ds(r0, r), :] = (cgate * _sigmoid(cgate) * cval).astype(BF16)
            return carry

        lax.fori_loop(0, t // r, chunk, 0)

    col = lambda off: _spec((d, tn), lambda j: (0, j + off), lead)
    cw = lambda off: pl.BlockSpec((3, tn), lambda j: (0, j + off))
    cb = lambda off: pl.BlockSpec((1, tn), lambda j: (0, j + off))
    out = pl.BlockSpec((t, tn), lambda j: (0, j))
    return pl.pallas_call(
        body, name=name, grid=(nf,),
        in_specs=[pl.BlockSpec((t, d), lambda j: (0, 0)), col(0), col(nf), cw(0), cw(nf), cb(0), cb(nf)],
        out_specs=[out, out, out],
        out_shape=[jax.ShapeDtypeStruct((t, f), BF16)] * 3,
        scratch_shapes=[pltpu.VMEM((t + HALO, tn), F32), pltpu.VMEM((t + HALO, tn), F32)],
        compiler_params=_params("parallel"),
    )(h, w_up, w_up, conv_w, conv_w, conv_b, conv_b)


def ffn_bwd_fused(dout, w_down, up_val, up_gate, conv_w, conv_b, name, lead=None):
    t, d = dout.shape
    f = w_down.shape[-2]
    tn = _tile(f, 256)
    nf = f // tn
    r = min(CHUNK_ROWS, t)

    def body(do_ref, wd_ref, uv_ref, ug_ref, cwv_ref, cwg_ref, cbv_ref, cbg_ref,
             dv_ref, dg_ref, dcwv_ref, dcwg_ref, dcbv_ref, dcbg_ref, da, sv, sg, ev, eg):
        zero = jnp.zeros((HALO, tn), F32)
        sv[0:HALO, :] = zero
        sg[0:HALO, :] = zero
        ev[t:, :] = zero
        eg[t:, :] = zero
        da[...] = lax.dot_general(do_ref[...], wd_ref[...], (((1,), (1,)), ((), ())), preferred_element_type=F32)
        sv[HALO:, :] = uv_ref[...].astype(F32)
        sg[HALO:, :] = ug_ref[...].astype(F32)
        cwv, cwg = cwv_ref[...], cwg_ref[...]
        cbv, cbg = cbv_ref[...], cbg_ref[...]

        def chunk(i, acc):
            r0 = pl.multiple_of(i * r, r)
            v = _conv_taps(sv[pl.ds(r0, r + HALO), :], r)
            g = _conv_taps(sg[pl.ds(r0, r + HALO), :], r)
            cval = cbv + cwv[2:3] * v[0] + cwv[1:2] * v[1] + cwv[0:1] * v[2]
            cgate = cbg + cwg[2:3] * g[0] + cwg[1:2] * g[1] + cwg[0:1] * g[2]
            s = _sigmoid(cgate)
            dav = da[pl.ds(r0, r), :]
            dval = dav * (cgate * s)
            dgate = dav * cval * (s * (1.0 + cgate * (1.0 - s)))
            ev[pl.ds(r0, r), :] = dval
            eg[pl.ds(r0, r), :] = dgate
            col = lambda z: jnp.sum(z, axis=0, keepdims=True)
            new = [acc[0] + col(dval), acc[1] + col(dgate)]
            new += [acc[2 + j] + col(dval * v[2 - j]) for j in range(3)]
            new += [acc[5 + j] + col(dgate * g[2 - j]) for j in range(3)]
            return tuple(new)

        z1 = jnp.zeros((1, tn), F32)
        acc = lax.fori_loop(0, t // r, chunk, (z1,) * 8)
        dcbv_ref[...] = acc[0]
        dcbg_ref[...] = acc[1]
        for j in range(3):
            dcwv_ref[j:j + 1, :] = acc[2 + j]
            dcwg_ref[j:j + 1, :] = acc[5 + j]

        def chunk2(i, carry):
            r0 = pl.multiple_of(i * r, r)
            for e_ref, cw_, o_ref in ((ev, cwv, dv_ref), (eg, cwg, dg_ref)):
                ext = e_ref[pl.ds(r0, r + HALO), :]
                d0, d1, d2 = ext[:r], _shift_up(ext, 1)[:r], _shift_up(ext, 2)[:r]
                o_ref[pl.ds(r0, r), :] = (cw_[2:3] * d0 + cw_[1:2] * d1 + cw_[0:1] * d2).astype(BF16)
            return carry

        lax.fori_loop(0, t // r, chunk2, 0)

    cw = lambda off: pl.BlockSpec((3, tn), lambda j: (0, j + off))
    cb = lambda off: pl.BlockSpec((1, tn), lambda j: (0, j + off))
    tile = pl.BlockSpec((t, tn), lambda j: (0, j))
    s = lambda rows, dt: jax.ShapeDtypeStruct((rows, f), dt)
    halo = pltpu.VMEM((t + HALO, tn), F32)
    return pl.pallas_call(
        body, name=name, grid=(nf,),
        in_specs=[pl.BlockSpec((t, d), lambda j: (0, 0)), _spec((tn, d), lambda j: (j, 0), lead),
                  tile, tile, cw(0), cw(nf), cb(0), cb(nf)],
        out_specs=[tile, tile, pl.BlockSpec((3, tn), lambda j: (0, j)), pl.BlockSpec((3, tn), lambda j: (0, j)),
                   pl.BlockSpec((1, tn), lambda j: (0, j)), pl.BlockSpec((1, tn), lambda j: (0, j))],
        out_shape=[s(t, BF16), s(t, BF16), s(3, F32), s(3, F32), s(1, F32), s(1, F32)],
        scratch_shapes=[pltpu.VMEM((t, tn), F32), halo, halo, halo, halo],
        compiler_params=_params("parallel"),
    )(dout, w_down, up_val, up_gate, conv_w, conv_w, conv_b, conv_b)


POOL_PAD = max(POOL_WINDOWS)


def _window_sum(ext, win, shift):
    assert POOL_WINDOWS == (2, 4, 8, 16)
    s2 = ext + shift(ext, 1)
    s4 = s2 + shift(s2, 2)
    s8 = s4 + shift(s4, 4)
    s16 = s8 + shift(s8, 8)
    return jnp.where(win == 2, s2, jnp.where(win == 4, s4, jnp.where(win == 8, s8, s16)))


def _pool_win_scalar(g):
    win = jnp.int32(POOL_WINDOWS[-1])
    for k in range(len(POOL_WINDOWS) - 2, -1, -1):
        win = jnp.where(g == k, jnp.int32(POOL_WINDOWS[k]), win)
    return win


def _pool_count(r0, r, win):
    rows = r0 + lax.broadcasted_iota(jnp.int32, (r, 1), 0)
    return jnp.minimum(rows + 1, win).astype(F32)


def _pooled_into(hp, pooled, h_ref, t, r, win):
    hp[0:POOL_PAD, :] = jnp.zeros((POOL_PAD, hp.shape[1]), F32)
    hp[POOL_PAD:, :] = h_ref[...]

    def chunk(i, carry):
        r0 = pl.multiple_of(i * r, r)
        ext = hp[pl.ds(r0, r + POOL_PAD), :]
        s = _window_sum(ext, win, _shift_down)[POOL_PAD:]
        pooled[pl.ds(r0, r), :] = (s / _pool_count(r0, r, win) - ext[POOL_PAD:]).astype(BF16)
        return carry

    lax.fori_loop(0, t // r, chunk, 0)


def pool_fwd(h, x, w, b, scale, name, lead=None):
    t, d = h.shape
    ng, dg = w.shape[-3], w.shape[-2]
    r = min(CHUNK_ROWS, t)

    def body(h_ref, x_ref, w_ref, b_ref, s_ref, o_ref, hp, pooled):
        win = _pool_win_scalar(pl.program_id(0))
        _pooled_into(hp, pooled, h_ref, t, r, win)
        y = jnp.dot(pooled[...], w_ref[...], preferred_element_type=F32)
        o_ref[...] = x_ref[...] + (y + b_ref[...]) * s_ref[...]

    col = pl.BlockSpec((t, dg), lambda g: (0, g))
    vec = pl.BlockSpec((1, dg), lambda g: (0, g))
    return pl.pallas_call(
        body, name=name, grid=(ng,),
        in_specs=[col, col, _spec((None, dg, dg), lambda g: (g, 0, 0), lead), vec, vec],
        out_specs=col, out_shape=jax.ShapeDtypeStruct((t, d), F32),
        scratch_shapes=[pltpu.VMEM((t + POOL_PAD, dg), F32), pltpu.VMEM((t, dg), BF16)],
        compiler_params=_params("parallel"),
    )(h, x, w, b, scale)


def pool_bwd(dm, h, w, b, scale, name, lead=None):
    t, d = h.shape
    ng, dg = w.shape[-3], w.shape[-2]
    r = min(CHUNK_ROWS, t)

    def body(dm_ref, h_ref, w_ref, b_ref, s_ref, dh_ref, dw_ref, db_ref, ds_ref, hp, pooled, q):
        win = _pool_win_scalar(pl.program_id(0))
        _pooled_into(hp, pooled, h_ref, t, r, win)
        wv = w_ref[...]
        y = jnp.dot(pooled[...], wv, preferred_element_type=F32)
        dmv = dm_ref[...]
        ds_ref[...] = jnp.sum(dmv * (y + b_ref[...]), axis=0, keepdims=True)
        dy = dmv * s_ref[...]
        db_ref[...] = jnp.sum(dy, axis=0, keepdims=True)
        dyb = dy.astype(BF16)
        dw_ref[...] = lax.dot_general(pooled[...], dyb, (((0,), (0,)), ((), ())),
                                      preferred_element_type=F32).astype(dw_ref.dtype)
        dp = lax.dot_general(dyb, wv, (((1,), (1,)), ((), ())), preferred_element_type=F32)
        q[t:, :] = jnp.zeros((POOL_PAD, dg), F32)
        q[0:t, :] = dp / _pool_count(0, t, win)
        dh_ref[...] = -dp

        def chunk(i, carry):
            r0 = pl.multiple_of(i * r, r)
            ext = q[pl.ds(r0, r + POOL_PAD), :]
            dh_ref[pl.ds(r0, r), :] += _window_sum(ext, win, _shift_up)[:r]
            return carry

        lax.fori_loop(0, t // r, chunk, 0)

    col = pl.BlockSpec((t, dg), lambda g: (0, g))
    vec = pl.BlockSpec((1, dg), lambda g: (0, g))
    wspec = pl.BlockSpec((None, dg, dg), lambda g: (g, 0, 0))
    return pl.pallas_call(
        body, name=name, grid=(ng,),
        in_specs=[col, col, _spec((None, dg, dg), lambda g: (g, 0, 0), lead), vec, vec],
        out_specs=[col, wspec, vec, vec],
        out_shape=[jax.ShapeDtypeStruct((t, d), F32), jax.ShapeDtypeStruct((ng, dg, dg), F32),
                   jax.ShapeDtypeStruct((1, d), F32), jax.ShapeDtypeStruct((1, d), F32)],
        scratch_shapes=[pltpu.VMEM((t + POOL_PAD, dg), F32), pltpu.VMEM((t, dg), BF16),
                        pltpu.VMEM((t + POOL_PAD, dg), F32)],
        compiler_params=_params("parallel"),
    )(dm, h, w, b, scale)


SB_BLOCK = 256


def _tri_sum(v, tri):
    hi = v.astype(BF16)
    lo = (v - hi.astype(F32)).astype(BF16)
    dot = lambda p: jnp.dot(p, tri, preferred_element_type=F32)
    return dot(hi) + dot(lo)


def _tri(bk, cmp):
    return cmp(lax.broadcasted_iota(jnp.int32, (bk, bk), 0), lax.broadcasted_iota(jnp.int32, (bk, bk), 1)).astype(BF16)


def _sb_logits(qblk, kblk, q0, k0, inv):
    bq, bk = qblk.shape[0], kblk.shape[0]
    z = lax.dot_general(qblk, kblk, (((1,), (1,)), ((), ())), preferred_element_type=F32) * inv
    qpos = q0 + lax.broadcasted_iota(jnp.int32, (bq, bk), 0)
    kpos = k0 + lax.broadcasted_iota(jnp.int32, (bq, bk), 1)
    mask = kpos < qpos
    lb = jnp.minimum(z, 0.0) - jnp.log(1.0 + jnp.exp(-jnp.abs(z)))
    lm = jnp.where(mask, lb - z, 0.0)
    return lb, lm, mask


def _head_norm(ref, gain):
    xv = ref[...].astype(F32)
    r = lax.rsqrt(jnp.mean(xv * xv, axis=-1, keepdims=True) + RMS_EPS)
    xhat = xv * r
    return xhat, r, (xhat * gain).astype(BF16)


def sb_attn_fwd(qkv, q_gain, k_gain, name):
    t = qkv.shape[0]
    d = qkv.shape[1] // 3
    dh = SB_HEAD_DIM
    nh = d // dh
    blk = min(SB_BLOCK, t)
    inv = 1.0 / math.sqrt(dh)

    def body(q_ref, k_ref, v_ref, qg_ref, kg_ref, o_ref, lt_ref, qn, kn):
        qn[...] = _head_norm(q_ref, qg_ref[...])[2]
        kn[...] = _head_norm(k_ref, kg_ref[...])[2]
        later = _tri(blk, lambda j, s: j > s)

        def q_loop(qb, carry):
            q0 = pl.multiple_of(qb * blk, blk)
            qblk = qn[pl.ds(q0, blk), :]

            def k_loop(i, st):
                c, acc = st
                k0 = pl.multiple_of((qb - i) * blk, blk)
                lb, lm, mask = _sb_logits(qblk, kn[pl.ds(k0, blk), :], q0, k0, inv)
                a = jnp.where(mask, jnp.exp(lb + _tri_sum(lm, later) + c), 0.0)
                acc = acc + jnp.dot(a.astype(BF16), v_ref[pl.ds(k0, blk), :], preferred_element_type=F32)
                return c + jnp.sum(lm, axis=1, keepdims=True), acc

            c, acc = lax.fori_loop(0, qb + 1, k_loop, (jnp.zeros((blk, 1), F32), jnp.zeros((blk, dh), F32)))
            o_ref[pl.ds(q0, blk), :] = acc.astype(BF16)
            lt_ref[pl.ds(q0, blk), :] = c
            return carry

        lax.fori_loop(0, t // blk, q_loop, 0)

    head = lambda off: pl.BlockSpec((t, dh), lambda h: (0, h + off))
    gain = pl.BlockSpec((1, dh), lambda h: (0, 0))
    return pl.pallas_call(
        body, name=name, grid=(nh,),
        in_specs=[head(0), head(nh), head(2 * nh), gain, gain],
        out_specs=[head(0), pl.BlockSpec((None, t, 1), lambda h: (h, 0, 0))],
        out_shape=[jax.ShapeDtypeStruct((t, d), BF16), jax.ShapeDtypeStruct((nh, t, 1), F32)],
        scratch_shapes=[pltpu.VMEM((t, dh), BF16), pltpu.VMEM((t, dh), BF16)],
        compiler_params=_params("parallel"),
    )(qkv, qkv, qkv, q_gain, k_gain)


def sb_attn_bwd(qkv, ltot, do, q_gain, k_gain, name):
    t = qkv.shape[0]
    d = qkv.shape[1] // 3
    dh = SB_HEAD_DIM
    nh = d // dh
    blk = min(SB_BLOCK, t)
    inv = 1.0 / math.sqrt(dh)
    tn_dims = (((0,), (0,)), ((), ()))

    def body(q_ref, k_ref, v_ref, lt_ref, do_ref, qg_ref, kg_ref, dq_ref, dk_ref, dv_ref, dqg_ref, dkg_ref,
             qn, kn, dqn, dkn, dvn):
        qg, kg = qg_ref[...], kg_ref[...]
        qhat, rq, qnb = _head_norm(q_ref, qg)
        khat, rk, knb = _head_norm(k_ref, kg)
        qn[...] = qnb
        kn[...] = knb
        dkn[...] = jnp.zeros_like(dkn)
        dvn[...] = jnp.zeros_like(dvn)
        upto = _tri(blk, lambda j, s: j <= s)
        before = _tri(blk, lambda j, s: j < s)

        def q_loop(qb, carry):
            q0 = pl.multiple_of(qb * blk, blk)
            qblk = qn[pl.ds(q0, blk), :]
            doblk = do_ref[pl.ds(q0, blk), :]
            ltv = lt_ref[pl.ds(q0, blk), :]

            def k_loop(kb, st):
                pl_, pg, dq = st
                k0 = pl.multiple_of(kb * blk, blk)
                kblk = kn[pl.ds(k0, blk), :]
                lb, lm, mask = _sb_logits(qblk, kblk, q0, k0, inv)
                a = jnp.where(mask, jnp.exp(lb + (ltv - pl_ - _tri_sum(lm, upto))), 0.0)
                da = lax.dot_general(doblk, v_ref[pl.ds(k0, blk), :], (((1,), (1,)), ((), ())),
                                     preferred_element_type=F32)
                g = da * a
                g_before = pg + _tri_sum(g, before)
                beta = jnp.exp(lb)
                dz = (jnp.where(mask, g * (1.0 - beta) - beta * g_before, 0.0) * inv).astype(BF16)
                dq = dq + jnp.dot(dz, kblk, preferred_element_type=F32)
                dkn[pl.ds(k0, blk), :] += lax.dot_general(dz, qblk, tn_dims, preferred_element_type=F32)
                dvn[pl.ds(k0, blk), :] += lax.dot_general(a.astype(BF16), doblk, tn_dims, preferred_element_type=F32)
                return pl_ + jnp.sum(lm, axis=1, keepdims=True), pg + jnp.sum(g, axis=1, keepdims=True), dq

            z1 = jnp.zeros((blk, 1), F32)
            _, _, dq = lax.fori_loop(0, qb + 1, k_loop, (z1, z1, jnp.zeros((blk, dh), F32)))
            dqn[pl.ds(q0, blk), :] = dq
            return carry

        lax.fori_loop(0, t // blk, q_loop, 0)

        first = pl.program_id(0) == 0
        for dn, xhat, r, gain, out_ref, dgain_ref in ((dqn, qhat, rq, qg, dq_ref, dqg_ref),
                                                      (dkn, khat, rk, kg, dk_ref, dkg_ref)):
            dnv = dn[...]
            dxhat = dnv * gain
            out_ref[...] = (r * (dxhat - xhat * jnp.mean(dxhat * xhat, axis=-1, keepdims=True))).astype(BF16)
            part = jnp.sum(dnv * xhat, axis=0, keepdims=True)

            @pl.when(first)
            def _(dgain_ref=dgain_ref, part=part):
                dgain_ref[...] = part

            @pl.when(jnp.logical_not(first))
            def _(dgain_ref=dgain_ref, part=part):
                dgain_ref[...] += part

        dv_ref[...] = dvn[...].astype(BF16)

    head = lambda off: pl.BlockSpec((t, dh), lambda h: (0, h + off))
    gain = pl.BlockSpec((1, dh), lambda h: (0, 0))
    big = jax.ShapeDtypeStruct((t, d), BF16)
    small = jax.ShapeDtypeStruct((1, dh), F32)
    return pl.pallas_call(
        body, name=name, grid=(nh,),
        in_specs=[head(0), head(nh), head(2 * nh), pl.BlockSpec((None, t, 1), lambda h: (h, 0, 0)), head(0),
                  gain, gain],
        out_specs=[head(0), head(0), head(0), gain, gain],
        out_shape=[big, big, big, small, small],
        scratch_shapes=[pltpu.VMEM((t, dh), BF16), pltpu.VMEM((t, dh), BF16),
                        pltpu.VMEM((t, dh), F32), pltpu.VMEM((t, dh), F32), pltpu.VMEM((t, dh), F32)],
        compiler_params=_params("arbitrary"),
    )(qkv, qkv, qkv, ltot, do, q_gain, k_gain)


GELU_C = math.sqrt(2.0 / math.pi)
GELU_A = 0.044715
SCAN_ROWS = SUBLANES


def _gelu(y):
    return 0.5 * y * (1.0 + jnp.tanh(GELU_C * (y + GELU_A * y * y * y)))


def _gelu_grad(y):
    th = jnp.tanh(GELU_C * (y + GELU_A * y * y * y))
    return 0.5 * (1.0 + th) + 0.5 * y * (1.0 - th * th) * GELU_C * (1.0 + 3.0 * GELU_A * y * y)


def _powers(ar, ai):
    out = [(ar, ai)]
    for _ in range(SCAN_ROWS - 1):
        pr, pi = out[-1]
        out.append((pr * ar - pi * ai, pr * ai + pi * ar))
    return out


def _rows(vals):
    c = vals[0].shape[1]
    row = lax.broadcasted_iota(jnp.int32, (SCAN_ROWS, c), 0)
    out = jnp.broadcast_to(vals[SCAN_ROWS - 1], (SCAN_ROWS, c))
    for j in range(SCAN_ROWS - 2, -1, -1):
        out = jnp.where(row == j, vals[j], out)
    return out


def _scan_forward(sr, si, off, t, ar, ai):
    c = ar.shape[1]
    p = _powers(ar, ai)
    pwr = _rows([q[0] for q in p])
    pwi = _rows([q[1] for q in p])
    row = lax.broadcasted_iota(jnp.int32, (SCAN_ROWS, c), 0)

    def tile(i, carry):
        cr, ci = carry
        r0 = pl.multiple_of(off + i * SCAN_ROWS, SCAN_ROWS)
        xr = sr[pl.ds(r0, SCAN_ROWS), :]
        xi = si[pl.ds(r0, SCAN_ROWS), :]
        for k in (1, 2, 4):
            pr, pi = p[k - 1]
            shr = jnp.where(row >= k, _shift_down(xr, k), 0.0)
            shi = jnp.where(row >= k, _shift_down(xi, k), 0.0)
            xr, xi = xr + pr * shr - pi * shi, xi + pr * shi + pi * shr
        xr, xi = xr + pwr * cr - pwi * ci, xi + pwr * ci + pwi * cr
        sr[pl.ds(r0, SCAN_ROWS), :] = xr
        si[pl.ds(r0, SCAN_ROWS), :] = xi
        return xr[SCAN_ROWS - 1:SCAN_ROWS], xi[SCAN_ROWS - 1:SCAN_ROWS]

    z = jnp.zeros((1, c), F32)
    lax.fori_loop(0, t // SCAN_ROWS, tile, (z, z))


def _scan_reverse(gr, gi, t, ar, ai, xr_ref, xi_ref):
    c = ar.shape[1]
    p = _powers(ar, ai)
    pwr = _rows([p[SCAN_ROWS - 1 - j][0] for j in range(SCAN_ROWS)])
    pwi = _rows([p[SCAN_ROWS - 1 - j][1] for j in range(SCAN_ROWS)])
    row = lax.broadcasted_iota(jnp.int32, (SCAN_ROWS, c), 0)
    n = t // SCAN_ROWS

    def tile(ii, carry):
        cr, ci, dar, dai = carry
        r0 = pl.multiple_of((n - 1 - ii) * SCAN_ROWS, SCAN_ROWS)
        xr = gr[pl.ds(r0, SCAN_ROWS), :]
        xi = gi[pl.ds(r0, SCAN_ROWS), :]
        for k in (1, 2, 4):
            pr, pi = p[k - 1]
            shr = jnp.where(row < SCAN_ROWS - k, _shift_up(xr, k), 0.0)
            shi = jnp.where(row < SCAN_ROWS - k, _shift_up(xi, k), 0.0)
            xr, xi = xr + pr * shr + pi * shi, xi + pr * shi - pi * shr
        xr, xi = xr + pwr * cr + pwi * ci, xi + pwr * ci - pwi * cr
        gr[pl.ds(r0, SCAN_ROWS), :] = xr
        gi[pl.ds(r0, SCAN_ROWS), :] = xi
        xpr = _shift_down(xr_ref[pl.ds(r0, 2 * SCAN_ROWS), :], 1)[SCAN_ROWS:]
        xpi = _shift_down(xi_ref[pl.ds(r0, 2 * SCAN_ROWS), :], 1)[SCAN_ROWS:]
        return xr[0:1], xi[0:1], dar + xr * xpr + xi * xpi, dai + xi * xpr - xr * xpi

    z = jnp.zeros((1, c), F32)
    z8 = jnp.zeros((SCAN_ROWS, c), F32)
    _, _, dar, dai = lax.fori_loop(0, n, tile, (z, z, z8, z8))
    return jnp.sum(dar, axis=0, keepdims=True), jnp.sum(dai, axis=0, keepdims=True)


def _ssm_specs(t, nb, ch, st):
    col = pl.BlockSpec((t, ch), lambda b: (0, b))
    vec = pl.BlockSpec((1, ch), lambda b: (0, b))
    bspec = pl.BlockSpec((None, ch, st), lambda b: (b, 0, 0))
    cspec = pl.BlockSpec((None, st, ch), lambda b: (b, 0, 0))
    aspec = pl.BlockSpec((None, 1, st), lambda b: (b, 0, 0))
    return col, vec, bspec, cspec, aspec


def ssm_core_fwd(u, bre, bim, cre, cim, a_re, a_im, dskip, name):
    t, d = u.shape
    nb, ch, st = bre.shape

    def body(u_ref, bre_ref, bim_ref, cre_ref, cim_ref, ar_ref, ai_ref, d_ref, y_ref, yg_ref, sr, si):
        uv = u_ref[...]
        ub = uv.astype(BF16)
        sr[...] = jnp.dot(ub, bre_ref[...], preferred_element_type=F32)
        si[...] = jnp.dot(ub, bim_ref[...], preferred_element_type=F32)
        _scan_forward(sr, si, 0, t, ar_ref[...], ai_ref[...])
        y = (jnp.dot(sr[...].astype(BF16), cre_ref[...], preferred_element_type=F32)
             - jnp.dot(si[...].astype(BF16), cim_ref[...], preferred_element_type=F32) + d_ref[...] * uv)
        y_ref[...] = y
        yg_ref[...] = _gelu(y).astype(BF16)

    col, vec, bspec, cspec, aspec = _ssm_specs(t, nb, ch, st)
    return pl.pallas_call(
        body, name=name, grid=(nb,),
        in_specs=[col, bspec, bspec, cspec, cspec, aspec, aspec, vec],
        out_specs=[col, col],
        out_shape=[jax.ShapeDtypeStruct((t, d), F32), jax.ShapeDtypeStruct((t, d), BF16)],
        scratch_shapes=[pltpu.VMEM((t, st), F32), pltpu.VMEM((t, st), F32)],
        compiler_params=_params("parallel"),
    )(u, bre, bim, cre, cim, a_re, a_im, dskip)


def ssm_core_bwd(u, y, dyg, bre, bim, cre, cim, a_re, a_im, dskip, name):
    t, d = u.shape
    nb, ch, st = bre.shape
    tn_dims = (((0,), (0,)), ((), ()))
    nt_dims = (((1,), (1,)), ((), ()))

    def body(u_ref, y_ref, dyg_ref, bre_ref, bim_ref, cre_ref, cim_ref, ar_ref, ai_ref, d_ref,
             du_ref, dd_ref, dbre_ref, dbim_ref, dcre_ref, dcim_ref, dar_ref, dai_ref, xr, xi, gr, gi):
        uv = u_ref[...]
        ub = uv.astype(BF16)
        ar, ai = ar_ref[...], ai_ref[...]
        dy = dyg_ref[...] * _gelu_grad(y_ref[...])
        dd_ref[...] = jnp.sum(dy * uv, axis=0, keepdims=True)
        zero = jnp.zeros((HALO, st), F32)
        xr[0:HALO, :] = zero
        xi[0:HALO, :] = zero
        xr[HALO:, :] = jnp.dot(ub, bre_ref[...], preferred_element_type=F32)
        xi[HALO:, :] = jnp.dot(ub, bim_ref[...], preferred_element_type=F32)
        _scan_forward(xr, xi, HALO, t, ar, ai)
        dyb = dy.astype(BF16)
        dcre_ref[...] = lax.dot_general(xr[HALO:, :].astype(BF16), dyb, tn_dims, preferred_element_type=F32)
        dcim_ref[...] = -lax.dot_general(xi[HALO:, :].astype(BF16), dyb, tn_dims, preferred_element_type=F32)
        gr[...] = lax.dot_general(dyb, cre_ref[...], nt_dims, preferred_element_type=F32)
        gi[...] = -lax.dot_general(dyb, cim_ref[...], nt_dims, preferred_element_type=F32)
        dar, dai = _scan_reverse(gr, gi, t, ar, ai, xr, xi)
        dar_ref[...] = dar
        dai_ref[...] = dai
        grb = gr[...].astype(BF16)
        gib = gi[...].astype(BF16)
        dbre_ref[...] = lax.dot_general(ub, grb, tn_dims, preferred_element_type=F32)
        dbim_ref[...] = lax.dot_general(ub, gib, tn_dims, preferred_element_type=F32)
        du_ref[...] = (d_ref[...] * dy + lax.dot_general(grb, bre_ref[...], nt_dims, preferred_element_type=F32)
                       + lax.dot_general(gib, bim_ref[...], nt_dims, preferred_element_type=F32))

    col, vec, bspec, cspec, aspec = _ssm_specs(t, nb, ch, st)
    sh = jax.ShapeDtypeStruct
    return pl.pallas_call(
        body, name=name, grid=(nb,),
        in_specs=[col, col, col, bspec, bspec, cspec, cspec, aspec, aspec, vec],
        out_specs=[col, vec, bspec, bspec, cspec, cspec, aspec, aspec],
        out_shape=[sh((t, d), F32), sh((1, d), F32), sh((nb, ch, st), F32), sh((nb, ch, st), F32),
                   sh((nb, st, ch), F32), sh((nb, st, ch), F32), sh((nb, 1, st), F32), sh((nb, 1, st), F32)],
        scratch_shapes=[pltpu.VMEM((t + HALO, st), F32), pltpu.VMEM((t + HALO, st), F32),
                        pltpu.VMEM((t, st), F32), pltpu.VMEM((t, st), F32)],
        compiler_params=_params("parallel"),
    )(u, y, dyg, bre, bim, cre, cim, a_re, a_im, dskip)


def glu_fwd(yg, w_glu, b_glu, x, name, lead=None):
    t, d = yg.shape
    tn = _tile(d, 256)
    nd = d // tn

    def body(yg_ref, wv_ref, wg_ref, bv_ref, bg_ref, x_ref, val_ref, gate_ref, o_ref):
        ygv = yg_ref[...]
        vb = (jnp.dot(ygv, wv_ref[...], preferred_element_type=F32) + bv_ref[...]).astype(BF16)
        gb = (jnp.dot(ygv, wg_ref[...], preferred_element_type=F32) + bg_ref[...]).astype(BF16)
        val_ref[...] = vb
        gate_ref[...] = gb
        o_ref[...] = x_ref[...] + vb.astype(F32) * _sigmoid(gb.astype(F32))

    col = lambda off: _spec((d, tn), lambda j: (0, j + off), lead)
    vec = lambda off: pl.BlockSpec((1, tn), lambda j: (0, j + off))
    tile = pl.BlockSpec((t, tn), lambda j: (0, j))
    return pl.pallas_call(
        body, name=name, grid=(nd,),
        in_specs=[pl.BlockSpec((t, d), lambda j: (0, 0)), col(0), col(nd), vec(0), vec(nd), tile],
        out_specs=[tile, tile, tile],
        out_shape=[jax.ShapeDtypeStruct((t, d), BF16), jax.ShapeDtypeStruct((t, d), BF16),
                   jax.ShapeDtypeStruct((t, d), F32)],
        compiler_params=_params("parallel"),
    )(yg, w_glu, w_glu, b_glu, b_glu, x)


def glu_bwd(dm, val, gate, name):
    t, d = dm.shape
    tn = _tile(d, 256)

    def body(dm_ref, val_ref, gate_ref, dv_ref, dg_ref, dbv_ref, dbg_ref):
        dmv = dm_ref[...]
        s = _sigmoid(gate_ref[...].astype(F32))
        dval = dmv * s
        dgate = dmv * val_ref[...].astype(F32) * s * (1.0 - s)
        dv_ref[...] = dval.astype(BF16)
        dg_ref[...] = dgate.astype(BF16)
        dbv_ref[...] = jnp.sum(dval, axis=0, keepdims=True)
        dbg_ref[...] = jnp.sum(dgate, axis=0, keepdims=True)

    tile = pl.BlockSpec((t, tn), lambda j: (0, j))
    vec = pl.BlockSpec((1, tn), lambda j: (0, j))
    return pl.pallas_call(
        body, name=name, grid=(d // tn,),
        in_specs=[tile, tile, tile], out_specs=[tile, tile, vec, vec],
        out_shape=[jax.ShapeDtypeStruct((t, d), BF16), jax.ShapeDtypeStruct((t, d), BF16),
                   jax.ShapeDtypeStruct((1, d), F32), jax.ShapeDtypeStruct((1, d), F32)],
        compiler_params=_params("parallel"),
    )(dm, val, gate)


def _block_diag(m, gb):
    g, a, b = m.shape
    eye = jnp.eye(gb, dtype=m.dtype)
    return jnp.einsum("ngab,gk->ngakb", m.reshape(g // gb, gb, a, b), eye).reshape(g // gb, gb * a, gb * b)


def ssm_prepare(lam_re, lam_im, log_step, b_re, b_im, c_re, c_im):
    gb = SSM_BLOCK_GROUPS
    g, p = lam_re.shape
    step = jnp.exp(log_step)[:, None]
    mag = jnp.exp(lam_re * step)
    lb_re = mag * jnp.cos(lam_im * step)
    lb_im = mag * jnp.sin(lam_im * step)
    den = lam_re * lam_re + lam_im * lam_im
    f_re = ((lb_re - 1.0) * lam_re + lb_im * lam_im) / den
    f_im = (lb_im * lam_re - (lb_re - 1.0) * lam_im) / den
    bb_re = f_re[..., None] * b_re - f_im[..., None] * b_im
    bb_im = f_re[..., None] * b_im + f_im[..., None] * b_re
    tr = lambda m: jnp.transpose(m, (0, 2, 1))
    return (_block_diag(tr(bb_re), gb), _block_diag(tr(bb_im), gb), _block_diag(tr(c_re), gb), _block_diag(tr(c_im), gb),
            lb_re.reshape(g // gb, 1, gb * p), lb_im.reshape(g // gb, 1, gb * p))


EW_BLOCK_BYTES = 2 * 1024 * 1024
BF16_ROWS = 16


def _row_tile(rows, cols, block_bytes=EW_BLOCK_BYTES):
    limit = max(BF16_ROWS, block_bytes // (cols * 4))
    best = None
    for tr in range(BF16_ROWS, min(rows, limit) + 1, BF16_ROWS):
        if rows % tr == 0:
            best = tr
    return best if best is not None else rows


def _as2d(a):
    return a.reshape(-1, a.shape[-1])


def ew(fn, ins, out_dtypes, name):
    rows, cols = ins[0].shape
    tr = _row_tile(rows, cols)
    n_in = len(ins)

    def body(*refs):
        outs = fn(*[r[...] for r in refs[:n_in]])
        for o_ref, v in zip(refs[n_in:], outs):
            o_ref[...] = v.astype(o_ref.dtype)

    spec = pl.BlockSpec((tr, cols), lambda i: (i, 0))
    return pl.pallas_call(
        body, name=name, grid=(rows // tr,), in_specs=[spec] * n_in, out_specs=[spec] * len(out_dtypes),
        out_shape=[jax.ShapeDtypeStruct((rows, cols), dt) for dt in out_dtypes],
        compiler_params=_params("parallel"),
    )(*ins)


def _adamw(w, g, m, v):
    m = ADAM_B1 * m + (1.0 - ADAM_B1) * g
    v = ADAM_B2 * v + (1.0 - ADAM_B2) * (g * g)
    m_hat = m / (1.0 - ADAM_B1 ** ADAM_STEP)
    v_hat = v / (1.0 - ADAM_B2 ** ADAM_STEP)
    delta = -ADAM_LR * (m_hat / (jnp.sqrt(v_hat) + ADAM_EPS) + ADAM_WD * w)
    return delta, m, v


def adamw(w, g, m, v, name):
    outs = ew(_adamw, [_as2d(w), _as2d(g), _as2d(m), _as2d(v)], [F32, F32, F32], name)
    return [o.reshape(w.shape) for o in outs]


def adamw_layer(w, g, m, v, layer, name, into=None):
    nl, r, cw = w.shape
    tr = _row_tile(r, cw)

    def body(w_ref, g_ref, m_ref, v_ref, *rest):
        g_out, d_out, m_out, v_out = rest[-4:]
        gv = g_ref[...]
        d_out[...], m_out[...], v_out[...] = _adamw(w_ref[...], gv, m_ref[...], v_ref[...])
        g_out[...] = gv

    lay = pl.BlockSpec((None, tr, cw), lambda i: (layer, i, 0))
    args = [w, g, m, v]
    in_specs = [lay, pl.BlockSpec((None, tr, cw), lambda i: (0, i, 0)), lay, lay]
    aliases = {}
    if into is not None:
        args += list(into)
        in_specs += [pl.BlockSpec(memory_space=pl.ANY)] * 4
        aliases = {4 + k: k for k in range(4)}
    return pl.pallas_call(
        body, name=name, grid=(r // tr,), in_specs=in_specs, out_specs=[lay] * 4,
        out_shape=[jax.ShapeDtypeStruct((nl, r, cw), F32)] * 4,
        input_output_aliases=aliases,
        compiler_params=_params("parallel"),
    )(*args)


def loss_head(y, target, name):
    t, d = y.shape
    tr = min(t, NORM_ROWS)
    n = t // tr

    def body(y_ref, t_ref, dy_ref, dyb_ref, loss_ref, acc):
        i = pl.program_id(0)
        err = y_ref[...] - t_ref[...]
        dy = err * (1.0 / d)
        dy_ref[...] = dy
        dyb_ref[...] = dy.astype(BF16)
        part = jnp.sum(err * err, axis=0, keepdims=True)

        @pl.when(i == 0)
        def _():
            acc[...] = part

        @pl.when(i != 0)
        def _():
            acc[...] += part

        @pl.when(i == n - 1)
        def _():
            loss_ref[...] = jnp.full((1, LANES), 0.5 / d, F32) * jnp.sum(acc[...])

    row = pl.BlockSpec((tr, d), lambda i: (i, 0))
    return pl.pallas_call(
        body, name=name, grid=(n,), in_specs=[row, row],
        out_specs=[row, row, pl.BlockSpec((1, LANES), lambda i: (0, 0))],
        out_shape=[jax.ShapeDtypeStruct((t, d), F32), jax.ShapeDtypeStruct((t, d), BF16),
                   jax.ShapeDtypeStruct((1, LANES), F32)],
        scratch_shapes=[pltpu.VMEM((1, d), F32)],
        compiler_params=_params("arbitrary"),
    )(y, target)


HBM_SPEC = pl.BlockSpec(memory_space=pltpu.HBM)
VMEM_SPEC = pl.BlockSpec(memory_space=pltpu.VMEM)


def _place():
    return lax.axis_index("x"), lax.axis_index("y"), lax.axis_index("c")


def _other_chips(x, y):
    return [(1 - x, y), (x, 1 - y), (1 - x, 1 - y)]


def _remote(src, dst, send_sem, recv_sem, dev):
    return pltpu.make_async_remote_copy(src_ref=src, dst_ref=dst, send_sem=send_sem, recv_sem=recv_sem,
                                        device_id=dev, device_id_type=MESH)


def _piece(refs, shard_shape, ax, j, half):
    w = shard_shape[ax]
    a, off = divmod(j * w, refs[0].shape[ax]) if isinstance(j, int) else (0, j * w)
    idx = [pl.ds(0, s) for s in shard_shape]
    idx[ax] = pl.ds(off, w)
    if half is not None:
        h0 = shard_shape[0] // 2
        idx[0] = pl.ds((off if ax == 0 else 0) + half * h0, h0)
    return refs[a].at[tuple(idx)]


def small_allreduce(v, name):
    n, r, l = v.shape
    assert n == N_DEV

    def body(v_ref, o_ref, recv, red, send1, recv1, send2, recv2):
        x, y, c = _place()
        me = 4 * x + 2 * y + c
        dev = lambda k: (k // 4, (k // 2) % 2, k % 2)
        firsts = []
        for o in range(1, N_DEV):
            tgt = (me + o) % N_DEV
            cp = _remote(v_ref.at[tgt], recv.at[me], send1.at[o], recv1.at[me], dev(tgt))
            cp.start()
            firsts.append(cp)
        recv[me] = v_ref[me]
        for o in range(1, N_DEV):
            src = (me + o) % N_DEV
            _remote(v_ref.at[src], recv.at[src], send1.at[o], recv1.at[src], dev(src)).wait_recv()
        acc = recv[0]
        for s in range(1, N_DEV):
            acc = acc + recv[s]
        red[...] = acc
        o_ref[me] = acc
        seconds = []
        for o in range(1, N_DEV):
            tgt = (me + o) % N_DEV
            cp = _remote(red, o_ref.at[me], send2.at[o], recv2.at[me], dev(tgt))
            cp.start()
            seconds.append(cp)
        for o in range(1, N_DEV):
            src = (me + o) % N_DEV
            _remote(red, o_ref.at[src], send2.at[o], recv2.at[src], dev(src)).wait_recv()
        for cp in firsts + seconds:
            cp.wait_send()

    sems = pltpu.SemaphoreType.DMA((N_DEV,))
    return pl.pallas_call(
        body, name=name, in_specs=[VMEM_SPEC], out_specs=VMEM_SPEC,
        out_shape=jax.ShapeDtypeStruct(v.shape, F32),
        scratch_shapes=[pltpu.VMEM((N_DEV, r, l), F32), pltpu.VMEM((r, l), F32), sems, sems, sems, sems],
        compiler_params=pltpu.CompilerParams(vmem_limit_bytes=VMEM_LIMIT),
    )(v)


def _me_scalar():
    return (2 * lax.axis_index("x") + lax.axis_index("y")).astype(jnp.int32).reshape(1)


def cast_into_gathered(wf, layer, axis, me1, name):
    _, r, cw = wf.shape
    tr = _row_tile(r, cw)
    nrb = r // tr
    full = (1, r * N_CHIPS, cw) if axis == 0 else (1, r, cw * N_CHIPS)
    omap = (lambda i, me: (0, me[0] * nrb + i, 0)) if axis == 0 else (lambda i, me: (0, i, me[0]))

    def body(me_ref, w_ref, o_ref):
        o_ref[...] = w_ref[...].astype(BF16)

    return pl.pallas_call(
        body, name=name,
        grid_spec=pltpu.PrefetchScalarGridSpec(
            num_scalar_prefetch=1, grid=(nrb,),
            in_specs=[pl.BlockSpec((None, tr, cw), lambda i, me: (layer, i, 0))],
            out_specs=pl.BlockSpec((None, tr, cw), omap)),
        out_shape=jax.ShapeDtypeStruct(full, BF16),
        compiler_params=_params("parallel"),
    )(me1, wf)


SEM_SPEC = pl.BlockSpec(memory_space=pltpu.SEMAPHORE)
SPLIT_COPY_PARAMS = pltpu.CompilerParams(has_side_effects=pltpu.SideEffectType.DATAFLOW_SIDE_EFFECTING)
TOKEN_SHAPE = (SUBLANES, LANES)


def _hbm(a):
    return pltpu.with_memory_space_constraint(a, pltpu.HBM)


def _gather_copies(refs, shapes, axes, send_sem, recv_sem):
    x, y, c = _place()
    me = 2 * x + y
    out = []
    for p, ref in enumerate(refs):
        place = lambda j: _piece([ref.at[0]], shapes[p], axes[p], j, c)
        for q, chip in enumerate(_other_chips(x, y)):
            dev = (chip[0], chip[1], c)
            sems = (send_sem.at[3 * p + q], recv_sem.at[3 * p + q])
            theirs = place(2 * chip[0] + chip[1])
            out.append((_remote(place(me), place(me), *sems, dev), _remote(theirs, theirs, *sems, dev)))
    return out


def gather_start(bufs, shapes, axes, after, name):
    n = len(bufs)

    def body(*refs):
        send_sem, recv_sem = refs[n + 1:n + 3]
        o_refs = refs[n + 3:2 * n + 3]
        token = refs[-1]
        for mine, _ in _gather_copies(o_refs, shapes, axes, send_sem, recv_sem):
            mine.start()
        token[...] = jnp.zeros(TOKEN_SHAPE, F32)

    sems = pltpu.SemaphoreType.DMA((3 * n,))
    outs = pl.pallas_call(
        body, name=name,
        in_specs=[HBM_SPEC] * n + [pl.BlockSpec(memory_space=pl.ANY)],
        out_specs=[SEM_SPEC, SEM_SPEC] + [HBM_SPEC] * n + [VMEM_SPEC],
        out_shape=[sems, sems] + [pltpu.HBM(b.shape, b.dtype) for b in bufs] + [jax.ShapeDtypeStruct(TOKEN_SHAPE, F32)],
        input_output_aliases={p: p + 2 for p in range(n)},
        compiler_params=SPLIT_COPY_PARAMS,
    )(*[_hbm(b) for b in bufs], after)
    return outs[0], outs[1], list(outs[2:2 + n]), outs[-1]


def gather_wait(send_sem, recv_sem, bufs, shapes, axes, after, name):
    n = len(bufs)

    def body(*refs):
        s_sem, r_sem = refs[n:n + 2]
        o_refs = refs[n + 3:]
        for mine, theirs in _gather_copies(o_refs, shapes, axes, s_sem, r_sem):
            mine.wait_send()
            theirs.wait_recv()

    return pl.pallas_call(
        body, name=name,
        in_specs=[HBM_SPEC] * n + [SEM_SPEC, SEM_SPEC, pl.BlockSpec(memory_space=pl.ANY)],
        out_specs=[HBM_SPEC] * n,
        out_shape=[pltpu.HBM(b.shape, b.dtype) for b in bufs],
        input_output_aliases={p: p for p in range(n)},
        compiler_params=SPLIT_COPY_PARAMS,
    )(*bufs, send_sem, recv_sem, after)


SIBLING_SLOTS = 2
SIBLING_BLOCK_BYTES = 8 * 1024 * 1024
SHARE_BLOCK_BYTES = 4 * 1024 * 1024


def _row_step(nrb):
    s = pl.program_id(0)
    for ax in range(1, len(nrb)):
        s = s * nrb[ax] + pl.program_id(ax)
    return s


def gather_forward(buf, axis, r, cw, me1, name):
    nl = buf.shape[0]
    h0 = r // 2
    tr = _row_tile(h0, cw, SIBLING_BLOCK_BYTES)
    nrb = h0 // tr
    peer = lambda q, me: (me[0] + q + 1) % N_CHIPS
    if axis == 1:
        view = buf.reshape(nl, 1, 2, h0, N_CHIPS * cw)
        spec = pl.BlockSpec((1, 1, 2, tr, cw), lambda l, q, i, me: (l, 0, 0, i, peer(q, me)))
    else:
        view = buf.reshape(nl, N_CHIPS, 2, h0, cw)
        spec = pl.BlockSpec((1, 1, 2, tr, cw), lambda l, q, i, me: (l, peer(q, me), 0, i, 0))

    def body(me_ref, in_ref, o_ref, rbuf, send_sem, recv_sem):
        x, y, c = _place()
        slot = _row_step((nl, N_CHIPS - 1, nrb)) % SIBLING_SLOTS
        cp = _remote(in_ref.at[0, 0, c], rbuf.at[slot], send_sem.at[slot], recv_sem.at[slot], (x, y, 1 - c))
        cp.start()
        o_ref[0, 0, c] = in_ref[0, 0, c]
        cp.wait_recv()
        o_ref[0, 0, 1 - c] = rbuf[slot]
        cp.wait_send()

    out = pl.pallas_call(
        body, name=name,
        grid_spec=pltpu.PrefetchScalarGridSpec(
            num_scalar_prefetch=1, grid=(nl, N_CHIPS - 1, nrb), in_specs=[spec], out_specs=spec,
            scratch_shapes=[pltpu.VMEM((SIBLING_SLOTS, tr, cw), buf.dtype),
                            pltpu.SemaphoreType.DMA((SIBLING_SLOTS,)), pltpu.SemaphoreType.DMA((SIBLING_SLOTS,))]),
        out_shape=jax.ShapeDtypeStruct(view.shape, view.dtype),
        input_output_aliases={1: 0},
        compiler_params=_params("arbitrary", "arbitrary", "arbitrary"),
    )(me1, view)
    return out.reshape(buf.shape)


def pair_reduce(g, axis, r, cw, name, into=None, first_slot=0):
    h0 = r // 2
    tr = _row_tile(h0, cw, SIBLING_BLOCK_BYTES)
    nrb = h0 // tr
    if axis == 1:
        n_sh = g.shape[1] // cw
        view = g.reshape(1, 2, h0, n_sh * cw)
        spec = pl.BlockSpec((1, 2, tr, cw), lambda j, i: (0, 0, i, j))
    else:
        n_sh = g.shape[0] // r
        view = g.reshape(n_sh, 2, h0, cw)
        spec = pl.BlockSpec((1, 2, tr, cw), lambda j, i: (j, 0, i, 0))

    def body(g_ref, *rest):
        o_ref, rbuf, send_sem, recv_sem = rest[-4:]
        x, y, c = _place()
        slot = _row_step((n_sh, nrb)) % SIBLING_SLOTS
        cp = _remote(g_ref.at[0, 1 - c], rbuf.at[slot], send_sem.at[slot], recv_sem.at[slot], (x, y, 1 - c))
        cp.start()
        mine = g_ref[0, c].astype(F32)
        cp.wait_recv()
        o_ref[0] = (mine + rbuf[slot].astype(F32)).astype(BF16)
        cp.wait_send()

    args, in_specs, aliases = [view], [spec], {}
    if into is not None:
        args.append(into)
        in_specs.append(pl.BlockSpec(memory_space=pl.ANY))
        aliases = {1: 0}
    return pl.pallas_call(
        body, name=name, grid=(n_sh, nrb), in_specs=in_specs,
        out_specs=pl.BlockSpec((1, tr, cw), lambda j, i: (j + first_slot, i, 0)),
        out_shape=jax.ShapeDtypeStruct((N_CHIPS, h0, cw), BF16),
        input_output_aliases=aliases,
        scratch_shapes=[pltpu.VMEM((SIBLING_SLOTS, tr, cw), BF16),
                        pltpu.SemaphoreType.DMA((SIBLING_SLOTS,)), pltpu.SemaphoreType.DMA((SIBLING_SLOTS,))],
        compiler_params=_params("arbitrary", "arbitrary"),
    )(*args)


def _chip_copies(h_refs, lb_refs, send_sem, recv_sem):
    x, y, c = _place()
    out = []
    for k, (h, lb) in enumerate(zip(h_refs, lb_refs)):
        for q, chip in enumerate(_other_chips(x, y)):
            out.append(_remote(h.at[2 * chip[0] + chip[1]], lb.at[q], send_sem.at[3 * k + q], recv_sem.at[3 * k + q],
                               (chip[0], chip[1], c)))
    return out


def chip_start(halves, name):
    n = len(halves)
    landed = [lax.empty((N_CHIPS - 1,) + h.shape[1:], h.dtype) for h in halves]

    def body(*refs):
        send_sem, recv_sem = refs[2 * n:2 * n + 2]
        h_refs, lb_refs = refs[2 * n + 2:3 * n + 2], refs[3 * n + 2:4 * n + 2]
        for cp in _chip_copies(h_refs, lb_refs, send_sem, recv_sem):
            cp.start()
        refs[-1][...] = jnp.zeros(TOKEN_SHAPE, F32)

    sems = pltpu.SemaphoreType.DMA((3 * n,))
    outs = pl.pallas_call(
        body, name=name,
        in_specs=[HBM_SPEC] * (2 * n),
        out_specs=[SEM_SPEC, SEM_SPEC] + [HBM_SPEC] * (2 * n) + [VMEM_SPEC],
        out_shape=[sems, sems] + [pltpu.HBM(a.shape, a.dtype) for a in halves + landed]
        + [jax.ShapeDtypeStruct(TOKEN_SHAPE, F32)],
        input_output_aliases={p: p + 2 for p in range(2 * n)},
        compiler_params=SPLIT_COPY_PARAMS,
    )(*[_hbm(a) for a in halves + landed])
    return outs[0], outs[1], list(outs[2:2 + n]), list(outs[2 + n:2 + 2 * n]), outs[-1]


def chip_wait(send_sem, recv_sem, halves, landed, after, name):
    n = len(halves)

    def body(*refs):
        s_sem, r_sem = refs[2 * n:2 * n + 2]
        h_refs, lb_refs = refs[2 * n + 3:3 * n + 3], refs[3 * n + 3:]
        for cp in _chip_copies(h_refs, lb_refs, s_sem, r_sem):
            cp.wait_send()
            cp.wait_recv()

    outs = pl.pallas_call(
        body, name=name,
        in_specs=[HBM_SPEC] * (2 * n) + [SEM_SPEC, SEM_SPEC, pl.BlockSpec(memory_space=pl.ANY)],
        out_specs=[HBM_SPEC] * (2 * n),
        out_shape=[pltpu.HBM(a.shape, a.dtype) for a in halves + landed],
        input_output_aliases={p: p for p in range(2 * n)},
        compiler_params=SPLIT_COPY_PARAMS,
    )(*halves, *landed, send_sem, recv_sem, after)
    return list(outs[:n]), list(outs[n:])


def reduce_share(half, landed, me1, name):
    _, h0, cw = half.shape
    tr = _row_tile(h0, cw, SHARE_BLOCK_BYTES)
    nrb = h0 // tr

    def body(me_ref, h_ref, l0, l1, l2, o_ref, sbuf, rbuf, send_sem, recv_sem):
        x, y, c = _place()
        slot = pl.program_id(0) % SIBLING_SLOTS
        total = ((h_ref[...].astype(F32) + l0[...].astype(F32)) + l1[...].astype(F32)) + l2[...].astype(F32)
        sbuf[slot] = total
        cp = _remote(sbuf.at[slot], rbuf.at[slot], send_sem.at[slot], recv_sem.at[slot], (x, y, 1 - c))
        cp.start()
        o_ref[0, c] = total
        cp.wait_recv()
        o_ref[0, 1 - c] = rbuf[slot]
        cp.wait_send()

    landed_spec = lambda q: pl.BlockSpec((None, tr, cw), lambda i, me: (q, i, 0))
    args = [me1, half, landed, landed, landed]
    in_specs = [pl.BlockSpec((None, tr, cw), lambda i, me: (me[0], i, 0))] + [landed_spec(q) for q in range(N_CHIPS - 1)]
    out = pl.pallas_call(
        body, name=name,
        grid_spec=pltpu.PrefetchScalarGridSpec(
            num_scalar_prefetch=1, grid=(nrb,), in_specs=in_specs,
            out_specs=pl.BlockSpec((1, 2, tr, cw), lambda i, me: (0, 0, i, 0)),
            scratch_shapes=[pltpu.VMEM((SIBLING_SLOTS, tr, cw), F32), pltpu.VMEM((SIBLING_SLOTS, tr, cw), F32),
                            pltpu.SemaphoreType.DMA((SIBLING_SLOTS,)), pltpu.SemaphoreType.DMA((SIBLING_SLOTS,))]),
        out_shape=jax.ShapeDtypeStruct((1, 2, h0, cw), F32),
        compiler_params=_params("arbitrary"),
    )(*args)
    return out.reshape(1, 2 * h0, cw)


WEIGHTS = ["norm_mix_g", "norm_ffn_g", "pool_w", "pool_b", "pool_scale", "sb_w_qkv", "sb_q_gain", "sb_k_gain",
           "sb_w_o", "ssm_lam_re", "ssm_lam_im", "ssm_log_step", "ssm_b_re", "ssm_b_im", "ssm_c_re", "ssm_c_im",
           "ssm_d", "ssm_w_glu", "ssm_b_glu", "ffn_w_up", "ffn_conv_w", "ffn_conv_b", "ffn_w_down"]
BIG = {"sb_w_qkv": 1, "sb_w_o": 0, "ssm_w_glu": 1, "ffn_w_up": 1, "ffn_w_down": 0}
SMALL_SHARDED = {"pool_w": 2, "pool_b": 1, "pool_scale": 1, "ssm_d": 1, "ssm_b_glu": 1, "ffn_conv_w": 2}
SMALL = [n for n in WEIGHTS if n not in BIG]
SMALL_PAD = N_DEV * SUBLANES * LANES
N_MIXERS = 3


def _pack(arrays):
    flat = jnp.concatenate([a.reshape(-1).astype(F32) for a in arrays])
    total = -(-flat.shape[0] // SMALL_PAD) * SMALL_PAD
    flat = jnp.pad(flat, (0, total - flat.shape[0]))
    return flat.reshape(N_DEV, -1, LANES)


def _unpack(packed, like):
    flat = packed.reshape(-1)
    out, off = [], 0
    for a in like:
        out.append(flat[off:off + a.size].reshape(a.shape))
        off += a.size
    return out


def kernel(x, norm_mix_g, norm_ffn_g, pool_w, pool_b, pool_scale, sb_w_qkv, sb_q_gain, sb_k_gain, sb_w_o, ssm_lam_re, ssm_lam_im, ssm_log_step, ssm_b_re, ssm_b_im, ssm_c_re, ssm_c_im, ssm_d, ssm_w_glu, ssm_b_glu, ffn_w_up, ffn_conv_w, ffn_conv_b, ffn_w_down, loss_target, m_norm_mix_g, m_norm_ffn_g, m_pool_w, m_pool_b, m_pool_scale, m_sb_w_qkv, m_sb_q_gain, m_sb_k_gain, m_sb_w_o, m_ssm_lam_re, m_ssm_lam_im, m_ssm_log_step, m_ssm_b_re, m_ssm_b_im, m_ssm_c_re, m_ssm_c_im, m_ssm_d, m_ssm_w_glu, m_ssm_b_glu, m_ffn_w_up, m_ffn_conv_w, m_ffn_conv_b, m_ffn_w_down, v_norm_mix_g, v_norm_ffn_g, v_pool_w, v_pool_b, v_pool_scale, v_sb_w_qkv, v_sb_q_gain, v_sb_k_gain, v_sb_w_o, v_ssm_lam_re, v_ssm_lam_im, v_ssm_log_step, v_ssm_b_re, v_ssm_b_im, v_ssm_c_re, v_ssm_c_im, v_ssm_d, v_ssm_w_glu, v_ssm_b_glu, v_ffn_w_up, v_ffn_conv_w, v_ffn_conv_b, v_ffn_w_down):
    given = dict(locals())
    w = {n: given[n] for n in WEIGHTS}
    mom = {n: given["m_" + n] for n in WEIGHTS}
    var = {n: given["v_" + n] for n in WEIGHTS}
    xi, yi, ci = _place()
    me = 2 * xi + yi
    depth = norm_mix_g.shape[0]
    x_in = x[0]
    t, d = x_in.shape

    def placed(a, ax):
        shp = list(a.shape)
        shp[ax] *= N_CHIPS
        full = lax.dynamic_update_slice_in_dim(jnp.zeros(shp, F32), a, me * a.shape[ax], ax)
        return jnp.where(ci == 0, full, 0.0)

    sharded_full = [placed(w[n], ax) for n, ax in SMALL_SHARDED.items()]

    big = list(BIG)
    me1 = _me_scalar()
    shard = {n: tuple(w[n].shape[1:]) for n in big}
    vec = lambda a, i: a[i:i + 1]
    tie = lambda v, token: v + token[0, 0]

    def layer_weights(i):
        kind, j = i % N_MIXERS, i // N_MIXERS
        mixer = {0: [], 1: [("sb_w_qkv", j), ("sb_w_o", j)], 2: [("ssm_w_glu", j)]}[kind]
        return mixer + [("ffn_w_up", i), ("ffn_w_down", i)]

    started, token = [], x_in
    for i in range(depth):
        keys = layer_weights(i)
        bufs = [cast_into_gathered(w[n], l, BIG[n], me1, f"cast_{n}{l}") for n, l in keys]
        send_sem, recv_sem, bufs, token = gather_start(bufs, [shard[n] for n, _ in keys], [BIG[n] for n, _ in keys],
                                                       token, f"gather_start{i}")
        started.append((keys, send_sem, recv_sem, bufs))
        if i == 0:
            token = small_allreduce(tie(_pack(sharded_full), token), "gather_vectors")
            whole = dict(zip(SMALL_SHARDED, _unpack(token, sharded_full)))
    gathered = {}
    pool_w_bf = whole["pool_w"].astype(BF16)

    saved = []
    xc = x_in
    for i in range(depth):
        kind, j = i % N_MIXERS, i // N_MIXERS
        keys, send_sem, recv_sem, bufs = started[i]
        bufs = gather_wait(send_sem, recv_sem, bufs, [shard[n] for n, _ in keys], [BIG[n] for n, _ in keys],
                           token if i == 0 else xc, f"gather_wait{i}")
        for (n, l), b in zip(keys, bufs):
            gathered[n, l] = gather_forward(b, BIG[n], *shard[n], me1, f"gather_forward_{n}{l}")
        s = {"x_in": xc}
        g_mix = vec(norm_mix_g, i)
        if kind == 0:
            (h,) = rmsnorm_fwd(xc, g_mix, [F32], f"norm_mix{i}")
            x_mid = pool_fwd(h, xc, pool_w_bf, vec(whole["pool_b"], j), vec(whole["pool_scale"], j),
                             f"pool_fwd{i}", lead=j)
        elif kind == 1:
            (h,) = rmsnorm_fwd(xc, g_mix, [BF16], f"norm_mix{i}")
            s["qkv"] = mm_cols(h, gathered["sb_w_qkv", j], out_dtype=BF16, name=f"sb_qkv{i}", lead=0)
            s["o"], s["ltot"] = sb_attn_fwd(s["qkv"], vec(sb_q_gain, j), vec(sb_k_gain, j), f"sb_attn_fwd{i}")
            x_mid = mm_cols(s["o"], gathered["sb_w_o", j], out_dtype=F32, name=f"sb_out{i}", resid=xc, lead=0)
        else:
            (h,) = rmsnorm_fwd(xc, g_mix, [F32], f"norm_mix{i}")
            prm = tuple(w[n][j] for n in ("ssm_lam_re", "ssm_lam_im", "ssm_log_step", "ssm_b_re", "ssm_b_im",
                                          "ssm_c_re", "ssm_c_im"))
            prep, s["prep_vjp"] = jax.vjp(ssm_prepare, *prm)
            s["prep"] = tuple(a.astype(BF16) for a in prep[:4]) + tuple(prep[4:])
            s["y"], s["yg"] = ssm_core_fwd(h, *s["prep"], vec(whole["ssm_d"], j), f"ssm_fwd{i}")
            s["val"], s["gate"], x_mid = glu_fwd(s["yg"], gathered["ssm_w_glu", j], vec(whole["ssm_b_glu"], j), xc,
                                                 f"ssm_glu{i}", lead=0)
        s["x_mid"], s["h"] = x_mid, h
        (h2,) = rmsnorm_fwd(x_mid, vec(norm_ffn_g, i), [BF16], f"norm_ffn{i}")
        s["h2"] = h2
        s["up_val"], s["up_gate"], s["act"] = ffn_up_fused(h2, gathered["ffn_w_up", i], whole["ffn_conv_w"][i],
                                                           vec(ffn_conv_b, i), f"ffn_up{i}", lead=0)
        xc = mm_k([s["act"]], gathered["ffn_w_down", i], b_nt=False, name=f"ffn_down{i}", resid=x_mid, lead=0)
        saved.append(s)

    dx, dxb, loss_part = loss_head(xc, loss_target[0], "loss_head")
    loss = lax.psum(loss_part[0, 0], ("x", "y", "c"))

    small = {n: [None] * w[n].shape[0] for n in SMALL}
    big_g, updated = {}, {}

    def finish_reduction(pending, after):
        layer, keys, send_sem, recv_sem, halves, landed, _ = pending
        halves, landed = chip_wait(send_sem, recv_sem, halves, landed, after, f"grads_chip_wait{layer}")
        for (n, l), h, lb in zip(keys, halves, landed):
            g = reduce_share(h, lb, me1, f"grads_share_{n}{l}")
            updated[n] = adamw_layer(w[n], g, mom[n], var[n], l, f"adamw_{n}{l}", into=updated.get(n))

    pending = None
    for i in reversed(range(depth)):
        kind, j = i % N_MIXERS, i // N_MIXERS
        s = saved[i]
        g_ffn, g_mix = vec(norm_ffn_g, i), vec(norm_mix_g, i)
        cw, cb = whole["ffn_conv_w"][i], vec(ffn_conv_b, i)
        if pending is not None:
            cb = tie(cb, pending[-1])
        h, h2 = s["h"], s["h2"]
        dupv, dupg, dcwv, dcwg, dcbv, dcbg = ffn_bwd_fused(dxb, gathered["ffn_w_down", i], s["up_val"], s["up_gate"], cw, cb,
                                                           f"ffn_bwd{i}", lead=0)
        big_g["ffn_w_down", i] = [mm_rows(s["act"], dxb, out_dtype=BF16, name=f"ffn_dwdown{i}")]
        big_g["ffn_w_up", i] = [mm_cols(h2, dupv, a_contract=0, out_dtype=BF16, name=f"ffn_dwup_val{i}"),
                                mm_cols(h2, dupg, a_contract=0, out_dtype=BF16, name=f"ffn_dwup_gate{i}")]
        dh2 = mm_k([dupv, dupg], gathered["ffn_w_up", i], b_nt=True, name=f"ffn_dh{i}", lead=0)
        dx_mid, dxb_mid, small["norm_ffn_g"][i] = rmsnorm_bwd(s["x_mid"], g_ffn, dh2, dx, f"norm_ffn_bwd{i}")
        small["ffn_conv_w"][i] = jnp.concatenate([dcwv, dcwg], axis=1)[None]
        small["ffn_conv_b"][i] = jnp.concatenate([dcbv, dcbg], axis=1)

        if kind == 0:
            dh, dwp, small["pool_b"][j], small["pool_scale"][j] = pool_bwd(
                dx_mid, h, pool_w_bf, vec(whole["pool_b"], j), vec(whole["pool_scale"], j), f"pool_bwd{i}", lead=j)
            small["pool_w"][j] = dwp[None]
        elif kind == 1:
            do = mm_cols(dxb_mid, gathered["sb_w_o", j], b_nt=True, out_dtype=BF16, name=f"sb_do{i}", lead=0)
            big_g["sb_w_o", j] = [mm_cols(s["o"], dxb_mid, a_contract=0, out_dtype=BF16, name=f"sb_dwo{i}")]
            dq, dk, dv, small["sb_q_gain"][j], small["sb_k_gain"][j] = sb_attn_bwd(
                s["qkv"], s["ltot"], do, vec(sb_q_gain, j), vec(sb_k_gain, j), f"sb_attn_bwd{i}")
            dqkv = jnp.concatenate([dq, dk, dv], axis=1)
            big_g["sb_w_qkv", j] = [mm_cols(h, dqkv, a_contract=0, out_dtype=BF16, name=f"sb_dwqkv{i}")]
            dh = mm_k([dqkv], gathered["sb_w_qkv", j], b_nt=True, name=f"sb_dh{i}", lead=0)
        else:
            dval, dgate, dbv, dbg = glu_bwd(dx_mid, s["val"], s["gate"], f"ssm_glu_bwd{i}")
            small["ssm_b_glu"][j] = jnp.concatenate([dbv, dbg], axis=1)
            big_g["ssm_w_glu", j] = [mm_cols(s["yg"], dval, a_contract=0, out_dtype=BF16, name=f"ssm_dwglu_val{i}"),
                                     mm_cols(s["yg"], dgate, a_contract=0, out_dtype=BF16, name=f"ssm_dwglu_gate{i}")]
            dyg = mm_k([dval, dgate], gathered["ssm_w_glu", j], b_nt=True, name=f"ssm_dyg{i}", lead=0)
            dh, small["ssm_d"][j], *dprep = ssm_core_bwd(h, s["y"], dyg, *s["prep"], vec(whole["ssm_d"], j), f"ssm_bwd{i}")
            dprm = s["prep_vjp"](tuple(dprep))
            for n, g in zip(("ssm_lam_re", "ssm_lam_im", "ssm_log_step", "ssm_b_re", "ssm_b_im", "ssm_c_re", "ssm_c_im"),
                            dprm):
                small[n][j] = g[None]
        dx, dxb, small["norm_mix_g"][i] = rmsnorm_bwd(s["x_in"], g_mix, dh, dx_mid, f"norm_mix_bwd{i}")

        keys, halves = layer_weights(i), []
        for n, l in keys:
            h = None
            for a, g in enumerate(big_g[n, l]):
                h = pair_reduce(g, BIG[n], *shard[n], f"grads_pair_{n}{l}_{a}", into=h,
                                first_slot=a * (N_CHIPS // len(big_g[n, l])))
            halves.append(h)
        started_copies = chip_start(halves, f"grads_chip_start{i}")
        if pending is not None:
            finish_reduction(pending, dx)
        pending = (i, keys) + tuple(started_copies)

    small_full = [jnp.concatenate(small[n], axis=0) for n in SMALL]
    small_sum = dict(zip(SMALL, _unpack(small_allreduce(_pack(small_full), "reduce_vectors"), small_full)))
    grads = {}
    for n in SMALL:
        g = small_sum[n]
        if n in SMALL_SHARDED:
            ax = SMALL_SHARDED[n]
            g = lax.dynamic_slice_in_dim(g, me * w[n].shape[ax], w[n].shape[ax], ax)
        grads[n] = g
    delta, new_m, new_v = {}, {}, {}
    like = [w[n] for n in SMALL]
    packed = [_pack([src[n] for n in SMALL]).reshape(-1, LANES) for src in (w, grads, mom, var)]
    vector_updates = adamw(*packed, "adamw_vectors")
    for dst, out in zip((delta, new_m, new_v), vector_updates):
        dst.update(zip(SMALL, _unpack(out, like)))

    finish_reduction(pending, vector_updates[0])
    for n in big:
        grads[n], delta[n], new_m[n], new_v[n] = updated[n]

    return (loss, dx[None], *[grads[n] for n in WEIGHTS], *[delta[n] for n in WEIGHTS],
            *[new_m[n] for n in WEIGHTS], *[new_v[n] for n in WEIGHTS])
```

```python
import functools
import math

import jax
import jax.numpy as jnp
from jax import lax
from jax.experimental import pallas as pl
from jax.experimental.pallas import tpu as pltpu

F32 = jnp.float32
BF16 = jnp.bfloat16

RMS_EPS = 1e-6
POOL_WINDOWS = (2, 4, 8, 16)
SB_HEAD_DIM = 128
SSM_GROUP_CH = 16
SSM_STATE = 64
SSM_BLOCK_GROUPS = 8
ADAM_LR = 0.001
ADAM_B1 = 0.9
ADAM_B2 = 0.999
ADAM_EPS = 1e-08
ADAM_WD = 0.01
ADAM_STEP = 10

V7X_VMEM_BYTES = 64 * 1024 * 1024
VMEM_LIMIT = V7X_VMEM_BYTES - 8 * 1024 * 1024
SUBLANES = 8
LANES = 128
NORM_ROWS = 256
MESH = pl.DeviceIdType.MESH
N_CHIPS = 4
N_DEV = 8


def _params(*sem):
    return pltpu.CompilerParams(dimension_semantics=tuple(sem) if sem else None, vmem_limit_bytes=VMEM_LIMIT)


def _tile(n, want):
    if n <= want:
        return n
    t = (want // LANES) * LANES
    while t > LANES and n % t:
        t -= LANES
    assert n % t == 0, (n, want)
    return t


def _spec(shape, imap, lead=None):
    if lead is None:
        return pl.BlockSpec(tuple(shape), imap)
    return pl.BlockSpec((None,) + tuple(shape), lambda *a: (lead,) + tuple(imap(*a)))


def _sigmoid(v):
    return 1.0 / (1.0 + jnp.exp(-v))


def _shift_down(v, k):
    return pltpu.roll(v, k, 0)


def _shift_up(v, k):
    return pltpu.roll(v, v.shape[0] - k, 0)


def rmsnorm_fwd(x, g, out_dtypes, name):
    t, d = x.shape
    tr = min(t, NORM_ROWS)

    def body(x_ref, g_ref, *o_refs):
        xv = x_ref[...]
        r = lax.rsqrt(jnp.mean(xv * xv, axis=-1, keepdims=True) + RMS_EPS)
        h = xv * r * g_ref[...]
        for o in o_refs:
            o[...] = h.astype(o.dtype)

    outs = pl.pallas_call(
        body, name=name, grid=(t // tr,),
        in_specs=[pl.BlockSpec((tr, d), lambda i: (i, 0)), pl.BlockSpec((1, d), lambda i: (0, 0))],
        out_specs=[pl.BlockSpec((tr, d), lambda i: (i, 0)) for _ in out_dtypes],
        out_shape=[jax.ShapeDtypeStruct((t, d), dt) for dt in out_dtypes],
        compiler_params=_params("parallel"),
    )(x, g)
    return outs


def rmsnorm_bwd(x, g, dh, dres, name):
    t, d = x.shape
    tr = min(t, NORM_ROWS)

    def body(x_ref, g_ref, dh_ref, dres_ref, dx_ref, dxb_ref, dg_ref):
        xv = x_ref[...]
        r = lax.rsqrt(jnp.mean(xv * xv, axis=-1, keepdims=True) + RMS_EPS)
        xhat = xv * r
        dhv = dh_ref[...]
        dxhat = dhv * g_ref[...]
        dx = dres_ref[...] + r * (dxhat - xhat * jnp.mean(dxhat * xhat, axis=-1, keepdims=True))
        dx_ref[...] = dx
        dxb_ref[...] = dx.astype(BF16)
        part = jnp.sum(dhv * xhat, axis=0, keepdims=True)

        @pl.when(pl.program_id(0) == 0)
        def _():
            dg_ref[...] = part

        @pl.when(pl.program_id(0) != 0)
        def _():
            dg_ref[...] += part

    row = pl.BlockSpec((tr, d), lambda i: (i, 0))
    vec = pl.BlockSpec((1, d), lambda i: (0, 0))
    return pl.pallas_call(
        body, name=name, grid=(t // tr,),
        in_specs=[row, vec, row, row],
        out_specs=[row, row, vec],
        out_shape=[jax.ShapeDtypeStruct((t, d), F32), jax.ShapeDtypeStruct((t, d), BF16),
                   jax.ShapeDtypeStruct((1, d), F32)],
        compiler_params=_params("arbitrary"),
    )(x, g, dh, dres)


def mm_cols(a, b, *, a_contract=1, b_nt=False, out_dtype, name, resid=None, tn=512, lead=None):
    m = a.shape[1 - a_contract]
    k = a.shape[a_contract]
    n = b.shape[-2] if b_nt else b.shape[-1]
    assert (b.shape[-1] if b_nt else b.shape[-2]) == k
    tn = _tile(n, tn)

    def body(a_ref, b_ref, *rest):
        o_ref = rest[-1]
        dn = (((a_contract,), (1 if b_nt else 0,)), ((), ()))
        acc = lax.dot_general(a_ref[...], b_ref[...], dn, preferred_element_type=F32)
        if resid is not None:
            acc = acc + rest[0][...]
        o_ref[...] = acc.astype(o_ref.dtype)

    in_specs = [pl.BlockSpec(a.shape, lambda j: (0, 0)),
                _spec((tn, k), lambda j: (j, 0), lead) if b_nt else _spec((k, tn), lambda j: (0, j), lead)]
    args = [a, b]
    if resid is not None:
        in_specs.append(pl.BlockSpec((m, tn), lambda j: (0, j)))
        args.append(resid)
    return pl.pallas_call(
        body, name=name, grid=(n // tn,), in_specs=in_specs,
        out_specs=pl.BlockSpec((m, tn), lambda j: (0, j)),
        out_shape=jax.ShapeDtypeStruct((m, n), out_dtype),
        compiler_params=_params("parallel"),
    )(*args)


def mm_rows(st, res, *, out_dtype, name, tn=512):
    k, m = st.shape
    n = res.shape[1]
    assert res.shape[0] == k
    tn = _tile(m, tn)

    def body(st_ref, res_ref, o_ref):
        o_ref[...] = lax.dot_general(st_ref[...], res_ref[...], (((0,), (0,)), ((), ())),
                                     preferred_element_type=F32).astype(o_ref.dtype)

    return pl.pallas_call(
        body, name=name, grid=(m // tn,),
        in_specs=[pl.BlockSpec((k, tn), lambda j: (0, j)), pl.BlockSpec((k, n), lambda j: (0, 0))],
        out_specs=pl.BlockSpec((tn, n), lambda j: (j, 0)),
        out_shape=jax.ShapeDtypeStruct((m, n), out_dtype),
        compiler_params=_params("parallel"),
    )(st, res)


def mm_k(a_list, b, *, b_nt, name, resid=None, tk=512, tnn=1024, lead=None):
    m = a_list[0].shape[0]
    ks = [a.shape[1] for a in a_list]
    ktot = sum(ks)
    n = b.shape[-2] if b_nt else b.shape[-1]
    assert (b.shape[-1] if b_nt else b.shape[-2]) == ktot
    tk = _tile(ks[0], tk)
    assert all(kk % tk == 0 for kk in ks)
    tnn = _tile(n, tnn)
    nks = [kk // tk for kk in ks]
    starts = [sum(nks[:i]) for i in range(len(nks))]
    nk = sum(nks)

    def body(*refs):
        a_refs = refs[:len(a_list)]
        b_ref = refs[len(a_list)]
        o_ref = refs[-1]
        kk = pl.program_id(1)

        @pl.when(kk == 0)
        def _():
            if resid is not None:
                o_ref[...] = refs[len(a_list) + 1][...]
            else:
                o_ref[...] = jnp.zeros_like(o_ref)

        dn = (((1,), (1 if b_nt else 0,)), ((), ()))
        for i, a_ref in enumerate(a_refs):
            @pl.when(jnp.logical_and(kk >= starts[i], kk < starts[i] + nks[i]))
            def _(a_ref=a_ref):
                o_ref[...] += lax.dot_general(a_ref[...], b_ref[...], dn, preferred_element_type=F32)

    def a_spec(i):
        return pl.BlockSpec((m, tk), lambda nn, kk: (0, jnp.clip(kk - starts[i], 0, nks[i] - 1)))

    in_specs = [a_spec(i) for i in range(len(a_list))]
    in_specs.append(_spec((tnn, tk), lambda nn, kk: (nn, kk), lead) if b_nt
                    else _spec((tk, tnn), lambda nn, kk: (kk, nn), lead))
    args = list(a_list) + [b]
    if resid is not None:
        in_specs.append(pl.BlockSpec((m, tnn), lambda nn, kk: (0, nn)))
        args.append(resid)
    return pl.pallas_call(
        body, name=name, grid=(n // tnn, nk), in_specs=in_specs,
        out_specs=pl.BlockSpec((m, tnn), lambda nn, kk: (0, nn)),
        out_shape=jax.ShapeDtypeStruct((m, n), F32),
        compiler_params=_params("parallel", "arbitrary"),
    )(*args)


HALO = SUBLANES
CHUNK_ROWS = 64


def _conv_taps(ext, r):
    return ext[HALO:], _shift_down(ext, 1)[HALO:], _shift_down(ext, 2)[HALO:]


def ffn_up_fused(h, w_up, conv_w, conv_b, name, lead=None):
    t, d = h.shape
    f = w_up.shape[-1] // 2
    tn = _tile(f, 256)
    nf = f // tn
    r = min(CHUNK_ROWS, t)

    def body(h_ref, wv_ref, wg_ref, cwv_ref, cwg_ref, cbv_ref, cbg_ref, uv_ref, ug_ref, act_ref, sv, sg):
        zero = jnp.zeros((HALO, tn), F32)
        sv[0:HALO, :] = zero
        sg[0:HALO, :] = zero
        hv = h_ref[...]
        sv[HALO:, :] = jnp.dot(hv, wv_ref[...], preferred_element_type=F32).astype(BF16).astype(F32)
        sg[HALO:, :] = jnp.dot(hv, wg_ref[...], preferred_element_type=F32).astype(BF16).astype(F32)
        cwv, cwg = cwv_ref[...], cwg_ref[...]
        cbv, cbg = cbv_ref[...], cbg_ref[...]

        def chunk(i, carry):
            r0 = pl.multiple_of(i * r, r)
            v0, v1, v2 = _conv_taps(sv[pl.ds(r0, r + HALO), :], r)
            g0, g1, g2 = _conv_taps(sg[pl.ds(r0, r + HALO), :], r)
            cval = cbv + cwv[2:3] * v0 + cwv[1:2] * v1 + cwv[0:1] * v2
            cgate = cbg + cwg[2:3] * g0 + cwg[1:2] * g1 + cwg[0:1] * g2
            uv_ref[pl.ds(r0, r), :] = v0.astype(BF16)
            ug_ref[pl.ds(r0, r), :] = g0.astype(BF16)
            act_ref[pl.ds(r0, r), :] = (cgate * _sigmoid(cgate) * cval).astype(BF16)
            return carry

        lax.fori_loop(0, t // r, chunk, 0)

    col = lambda off: _spec((d, tn), lambda j: (0, j + off), lead)
    cw = lambda off: pl.BlockSpec((3, tn), lambda j: (0, j + off))
    cb = lambda off: pl.BlockSpec((1, tn), lambda j: (0, j + off))
    out = pl.BlockSpec((t, tn), lambda j: (0, j))
    return pl.pallas_call(
        body, name=name, grid=(nf,),
        in_specs=[pl.BlockSpec((t, d), lambda j: (0, 0)), col(0), col(nf), cw(0), cw(nf), cb(0), cb(nf)],
        out_specs=[out, out, out],
        out_shape=[jax.ShapeDtypeStruct((t, f), BF16)] * 3,
        scratch_shapes=[pltpu.VMEM((t + HALO, tn), F32), pltpu.VMEM((t + HALO, tn), F32)],
        compiler_params=_params("parallel"),
    )(h, w_up, w_up, conv_w, conv_w, conv_b, conv_b)


def ffn_bwd_fused(dout, w_down, up_val, up_gate, conv_w, conv_b, name, lead=None):
    t, d = dout.shape
    f = w_down.shape[-2]
    tn = _tile(f, 256)
    nf = f // tn
    r = min(CHUNK_ROWS, t)

    def body(do_ref, wd_ref, uv_ref, ug_ref, cwv_ref, cwg_ref, cbv_ref, cbg_ref,
             dv_ref, dg_ref, dcwv_ref, dcwg_ref, dcbv_ref, dcbg_ref, da, sv, sg, ev, eg):
        zero = jnp.zeros((HALO, tn), F32)
        sv[0:HALO, :] = zero
        sg[0:HALO, :] = zero
        ev[t:, :] = zero
        eg[t:, :] = zero
        da[...] = lax.dot_general(do_ref[...], wd_ref[...], (((1,), (1,)), ((), ())), preferred_element_type=F32)
        sv[HALO:, :] = uv_ref[...].astype(F32)
        sg[HALO:, :] = ug_ref[...].astype(F32)
        cwv, cwg = cwv_ref[...], cwg_ref[...]
        cbv, cbg = cbv_ref[...], cbg_ref[...]

        def chunk(i, acc):
            r0 = pl.multiple_of(i * r, r)
            v = _conv_taps(sv[pl.ds(r0, r + HALO), :], r)
            g = _conv_taps(sg[pl.ds(r0, r + HALO), :], r)
            cval = cbv + cwv[2:3] * v[0] + cwv[1:2] * v[1] + cwv[0:1] * v[2]
            cgate = cbg + cwg[2:3] * g[0] + cwg[1:2] * g[1] + cwg[0:1] * g[2]
            s = _sigmoid(cgate)
            dav = da[pl.ds(r0, r), :]
            dval = dav * (cgate * s)
            dgate = dav * cval * (s * (1.0 + cgate * (1.0 - s)))
            ev[pl.ds(r0, r), :] = dval
            eg[pl.ds(r0, r), :] = dgate
            col = lambda z: jnp.sum(z, axis=0, keepdims=True)
            new = [acc[0] + col(dval), acc[1] + col(dgate)]
            new += [acc[2 + j] + col(dval * v[2 - j]) for j in range(3)]
            new += [acc[5 + j] + col(dgate * g[2 - j]) for j in range(3)]
            return tuple(new)

        z1 = jnp.zeros((1, tn), F32)
        acc = lax.fori_loop(0, t // r, chunk, (z1,) * 8)
        dcbv_ref[...] = acc[0]
        dcbg_ref[...] = acc[1]
        for j in range(3):
            dcwv_ref[j:j + 1, :] = acc[2 + j]
            dcwg_ref[j:j + 1, :] = acc[5 + j]

        def chunk2(i, carry):
            r0 = pl.multiple_of(i * r, r)
            for e_ref, cw_, o_ref in ((ev, cwv, dv_ref), (eg, cwg, dg_ref)):
                ext = e_ref[pl.ds(r0, r + HALO), :]
                d0, d1, d2 = ext[:r], _shift_up(ext, 1)[:r], _shift_up(ext, 2)[:r]
                o_ref[pl.ds(r0, r), :] = (cw_[2:3] * d0 + cw_[1:2] * d1 + cw_[0:1] * d2).astype(BF16)
            return carry

        lax.fori_loop(0, t // r, chunk2, 0)

    cw = lambda off: pl.BlockSpec((3, tn), lambda j: (0, j + off))
    cb = lambda off: pl.BlockSpec((1, tn), lambda j: (0, j + off))
    tile = pl.BlockSpec((t, tn), lambda j: (0, j))
    s = lambda rows, dt: jax.ShapeDtypeStruct((rows, f), dt)
    halo = pltpu.VMEM((t + HALO, tn), F32)
    return pl.pallas_call(
        body, name=name, grid=(nf,),
        in_specs=[pl.BlockSpec((t, d), lambda j: (0, 0)), _spec((tn, d), lambda j: (j, 0), lead),
                  tile, tile, cw(0), cw(nf), cb(0), cb(nf)],
        out_specs=[tile, tile, pl.BlockSpec((3, tn), lambda j: (0, j)), pl.BlockSpec((3, tn), lambda j: (0, j)),
                   pl.BlockSpec((1, tn), lambda j: (0, j)), pl.BlockSpec((1, tn), lambda j: (0, j))],
        out_shape=[s(t, BF16), s(t, BF16), s(3, F32), s(3, F32), s(1, F32), s(1, F32)],
        scratch_shapes=[pltpu.VMEM((t, tn), F32), halo, halo, halo, halo],
        compiler_params=_params("parallel"),
    )(dout, w_down, up_val, up_gate, conv_w, conv_w, conv_b, conv_b)


POOL_PAD = max(POOL_WINDOWS)


def _window_sum(ext, win, shift):
    assert POOL_WINDOWS == (2, 4, 8, 16)
    s2 = ext + shift(ext, 1)
    s4 = s2 + shift(s2, 2)
    s8 = s4 + shift(s4, 4)
    s16 = s8 + shift(s8, 8)
    return jnp.where(win == 2, s2, jnp.where(win == 4, s4, jnp.where(win == 8, s8, s16)))


def _pool_win_scalar(g):
    win = jnp.int32(POOL_WINDOWS[-1])
    for k in range(len(POOL_WINDOWS) - 2, -1, -1):
        win = jnp.where(g == k, jnp.int32(POOL_WINDOWS[k]), win)
    return win


def _pool_count(r0, r, win):
    rows = r0 + lax.broadcasted_iota(jnp.int32, (r, 1), 0)
    return jnp.minimum(rows + 1, win).astype(F32)


def _pooled_into(hp, pooled, h_ref, t, r, win):
    hp[0:POOL_PAD, :] = jnp.zeros((POOL_PAD, hp.shape[1]), F32)
    hp[POOL_PAD:, :] = h_ref[...]

    def chunk(i, carry):
        r0 = pl.multiple_of(i * r, r)
        ext = hp[pl.ds(r0, r + POOL_PAD), :]
        s = _window_sum(ext, win, _shift_down)[POOL_PAD:]
        pooled[pl.ds(r0, r), :] = (s / _pool_count(r0, r, win) - ext[POOL_PAD:]).astype(BF16)
        return carry

    lax.fori_loop(0, t // r, chunk, 0)


def pool_fwd(h, x, w, b, scale, name, lead=None):
    t, d = h.shape
    ng, rows, dg = w.shape[-3:]
    r = min(CHUNK_ROWS, t)

    def body(h_ref, x_ref, w_ref, b_ref, s_ref, o_ref, hp, pooled):
        win = _pool_win_scalar(pl.program_id(0))
        _pooled_into(hp, pooled, h_ref, t, r, win)
        y = jnp.dot(pooled[...], w_ref[...].reshape(dg, dg), preferred_element_type=F32)
        o_ref[...] = x_ref[...] + (y + b_ref[...]) * s_ref[...]

    col = pl.BlockSpec((t, dg), lambda g: (0, g))
    vec = pl.BlockSpec((1, dg), lambda g: (0, g))
    return pl.pallas_call(
        body, name=name, grid=(ng,),
        in_specs=[col, col, _spec((N_CHIPS, None, rows, dg), lambda g: (0, g, 0, 0), lead), vec, vec],
        out_specs=col, out_shape=jax.ShapeDtypeStruct((t, d), F32),
        scratch_shapes=[pltpu.VMEM((t + POOL_PAD, dg), F32), pltpu.VMEM((t, dg), BF16)],
        compiler_params=_params("parallel"),
    )(h, x, w, b, scale)


def pool_bwd(dm, h, w, b, scale, name, lead=None):
    t, d = h.shape
    ng, rows, dg = w.shape[-3:]
    r = min(CHUNK_ROWS, t)

    def body(dm_ref, h_ref, w_ref, b_ref, s_ref, dh_ref, dw_ref, db_ref, ds_ref, hp, pooled, q):
        win = _pool_win_scalar(pl.program_id(0))
        _pooled_into(hp, pooled, h_ref, t, r, win)
        wv = w_ref[...].reshape(dg, dg)
        y = jnp.dot(pooled[...], wv, preferred_element_type=F32)
        dmv = dm_ref[...]
        ds_ref[...] = jnp.sum(dmv * (y + b_ref[...]), axis=0, keepdims=True)
        dy = dmv * s_ref[...]
        db_ref[...] = jnp.sum(dy, axis=0, keepdims=True)
        dyb = dy.astype(BF16)
        dw_ref[...] = lax.dot_general(pooled[...], dyb, (((0,), (0,)), ((), ())),
                                      preferred_element_type=F32).astype(BF16).reshape(N_CHIPS, rows, dg)
        dp = lax.dot_general(dyb, wv, (((1,), (1,)), ((), ())), preferred_element_type=F32)
        q[t:, :] = jnp.zeros((POOL_PAD, dg), F32)
        q[0:t, :] = dp / _pool_count(0, t, win)
        dh_ref[...] = -dp

        def chunk(i, carry):
            r0 = pl.multiple_of(i * r, r)
            ext = q[pl.ds(r0, r + POOL_PAD), :]
            dh_ref[pl.ds(r0, r), :] += _window_sum(ext, win, _shift_up)[:r]
            return carry

        lax.fori_loop(0, t // r, chunk, 0)

    col = pl.BlockSpec((t, dg), lambda g: (0, g))
    vec = pl.BlockSpec((1, dg), lambda g: (0, g))
    wshape = (N_CHIPS, None, rows, dg)
    wmap = lambda g: (0, g, 0, 0)
    return pl.pallas_call(
        body, name=name, grid=(ng,),
        in_specs=[col, col, _spec(wshape, wmap, lead), vec, vec],
        out_specs=[col, pl.BlockSpec(wshape, wmap), vec, vec],
        out_shape=[jax.ShapeDtypeStruct((t, d), F32), jax.ShapeDtypeStruct((N_CHIPS, ng, rows, dg), BF16),
                   jax.ShapeDtypeStruct((1, d), F32), jax.ShapeDtypeStruct((1, d), F32)],
        scratch_shapes=[pltpu.VMEM((t + POOL_PAD, dg), F32), pltpu.VMEM((t, dg), BF16),
                        pltpu.VMEM((t + POOL_PAD, dg), F32)],
        compiler_params=_params("parallel"),
    )(dm, h, w, b, scale)


SB_BLOCK = 256


def _tri_sum(v, tri):
    hi = v.astype(BF16)
    lo = (v - hi.astype(F32)).astype(BF16)
    dot = lambda p: jnp.dot(p, tri, preferred_element_type=F32)
    return dot(hi) + dot(lo)


def _tri(bk, cmp):
    return cmp(lax.broadcasted_iota(jnp.int32, (bk, bk), 0), lax.broadcasted_iota(jnp.int32, (bk, bk), 1)).astype(BF16)


def _sb_logits(qblk, kblk, q0, k0, inv):
    bq, bk = qblk.shape[0], kblk.shape[0]
    z = lax.dot_general(qblk, kblk, (((1,), (1,)), ((), ())), preferred_element_type=F32) * inv
    qpos = q0 + lax.broadcasted_iota(jnp.int32, (bq, bk), 0)
    kpos = k0 + lax.broadcasted_iota(jnp.int32, (bq, bk), 1)
    mask = kpos < qpos
    lb = jnp.minimum(z, 0.0) - jnp.log(1.0 + jnp.exp(-jnp.abs(z)))
    lm = jnp.where(mask, lb - z, 0.0)
    return lb, lm, mask


def _head_norm(ref, gain):
    xv = ref[...].astype(F32)
    r = lax.rsqrt(jnp.mean(xv * xv, axis=-1, keepdims=True) + RMS_EPS)
    xhat = xv * r
    return xhat, r, (xhat * gain).astype(BF16)


def sb_attn_fwd(qkv, q_gain, k_gain, name):
    t = qkv.shape[0]
    d = qkv.shape[1] // 3
    dh = SB_HEAD_DIM
    nh = d // dh
    blk = min(SB_BLOCK, t)
    inv = 1.0 / math.sqrt(dh)

    def body(q_ref, k_ref, v_ref, qg_ref, kg_ref, o_ref, lt_ref, qn, kn):
        qn[...] = _head_norm(q_ref, qg_ref[...])[2]
        kn[...] = _head_norm(k_ref, kg_ref[...])[2]
        later = _tri(blk, lambda j, s: j > s)

        def q_loop(qb, carry):
            q0 = pl.multiple_of(qb * blk, blk)
            qblk = qn[pl.ds(q0, blk), :]

            def k_loop(i, st):
                c, acc = st
                k0 = pl.multiple_of((qb - i) * blk, blk)
                lb, lm, mask = _sb_logits(qblk, kn[pl.ds(k0, blk), :], q0, k0, inv)
                a = jnp.where(mask, jnp.exp(lb + _tri_sum(lm, later) + c), 0.0)
                acc = acc + jnp.dot(a.astype(BF16), v_ref[pl.ds(k0, blk), :], preferred_element_type=F32)
                return c + jnp.sum(lm, axis=1, keepdims=True), acc

            c, acc = lax.fori_loop(0, qb + 1, k_loop, (jnp.zeros((blk, 1), F32), jnp.zeros((blk, dh), F32)))
            o_ref[pl.ds(q0, blk), :] = acc.astype(BF16)
            lt_ref[pl.ds(q0, blk), :] = c
            return carry

        lax.fori_loop(0, t // blk, q_loop, 0)

    head = lambda off: pl.BlockSpec((t, dh), lambda h: (0, h + off))
    gain = pl.BlockSpec((1, dh), lambda h: (0, 0))
    return pl.pallas_call(
        body, name=name, grid=(nh,),
        in_specs=[head(0), head(nh), head(2 * nh), gain, gain],
        out_specs=[head(0), pl.BlockSpec((None, t, 1), lambda h: (h, 0, 0))],
        out_shape=[jax.ShapeDtypeStruct((t, d), BF16), jax.ShapeDtypeStruct((nh, t, 1), F32)],
        scratch_shapes=[pltpu.VMEM((t, dh), BF16), pltpu.VMEM((t, dh), BF16)],
        compiler_params=_params("parallel"),
    )(qkv, qkv, qkv, q_gain, k_gain)


def sb_attn_bwd(qkv, ltot, do, q_gain, k_gain, name):
    t = qkv.shape[0]
    d = qkv.shape[1] // 3
    dh = SB_HEAD_DIM
    nh = d // dh
    blk = min(SB_BLOCK, t)
    inv = 1.0 / math.sqrt(dh)
    tn_dims = (((0,), (0,)), ((), ()))

    def body(q_ref, k_ref, v_ref, lt_ref, do_ref, qg_ref, kg_ref, dq_ref, dk_ref, dv_ref, dqg_ref, dkg_ref,
             qn, kn, dqn, dkn, dvn):
        qg, kg = qg_ref[...], kg_ref[...]
        qhat, rq, qnb = _head_norm(q_ref, qg)
        khat, rk, knb = _head_norm(k_ref, kg)
        qn[...] = qnb
        kn[...] = knb
        dkn[...] = jnp.zeros_like(dkn)
        dvn[...] = jnp.zeros_like(dvn)
        upto = _tri(blk, lambda j, s: j <= s)
        before = _tri(blk, lambda j, s: j < s)

        def q_loop(qb, carry):
            q0 = pl.multiple_of(qb * blk, blk)
            qblk = qn[pl.ds(q0, blk), :]
            doblk = do_ref[pl.ds(q0, blk), :]
            ltv = lt_ref[pl.ds(q0, blk), :]

            def k_loop(kb, st):
                pl_, pg, dq = st
                k0 = pl.multiple_of(kb * blk, blk)
                kblk = kn[pl.ds(k0, blk), :]
                lb, lm, mask = _sb_logits(qblk, kblk, q0, k0, inv)
                a = jnp.where(mask, jnp.exp(lb + (ltv - pl_ - _tri_sum(lm, upto))), 0.0)
                da = lax.dot_general(doblk, v_ref[pl.ds(k0, blk), :], (((1,), (1,)), ((), ())),
                                     preferred_element_type=F32)
                g = da * a
                g_before = pg + _tri_sum(g, before)
                beta = jnp.exp(lb)
                dz = (jnp.where(mask, g * (1.0 - beta) - beta * g_before, 0.0) * inv).astype(BF16)
                dq = dq + jnp.dot(dz, kblk, preferred_element_type=F32)
                dkn[pl.ds(k0, blk), :] += lax.dot_general(dz, qblk, tn_dims, preferred_element_type=F32)
                dvn[pl.ds(k0, blk), :] += lax.dot_general(a.astype(BF16), doblk, tn_dims, preferred_element_type=F32)
                return pl_ + jnp.sum(lm, axis=1, keepdims=True), pg + jnp.sum(g, axis=1, keepdims=True), dq

            z1 = jnp.zeros((blk, 1), F32)
            _, _, dq = lax.fori_loop(0, qb + 1, k_loop, (z1, z1, jnp.zeros((blk, dh), F32)))
            dqn[pl.ds(q0, blk), :] = dq
            return carry

        lax.fori_loop(0, t // blk, q_loop, 0)

        first = pl.program_id(0) == 0
        for dn, xhat, r, gain, out_ref, dgain_ref in ((dqn, qhat, rq, qg, dq_ref, dqg_ref),
                                                      (dkn, khat, rk, kg, dk_ref, dkg_ref)):
            dnv = dn[...]
            dxhat = dnv * gain
            out_ref[...] = (r * (dxhat - xhat * jnp.mean(dxhat * xhat, axis=-1, keepdims=True))).astype(BF16)
            part = jnp.sum(dnv * xhat, axis=0, keepdims=True)

            @pl.when(first)
            def _(dgain_ref=dgain_ref, part=part):
                dgain_ref[...] = part

            @pl.when(jnp.logical_not(first))
            def _(dgain_ref=dgain_ref, part=part):
                dgain_ref[...] += part

        dv_ref[...] = dvn[...].astype(BF16)

    head = lambda off: pl.BlockSpec((t, dh), lambda h: (0, h + off))
    gain = pl.BlockSpec((1, dh), lambda h: (0, 0))
    big = jax.ShapeDtypeStruct((t, d), BF16)
    small = jax.ShapeDtypeStruct((1, dh), F32)
    return pl.pallas_call(
        body, name=name, grid=(nh,),
        in_specs=[head(0), head(nh), head(2 * nh), pl.BlockSpec((None, t, 1), lambda h: (h, 0, 0)), head(0),
                  gain, gain],
        out_specs=[head(0), head(0), head(0), gain, gain],
        out_shape=[big, big, big, small, small],
        scratch_shapes=[pltpu.VMEM((t, dh), BF16), pltpu.VMEM((t, dh), BF16),
                        pltpu.VMEM((t, dh), F32), pltpu.VMEM((t, dh), F32), pltpu.VMEM((t, dh), F32)],
        compiler_params=_params("arbitrary"),
    )(qkv, qkv, qkv, ltot, do, q_gain, k_gain)


GELU_C = math.sqrt(2.0 / math.pi)
GELU_A = 0.044715
SCAN_ROWS = SUBLANES


def _gelu(y):
    return 0.5 * y * (1.0 + jnp.tanh(GELU_C * (y + GELU_A * y * y * y)))


def _gelu_grad(y):
    th = jnp.tanh(GELU_C * (y + GELU_A * y * y * y))
    return 0.5 * (1.0 + th) + 0.5 * y * (1.0 - th * th) * GELU_C * (1.0 + 3.0 * GELU_A * y * y)


def _powers(ar, ai):
    out = [(ar, ai)]
    for _ in range(SCAN_ROWS - 1):
        pr, pi = out[-1]
        out.append((pr * ar - pi * ai, pr * ai + pi * ar))
    return out


def _rows(vals):
    c = vals[0].shape[1]
    row = lax.broadcasted_iota(jnp.int32, (SCAN_ROWS, c), 0)
    out = jnp.broadcast_to(vals[SCAN_ROWS - 1], (SCAN_ROWS, c))
    for j in range(SCAN_ROWS - 2, -1, -1):
        out = jnp.where(row == j, vals[j], out)
    return out


def _scan_forward(sr, si, off, t, ar, ai):
    c = ar.shape[1]
    p = _powers(ar, ai)
    pwr = _rows([q[0] for q in p])
    pwi = _rows([q[1] for q in p])
    row = lax.broadcasted_iota(jnp.int32, (SCAN_ROWS, c), 0)

    def tile(i, carry):
        cr, ci = carry
        r0 = pl.multiple_of(off + i * SCAN_ROWS, SCAN_ROWS)
        xr = sr[pl.ds(r0, SCAN_ROWS), :]
        xi = si[pl.ds(r0, SCAN_ROWS), :]
        for k in (1, 2, 4):
            pr, pi = p[k - 1]
            shr = jnp.where(row >= k, _shift_down(xr, k), 0.0)
            shi = jnp.where(row >= k, _shift_down(xi, k), 0.0)
            xr, xi = xr + pr * shr - pi * shi, xi + pr * shi + pi * shr
        xr, xi = xr + pwr * cr - pwi * ci, xi + pwr * ci + pwi * cr
        sr[pl.ds(r0, SCAN_ROWS), :] = xr
        si[pl.ds(r0, SCAN_ROWS), :] = xi
        return xr[SCAN_ROWS - 1:SCAN_ROWS], xi[SCAN_ROWS - 1:SCAN_ROWS]

    z = jnp.zeros((1, c), F32)
    lax.fori_loop(0, t // SCAN_ROWS, tile, (z, z))


def _scan_reverse(gr, gi, t, ar, ai, xr_ref, xi_ref):
    c = ar.shape[1]
    p = _powers(ar, ai)
    pwr = _rows([p[SCAN_ROWS - 1 - j][0] for j in range(SCAN_ROWS)])
    pwi = _rows([p[SCAN_ROWS - 1 - j][1] for j in range(SCAN_ROWS)])
    row = lax.broadcasted_iota(jnp.int32, (SCAN_ROWS, c), 0)
    n = t // SCAN_ROWS

    def tile(ii, carry):
        cr, ci, dar, dai = carry
        r0 = pl.multiple_of((n - 1 - ii) * SCAN_ROWS, SCAN_ROWS)
        xr = gr[pl.ds(r0, SCAN_ROWS), :]
        xi = gi[pl.ds(r0, SCAN_ROWS), :]
        for k in (1, 2, 4):
            pr, pi = p[k - 1]
            shr = jnp.where(row < SCAN_ROWS - k, _shift_up(xr, k), 0.0)
            shi = jnp.where(row < SCAN_ROWS - k, _shift_up(xi, k), 0.0)
            xr, xi = xr + pr * shr + pi * shi, xi + pr * shi - pi * shr
        xr, xi = xr + pwr * cr + pwi * ci, xi + pwr * ci - pwi * cr
        gr[pl.ds(r0, SCAN_ROWS), :] = xr
        gi[pl.ds(r0, SCAN_ROWS), :] = xi
        xpr = _shift_down(xr_ref[pl.ds(r0, 2 * SCAN_ROWS), :], 1)[SCAN_ROWS:]
        xpi = _shift_down(xi_ref[pl.ds(r0, 2 * SCAN_ROWS), :], 1)[SCAN_ROWS:]
        return xr[0:1], xi[0:1], dar + xr * xpr + xi * xpi, dai + xi * xpr - xr * xpi

    z = jnp.zeros((1, c), F32)
    z8 = jnp.zeros((SCAN_ROWS, c), F32)
    _, _, dar, dai = lax.fori_loop(0, n, tile, (z, z, z8, z8))
    return jnp.sum(dar, axis=0, keepdims=True), jnp.sum(dai, axis=0, keepdims=True)


def _ssm_specs(t, nb, ch, st):
    col = pl.BlockSpec((t, ch), lambda b: (0, b))
    vec = pl.BlockSpec((1, ch), lambda b: (0, b))
    bspec = pl.BlockSpec((None, ch, st), lambda b: (b, 0, 0))
    cspec = pl.BlockSpec((None, st, ch), lambda b: (b, 0, 0))
    aspec = pl.BlockSpec((None, 1, st), lambda b: (b, 0, 0))
    return col, vec, bspec, cspec, aspec


def ssm_core_fwd(u, bre, bim, cre, cim, a_re, a_im, dskip, name):
    t, d = u.shape
    nb, ch, st = bre.shape

    def body(u_ref, bre_ref, bim_ref, cre_ref, cim_ref, ar_ref, ai_ref, d_ref, y_ref, yg_ref, sr, si):
        uv = u_ref[...]
        ub = uv.astype(BF16)
        sr[...] = jnp.dot(ub, bre_ref[...], preferred_element_type=F32)
        si[...] = jnp.dot(ub, bim_ref[...], preferred_element_type=F32)
        _scan_forward(sr, si, 0, t, ar_ref[...], ai_ref[...])
        y = (jnp.dot(sr[...].astype(BF16), cre_ref[...], preferred_element_type=F32)
             - jnp.dot(si[...].astype(BF16), cim_ref[...], preferred_element_type=F32) + d_ref[...] * uv)
        y_ref[...] = y
        yg_ref[...] = _gelu(y).astype(BF16)

    col, vec, bspec, cspec, aspec = _ssm_specs(t, nb, ch, st)
    return pl.pallas_call(
        body, name=name, grid=(nb,),
        in_specs=[col, bspec, bspec, cspec, cspec, aspec, aspec, vec],
        out_specs=[col, col],
        out_shape=[jax.ShapeDtypeStruct((t, d), F32), jax.ShapeDtypeStruct((t, d), BF16)],
        scratch_shapes=[pltpu.VMEM((t, st), F32), pltpu.VMEM((t, st), F32)],
        compiler_params=_params("parallel"),
    )(u, bre, bim, cre, cim, a_re, a_im, dskip)


def ssm_core_bwd(u, y, dyg, bre, bim, cre, cim, a_re, a_im, dskip, name):
    t, d = u.shape
    nb, ch, st = bre.shape
    tn_dims = (((0,), (0,)), ((), ()))
    nt_dims = (((1,), (1,)), ((), ()))

    def body(u_ref, y_ref, dyg_ref, bre_ref, bim_ref, cre_ref, cim_ref, ar_ref, ai_ref, d_ref,
             du_ref, dd_ref, dbre_ref, dbim_ref, dcre_ref, dcim_ref, dar_ref, dai_ref, xr, xi, gr, gi):
        uv = u_ref[...]
        ub = uv.astype(BF16)
        ar, ai = ar_ref[...], ai_ref[...]
        dy = dyg_ref[...] * _gelu_grad(y_ref[...])
        dd_ref[...] = jnp.sum(dy * uv, axis=0, keepdims=True)
        zero = jnp.zeros((HALO, st), F32)
        xr[0:HALO, :] = zero
        xi[0:HALO, :] = zero
        xr[HALO:, :] = jnp.dot(ub, bre_ref[...], preferred_element_type=F32)
        xi[HALO:, :] = jnp.dot(ub, bim_ref[...], preferred_element_type=F32)
        _scan_forward(xr, xi, HALO, t, ar, ai)
        dyb = dy.astype(BF16)
        dcre_ref[...] = lax.dot_general(xr[HALO:, :].astype(BF16), dyb, tn_dims, preferred_element_type=F32)
        dcim_ref[...] = -lax.dot_general(xi[HALO:, :].astype(BF16), dyb, tn_dims, preferred_element_type=F32)
        gr[...] = lax.dot_general(dyb, cre_ref[...], nt_dims, preferred_element_type=F32)
        gi[...] = -lax.dot_general(dyb, cim_ref[...], nt_dims, preferred_element_type=F32)
        dar, dai = _scan_reverse(gr, gi, t, ar, ai, xr, xi)
        dar_ref[...] = dar
        dai_ref[...] = dai
        grb = gr[...].astype(BF16)
        gib = gi[...].astype(BF16)
        dbre_ref[...] = lax.dot_general(ub, grb, tn_dims, preferred_element_type=F32)
        dbim_ref[...] = lax.dot_general(ub, gib, tn_dims, preferred_element_type=F32)
        du_ref[...] = (d_ref[...] * dy + lax.dot_general(grb, bre_ref[...], nt_dims, preferred_element_type=F32)
                       + lax.dot_general(gib, bim_ref[...], nt_dims, preferred_element_type=F32))

    col, vec, bspec, cspec, aspec = _ssm_specs(t, nb, ch, st)
    sh = jax.ShapeDtypeStruct
    return pl.pallas_call(
        body, name=name, grid=(nb,),
        in_specs=[col, col, col, bspec, bspec, cspec, cspec, aspec, aspec, vec],
        out_specs=[col, vec, bspec, bspec, cspec, cspec, aspec, aspec],
        out_shape=[sh((t, d), F32), sh((1, d), F32), sh((nb, ch, st), F32), sh((nb, ch, st), F32),
                   sh((nb, st, ch), F32), sh((nb, st, ch), F32), sh((nb, 1, st), F32), sh((nb, 1, st), F32)],
        scratch_shapes=[pltpu.VMEM((t + HALO, st), F32), pltpu.VMEM((t + HALO, st), F32),
                        pltpu.VMEM((t, st), F32), pltpu.VMEM((t, st), F32)],
        compiler_params=_params("parallel"),
    )(u, y, dyg, bre, bim, cre, cim, a_re, a_im, dskip)


def glu_fwd(yg, w_glu, b_glu, x, name, lead=None):
    t, d = yg.shape
    tn = _tile(d, 256)
    nd = d // tn

    def body(yg_ref, wv_ref, wg_ref, bv_ref, bg_ref, x_ref, val_ref, gate_ref, o_ref):
        ygv = yg_ref[...]
        vb = (jnp.dot(ygv, wv_ref[...], preferred_element_type=F32) + bv_ref[...]).astype(BF16)
        gb = (jnp.dot(ygv, wg_ref[...], preferred_element_type=F32) + bg_ref[...]).astype(BF16)
        val_ref[...] = vb
        gate_ref[...] = gb
        o_ref[...] = x_ref[...] + vb.astype(F32) * _sigmoid(gb.astype(F32))

    col = lambda off: _spec((d, tn), lambda j: (0, j + off), lead)
    vec = lambda off: pl.BlockSpec((1, tn), lambda j: (0, j + off))
    tile = pl.BlockSpec((t, tn), lambda j: (0, j))
    return pl.pallas_call(
        body, name=name, grid=(nd,),
        in_specs=[pl.BlockSpec((t, d), lambda j: (0, 0)), col(0), col(nd), vec(0), vec(nd), tile],
        out_specs=[tile, tile, tile],
        out_shape=[jax.ShapeDtypeStruct((t, d), BF16), jax.ShapeDtypeStruct((t, d), BF16),
                   jax.ShapeDtypeStruct((t, d), F32)],
        compiler_params=_params("parallel"),
    )(yg, w_glu, w_glu, b_glu, b_glu, x)


def glu_bwd(dm, val, gate, name):
    t, d = dm.shape
    tn = _tile(d, 256)

    def body(dm_ref, val_ref, gate_ref, dv_ref, dg_ref, dbv_ref, dbg_ref):
        dmv = dm_ref[...]
        s = _sigmoid(gate_ref[...].astype(F32))
        dval = dmv * s
        dgate = dmv * val_ref[...].astype(F32) * s * (1.0 - s)
        dv_ref[...] = dval.astype(BF16)
        dg_ref[...] = dgate.astype(BF16)
        dbv_ref[...] = jnp.sum(dval, axis=0, keepdims=True)
        dbg_ref[...] = jnp.sum(dgate, axis=0, keepdims=True)

    tile = pl.BlockSpec((t, tn), lambda j: (0, j))
    vec = pl.BlockSpec((1, tn), lambda j: (0, j))
    return pl.pallas_call(
        body, name=name, grid=(d // tn,),
        in_specs=[tile, tile, tile], out_specs=[tile, tile, vec, vec],
        out_shape=[jax.ShapeDtypeStruct((t, d), BF16), jax.ShapeDtypeStruct((t, d), BF16),
                   jax.ShapeDtypeStruct((1, d), F32), jax.ShapeDtypeStruct((1, d), F32)],
        compiler_params=_params("parallel"),
    )(dm, val, gate)


def _block_diag(m, gb):
    g, a, b = m.shape
    eye = jnp.eye(gb, dtype=m.dtype)
    return jnp.einsum("ngab,gk->ngakb", m.reshape(g // gb, gb, a, b), eye).reshape(g // gb, gb * a, gb * b)


def ssm_prepare(lam_re, lam_im, log_step, b_re, b_im, c_re, c_im):
    gb = SSM_BLOCK_GROUPS
    g, p = lam_re.shape
    step = jnp.exp(log_step)[:, None]
    mag = jnp.exp(lam_re * step)
    lb_re = mag * jnp.cos(lam_im * step)
    lb_im = mag * jnp.sin(lam_im * step)
    den = lam_re * lam_re + lam_im * lam_im
    f_re = ((lb_re - 1.0) * lam_re + lb_im * lam_im) / den
    f_im = (lb_im * lam_re - (lb_re - 1.0) * lam_im) / den
    bb_re = f_re[..., None] * b_re - f_im[..., None] * b_im
    bb_im = f_re[..., None] * b_im + f_im[..., None] * b_re
    tr = lambda m: jnp.transpose(m, (0, 2, 1))
    return (_block_diag(tr(bb_re), gb), _block_diag(tr(bb_im), gb), _block_diag(tr(c_re), gb), _block_diag(tr(c_im), gb),
            lb_re.reshape(g // gb, 1, gb * p), lb_im.reshape(g // gb, 1, gb * p))


EW_BLOCK_BYTES = 2 * 1024 * 1024
BF16_ROWS = 16


def _row_tile(rows, cols, block_bytes=EW_BLOCK_BYTES):
    limit = max(BF16_ROWS, block_bytes // (cols * 4))
    best = None
    for tr in range(BF16_ROWS, min(rows, limit) + 1, BF16_ROWS):
        if rows % tr == 0:
            best = tr
    return best if best is not None else rows


def _as2d(a):
    return a.reshape(-1, a.shape[-1])


def ew(fn, ins, out_dtypes, name):
    rows, cols = ins[0].shape
    tr = _row_tile(rows, cols)
    n_in = len(ins)

    def body(*refs):
        outs = fn(*[r[...] for r in refs[:n_in]])
        for o_ref, v in zip(refs[n_in:], outs):
            o_ref[...] = v.astype(o_ref.dtype)

    spec = pl.BlockSpec((tr, cols), lambda i: (i, 0))
    return pl.pallas_call(
        body, name=name, grid=(rows // tr,), in_specs=[spec] * n_in, out_specs=[spec] * len(out_dtypes),
        out_shape=[jax.ShapeDtypeStruct((rows, cols), dt) for dt in out_dtypes],
        compiler_params=_params("parallel"),
    )(*ins)


def _adamw(w, g, m, v):
    m = ADAM_B1 * m + (1.0 - ADAM_B1) * g
    v = ADAM_B2 * v + (1.0 - ADAM_B2) * (g * g)
    m_hat = m / (1.0 - ADAM_B1 ** ADAM_STEP)
    v_hat = v / (1.0 - ADAM_B2 ** ADAM_STEP)
    delta = -ADAM_LR * (m_hat / (jnp.sqrt(v_hat) + ADAM_EPS) + ADAM_WD * w)
    return delta, m, v


def adamw(w, g, m, v, name):
    outs = ew(_adamw, [_as2d(w), _as2d(g), _as2d(m), _as2d(v)], [F32, F32, F32], name)
    return [o.reshape(w.shape) for o in outs]


def adamw_layer(w, g, m, v, layer, name, into=None):
    nl, r, cw = w.shape
    tr = _row_tile(r, cw)

    def body(w_ref, g_ref, m_ref, v_ref, *rest):
        g_out, d_out, m_out, v_out = rest[-4:]
        gv = g_ref[...]
        d_out[...], m_out[...], v_out[...] = _adamw(w_ref[...], gv, m_ref[...], v_ref[...])
        g_out[...] = gv

    lay = pl.BlockSpec((None, tr, cw), lambda i: (layer, i, 0))
    args = [w, g, m, v]
    in_specs = [lay, pl.BlockSpec((None, tr, cw), lambda i: (0, i, 0)), lay, lay]
    aliases = {}
    if into is not None:
        args += list(into)
        in_specs += [pl.BlockSpec(memory_space=pl.ANY)] * 4
        aliases = {4 + k: k for k in range(4)}
    return pl.pallas_call(
        body, name=name, grid=(r // tr,), in_specs=in_specs, out_specs=[lay] * 4,
        out_shape=[jax.ShapeDtypeStruct((nl, r, cw), F32)] * 4,
        input_output_aliases=aliases,
        compiler_params=_params("parallel"),
    )(*args)


def loss_head(y, target, name):
    t, d = y.shape
    tr = min(t, NORM_ROWS)
    n = t // tr

    def body(y_ref, t_ref, dy_ref, dyb_ref, loss_ref, acc):
        i = pl.program_id(0)
        err = y_ref[...] - t_ref[...]
        dy = err * (1.0 / d)
        dy_ref[...] = dy
        dyb_ref[...] = dy.astype(BF16)
        part = jnp.sum(err * err, axis=0, keepdims=True)

        @pl.when(i == 0)
        def _():
            acc[...] = part

        @pl.when(i != 0)
        def _():
            acc[...] += part

        @pl.when(i == n - 1)
        def _():
            loss_ref[...] = jnp.full((1, LANES), 0.5 / d, F32) * jnp.sum(acc[...])

    row = pl.BlockSpec((tr, d), lambda i: (i, 0))
    return pl.pallas_call(
        body, name=name, grid=(n,), in_specs=[row, row],
        out_specs=[row, row, pl.BlockSpec((1, LANES), lambda i: (0, 0))],
        out_shape=[jax.ShapeDtypeStruct((t, d), F32), jax.ShapeDtypeStruct((t, d), BF16),
                   jax.ShapeDtypeStruct((1, LANES), F32)],
        scratch_shapes=[pltpu.VMEM((1, d), F32)],
        compiler_params=_params("arbitrary"),
    )(y, target)


HBM_SPEC = pl.BlockSpec(memory_space=pltpu.HBM)
VMEM_SPEC = pl.BlockSpec(memory_space=pltpu.VMEM)


def _place():
    return lax.axis_index("x"), lax.axis_index("y"), lax.axis_index("c")


def _other_chips(x, y):
    return [(1 - x, y), (x, 1 - y), (1 - x, 1 - y)]


def _remote(src, dst, send_sem, recv_sem, dev):
    return pltpu.make_async_remote_copy(src_ref=src, dst_ref=dst, send_sem=send_sem, recv_sem=recv_sem,
                                        device_id=dev, device_id_type=MESH)


def _piece(refs, shard_shape, ax, j, half):
    w = shard_shape[ax]
    a, off = divmod(j * w, refs[0].shape[ax]) if isinstance(j, int) else (0, j * w)
    idx = [pl.ds(0, s) for s in shard_shape]
    idx[ax] = pl.ds(off, w)
    if half is not None:
        h0 = shard_shape[0] // 2
        idx[0] = pl.ds((off if ax == 0 else 0) + half * h0, h0)
    return refs[a].at[tuple(idx)]


def small_allreduce(v, name):
    n, r, l = v.shape
    assert n == N_DEV

    def body(v_ref, o_ref, recv, red, send1, recv1, send2, recv2):
        x, y, c = _place()
        me = 4 * x + 2 * y + c
        dev = lambda k: (k // 4, (k // 2) % 2, k % 2)
        firsts = []
        for o in range(1, N_DEV):
            tgt = (me + o) % N_DEV
            cp = _remote(v_ref.at[tgt], recv.at[me], send1.at[o], recv1.at[me], dev(tgt))
            cp.start()
            firsts.append(cp)
        recv[me] = v_ref[me]
        for o in range(1, N_DEV):
            src = (me + o) % N_DEV
            _remote(v_ref.at[src], recv.at[src], send1.at[o], recv1.at[src], dev(src)).wait_recv()
        acc = recv[0]
        for s in range(1, N_DEV):
            acc = acc + recv[s]
        red[...] = acc
        o_ref[me] = acc
        seconds = []
        for o in range(1, N_DEV):
            tgt = (me + o) % N_DEV
            cp = _remote(red, o_ref.at[me], send2.at[o], recv2.at[me], dev(tgt))
            cp.start()
            seconds.append(cp)
        for o in range(1, N_DEV):
            src = (me + o) % N_DEV
            _remote(red, o_ref.at[src], send2.at[o], recv2.at[src], dev(src)).wait_recv()
        for cp in firsts + seconds:
            cp.wait_send()

    sems = pltpu.SemaphoreType.DMA((N_DEV,))
    return pl.pallas_call(
        body, name=name, in_specs=[VMEM_SPEC], out_specs=VMEM_SPEC,
        out_shape=jax.ShapeDtypeStruct(v.shape, F32),
        scratch_shapes=[pltpu.VMEM((N_DEV, r, l), F32), pltpu.VMEM((r, l), F32), sems, sems, sems, sems],
        compiler_params=pltpu.CompilerParams(vmem_limit_bytes=VMEM_LIMIT),
    )(v)


def _me_scalar():
    return (2 * lax.axis_index("x") + lax.axis_index("y")).astype(jnp.int32).reshape(1)


def cast_into_gathered(wf, layer, axis, me1, name):
    _, r, cw = wf.shape
    tr = _row_tile(r, cw)
    nrb = r // tr
    full = (1, r * N_CHIPS, cw) if axis == 0 else (1, r, cw * N_CHIPS)
    omap = (lambda i, me: (0, me[0] * nrb + i, 0)) if axis == 0 else (lambda i, me: (0, i, me[0]))

    def body(me_ref, w_ref, o_ref):
        o_ref[...] = w_ref[...].astype(BF16)

    return pl.pallas_call(
        body, name=name,
        grid_spec=pltpu.PrefetchScalarGridSpec(
            num_scalar_prefetch=1, grid=(nrb,),
            in_specs=[pl.BlockSpec((None, tr, cw), lambda i, me: (layer, i, 0))],
            out_specs=pl.BlockSpec((None, tr, cw), omap)),
        out_shape=jax.ShapeDtypeStruct(full, BF16),
        compiler_params=_params("parallel"),
    )(me1, wf)


SEM_SPEC = pl.BlockSpec(memory_space=pltpu.SEMAPHORE)
SPLIT_COPY_PARAMS = pltpu.CompilerParams(has_side_effects=pltpu.SideEffectType.DATAFLOW_SIDE_EFFECTING)
TOKEN_SHAPE = (SUBLANES, LANES)


def _hbm(a):
    return pltpu.with_memory_space_constraint(a, pltpu.HBM)


def _gather_copies(refs, shapes, axes, send_sem, recv_sem):
    x, y, c = _place()
    me = 2 * x + y
    out = []
    for p, ref in enumerate(refs):
        place = lambda j: _piece([ref.at[0]], shapes[p], axes[p], j, c)
        for q, chip in enumerate(_other_chips(x, y)):
            dev = (chip[0], chip[1], c)
            sems = (send_sem.at[3 * p + q], recv_sem.at[3 * p + q])
            theirs = place(2 * chip[0] + chip[1])
            out.append((_remote(place(me), place(me), *sems, dev), _remote(theirs, theirs, *sems, dev)))
    return out


def gather_start(bufs, shapes, axes, after, name):
    n = len(bufs)

    def body(*refs):
        send_sem, recv_sem = refs[n + 1:n + 3]
        o_refs = refs[n + 3:2 * n + 3]
        token = refs[-1]
        for mine, _ in _gather_copies(o_refs, shapes, axes, send_sem, recv_sem):
            mine.start()
        token[...] = jnp.zeros(TOKEN_SHAPE, F32)

    sems = pltpu.SemaphoreType.DMA((3 * n,))
    outs = pl.pallas_call(
        body, name=name,
        in_specs=[HBM_SPEC] * n + [pl.BlockSpec(memory_space=pl.ANY)],
        out_specs=[SEM_SPEC, SEM_SPEC] + [HBM_SPEC] * n + [VMEM_SPEC],
        out_shape=[sems, sems] + [pltpu.HBM(b.shape, b.dtype) for b in bufs] + [jax.ShapeDtypeStruct(TOKEN_SHAPE, F32)],
        input_output_aliases={p: p + 2 for p in range(n)},
        compiler_params=SPLIT_COPY_PARAMS,
    )(*[_hbm(b) for b in bufs], after)
    return outs[0], outs[1], list(outs[2:2 + n]), outs[-1]


def gather_wait(send_sem, recv_sem, bufs, shapes, axes, after, name):
    n = len(bufs)

    def body(*refs):
        s_sem, r_sem = refs[n:n + 2]
        o_refs = refs[n + 3:]
        for mine, theirs in _gather_copies(o_refs, shapes, axes, s_sem, r_sem):
            mine.wait_send()
            theirs.wait_recv()

    return pl.pallas_call(
        body, name=name,
        in_specs=[HBM_SPEC] * n + [SEM_SPEC, SEM_SPEC, pl.BlockSpec(memory_space=pl.ANY)],
        out_specs=[HBM_SPEC] * n,
        out_shape=[pltpu.HBM(b.shape, b.dtype) for b in bufs],
        input_output_aliases={p: p for p in range(n)},
        compiler_params=SPLIT_COPY_PARAMS,
    )(*bufs, send_sem, recv_sem, after)


SIBLING_SLOTS = 2
SIBLING_BLOCK_BYTES = 8 * 1024 * 1024
SHARE_BLOCK_BYTES = 4 * 1024 * 1024


def _row_step(nrb):
    s = pl.program_id(0)
    for ax in range(1, len(nrb)):
        s = s * nrb[ax] + pl.program_id(ax)
    return s


def gather_forward(buf, axis, r, cw, me1, name):
    nl = buf.shape[0]
    h0 = r // 2
    tr = _row_tile(h0, cw, SIBLING_BLOCK_BYTES)
    nrb = h0 // tr
    peer = lambda q, me: (me[0] + q + 1) % N_CHIPS
    if axis == 1:
        view = buf.reshape(nl, 1, 2, h0, N_CHIPS * cw)
        spec = pl.BlockSpec((1, 1, 2, tr, cw), lambda l, q, i, me: (l, 0, 0, i, peer(q, me)))
    else:
        view = buf.reshape(nl, N_CHIPS, 2, h0, cw)
        spec = pl.BlockSpec((1, 1, 2, tr, cw), lambda l, q, i, me: (l, peer(q, me), 0, i, 0))

    def body(me_ref, in_ref, o_ref, rbuf, send_sem, recv_sem):
        x, y, c = _place()
        slot = _row_step((nl, N_CHIPS - 1, nrb)) % SIBLING_SLOTS
        cp = _remote(in_ref.at[0, 0, c], rbuf.at[slot], send_sem.at[slot], recv_sem.at[slot], (x, y, 1 - c))
        cp.start()
        o_ref[0, 0, c] = in_ref[0, 0, c]
        cp.wait_recv()
        o_ref[0, 0, 1 - c] = rbuf[slot]
        cp.wait_send()

    out = pl.pallas_call(
        body, name=name,
        grid_spec=pltpu.PrefetchScalarGridSpec(
            num_scalar_prefetch=1, grid=(nl, N_CHIPS - 1, nrb), in_specs=[spec], out_specs=spec,
            scratch_shapes=[pltpu.VMEM((SIBLING_SLOTS, tr, cw), buf.dtype),
                            pltpu.SemaphoreType.DMA((SIBLING_SLOTS,)), pltpu.SemaphoreType.DMA((SIBLING_SLOTS,))]),
        out_shape=jax.ShapeDtypeStruct(view.shape, view.dtype),
        input_output_aliases={1: 0},
        compiler_params=_params("arbitrary", "arbitrary", "arbitrary"),
    )(me1, view)
    return out.reshape(buf.shape)


def pair_reduce(g, axis, r, cw, name, into=None, first_slot=0):
    h0 = r // 2
    tr = _row_tile(h0, cw, SIBLING_BLOCK_BYTES)
    nrb = h0 // tr
    if axis == 1:
        n_sh = g.shape[1] // cw
        view = g.reshape(1, 2, h0, n_sh * cw)
        spec = pl.BlockSpec((1, 2, tr, cw), lambda j, i: (0, 0, i, j))
    else:
        n_sh = g.shape[0] // r
        view = g.reshape(n_sh, 2, h0, cw)
        spec = pl.BlockSpec((1, 2, tr, cw), lambda j, i: (j, 0, i, 0))

    def body(g_ref, *rest):
        o_ref, rbuf, send_sem, recv_sem = rest[-4:]
        x, y, c = _place()
        slot = _row_step((n_sh, nrb)) % SIBLING_SLOTS
        cp = _remote(g_ref.at[0, 1 - c], rbuf.at[slot], send_sem.at[slot], recv_sem.at[slot], (x, y, 1 - c))
        cp.start()
        mine = g_ref[0, c].astype(F32)
        cp.wait_recv()
        o_ref[0] = (mine + rbuf[slot].astype(F32)).astype(BF16)
        cp.wait_send()

    args, in_specs, aliases = [view], [spec], {}
    if into is not None:
        args.append(into)
        in_specs.append(pl.BlockSpec(memory_space=pl.ANY))
        aliases = {1: 0}
    return pl.pallas_call(
        body, name=name, grid=(n_sh, nrb), in_specs=in_specs,
        out_specs=pl.BlockSpec((1, tr, cw), lambda j, i: (j + first_slot, i, 0)),
        out_shape=jax.ShapeDtypeStruct((N_CHIPS, h0, cw), BF16),
        input_output_aliases=aliases,
        scratch_shapes=[pltpu.VMEM((SIBLING_SLOTS, tr, cw), BF16),
                        pltpu.SemaphoreType.DMA((SIBLING_SLOTS,)), pltpu.SemaphoreType.DMA((SIBLING_SLOTS,))],
        compiler_params=_params("arbitrary", "arbitrary"),
    )(*args)


def _chip_copies(h_refs, lb_refs, send_sem, recv_sem):
    x, y, c = _place()
    out = []
    for k, (h, lb) in enumerate(zip(h_refs, lb_refs)):
        for q, chip in enumerate(_other_chips(x, y)):
            out.append(_remote(h.at[2 * chip[0] + chip[1]], lb.at[q], send_sem.at[3 * k + q], recv_sem.at[3 * k + q],
                               (chip[0], chip[1], c)))
    return out


def chip_start(halves, name):
    n = len(halves)
    landed = [lax.empty((N_CHIPS - 1,) + h.shape[1:], h.dtype) for h in halves]

    def body(*refs):
        send_sem, recv_sem = refs[2 * n:2 * n + 2]
        h_refs, lb_refs = refs[2 * n + 2:3 * n + 2], refs[3 * n + 2:4 * n + 2]
        for cp in _chip_copies(h_refs, lb_refs, send_sem, recv_sem):
            cp.start()
        refs[-1][...] = jnp.zeros(TOKEN_SHAPE, F32)

    sems = pltpu.SemaphoreType.DMA((3 * n,))
    outs = pl.pallas_call(
        body, name=name,
        in_specs=[HBM_SPEC] * (2 * n),
        out_specs=[SEM_SPEC, SEM_SPEC] + [HBM_SPEC] * (2 * n) + [VMEM_SPEC],
        out_shape=[sems, sems] + [pltpu.HBM(a.shape, a.dtype) for a in halves + landed]
        + [jax.ShapeDtypeStruct(TOKEN_SHAPE, F32)],
        input_output_aliases={p: p + 2 for p in range(2 * n)},
        compiler_params=SPLIT_COPY_PARAMS,
    )(*[_hbm(a) for a in halves + landed])
    return outs[0], outs[1], list(outs[2:2 + n]), list(outs[2 + n:2 + 2 * n]), outs[-1]


def chip_wait(send_sem, recv_sem, halves, landed, after, name):
    n = len(halves)

    def body(*refs):
        s_sem, r_sem = refs[2 * n:2 * n + 2]
        h_refs, lb_refs = refs[2 * n + 3:3 * n + 3], refs[3 * n + 3:]
        for cp in _chip_copies(h_refs, lb_refs, s_sem, r_sem):
            cp.wait_send()
            cp.wait_recv()

    outs = pl.pallas_call(
        body, name=name,
        in_specs=[HBM_SPEC] * (2 * n) + [SEM_SPEC, SEM_SPEC, pl.BlockSpec(memory_space=pl.ANY)],
        out_specs=[HBM_SPEC] * (2 * n),
        out_shape=[pltpu.HBM(a.shape, a.dtype) for a in halves + landed],
        input_output_aliases={p: p for p in range(2 * n)},
        compiler_params=SPLIT_COPY_PARAMS,
    )(*halves, *landed, send_sem, recv_sem, after)
    return list(outs[:n]), list(outs[n:])


def reduce_share(half, landed, me1, name):
    _, h0, cw = half.shape
    tr = _row_tile(h0, cw, SHARE_BLOCK_BYTES)
    nrb = h0 // tr

    def body(me_ref, h_ref, l0, l1, l2, o_ref, sbuf, rbuf, send_sem, recv_sem):
        x, y, c = _place()
        slot = pl.program_id(0) % SIBLING_SLOTS
        total = ((h_ref[...].astype(F32) + l0[...].astype(F32)) + l1[...].astype(F32)) + l2[...].astype(F32)
        sbuf[slot] = total
        cp = _remote(sbuf.at[slot], rbuf.at[slot], send_sem.at[slot], recv_sem.at[slot], (x, y, 1 - c))
        cp.start()
        o_ref[0, c] = total
        cp.wait_recv()
        o_ref[0, 1 - c] = rbuf[slot]
        cp.wait_send()

    landed_spec = lambda q: pl.BlockSpec((None, tr, cw), lambda i, me: (q, i, 0))
    args = [me1, half, landed, landed, landed]
    in_specs = [pl.BlockSpec((None, tr, cw), lambda i, me: (me[0], i, 0))] + [landed_spec(q) for q in range(N_CHIPS - 1)]
    out = pl.pallas_call(
        body, name=name,
        grid_spec=pltpu.PrefetchScalarGridSpec(
            num_scalar_prefetch=1, grid=(nrb,), in_specs=in_specs,
            out_specs=pl.BlockSpec((1, 2, tr, cw), lambda i, me: (0, 0, i, 0)),
            scratch_shapes=[pltpu.VMEM((SIBLING_SLOTS, tr, cw), F32), pltpu.VMEM((SIBLING_SLOTS, tr, cw), F32),
                            pltpu.SemaphoreType.DMA((SIBLING_SLOTS,)), pltpu.SemaphoreType.DMA((SIBLING_SLOTS,))]),
        out_shape=jax.ShapeDtypeStruct((1, 2, h0, cw), F32),
        compiler_params=_params("arbitrary"),
    )(*args)
    return out.reshape(1, 2 * h0, cw)


WEIGHTS = ["norm_mix_g", "norm_ffn_g", "pool_w", "pool_b", "pool_scale", "sb_w_qkv", "sb_q_gain", "sb_k_gain",
           "sb_w_o", "ssm_lam_re", "ssm_lam_im", "ssm_log_step", "ssm_b_re", "ssm_b_im", "ssm_c_re", "ssm_c_im",
           "ssm_d", "ssm_w_glu", "ssm_b_glu", "ffn_w_up", "ffn_conv_w", "ffn_conv_b", "ffn_w_down"]
BIG = {"pool_w": 0, "sb_w_qkv": 1, "sb_w_o": 0, "ssm_w_glu": 1, "ffn_w_up": 1, "ffn_w_down": 0}
SMALL_SHARDED = {"pool_b": 1, "pool_scale": 1, "ssm_d": 1, "ssm_b_glu": 1, "ffn_conv_w": 2}
SMALL = [n for n in WEIGHTS if n not in BIG]
SMALL_PAD = N_DEV * SUBLANES * LANES
N_MIXERS = 3


def _pack(arrays):
    flat = jnp.concatenate([a.reshape(-1).astype(F32) for a in arrays])
    total = -(-flat.shape[0] // SMALL_PAD) * SMALL_PAD
    flat = jnp.pad(flat, (0, total - flat.shape[0]))
    return flat.reshape(N_DEV, -1, LANES)


def _unpack(packed, like):
    flat = packed.reshape(-1)
    out, off = [], 0
    for a in like:
        out.append(flat[off:off + a.size].reshape(a.shape))
        off += a.size
    return out


def kernel(x, norm_mix_g, norm_ffn_g, pool_w, pool_b, pool_scale, sb_w_qkv, sb_q_gain, sb_k_gain, sb_w_o, ssm_lam_re, ssm_lam_im, ssm_log_step, ssm_b_re, ssm_b_im, ssm_c_re, ssm_c_im, ssm_d, ssm_w_glu, ssm_b_glu, ffn_w_up, ffn_conv_w, ffn_conv_b, ffn_w_down, loss_target, m_norm_mix_g, m_norm_ffn_g, m_pool_w, m_pool_b, m_pool_scale, m_sb_w_qkv, m_sb_q_gain, m_sb_k_gain, m_sb_w_o, m_ssm_lam_re, m_ssm_lam_im, m_ssm_log_step, m_ssm_b_re, m_ssm_b_im, m_ssm_c_re, m_ssm_c_im, m_ssm_d, m_ssm_w_glu, m_ssm_b_glu, m_ffn_w_up, m_ffn_conv_w, m_ffn_conv_b, m_ffn_w_down, v_norm_mix_g, v_norm_ffn_g, v_pool_w, v_pool_b, v_pool_scale, v_sb_w_qkv, v_sb_q_gain, v_sb_k_gain, v_sb_w_o, v_ssm_lam_re, v_ssm_lam_im, v_ssm_log_step, v_ssm_b_re, v_ssm_b_im, v_ssm_c_re, v_ssm_c_im, v_ssm_d, v_ssm_w_glu, v_ssm_b_glu, v_ffn_w_up, v_ffn_conv_w, v_ffn_conv_b, v_ffn_w_down):
    given = dict(locals())
    w = {n: given[n] for n in WEIGHTS}
    mom = {n: given["m_" + n] for n in WEIGHTS}
    var = {n: given["v_" + n] for n in WEIGHTS}
    pool_shape = pool_w.shape
    for group in (w, mom, var):
        group["pool_w"] = group["pool_w"].reshape(pool_shape[0], pool_shape[1] * pool_shape[2], pool_shape[3])
    xi, yi, ci = _place()
    me = 2 * xi + yi
    depth = norm_mix_g.shape[0]
    x_in = x[0]
    t, d = x_in.shape

    def placed(a, ax):
        shp = list(a.shape)
        shp[ax] *= N_CHIPS
        full = lax.dynamic_update_slice_in_dim(jnp.zeros(shp, F32), a, me * a.shape[ax], ax)
        return jnp.where(ci == 0, full, 0.0)

    sharded_full = [placed(w[n], ax) for n, ax in SMALL_SHARDED.items()]

    big = list(BIG)
    me1 = _me_scalar()
    shard = {n: tuple(w[n].shape[1:]) for n in big}
    vec = lambda a, i: a[i:i + 1]
    tie = lambda v, token: v + token[0, 0]

    def layer_weights(i):
        kind, j = i % N_MIXERS, i // N_MIXERS
        mixer = {0: [("pool_w", j)], 1: [("sb_w_qkv", j), ("sb_w_o", j)], 2: [("ssm_w_glu", j)]}[kind]
        return mixer + [("ffn_w_up", i), ("ffn_w_down", i)]

    def pool_matrices(j):
        return gathered["pool_w", j].reshape((1, N_CHIPS) + tuple(pool_shape[1:]))

    started, token = [], x_in
    for i in range(depth):
        keys = layer_weights(i)
        bufs = [cast_into_gathered(w[n], l, BIG[n], me1, f"cast_{n}{l}") for n, l in keys]
        send_sem, recv_sem, bufs, token = gather_start(bufs, [shard[n] for n, _ in keys], [BIG[n] for n, _ in keys],
                                                       token, f"gather_start{i}")
        started.append((keys, send_sem, recv_sem, bufs))
        if i == 0:
            token = small_allreduce(tie(_pack(sharded_full), token), "gather_vectors")
            whole = dict(zip(SMALL_SHARDED, _unpack(token, sharded_full)))
    gathered = {}

    saved = []
    xc = x_in
    for i in range(depth):
        kind, j = i % N_MIXERS, i // N_MIXERS
        keys, send_sem, recv_sem, bufs = started[i]
        bufs = gather_wait(send_sem, recv_sem, bufs, [shard[n] for n, _ in keys], [BIG[n] for n, _ in keys],
                           token if i == 0 else xc, f"gather_wait{i}")
        for (n, l), b in zip(keys, bufs):
            gathered[n, l] = gather_forward(b, BIG[n], *shard[n], me1, f"gather_forward_{n}{l}")
        s = {"x_in": xc}
        g_mix = vec(norm_mix_g, i)
        if kind == 0:
            (h,) = rmsnorm_fwd(xc, g_mix, [F32], f"norm_mix{i}")
            x_mid = pool_fwd(h, xc, pool_matrices(j), vec(whole["pool_b"], j), vec(whole["pool_scale"], j),
                             f"pool_fwd{i}", lead=0)
        elif kind == 1:
            (h,) = rmsnorm_fwd(xc, g_mix, [BF16], f"norm_mix{i}")
            s["qkv"] = mm_cols(h, gathered["sb_w_qkv", j], out_dtype=BF16, name=f"sb_qkv{i}", lead=0)
            s["o"], s["ltot"] = sb_attn_fwd(s["qkv"], vec(sb_q_gain, j), vec(sb_k_gain, j), f"sb_attn_fwd{i}")
            x_mid = mm_cols(s["o"], gathered["sb_w_o", j], out_dtype=F32, name=f"sb_out{i}", resid=xc, lead=0)
        else:
            (h,) = rmsnorm_fwd(xc, g_mix, [F32], f"norm_mix{i}")
            prm = tuple(w[n][j] for n in ("ssm_lam_re", "ssm_lam_im", "ssm_log_step", "ssm_b_re", "ssm_b_im",
                                          "ssm_c_re", "ssm_c_im"))
            prep, s["prep_vjp"] = jax.vjp(ssm_prepare, *prm)
            s["prep"] = tuple(a.astype(BF16) for a in prep[:4]) + tuple(prep[4:])
            s["y"], s["yg"] = ssm_core_fwd(h, *s["prep"], vec(whole["ssm_d"], j), f"ssm_fwd{i}")
            s["val"], s["gate"], x_mid = glu_fwd(s["yg"], gathered["ssm_w_glu", j], vec(whole["ssm_b_glu"], j), xc,
                                                 f"ssm_glu{i}", lead=0)
        s["x_mid"], s["h"] = x_mid, h
        (h2,) = rmsnorm_fwd(x_mid, vec(norm_ffn_g, i), [BF16], f"norm_ffn{i}")
        s["h2"] = h2
        s["up_val"], s["up_gate"], s["act"] = ffn_up_fused(h2, gathered["ffn_w_up", i], whole["ffn_conv_w"][i],
                                                           vec(ffn_conv_b, i), f"ffn_up{i}", lead=0)
        xc = mm_k([s["act"]], gathered["ffn_w_down", i], b_nt=False, name=f"ffn_down{i}", resid=x_mid, lead=0)
        saved.append(s)

    dx, dxb, loss_part = loss_head(xc, loss_target[0], "loss_head")
    loss = lax.psum(loss_part[0, 0], ("x", "y", "c"))

    small = {n: [None] * w[n].shape[0] for n in SMALL}
    big_g, updated = {}, {}

    def finish_reduction(pending, after):
        layer, keys, send_sem, recv_sem, halves, landed, _ = pending
        halves, landed = chip_wait(send_sem, recv_sem, halves, landed, after, f"grads_chip_wait{layer}")
        for (n, l), h, lb in zip(keys, halves, landed):
            g = reduce_share(h, lb, me1, f"grads_share_{n}{l}")
            updated[n] = adamw_layer(w[n], g, mom[n], var[n], l, f"adamw_{n}{l}", into=updated.get(n))

    pending = None
    for i in reversed(range(depth)):
        kind, j = i % N_MIXERS, i // N_MIXERS
        s = saved[i]
        g_ffn, g_mix = vec(norm_ffn_g, i), vec(norm_mix_g, i)
        cw, cb = whole["ffn_conv_w"][i], vec(ffn_conv_b, i)
        if pending is not None:
            cb = tie(cb, pending[-1])
        h, h2 = s["h"], s["h2"]
        dupv, dupg, dcwv, dcwg, dcbv, dcbg = ffn_bwd_fused(dxb, gathered["ffn_w_down", i], s["up_val"], s["up_gate"], cw, cb,
                                                           f"ffn_bwd{i}", lead=0)
        big_g["ffn_w_down", i] = [mm_rows(s["act"], dxb, out_dtype=BF16, name=f"ffn_dwdown{i}")]
        big_g["ffn_w_up", i] = [mm_cols(h2, dupv, a_contract=0, out_dtype=BF16, name=f"ffn_dwup_val{i}"),
                                mm_cols(h2, dupg, a_contract=0, out_dtype=BF16, name=f"ffn_dwup_gate{i}")]
        dh2 = mm_k([dupv, dupg], gathered["ffn_w_up", i], b_nt=True, name=f"ffn_dh{i}", lead=0)
        dx_mid, dxb_mid, small["norm_ffn_g"][i] = rmsnorm_bwd(s["x_mid"], g_ffn, dh2, dx, f"norm_ffn_bwd{i}")
        small["ffn_conv_w"][i] = jnp.concatenate([dcwv, dcwg], axis=1)[None]
        small["ffn_conv_b"][i] = jnp.concatenate([dcbv, dcbg], axis=1)

        if kind == 0:
            dh, dwp, small["pool_b"][j], small["pool_scale"][j] = pool_bwd(
                dx_mid, h, pool_matrices(j), vec(whole["pool_b"], j), vec(whole["pool_scale"], j), f"pool_bwd{i}", lead=0)
            big_g["pool_w", j] = [dwp.reshape(-1, dwp.shape[-1])]
        elif kind == 1:
            do = mm_cols(dxb_mid, gathered["sb_w_o", j], b_nt=True, out_dtype=BF16, name=f"sb_do{i}", lead=0)
            big_g["sb_w_o", j] = [mm_cols(s["o"], dxb_mid, a_contract=0, out_dtype=BF16, name=f"sb_dwo{i}")]
            dq, dk, dv, small["sb_q_gain"][j], small["sb_k_gain"][j] = sb_attn_bwd(
                s["qkv"], s["ltot"], do, vec(sb_q_gain, j), vec(sb_k_gain, j), f"sb_attn_bwd{i}")
            dqkv = jnp.concatenate([dq, dk, dv], axis=1)
            big_g["sb_w_qkv", j] = [mm_cols(h, dqkv, a_contract=0, out_dtype=BF16, name=f"sb_dwqkv{i}")]
            dh = mm_k([dqkv], gathered["sb_w_qkv", j], b_nt=True, name=f"sb_dh{i}", lead=0)
        else:
            dval, dgate, dbv, dbg = glu_bwd(dx_mid, s["val"], s["gate"], f"ssm_glu_bwd{i}")
            small["ssm_b_glu"][j] = jnp.concatenate([dbv, dbg], axis=1)
            big_g["ssm_w_glu", j] = [mm_cols(s["yg"], dval, a_contract=0, out_dtype=BF16, name=f"ssm_dwglu_val{i}"),
                                     mm_cols(s["yg"], dgate, a_contract=0, out_dtype=BF16, name=f"ssm_dwglu_gate{i}")]
            dyg = mm_k([dval, dgate], gathered["ssm_w_glu", j], b_nt=True, name=f"ssm_dyg{i}", lead=0)
            dh, small["ssm_d"][j], *dprep = ssm_core_bwd(h, s["y"], dyg, *s["prep"], vec(whole["ssm_d"], j), f"ssm_bwd{i}")
            dprm = s["prep_vjp"](tuple(dprep))
            for n, g in zip(("ssm_lam_re", "ssm_lam_im", "ssm_log_step", "ssm_b_re", "ssm_b_im", "ssm_c_re", "ssm_c_im"),
                            dprm):
                small[n][j] = g[None]
        dx, dxb, small["norm_mix_g"][i] = rmsnorm_bwd(s["x_in"], g_mix, dh, dx_mid, f"norm_mix_bwd{i}")

        keys, halves = layer_weights(i), []
        for n, l in keys:
            h = None
            for a, g in enumerate(big_g[n, l]):
                h = pair_reduce(g, BIG[n], *shard[n], f"grads_pair_{n}{l}_{a}", into=h,
                                first_slot=a * (N_CHIPS // len(big_g[n, l])))
            halves.append(h)
        started_copies = chip_start(halves, f"grads_chip_start{i}")
        if pending is not None:
            finish_reduction(pending, dx)
        pending = (i, keys) + tuple(started_copies)

    small_full = [jnp.concatenate(small[n], axis=0) for n in SMALL]
    small_sum = dict(zip(SMALL, _unpack(small_allreduce(_pack(small_full), "reduce_vectors"), small_full)))
    grads = {}
    for n in SMALL:
        g = small_sum[n]
        if n in SMALL_SHARDED:
            ax = SMALL_SHARDED[n]
            g = lax.dynamic_slice_in_dim(g, me * w[n].shape[ax], w[n].shape[ax], ax)
        grads[n] = g
    delta, new_m, new_v = {}, {}, {}
    like = [w[n] for n in SMALL]
    packed = [_pack([src[n] for n in SMALL]).reshape(-1, LANES) for src in (w, grads, mom, var)]
    vector_updates = adamw(*packed, "adamw_vectors")
    for dst, out in zip((delta, new_m, new_v), vector_updates):
        dst.update(zip(SMALL, _unpack(out, like)))

    finish_reduction(pending, vector_updates[0])
    for n in big:
        grads[n], delta[n], new_m[n], new_v[n] = updated[n]
    for group in (grads, delta, new_m, new_v):
        group["pool_w"] = group["pool_w"].reshape(pool_shape)

    return (loss, dx[None], *[grads[n] for n in WEIGHTS], *[delta[n] for n in WEIGHTS],
            *[new_m[n] for n in WEIGHTS], *[new_v[n] for n in WEIGHTS])
```

```python
import functools
import math

import jax
import jax.numpy as jnp
from jax import lax
from jax.experimental import pallas as pl
from jax.experimental.pallas import tpu as pltpu

F32 = jnp.float32
BF16 = jnp.bfloat16

RMS_EPS = 1e-6
POOL_WINDOWS = (2, 4, 8, 16)
SB_HEAD_DIM = 128
SSM_GROUP_CH = 16
SSM_STATE = 64
SSM_BLOCK_GROUPS = 8
ADAM_LR = 0.001
ADAM_B1 = 0.9
ADAM_B2 = 0.999
ADAM_EPS = 1e-08
ADAM_WD = 0.01
ADAM_STEP = 10

V7X_VMEM_BYTES = 64 * 1024 * 1024
VMEM_LIMIT = V7X_VMEM_BYTES - 8 * 1024 * 1024
SUBLANES = 8
LANES = 128
NORM_ROWS = 256
MESH = pl.DeviceIdType.MESH
N_CHIPS = 4
N_DEV = 8


def _params(*sem):
    return pltpu.CompilerParams(dimension_semantics=tuple(sem) if sem else None, vmem_limit_bytes=VMEM_LIMIT)


def _tile(n, want):
    if n <= want:
        return n
    t = (want // LANES) * LANES
    while t > LANES and n % t:
        t -= LANES
    assert n % t == 0, (n, want)
    return t


def _spec(shape, imap, lead=None):
    if lead is None:
        return pl.BlockSpec(tuple(shape), imap)
    return pl.BlockSpec((None,) + tuple(shape), lambda *a: (lead,) + tuple(imap(*a)))


def _sigmoid(v):
    return 1.0 / (1.0 + jnp.exp(-v))


def _shift_down(v, k):
    return pltpu.roll(v, k, 0)


def _shift_up(v, k):
    return pltpu.roll(v, v.shape[0] - k, 0)


def rmsnorm_fwd(x, g, out_dtypes, name):
    t, d = x.shape
    tr = min(t, NORM_ROWS)

    def body(x_ref, g_ref, *o_refs):
        xv = x_ref[...]
        r = lax.rsqrt(jnp.mean(xv * xv, axis=-1, keepdims=True) + RMS_EPS)
        h = xv * r * g_ref[...]
        for o in o_refs:
            o[...] = h.astype(o.dtype)

    outs = pl.pallas_call(
        body, name=name, grid=(t // tr,),
        in_specs=[pl.BlockSpec((tr, d), lambda i: (i, 0)), pl.BlockSpec((1, d), lambda i: (0, 0))],
        out_specs=[pl.BlockSpec((tr, d), lambda i: (i, 0)) for _ in out_dtypes],
        out_shape=[jax.ShapeDtypeStruct((t, d), dt) for dt in out_dtypes],
        compiler_params=_params("parallel"),
    )(x, g)
    return outs


def rmsnorm_bwd(x, g, dh, dres, name):
    t, d = x.shape
    tr = min(t, NORM_ROWS)

    def body(x_ref, g_ref, dh_ref, dres_ref, dx_ref, dxb_ref, dg_ref):
        xv = x_ref[...]
        r = lax.rsqrt(jnp.mean(xv * xv, axis=-1, keepdims=True) + RMS_EPS)
        xhat = xv * r
        dhv = dh_ref[...]
        dxhat = dhv * g_ref[...]
        dx = dres_ref[...] + r * (dxhat - xhat * jnp.mean(dxhat * xhat, axis=-1, keepdims=True))
        dx_ref[...] = dx
        dxb_ref[...] = dx.astype(BF16)
        part = jnp.sum(dhv * xhat, axis=0, keepdims=True)

        @pl.when(pl.program_id(0) == 0)
        def _():
            dg_ref[...] = part

        @pl.when(pl.program_id(0) != 0)
        def _():
            dg_ref[...] += part

    row = pl.BlockSpec((tr, d), lambda i: (i, 0))
    vec = pl.BlockSpec((1, d), lambda i: (0, 0))
    return pl.pallas_call(
        body, name=name, grid=(t // tr,),
        in_specs=[row, vec, row, row],
        out_specs=[row, row, vec],
        out_shape=[jax.ShapeDtypeStruct((t, d), F32), jax.ShapeDtypeStruct((t, d), BF16),
                   jax.ShapeDtypeStruct((1, d), F32)],
        compiler_params=_params("arbitrary"),
    )(x, g, dh, dres)


def mm_cols(a, b, *, a_contract=1, b_nt=False, out_dtype, name, resid=None, tn=512, lead=None):
    m = a.shape[1 - a_contract]
    k = a.shape[a_contract]
    n = b.shape[-2] if b_nt else b.shape[-1]
    assert (b.shape[-1] if b_nt else b.shape[-2]) == k
    tn = _tile(n, tn)

    def body(a_ref, b_ref, *rest):
        o_ref = rest[-1]
        dn = (((a_contract,), (1 if b_nt else 0,)), ((), ()))
        acc = lax.dot_general(a_ref[...], b_ref[...], dn, preferred_element_type=F32)
        if resid is not None:
            acc = acc + rest[0][...]
        o_ref[...] = acc.astype(o_ref.dtype)

    in_specs = [pl.BlockSpec(a.shape, lambda j: (0, 0)),
                _spec((tn, k), lambda j: (j, 0), lead) if b_nt else _spec((k, tn), lambda j: (0, j), lead)]
    args = [a, b]
    if resid is not None:
        in_specs.append(pl.BlockSpec((m, tn), lambda j: (0, j)))
        args.append(resid)
    return pl.pallas_call(
        body, name=name, grid=(n // tn,), in_specs=in_specs,
        out_specs=pl.BlockSpec((m, tn), lambda j: (0, j)),
        out_shape=jax.ShapeDtypeStruct((m, n), out_dtype),
        compiler_params=_params("parallel"),
    )(*args)


def mm_rows(st, res, *, out_dtype, name, tn=512):
    k, m = st.shape
    n = res.shape[1]
    assert res.shape[0] == k
    tn = _tile(m, tn)

    def body(st_ref, res_ref, o_ref):
        o_ref[...] = lax.dot_general(st_ref[...], res_ref[...], (((0,), (0,)), ((), ())),
                                     preferred_element_type=F32).astype(o_ref.dtype)

    return pl.pallas_call(
        body, name=name, grid=(m // tn,),
        in_specs=[pl.BlockSpec((k, tn), lambda j: (0, j)), pl.BlockSpec((k, n), lambda j: (0, 0))],
        out_specs=pl.BlockSpec((tn, n), lambda j: (j, 0)),
        out_shape=jax.ShapeDtypeStruct((m, n), out_dtype),
        compiler_params=_params("parallel"),
    )(st, res)


def mm_k(a_list, b, *, b_nt, name, resid=None, tk=512, tnn=1024, lead=None):
    m = a_list[0].shape[0]
    ks = [a.shape[1] for a in a_list]
    ktot = sum(ks)
    n = b.shape[-2] if b_nt else b.shape[-1]
    assert (b.shape[-1] if b_nt else b.shape[-2]) == ktot
    tk = _tile(ks[0], tk)
    assert all(kk % tk == 0 for kk in ks)
    tnn = _tile(n, tnn)
    nks = [kk // tk for kk in ks]
    starts = [sum(nks[:i]) for i in range(len(nks))]
    nk = sum(nks)

    def body(*refs):
        a_refs = refs[:len(a_list)]
        b_ref = refs[len(a_list)]
        o_ref = refs[-1]
        kk = pl.program_id(1)

        @pl.when(kk == 0)
        def _():
            if resid is not None:
                o_ref[...] = refs[len(a_list) + 1][...]
            else:
                o_ref[...] = jnp.zeros_like(o_ref)

        dn = (((1,), (1 if b_nt else 0,)), ((), ()))
        for i, a_ref in enumerate(a_refs):
            @pl.when(jnp.logical_and(kk >= starts[i], kk < starts[i] + nks[i]))
            def _(a_ref=a_ref):
                o_ref[...] += lax.dot_general(a_ref[...], b_ref[...], dn, preferred_element_type=F32)

    def a_spec(i):
        return pl.BlockSpec((m, tk), lambda nn, kk: (0, jnp.clip(kk - starts[i], 0, nks[i] - 1)))

    in_specs = [a_spec(i) for i in range(len(a_list))]
    in_specs.append(_spec((tnn, tk), lambda nn, kk: (nn, kk), lead) if b_nt
                    else _spec((tk, tnn), lambda nn, kk: (kk, nn), lead))
    args = list(a_list) + [b]
    if resid is not None:
        in_specs.append(pl.BlockSpec((m, tnn), lambda nn, kk: (0, nn)))
        args.append(resid)
    return pl.pallas_call(
        body, name=name, grid=(n // tnn, nk), in_specs=in_specs,
        out_specs=pl.BlockSpec((m, tnn), lambda nn, kk: (0, nn)),
        out_shape=jax.ShapeDtypeStruct((m, n), F32),
        compiler_params=_params("parallel", "arbitrary"),
    )(*args)


HALO = SUBLANES
CHUNK_ROWS = 64


def _conv_taps(ext, r):
    return ext[HALO:], _shift_down(ext, 1)[HALO:], _shift_down(ext, 2)[HALO:]


def ffn_up_fused(h, w_up, conv_w, conv_b, name, lead=None):
    t, d = h.shape
    f = w_up.shape[-1] // 2
    tn = _tile(f, 256)
    nf = f // tn
    r = min(CHUNK_ROWS, t)

    def body(h_ref, wv_ref, wg_ref, cwv_ref, cwg_ref, cbv_ref, cbg_ref, uv_ref, ug_ref, act_ref, sv, sg):
        zero = jnp.zeros((HALO, tn), F32)
        sv[0:HALO, :] = zero
        sg[0:HALO, :] = zero
        hv = h_ref[...]
        sv[HALO:, :] = jnp.dot(hv, wv_ref[...], preferred_element_type=F32).astype(BF16).astype(F32)
        sg[HALO:, :] = jnp.dot(hv, wg_ref[...], preferred_element_type=F32).astype(BF16).astype(F32)
        cwv, cwg = cwv_ref[...], cwg_ref[...]
        cbv, cbg = cbv_ref[...], cbg_ref[...]

        def chunk(i, carry):
            r0 = pl.multiple_of(i * r, r)
            v0, v1, v2 = _conv_taps(sv[pl.ds(r0, r + HALO), :], r)
            g0, g1, g2 = _conv_taps(sg[pl.ds(r0, r + HALO), :], r)
            cval = cbv + cwv[2:3] * v0 + cwv[1:2] * v1 + cwv[0:1] * v2
            cgate = cbg + cwg[2:3] * g0 + cwg[1:2] * g1 + cwg[0:1] * g2
            uv_ref[pl.ds(r0, r), :] = v0.astype(BF16)
            ug_ref[pl.ds(r0, r), :] = g0.astype(BF16)
            act_ref[pl.ds(r0, r), :] = (cgate * _sigmoid(cgate) * cval).astype(BF16)
            return carry

        lax.fori_loop(0, t // r, chunk, 0)

    col = lambda off: _spec((d, tn), lambda j: (0, j + off), lead)
    cw = lambda off: pl.BlockSpec((3, tn), lambda j: (0, j + off))
    cb = lambda off: pl.BlockSpec((1, tn), lambda j: (0, j + off))
    out = pl.BlockSpec((t, tn), lambda j: (0, j))
    return pl.pallas_call(
        body, name=name, grid=(nf,),
        in_specs=[pl.BlockSpec((t, d), lambda j: (0, 0)), col(0), col(nf), cw(0), cw(nf), cb(0), cb(nf)],
        out_specs=[out, out, out],
        out_shape=[jax.ShapeDtypeStruct((t, f), BF16)] * 3,
        scratch_shapes=[pltpu.VMEM((t + HALO, tn), F32), pltpu.VMEM((t + HALO, tn), F32)],
        compiler_params=_params("parallel"),
    )(h, w_up, w_up, conv_w, conv_w, conv_b, conv_b)


def ffn_bwd_fused(dout, w_down, up_val, up_gate, conv_w, conv_b, name, lead=None):
    t, d = dout.shape
    f = w_down.shape[-2]
    tn = _tile(f, 256)
    nf = f // tn
    r = min(CHUNK_ROWS, t)

    def body(do_ref, wd_ref, uv_ref, ug_ref, cwv_ref, cwg_ref, cbv_ref, cbg_ref,
             dv_ref, dg_ref, dcwv_ref, dcwg_ref, dcbv_ref, dcbg_ref, da, sv, sg, ev, eg):
        zero = jnp.zeros((HALO, tn), F32)
        sv[0:HALO, :] = zero
        sg[0:HALO, :] = zero
        ev[t:, :] = zero
        eg[t:, :] = zero
        da[...] = lax.dot_general(do_ref[...], wd_ref[...], (((1,), (1,)), ((), ())), preferred_element_type=F32)
        sv[HALO:, :] = uv_ref[...].astype(F32)
        sg[HALO:, :] = ug_ref[...].astype(F32)
        cwv, cwg = cwv_ref[...], cwg_ref[...]
        cbv, cbg = cbv_ref[...], cbg_ref[...]

        def chunk(i, acc):
            r0 = pl.multiple_of(i * r, r)
            v = _conv_taps(sv[pl.ds(r0, r + HALO), :], r)
            g = _conv_taps(sg[pl.ds(r0, r + HALO), :], r)
            cval = cbv + cwv[2:3] * v[0] + cwv[1:2] * v[1] + cwv[0:1] * v[2]
            cgate = cbg + cwg[2:3] * g[0] + cwg[1:2] * g[1] + cwg[0:1] * g[2]
            s = _sigmoid(cgate)
            dav = da[pl.ds(r0, r), :]
            dval = dav * (cgate * s)
            dgate = dav * cval * (s * (1.0 + cgate * (1.0 - s)))
            ev[pl.ds(r0, r), :] = dval
            eg[pl.ds(r0, r), :] = dgate
            col = lambda z: jnp.sum(z, axis=0, keepdims=True)
            new = [acc[0] + col(dval), acc[1] + col(dgate)]
            new += [acc[2 + j] + col(dval * v[2 - j]) for j in range(3)]
            new += [acc[5 + j] + col(dgate * g[2 - j]) for j in range(3)]
            return tuple(new)

        z1 = jnp.zeros((1, tn), F32)
        acc = lax.fori_loop(0, t // r, chunk, (z1,) * 8)
        dcbv_ref[...] = acc[0]
        dcbg_ref[...] = acc[1]
        for j in range(3):
            dcwv_ref[j:j + 1, :] = acc[2 + j]
            dcwg_ref[j:j + 1, :] = acc[5 + j]

        def chunk2(i, carry):
            r0 = pl.multiple_of(i * r, r)
            for e_ref, cw_, o_ref in ((ev, cwv, dv_ref), (eg, cwg, dg_ref)):
                ext = e_ref[pl.ds(r0, r + HALO), :]
                d0, d1, d2 = ext[:r], _shift_up(ext, 1)[:r], _shift_up(ext, 2)[:r]
                o_ref[pl.ds(r0, r), :] = (cw_[2:3] * d0 + cw_[1:2] * d1 + cw_[0:1] * d2).astype(BF16)
            return carry

        lax.fori_loop(0, t // r, chunk2, 0)

    cw = lambda off: pl.BlockSpec((3, tn), lambda j: (0, j + off))
    cb = lambda off: pl.BlockSpec((1, tn), lambda j: (0, j + off))
    tile = pl.BlockSpec((t, tn), lambda j: (0, j))
    s = lambda rows, dt: jax.ShapeDtypeStruct((rows, f), dt)
    halo = pltpu.VMEM((t + HALO, tn), F32)
    return pl.pallas_call(
        body, name=name, grid=(nf,),
        in_specs=[pl.BlockSpec((t, d), lambda j: (0, 0)), _spec((tn, d), lambda j: (j, 0), lead),
                  tile, tile, cw(0), cw(nf), cb(0), cb(nf)],
        out_specs=[tile, tile, pl.BlockSpec((3, tn), lambda j: (0, j)), pl.BlockSpec((3, tn), lambda j: (0, j)),
                   pl.BlockSpec((1, tn), lambda j: (0, j)), pl.BlockSpec((1, tn), lambda j: (0, j))],
        out_shape=[s(t, BF16), s(t, BF16), s(3, F32), s(3, F32), s(1, F32), s(1, F32)],
        scratch_shapes=[pltpu.VMEM((t, tn), F32), halo, halo, halo, halo],
        compiler_params=_params("parallel"),
    )(dout, w_down, up_val, up_gate, conv_w, conv_w, conv_b, conv_b)


POOL_PAD = max(POOL_WINDOWS)


def _window_sum(ext, win, shift):
    assert POOL_WINDOWS == (2, 4, 8, 16)
    s2 = ext + shift(ext, 1)
    s4 = s2 + shift(s2, 2)
    s8 = s4 + shift(s4, 4)
    s16 = s8 + shift(s8, 8)
    return jnp.where(win == 2, s2, jnp.where(win == 4, s4, jnp.where(win == 8, s8, s16)))


def _pool_win_scalar(g):
    win = jnp.int32(POOL_WINDOWS[-1])
    for k in range(len(POOL_WINDOWS) - 2, -1, -1):
        win = jnp.where(g == k, jnp.int32(POOL_WINDOWS[k]), win)
    return win


def _pool_count(r0, r, win):
    rows = r0 + lax.broadcasted_iota(jnp.int32, (r, 1), 0)
    return jnp.minimum(rows + 1, win).astype(F32)


def _pooled_into(hp, pooled, h_ref, t, r, win):
    hp[0:POOL_PAD, :] = jnp.zeros((POOL_PAD, hp.shape[1]), F32)
    hp[POOL_PAD:, :] = h_ref[...]

    def chunk(i, carry):
        r0 = pl.multiple_of(i * r, r)
        ext = hp[pl.ds(r0, r + POOL_PAD), :]
        s = _window_sum(ext, win, _shift_down)[POOL_PAD:]
        pooled[pl.ds(r0, r), :] = (s / _pool_count(r0, r, win) - ext[POOL_PAD:]).astype(BF16)
        return carry

    lax.fori_loop(0, t // r, chunk, 0)


def pool_fwd(h, x, w, b, scale, name, lead=None):
    t, d = h.shape
    ng, rows, dg = w.shape[-3:]
    r = min(CHUNK_ROWS, t)

    def body(h_ref, x_ref, w_ref, b_ref, s_ref, o_ref, hp, pooled):
        win = _pool_win_scalar(pl.program_id(0))
        _pooled_into(hp, pooled, h_ref, t, r, win)
        y = jnp.dot(pooled[...], w_ref[...].reshape(dg, dg), preferred_element_type=F32)
        o_ref[...] = x_ref[...] + (y + b_ref[...]) * s_ref[...]

    col = pl.BlockSpec((t, dg), lambda g: (0, g))
    vec = pl.BlockSpec((1, dg), lambda g: (0, g))
    return pl.pallas_call(
        body, name=name, grid=(ng,),
        in_specs=[col, col, _spec((N_CHIPS, None, rows, dg), lambda g: (0, g, 0, 0), lead), vec, vec],
        out_specs=col, out_shape=jax.ShapeDtypeStruct((t, d), F32),
        scratch_shapes=[pltpu.VMEM((t + POOL_PAD, dg), F32), pltpu.VMEM((t, dg), BF16)],
        compiler_params=_params("parallel"),
    )(h, x, w, b, scale)


def pool_bwd(dm, h, w, b, scale, name, lead=None):
    t, d = h.shape
    ng, rows, dg = w.shape[-3:]
    r = min(CHUNK_ROWS, t)

    def body(dm_ref, h_ref, w_ref, b_ref, s_ref, dh_ref, dw_ref, db_ref, ds_ref, hp, pooled, q):
        win = _pool_win_scalar(pl.program_id(0))
        _pooled_into(hp, pooled, h_ref, t, r, win)
        wv = w_ref[...].reshape(dg, dg)
        y = jnp.dot(pooled[...], wv, preferred_element_type=F32)
        dmv = dm_ref[...]
        ds_ref[...] = jnp.sum(dmv * (y + b_ref[...]), axis=0, keepdims=True)
        dy = dmv * s_ref[...]
        db_ref[...] = jnp.sum(dy, axis=0, keepdims=True)
        dyb = dy.astype(BF16)
        dw_ref[...] = lax.dot_general(pooled[...], dyb, (((0,), (0,)), ((), ())),
                                      preferred_element_type=F32).astype(BF16).reshape(N_CHIPS, rows, dg)
        dp = lax.dot_general(dyb, wv, (((1,), (1,)), ((), ())), preferred_element_type=F32)
        q[t:, :] = jnp.zeros((POOL_PAD, dg), F32)
        q[0:t, :] = dp / _pool_count(0, t, win)
        dh_ref[...] = -dp

        def chunk(i, carry):
            r0 = pl.multiple_of(i * r, r)
            ext = q[pl.ds(r0, r + POOL_PAD), :]
            dh_ref[pl.ds(r0, r), :] += _window_sum(ext, win, _shift_up)[:r]
            return carry

        lax.fori_loop(0, t // r, chunk, 0)

    col = pl.BlockSpec((t, dg), lambda g: (0, g))
    vec = pl.BlockSpec((1, dg), lambda g: (0, g))
    wshape = (N_CHIPS, None, rows, dg)
    wmap = lambda g: (0, g, 0, 0)
    return pl.pallas_call(
        body, name=name, grid=(ng,),
        in_specs=[col, col, _spec(wshape, wmap, lead), vec, vec],
        out_specs=[col, pl.BlockSpec(wshape, wmap), vec, vec],
        out_shape=[jax.ShapeDtypeStruct((t, d), F32), jax.ShapeDtypeStruct((N_CHIPS, ng, rows, dg), BF16),
                   jax.ShapeDtypeStruct((1, d), F32), jax.ShapeDtypeStruct((1, d), F32)],
        scratch_shapes=[pltpu.VMEM((t + POOL_PAD, dg), F32), pltpu.VMEM((t, dg), BF16),
                        pltpu.VMEM((t + POOL_PAD, dg), F32)],
        compiler_params=_params("parallel"),
    )(dm, h, w, b, scale)


SB_BLOCK = 256


def _tri_sum(v, tri):
    hi = v.astype(BF16)
    lo = (v - hi.astype(F32)).astype(BF16)
    dot = lambda p: jnp.dot(p, tri, preferred_element_type=F32)
    return dot(hi) + dot(lo)


def _tri(bk, cmp):
    return cmp(lax.broadcasted_iota(jnp.int32, (bk, bk), 0), lax.broadcasted_iota(jnp.int32, (bk, bk), 1)).astype(BF16)


def _sb_logits(qblk, kblk, inv, diagonal):
    bq, bk = qblk.shape[0], kblk.shape[0]
    z = lax.dot_general(qblk, kblk, (((1,), (1,)), ((), ())), preferred_element_type=F32) * inv
    lb = jnp.minimum(z, 0.0) - jnp.log(1.0 + jnp.exp(-jnp.abs(z)))
    if not diagonal:
        return lb, lb - z, None
    mask = lax.broadcasted_iota(jnp.int32, (bq, bk), 1) < lax.broadcasted_iota(jnp.int32, (bq, bk), 0)
    return lb, jnp.where(mask, lb - z, 0.0), mask


def _keep(mask, v):
    return v if mask is None else jnp.where(mask, v, 0.0)


def _head_norm(ref, gain):
    xv = ref[...].astype(F32)
    r = lax.rsqrt(jnp.mean(xv * xv, axis=-1, keepdims=True) + RMS_EPS)
    xhat = xv * r
    return xhat, r, (xhat * gain).astype(BF16)


def sb_attn_fwd(qkv, q_gain, k_gain, name):
    t = qkv.shape[0]
    d = qkv.shape[1] // 3
    dh = SB_HEAD_DIM
    nh = d // dh
    blk = min(SB_BLOCK, t)
    inv = 1.0 / math.sqrt(dh)

    def body(q_ref, k_ref, v_ref, qg_ref, kg_ref, o_ref, lt_ref, qn, kn):
        qn[...] = _head_norm(q_ref, qg_ref[...])[2]
        kn[...] = _head_norm(k_ref, kg_ref[...])[2]
        later = _tri(blk, lambda j, s: j > s)

        def q_loop(qb, carry):
            q0 = pl.multiple_of(qb * blk, blk)
            qblk = qn[pl.ds(q0, blk), :]

            def pair(k0, st, diagonal):
                c, acc = st
                lb, lm, mask = _sb_logits(qblk, kn[pl.ds(k0, blk), :], inv, diagonal)
                a = _keep(mask, jnp.exp(lb + _tri_sum(lm, later) + c))
                acc = acc + jnp.dot(a.astype(BF16), v_ref[pl.ds(k0, blk), :], preferred_element_type=F32)
                return c + jnp.sum(lm, axis=1, keepdims=True), acc

            st = pair(q0, (jnp.zeros((blk, 1), F32), jnp.zeros((blk, dh), F32)), True)
            c, acc = lax.fori_loop(1, qb + 1, lambda i, st: pair(pl.multiple_of((qb - i) * blk, blk), st, False), st)
            o_ref[pl.ds(q0, blk), :] = acc.astype(BF16)
            lt_ref[pl.ds(q0, blk), :] = c
            return carry

        lax.fori_loop(0, t // blk, q_loop, 0)

    head = lambda off: pl.BlockSpec((t, dh), lambda h: (0, h + off))
    gain = pl.BlockSpec((1, dh), lambda h: (0, 0))
    return pl.pallas_call(
        body, name=name, grid=(nh,),
        in_specs=[head(0), head(nh), head(2 * nh), gain, gain],
        out_specs=[head(0), pl.BlockSpec((None, t, 1), lambda h: (h, 0, 0))],
        out_shape=[jax.ShapeDtypeStruct((t, d), BF16), jax.ShapeDtypeStruct((nh, t, 1), F32)],
        scratch_shapes=[pltpu.VMEM((t, dh), BF16), pltpu.VMEM((t, dh), BF16)],
        compiler_params=_params("parallel"),
    )(qkv, qkv, qkv, q_gain, k_gain)


def sb_attn_bwd(qkv, ltot, do, q_gain, k_gain, name):
    t = qkv.shape[0]
    d = qkv.shape[1] // 3
    dh = SB_HEAD_DIM
    nh = d // dh
    blk = min(SB_BLOCK, t)
    inv = 1.0 / math.sqrt(dh)
    tn_dims = (((0,), (0,)), ((), ()))

    def body(q_ref, k_ref, v_ref, lt_ref, do_ref, qg_ref, kg_ref, dq_ref, dk_ref, dv_ref, dqg_ref, dkg_ref,
             qn, kn, dqn, dkn, dvn):
        qg, kg = qg_ref[...], kg_ref[...]
        qhat, rq, qnb = _head_norm(q_ref, qg)
        khat, rk, knb = _head_norm(k_ref, kg)
        qn[...] = qnb
        kn[...] = knb
        dkn[...] = jnp.zeros_like(dkn)
        dvn[...] = jnp.zeros_like(dvn)
        upto = _tri(blk, lambda j, s: j <= s)
        before = _tri(blk, lambda j, s: j < s)

        def q_loop(qb, carry):
            q0 = pl.multiple_of(qb * blk, blk)
            qblk = qn[pl.ds(q0, blk), :]
            doblk = do_ref[pl.ds(q0, blk), :]
            ltv = lt_ref[pl.ds(q0, blk), :]

            def pair(k0, st, diagonal):
                pl_, pg, dq = st
                kblk = kn[pl.ds(k0, blk), :]
                lb, lm, mask = _sb_logits(qblk, kblk, inv, diagonal)
                a = _keep(mask, jnp.exp(lb + (ltv - pl_ - _tri_sum(lm, upto))))
                da = lax.dot_general(doblk, v_ref[pl.ds(k0, blk), :], (((1,), (1,)), ((), ())),
                                     preferred_element_type=F32)
                g = da * a
                g_before = pg + _tri_sum(g, before)
                beta = jnp.exp(lb)
                dz = (_keep(mask, g * (1.0 - beta) - beta * g_before) * inv).astype(BF16)
                dq = dq + jnp.dot(dz, kblk, preferred_element_type=F32)
                dkn[pl.ds(k0, blk), :] += lax.dot_general(dz, qblk, tn_dims, preferred_element_type=F32)
                dvn[pl.ds(k0, blk), :] += lax.dot_general(a.astype(BF16), doblk, tn_dims, preferred_element_type=F32)
                return pl_ + jnp.sum(lm, axis=1, keepdims=True), pg + jnp.sum(g, axis=1, keepdims=True), dq

            z1 = jnp.zeros((blk, 1), F32)
            st = lax.fori_loop(0, qb, lambda kb, st: pair(pl.multiple_of(kb * blk, blk), st, False),
                               (z1, z1, jnp.zeros((blk, dh), F32)))
            dqn[pl.ds(q0, blk), :] = pair(q0, st, True)[2]
            return carry

        lax.fori_loop(0, t // blk, q_loop, 0)

        first = pl.program_id(0) == 0
        for dn, xhat, r, gain, out_ref, dgain_ref in ((dqn, qhat, rq, qg, dq_ref, dqg_ref),
                                                      (dkn, khat, rk, kg, dk_ref, dkg_ref)):
            dnv = dn[...]
            dxhat = dnv * gain
            out_ref[...] = (r * (dxhat - xhat * jnp.mean(dxhat * xhat, axis=-1, keepdims=True))).astype(BF16)
            part = jnp.sum(dnv * xhat, axis=0, keepdims=True)

            @pl.when(first)
            def _(dgain_ref=dgain_ref, part=part):
                dgain_ref[...] = part

            @pl.when(jnp.logical_not(first))
            def _(dgain_ref=dgain_ref, part=part):
                dgain_ref[...] += part

        dv_ref[...] = dvn[...].astype(BF16)

    head = lambda off: pl.BlockSpec((t, dh), lambda h: (0, h + off))
    gain = pl.BlockSpec((1, dh), lambda h: (0, 0))
    big = jax.ShapeDtypeStruct((t, d), BF16)
    small = jax.ShapeDtypeStruct((1, dh), F32)
    return pl.pallas_call(
        body, name=name, grid=(nh,),
        in_specs=[head(0), head(nh), head(2 * nh), pl.BlockSpec((None, t, 1), lambda h: (h, 0, 0)), head(0),
                  gain, gain],
        out_specs=[head(0), head(0), head(0), gain, gain],
        out_shape=[big, big, big, small, small],
        scratch_shapes=[pltpu.VMEM((t, dh), BF16), pltpu.VMEM((t, dh), BF16),
                        pltpu.VMEM((t, dh), F32), pltpu.VMEM((t, dh), F32), pltpu.VMEM((t, dh), F32)],
        compiler_params=_params("arbitrary"),
    )(qkv, qkv, qkv, ltot, do, q_gain, k_gain)


GELU_C = math.sqrt(2.0 / math.pi)
GELU_A = 0.044715
SCAN_ROWS = SUBLANES


def _gelu(y):
    return 0.5 * y * (1.0 + jnp.tanh(GELU_C * (y + GELU_A * y * y * y)))


def _gelu_grad(y):
    th = jnp.tanh(GELU_C * (y + GELU_A * y * y * y))
    return 0.5 * (1.0 + th) + 0.5 * y * (1.0 - th * th) * GELU_C * (1.0 + 3.0 * GELU_A * y * y)


def _powers(ar, ai):
    out = [(ar, ai)]
    for _ in range(SCAN_ROWS - 1):
        pr, pi = out[-1]
        out.append((pr * ar - pi * ai, pr * ai + pi * ar))
    return out


def _rows(vals):
    c = vals[0].shape[1]
    row = lax.broadcasted_iota(jnp.int32, (SCAN_ROWS, c), 0)
    out = jnp.broadcast_to(vals[SCAN_ROWS - 1], (SCAN_ROWS, c))
    for j in range(SCAN_ROWS - 2, -1, -1):
        out = jnp.where(row == j, vals[j], out)
    return out


def _scan_forward(sr, si, off, t, ar, ai):
    c = ar.shape[1]
    p = _powers(ar, ai)
    pwr = _rows([q[0] for q in p])
    pwi = _rows([q[1] for q in p])
    row = lax.broadcasted_iota(jnp.int32, (SCAN_ROWS, c), 0)

    def tile(i, carry):
        cr, ci = carry
        r0 = pl.multiple_of(off + i * SCAN_ROWS, SCAN_ROWS)
        xr = sr[pl.ds(r0, SCAN_ROWS), :]
        xi = si[pl.ds(r0, SCAN_ROWS), :]
        for k in (1, 2, 4):
            pr, pi = p[k - 1]
            shr = jnp.where(row >= k, _shift_down(xr, k), 0.0)
            shi = jnp.where(row >= k, _shift_down(xi, k), 0.0)
            xr, xi = xr + pr * shr - pi * shi, xi + pr * shi + pi * shr
        xr, xi = xr + pwr * cr - pwi * ci, xi + pwr * ci + pwi * cr
        sr[pl.ds(r0, SCAN_ROWS), :] = xr
        si[pl.ds(r0, SCAN_ROWS), :] = xi
        return xr[SCAN_ROWS - 1:SCAN_ROWS], xi[SCAN_ROWS - 1:SCAN_ROWS]

    z = jnp.zeros((1, c), F32)
    lax.fori_loop(0, t // SCAN_ROWS, tile, (z, z))


def _scan_reverse(gr, gi, t, ar, ai, xr_ref, xi_ref):
    c = ar.shape[1]
    p = _powers(ar, ai)
    pwr = _rows([p[SCAN_ROWS - 1 - j][0] for j in range(SCAN_ROWS)])
    pwi = _rows([p[SCAN_ROWS - 1 - j][1] for j in range(SCAN_ROWS)])
    row = lax.broadcasted_iota(jnp.int32, (SCAN_ROWS, c), 0)
    n = t // SCAN_ROWS

    def tile(ii, carry):
        cr, ci, dar, dai = carry
        r0 = pl.multiple_of((n - 1 - ii) * SCAN_ROWS, SCAN_ROWS)
        xr = gr[pl.ds(r0, SCAN_ROWS), :]
        xi = gi[pl.ds(r0, SCAN_ROWS), :]
        for k in (1, 2, 4):
            pr, pi = p[k - 1]
            shr = jnp.where(row < SCAN_ROWS - k, _shift_up(xr, k), 0.0)
            shi = jnp.where(row < SCAN_ROWS - k, _shift_up(xi, k), 0.0)
            xr, xi = xr + pr * shr + pi * shi, xi + pr * shi - pi * shr
        xr, xi = xr + pwr * cr + pwi * ci, xi + pwr * ci - pwi * cr
        gr[pl.ds(r0, SCAN_ROWS), :] = xr
        gi[pl.ds(r0, SCAN_ROWS), :] = xi
        xpr = _shift_down(xr_ref[pl.ds(r0, 2 * SCAN_ROWS), :], 1)[SCAN_ROWS:]
        xpi = _shift_down(xi_ref[pl.ds(r0, 2 * SCAN_ROWS), :], 1)[SCAN_ROWS:]
        return xr[0:1], xi[0:1], dar + xr * xpr + xi * xpi, dai + xi * xpr - xr * xpi

    z = jnp.zeros((1, c), F32)
    z8 = jnp.zeros((SCAN_ROWS, c), F32)
    _, _, dar, dai = lax.fori_loop(0, n, tile, (z, z, z8, z8))
    return jnp.sum(dar, axis=0, keepdims=True), jnp.sum(dai, axis=0, keepdims=True)


def _ssm_specs(t, nb, ch, st):
    col = pl.BlockSpec((t, ch), lambda b: (0, b))
    vec = pl.BlockSpec((1, ch), lambda b: (0, b))
    bspec = pl.BlockSpec((None, ch, st), lambda b: (b, 0, 0))
    cspec = pl.BlockSpec((None, st, ch), lambda b: (b, 0, 0))
    aspec = pl.BlockSpec((None, 1, st), lambda b: (b, 0, 0))
    return col, vec, bspec, cspec, aspec


def ssm_core_fwd(u, bre, bim, cre, cim, a_re, a_im, dskip, name):
    t, d = u.shape
    nb, ch, st = bre.shape

    def body(u_ref, bre_ref, bim_ref, cre_ref, cim_ref, ar_ref, ai_ref, d_ref, y_ref, yg_ref, sr, si):
        uv = u_ref[...]
        ub = uv.astype(BF16)
        sr[...] = jnp.dot(ub, bre_ref[...], preferred_element_type=F32)
        si[...] = jnp.dot(ub, bim_ref[...], preferred_element_type=F32)
        _scan_forward(sr, si, 0, t, ar_ref[...], ai_ref[...])
        y = (jnp.dot(sr[...].astype(BF16), cre_ref[...], preferred_element_type=F32)
             - jnp.dot(si[...].astype(BF16), cim_ref[...], preferred_element_type=F32) + d_ref[...] * uv)
        y_ref[...] = y
        yg_ref[...] = _gelu(y).astype(BF16)

    col, vec, bspec, cspec, aspec = _ssm_specs(t, nb, ch, st)
    return pl.pallas_call(
        body, name=name, grid=(nb,),
        in_specs=[col, bspec, bspec, cspec, cspec, aspec, aspec, vec],
        out_specs=[col, col],
        out_shape=[jax.ShapeDtypeStruct((t, d), F32), jax.ShapeDtypeStruct((t, d), BF16)],
        scratch_shapes=[pltpu.VMEM((t, st), F32), pltpu.VMEM((t, st), F32)],
        compiler_params=_params("parallel"),
    )(u, bre, bim, cre, cim, a_re, a_im, dskip)


def ssm_core_bwd(u, y, dyg, bre, bim, cre, cim, a_re, a_im, dskip, name):
    t, d = u.shape
    nb, ch, st = bre.shape
    tn_dims = (((0,), (0,)), ((), ()))
    nt_dims = (((1,), (1,)), ((), ()))

    def body(u_ref, y_ref, dyg_ref, bre_ref, bim_ref, cre_ref, cim_ref, ar_ref, ai_ref, d_ref,
             du_ref, dd_ref, dbre_ref, dbim_ref, dcre_ref, dcim_ref, dar_ref, dai_ref, xr, xi, gr, gi):
        uv = u_ref[...]
        ub = uv.astype(BF16)
        ar, ai = ar_ref[...], ai_ref[...]
        dy = dyg_ref[...] * _gelu_grad(y_ref[...])
        dd_ref[...] = jnp.sum(dy * uv, axis=0, keepdims=True)
        zero = jnp.zeros((HALO, st), F32)
        xr[0:HALO, :] = zero
        xi[0:HALO, :] = zero
        xr[HALO:, :] = jnp.dot(ub, bre_ref[...], preferred_element_type=F32)
        xi[HALO:, :] = jnp.dot(ub, bim_ref[...], preferred_element_type=F32)
        _scan_forward(xr, xi, HALO, t, ar, ai)
        dyb = dy.astype(BF16)
        dcre_ref[...] = lax.dot_general(xr[HALO:, :].astype(BF16), dyb, tn_dims, preferred_element_type=F32)
        dcim_ref[...] = -lax.dot_general(xi[HALO:, :].astype(BF16), dyb, tn_dims, preferred_element_type=F32)
        gr[...] = lax.dot_general(dyb, cre_ref[...], nt_dims, preferred_element_type=F32)
        gi[...] = -lax.dot_general(dyb, cim_ref[...], nt_dims, preferred_element_type=F32)
        dar, dai = _scan_reverse(gr, gi, t, ar, ai, xr, xi)
        dar_ref[...] = dar
        dai_ref[...] = dai
        grb = gr[...].astype(BF16)
        gib = gi[...].astype(BF16)
        dbre_ref[...] = lax.dot_general(ub, grb, tn_dims, preferred_element_type=F32)
        dbim_ref[...] = lax.dot_general(ub, gib, tn_dims, preferred_element_type=F32)
        du_ref[...] = (d_ref[...] * dy + lax.dot_general(grb, bre_ref[...], nt_dims, preferred_element_type=F32)
                       + lax.dot_general(gib, bim_ref[...], nt_dims, preferred_element_type=F32))

    col, vec, bspec, cspec, aspec = _ssm_specs(t, nb, ch, st)
    sh = jax.ShapeDtypeStruct
    return pl.pallas_call(
        body, name=name, grid=(nb,),
        in_specs=[col, col, col, bspec, bspec, cspec, cspec, aspec, aspec, vec],
        out_specs=[col, vec, bspec, bspec, cspec, cspec, aspec, aspec],
        out_shape=[sh((t, d), F32), sh((1, d), F32), sh((nb, ch, st), F32), sh((nb, ch, st), F32),
                   sh((nb, st, ch), F32), sh((nb, st, ch), F32), sh((nb, 1, st), F32), sh((nb, 1, st), F32)],
        scratch_shapes=[pltpu.VMEM((t + HALO, st), F32), pltpu.VMEM((t + HALO, st), F32),
                        pltpu.VMEM((t, st), F32), pltpu.VMEM((t, st), F32)],
        compiler_params=_params("parallel"),
    )(u, y, dyg, bre, bim, cre, cim, a_re, a_im, dskip)


def glu_fwd(yg, w_glu, b_glu, x, name, lead=None):
    t, d = yg.shape
    tn = _tile(d, 256)
    nd = d // tn

    def body(yg_ref, wv_ref, wg_ref, bv_ref, bg_ref, x_ref, val_ref, gate_ref, o_ref):
        ygv = yg_ref[...]
        vb = (jnp.dot(ygv, wv_ref[...], preferred_element_type=F32) + bv_ref[...]).astype(BF16)
        gb = (jnp.dot(ygv, wg_ref[...], preferred_element_type=F32) + bg_ref[...]).astype(BF16)
        val_ref[...] = vb
        gate_ref[...] = gb
        o_ref[...] = x_ref[...] + vb.astype(F32) * _sigmoid(gb.astype(F32))

    col = lambda off: _spec((d, tn), lambda j: (0, j + off), lead)
    vec = lambda off: pl.BlockSpec((1, tn), lambda j: (0, j + off))
    tile = pl.BlockSpec((t, tn), lambda j: (0, j))
    return pl.pallas_call(
        body, name=name, grid=(nd,),
        in_specs=[pl.BlockSpec((t, d), lambda j: (0, 0)), col(0), col(nd), vec(0), vec(nd), tile],
        out_specs=[tile, tile, tile],
        out_shape=[jax.ShapeDtypeStruct((t, d), BF16), jax.ShapeDtypeStruct((t, d), BF16),
                   jax.ShapeDtypeStruct((t, d), F32)],
        compiler_params=_params("parallel"),
    )(yg, w_glu, w_glu, b_glu, b_glu, x)


def glu_bwd(dm, val, gate, name):
    t, d = dm.shape
    tn = _tile(d, 256)

    def body(dm_ref, val_ref, gate_ref, dv_ref, dg_ref, dbv_ref, dbg_ref):
        dmv = dm_ref[...]
        s = _sigmoid(gate_ref[...].astype(F32))
        dval = dmv * s
        dgate = dmv * val_ref[...].astype(F32) * s * (1.0 - s)
        dv_ref[...] = dval.astype(BF16)
        dg_ref[...] = dgate.astype(BF16)
        dbv_ref[...] = jnp.sum(dval, axis=0, keepdims=True)
        dbg_ref[...] = jnp.sum(dgate, axis=0, keepdims=True)

    tile = pl.BlockSpec((t, tn), lambda j: (0, j))
    vec = pl.BlockSpec((1, tn), lambda j: (0, j))
    return pl.pallas_call(
        body, name=name, grid=(d // tn,),
        in_specs=[tile, tile, tile], out_specs=[tile, tile, vec, vec],
        out_shape=[jax.ShapeDtypeStruct((t, d), BF16), jax.ShapeDtypeStruct((t, d), BF16),
                   jax.ShapeDtypeStruct((1, d), F32), jax.ShapeDtypeStruct((1, d), F32)],
        compiler_params=_params("parallel"),
    )(dm, val, gate)


def _block_diag(m, gb):
    g, a, b = m.shape
    eye = jnp.eye(gb, dtype=m.dtype)
    return jnp.einsum("ngab,gk->ngakb", m.reshape(g // gb, gb, a, b), eye).reshape(g // gb, gb * a, gb * b)


def ssm_prepare(lam_re, lam_im, log_step, b_re, b_im, c_re, c_im):
    gb = SSM_BLOCK_GROUPS
    g, p = lam_re.shape
    step = jnp.exp(log_step)[:, None]
    mag = jnp.exp(lam_re * step)
    lb_re = mag * jnp.cos(lam_im * step)
    lb_im = mag * jnp.sin(lam_im * step)
    den = lam_re * lam_re + lam_im * lam_im
    f_re = ((lb_re - 1.0) * lam_re + lb_im * lam_im) / den
    f_im = (lb_im * lam_re - (lb_re - 1.0) * lam_im) / den
    bb_re = f_re[..., None] * b_re - f_im[..., None] * b_im
    bb_im = f_re[..., None] * b_im + f_im[..., None] * b_re
    tr = lambda m: jnp.transpose(m, (0, 2, 1))
    return (_block_diag(tr(bb_re), gb), _block_diag(tr(bb_im), gb), _block_diag(tr(c_re), gb), _block_diag(tr(c_im), gb),
            lb_re.reshape(g // gb, 1, gb * p), lb_im.reshape(g // gb, 1, gb * p))


EW_BLOCK_BYTES = 2 * 1024 * 1024
BF16_ROWS = 16


def _row_tile(rows, cols, block_bytes=EW_BLOCK_BYTES):
    limit = max(BF16_ROWS, block_bytes // (cols * 4))
    best = None
    for tr in range(BF16_ROWS, min(rows, limit) + 1, BF16_ROWS):
        if rows % tr == 0:
            best = tr
    return best if best is not None else rows


def _as2d(a):
    return a.reshape(-1, a.shape[-1])


def ew(fn, ins, out_dtypes, name):
    rows, cols = ins[0].shape
    tr = _row_tile(rows, cols)
    n_in = len(ins)

    def body(*refs):
        outs = fn(*[r[...] for r in refs[:n_in]])
        for o_ref, v in zip(refs[n_in:], outs):
            o_ref[...] = v.astype(o_ref.dtype)

    spec = pl.BlockSpec((tr, cols), lambda i: (i, 0))
    return pl.pallas_call(
        body, name=name, grid=(rows // tr,), in_specs=[spec] * n_in, out_specs=[spec] * len(out_dtypes),
        out_shape=[jax.ShapeDtypeStruct((rows, cols), dt) for dt in out_dtypes],
        compiler_params=_params("parallel"),
    )(*ins)


def _adamw(w, g, m, v):
    m = ADAM_B1 * m + (1.0 - ADAM_B1) * g
    v = ADAM_B2 * v + (1.0 - ADAM_B2) * (g * g)
    m_hat = m / (1.0 - ADAM_B1 ** ADAM_STEP)
    v_hat = v / (1.0 - ADAM_B2 ** ADAM_STEP)
    delta = -ADAM_LR * (m_hat / (jnp.sqrt(v_hat) + ADAM_EPS) + ADAM_WD * w)
    return delta, m, v


def adamw(w, g, m, v, name):
    outs = ew(_adamw, [_as2d(w), _as2d(g), _as2d(m), _as2d(v)], [F32, F32, F32], name)
    return [o.reshape(w.shape) for o in outs]


def adamw_layer(w, g, m, v, layer, name, into=None):
    nl, r, cw = w.shape
    tr = _row_tile(r, cw)

    def body(w_ref, g_ref, m_ref, v_ref, *rest):
        g_out, d_out, m_out, v_out = rest[-4:]
        gv = g_ref[...]
        d_out[...], m_out[...], v_out[...] = _adamw(w_ref[...], gv, m_ref[...], v_ref[...])
        g_out[...] = gv

    lay = pl.BlockSpec((None, tr, cw), lambda i: (layer, i, 0))
    args = [w, g, m, v]
    in_specs = [lay, pl.BlockSpec((None, tr, cw), lambda i: (0, i, 0)), lay, lay]
    aliases = {}
    if into is not None:
        args += list(into)
        in_specs += [pl.BlockSpec(memory_space=pl.ANY)] * 4
        aliases = {4 + k: k for k in range(4)}
    return pl.pallas_call(
        body, name=name, grid=(r // tr,), in_specs=in_specs, out_specs=[lay] * 4,
        out_shape=[jax.ShapeDtypeStruct((nl, r, cw), F32)] * 4,
        input_output_aliases=aliases,
        compiler_params=_params("parallel"),
    )(*args)


def loss_head(y, target, name):
    t, d = y.shape
    tr = min(t, NORM_ROWS)
    n = t // tr

    def body(y_ref, t_ref, dy_ref, dyb_ref, loss_ref, acc):
        i = pl.program_id(0)
        err = y_ref[...] - t_ref[...]
        dy = err * (1.0 / d)
        dy_ref[...] = dy
        dyb_ref[...] = dy.astype(BF16)
        part = jnp.sum(err * err, axis=0, keepdims=True)

        @pl.when(i == 0)
        def _():
            acc[...] = part

        @pl.when(i != 0)
        def _():
            acc[...] += part

        @pl.when(i == n - 1)
        def _():
            loss_ref[...] = jnp.full((1, LANES), 0.5 / d, F32) * jnp.sum(acc[...])

    row = pl.BlockSpec((tr, d), lambda i: (i, 0))
    return pl.pallas_call(
        body, name=name, grid=(n,), in_specs=[row, row],
        out_specs=[row, row, pl.BlockSpec((1, LANES), lambda i: (0, 0))],
        out_shape=[jax.ShapeDtypeStruct((t, d), F32), jax.ShapeDtypeStruct((t, d), BF16),
                   jax.ShapeDtypeStruct((1, LANES), F32)],
        scratch_shapes=[pltpu.VMEM((1, d), F32)],
        compiler_params=_params("arbitrary"),
    )(y, target)


HBM_SPEC = pl.BlockSpec(memory_space=pltpu.HBM)
VMEM_SPEC = pl.BlockSpec(memory_space=pltpu.VMEM)


def _place():
    return lax.axis_index("x"), lax.axis_index("y"), lax.axis_index("c")


def _other_chips(x, y):
    return [(1 - x, y), (x, 1 - y), (1 - x, 1 - y)]


def _remote(src, dst, send_sem, recv_sem, dev):
    return pltpu.make_async_remote_copy(src_ref=src, dst_ref=dst, send_sem=send_sem, recv_sem=recv_sem,
                                        device_id=dev, device_id_type=MESH)


def _piece(refs, shard_shape, ax, j, half):
    w = shard_shape[ax]
    a, off = divmod(j * w, refs[0].shape[ax]) if isinstance(j, int) else (0, j * w)
    idx = [pl.ds(0, s) for s in shard_shape]
    idx[ax] = pl.ds(off, w)
    if half is not None:
        h0 = shard_shape[0] // 2
        idx[0] = pl.ds((off if ax == 0 else 0) + half * h0, h0)
    return refs[a].at[tuple(idx)]


def small_allreduce(v, name):
    n, r, l = v.shape
    assert n == N_DEV

    def body(v_ref, o_ref, recv, red, send1, recv1, send2, recv2):
        x, y, c = _place()
        me = 4 * x + 2 * y + c
        dev = lambda k: (k // 4, (k // 2) % 2, k % 2)
        firsts = []
        for o in range(1, N_DEV):
            tgt = (me + o) % N_DEV
            cp = _remote(v_ref.at[tgt], recv.at[me], send1.at[o], recv1.at[me], dev(tgt))
            cp.start()
            firsts.append(cp)
        recv[me] = v_ref[me]
        for o in range(1, N_DEV):
            src = (me + o) % N_DEV
            _remote(v_ref.at[src], recv.at[src], send1.at[o], recv1.at[src], dev(src)).wait_recv()
        acc = recv[0]
        for s in range(1, N_DEV):
            acc = acc + recv[s]
        red[...] = acc
        o_ref[me] = acc
        seconds = []
        for o in range(1, N_DEV):
            tgt = (me + o) % N_DEV
            cp = _remote(red, o_ref.at[me], send2.at[o], recv2.at[me], dev(tgt))
            cp.start()
            seconds.append(cp)
        for o in range(1, N_DEV):
            src = (me + o) % N_DEV
            _remote(red, o_ref.at[src], send2.at[o], recv2.at[src], dev(src)).wait_recv()
        for cp in firsts + seconds:
            cp.wait_send()

    sems = pltpu.SemaphoreType.DMA((N_DEV,))
    return pl.pallas_call(
        body, name=name, in_specs=[VMEM_SPEC], out_specs=VMEM_SPEC,
        out_shape=jax.ShapeDtypeStruct(v.shape, F32),
        scratch_shapes=[pltpu.VMEM((N_DEV, r, l), F32), pltpu.VMEM((r, l), F32), sems, sems, sems, sems],
        compiler_params=pltpu.CompilerParams(vmem_limit_bytes=VMEM_LIMIT),
    )(v)


def _me_scalar():
    return (2 * lax.axis_index("x") + lax.axis_index("y")).astype(jnp.int32).reshape(1)


def cast_into_gathered(wf, layer, axis, me1, name):
    _, r, cw = wf.shape
    tr = _row_tile(r, cw)
    nrb = r // tr
    full = (1, r * N_CHIPS, cw) if axis == 0 else (1, r, cw * N_CHIPS)
    omap = (lambda i, me: (0, me[0] * nrb + i, 0)) if axis == 0 else (lambda i, me: (0, i, me[0]))

    def body(me_ref, w_ref, o_ref):
        o_ref[...] = w_ref[...].astype(BF16)

    return pl.pallas_call(
        body, name=name,
        grid_spec=pltpu.PrefetchScalarGridSpec(
            num_scalar_prefetch=1, grid=(nrb,),
            in_specs=[pl.BlockSpec((None, tr, cw), lambda i, me: (layer, i, 0))],
            out_specs=pl.BlockSpec((None, tr, cw), omap)),
        out_shape=jax.ShapeDtypeStruct(full, BF16),
        compiler_params=_params("parallel"),
    )(me1, wf)


SEM_SPEC = pl.BlockSpec(memory_space=pltpu.SEMAPHORE)
SPLIT_COPY_PARAMS = pltpu.CompilerParams(has_side_effects=pltpu.SideEffectType.DATAFLOW_SIDE_EFFECTING)
TOKEN_SHAPE = (SUBLANES, LANES)


def _hbm(a):
    return pltpu.with_memory_space_constraint(a, pltpu.HBM)


def _gather_copies(refs, shapes, axes, send_sem, recv_sem):
    x, y, c = _place()
    me = 2 * x + y
    out = []
    for p, ref in enumerate(refs):
        place = lambda j: _piece([ref.at[0]], shapes[p], axes[p], j, c)
        for q, chip in enumerate(_other_chips(x, y)):
            dev = (chip[0], chip[1], c)
            sems = (send_sem.at[3 * p + q], recv_sem.at[3 * p + q])
            theirs = place(2 * chip[0] + chip[1])
            out.append((_remote(place(me), place(me), *sems, dev), _remote(theirs, theirs, *sems, dev)))
    return out


def gather_start(bufs, shapes, axes, after, name):
    n = len(bufs)

    def body(*refs):
        send_sem, recv_sem = refs[n + 1:n + 3]
        o_refs = refs[n + 3:2 * n + 3]
        token = refs[-1]
        for mine, _ in _gather_copies(o_refs, shapes, axes, send_sem, recv_sem):
            mine.start()
        token[...] = jnp.zeros(TOKEN_SHAPE, F32)

    sems = pltpu.SemaphoreType.DMA((3 * n,))
    outs = pl.pallas_call(
        body, name=name,
        in_specs=[HBM_SPEC] * n + [pl.BlockSpec(memory_space=pl.ANY)],
        out_specs=[SEM_SPEC, SEM_SPEC] + [HBM_SPEC] * n + [VMEM_SPEC],
        out_shape=[sems, sems] + [pltpu.HBM(b.shape, b.dtype) for b in bufs] + [jax.ShapeDtypeStruct(TOKEN_SHAPE, F32)],
        input_output_aliases={p: p + 2 for p in range(n)},
        compiler_params=SPLIT_COPY_PARAMS,
    )(*[_hbm(b) for b in bufs], after)
    return outs[0], outs[1], list(outs[2:2 + n]), outs[-1]


def gather_wait(send_sem, recv_sem, bufs, shapes, axes, after, name):
    n = len(bufs)

    def body(*refs):
        s_sem, r_sem = refs[n:n + 2]
        o_refs = refs[n + 3:]
        for mine, theirs in _gather_copies(o_refs, shapes, axes, s_sem, r_sem):
            mine.wait_send()
            theirs.wait_recv()

    return pl.pallas_call(
        body, name=name,
        in_specs=[HBM_SPEC] * n + [SEM_SPEC, SEM_SPEC, pl.BlockSpec(memory_space=pl.ANY)],
        out_specs=[HBM_SPEC] * n,
        out_shape=[pltpu.HBM(b.shape, b.dtype) for b in bufs],
        input_output_aliases={p: p for p in range(n)},
        compiler_params=SPLIT_COPY_PARAMS,
    )(*bufs, send_sem, recv_sem, after)


SIBLING_SLOTS = 2
SIBLING_BLOCK_BYTES = 8 * 1024 * 1024
SHARE_BLOCK_BYTES = 4 * 1024 * 1024


def _row_step(nrb):
    s = pl.program_id(0)
    for ax in range(1, len(nrb)):
        s = s * nrb[ax] + pl.program_id(ax)
    return s


def gather_forward(buf, axis, r, cw, me1, name):
    nl = buf.shape[0]
    h0 = r // 2
    tr = _row_tile(h0, cw, SIBLING_BLOCK_BYTES)
    nrb = h0 // tr
    peer = lambda q, me: (me[0] + q + 1) % N_CHIPS
    if axis == 1:
        view = buf.reshape(nl, 1, 2, h0, N_CHIPS * cw)
        spec = pl.BlockSpec((1, 1, 2, tr, cw), lambda l, q, i, me: (l, 0, 0, i, peer(q, me)))
    else:
        view = buf.reshape(nl, N_CHIPS, 2, h0, cw)
        spec = pl.BlockSpec((1, 1, 2, tr, cw), lambda l, q, i, me: (l, peer(q, me), 0, i, 0))

    def body(me_ref, in_ref, o_ref, rbuf, send_sem, recv_sem):
        x, y, c = _place()
        slot = _row_step((nl, N_CHIPS - 1, nrb)) % SIBLING_SLOTS
        cp = _remote(in_ref.at[0, 0, c], rbuf.at[slot], send_sem.at[slot], recv_sem.at[slot], (x, y, 1 - c))
        cp.start()
        o_ref[0, 0, c] = in_ref[0, 0, c]
        cp.wait_recv()
        o_ref[0, 0, 1 - c] = rbuf[slot]
        cp.wait_send()

    out = pl.pallas_call(
        body, name=name,
        grid_spec=pltpu.PrefetchScalarGridSpec(
            num_scalar_prefetch=1, grid=(nl, N_CHIPS - 1, nrb), in_specs=[spec], out_specs=spec,
            scratch_shapes=[pltpu.VMEM((SIBLING_SLOTS, tr, cw), buf.dtype),
                            pltpu.SemaphoreType.DMA((SIBLING_SLOTS,)), pltpu.SemaphoreType.DMA((SIBLING_SLOTS,))]),
        out_shape=jax.ShapeDtypeStruct(view.shape, view.dtype),
        input_output_aliases={1: 0},
        compiler_params=_params("arbitrary", "arbitrary", "arbitrary"),
    )(me1, view)
    return out.reshape(buf.shape)


def pair_reduce(g, axis, r, cw, name, into=None, first_slot=0):
    h0 = r // 2
    tr = _row_tile(h0, cw, SIBLING_BLOCK_BYTES)
    nrb = h0 // tr
    if axis == 1:
        n_sh = g.shape[1] // cw
        view = g.reshape(1, 2, h0, n_sh * cw)
        spec = pl.BlockSpec((1, 2, tr, cw), lambda j, i: (0, 0, i, j))
    else:
        n_sh = g.shape[0] // r
        view = g.reshape(n_sh, 2, h0, cw)
        spec = pl.BlockSpec((1, 2, tr, cw), lambda j, i: (j, 0, i, 0))

    def body(g_ref, *rest):
        o_ref, rbuf, send_sem, recv_sem = rest[-4:]
        x, y, c = _place()
        slot = _row_step((n_sh, nrb)) % SIBLING_SLOTS
        cp = _remote(g_ref.at[0, 1 - c], rbuf.at[slot], send_sem.at[slot], recv_sem.at[slot], (x, y, 1 - c))
        cp.start()
        mine = g_ref[0, c].astype(F32)
        cp.wait_recv()
        o_ref[0] = (mine + rbuf[slot].astype(F32)).astype(BF16)
        cp.wait_send()

    args, in_specs, aliases = [view], [spec], {}
    if into is not None:
        args.append(into)
        in_specs.append(pl.BlockSpec(memory_space=pl.ANY))
        aliases = {1: 0}
    return pl.pallas_call(
        body, name=name, grid=(n_sh, nrb), in_specs=in_specs,
        out_specs=pl.BlockSpec((1, tr, cw), lambda j, i: (j + first_slot, i, 0)),
        out_shape=jax.ShapeDtypeStruct((N_CHIPS, h0, cw), BF16),
        input_output_aliases=aliases,
        scratch_shapes=[pltpu.VMEM((SIBLING_SLOTS, tr, cw), BF16),
                        pltpu.SemaphoreType.DMA((SIBLING_SLOTS,)), pltpu.SemaphoreType.DMA((SIBLING_SLOTS,))],
        compiler_params=_params("arbitrary", "arbitrary"),
    )(*args)


def _chip_copies(h_refs, lb_refs, send_sem, recv_sem):
    x, y, c = _place()
    out = []
    for k, (h, lb) in enumerate(zip(h_refs, lb_refs)):
        for q, chip in enumerate(_other_chips(x, y)):
            out.append(_remote(h.at[2 * chip[0] + chip[1]], lb.at[q], send_sem.at[3 * k + q], recv_sem.at[3 * k + q],
                               (chip[0], chip[1], c)))
    return out


def chip_start(halves, name):
    n = len(halves)
    landed = [lax.empty((N_CHIPS - 1,) + h.shape[1:], h.dtype) for h in halves]

    def body(*refs):
        send_sem, recv_sem = refs[2 * n:2 * n + 2]
        h_refs, lb_refs = refs[2 * n + 2:3 * n + 2], refs[3 * n + 2:4 * n + 2]
        for cp in _chip_copies(h_refs, lb_refs, send_sem, recv_sem):
            cp.start()
        refs[-1][...] = jnp.zeros(TOKEN_SHAPE, F32)

    sems = pltpu.SemaphoreType.DMA((3 * n,))
    outs = pl.pallas_call(
        body, name=name,
        in_specs=[HBM_SPEC] * (2 * n),
        out_specs=[SEM_SPEC, SEM_SPEC] + [HBM_SPEC] * (2 * n) + [VMEM_SPEC],
        out_shape=[sems, sems] + [pltpu.HBM(a.shape, a.dtype) for a in halves + landed]
        + [jax.ShapeDtypeStruct(TOKEN_SHAPE, F32)],
        input_output_aliases={p: p + 2 for p in range(2 * n)},
        compiler_params=SPLIT_COPY_PARAMS,
    )(*[_hbm(a) for a in halves + landed])
    return outs[0], outs[1], list(outs[2:2 + n]), list(outs[2 + n:2 + 2 * n]), outs[-1]


def chip_wait(send_sem, recv_sem, halves, landed, after, name):
    n = len(halves)

    def body(*refs):
        s_sem, r_sem = refs[2 * n:2 * n + 2]
        h_refs, lb_refs = refs[2 * n + 3:3 * n + 3], refs[3 * n + 3:]
        for cp in _chip_copies(h_refs, lb_refs, s_sem, r_sem):
            cp.wait_send()
            cp.wait_recv()

    outs = pl.pallas_call(
        body, name=name,
        in_specs=[HBM_SPEC] * (2 * n) + [SEM_SPEC, SEM_SPEC, pl.BlockSpec(memory_space=pl.ANY)],
        out_specs=[HBM_SPEC] * (2 * n),
        out_shape=[pltpu.HBM(a.shape, a.dtype) for a in halves + landed],
        input_output_aliases={p: p for p in range(2 * n)},
        compiler_params=SPLIT_COPY_PARAMS,
    )(*halves, *landed, send_sem, recv_sem, after)
    return list(outs[:n]), list(outs[n:])


def reduce_share(half, landed, me1, name):
    _, h0, cw = half.shape
    tr = _row_tile(h0, cw, SHARE_BLOCK_BYTES)
    nrb = h0 // tr

    def body(me_ref, h_ref, l0, l1, l2, o_ref, sbuf, rbuf, send_sem, recv_sem):
        x, y, c = _place()
        slot = pl.program_id(0) % SIBLING_SLOTS
        total = ((h_ref[...].astype(F32) + l0[...].astype(F32)) + l1[...].astype(F32)) + l2[...].astype(F32)
        sbuf[slot] = total
        cp = _remote(sbuf.at[slot], rbuf.at[slot], send_sem.at[slot], recv_sem.at[slot], (x, y, 1 - c))
        cp.start()
        o_ref[0, c] = total
        cp.wait_recv()
        o_ref[0, 1 - c] = rbuf[slot]
        cp.wait_send()

    landed_spec = lambda q: pl.BlockSpec((None, tr, cw), lambda i, me: (q, i, 0))
    args = [me1, half, landed, landed, landed]
    in_specs = [pl.BlockSpec((None, tr, cw), lambda i, me: (me[0], i, 0))] + [landed_spec(q) for q in range(N_CHIPS - 1)]
    out = pl.pallas_call(
        body, name=name,
        grid_spec=pltpu.PrefetchScalarGridSpec(
            num_scalar_prefetch=1, grid=(nrb,), in_specs=in_specs,
            out_specs=pl.BlockSpec((1, 2, tr, cw), lambda i, me: (0, 0, i, 0)),
            scratch_shapes=[pltpu.VMEM((SIBLING_SLOTS, tr, cw), F32), pltpu.VMEM((SIBLING_SLOTS, tr, cw), F32),
                            pltpu.SemaphoreType.DMA((SIBLING_SLOTS,)), pltpu.SemaphoreType.DMA((SIBLING_SLOTS,))]),
        out_shape=jax.ShapeDtypeStruct((1, 2, h0, cw), F32),
        compiler_params=_params("arbitrary"),
    )(*args)
    return out.reshape(1, 2 * h0, cw)


WEIGHTS = ["norm_mix_g", "norm_ffn_g", "pool_w", "pool_b", "pool_scale", "sb_w_qkv", "sb_q_gain", "sb_k_gain",
           "sb_w_o", "ssm_lam_re", "ssm_lam_im", "ssm_log_step", "ssm_b_re", "ssm_b_im", "ssm_c_re", "ssm_c_im",
           "ssm_d", "ssm_w_glu", "ssm_b_glu", "ffn_w_up", "ffn_conv_w", "ffn_conv_b", "ffn_w_down"]
BIG = {"pool_w": 0, "sb_w_qkv": 1, "sb_w_o": 0, "ssm_w_glu": 1, "ffn_w_up": 1, "ffn_w_down": 0}
SMALL_SHARDED = {"pool_b": 1, "pool_scale": 1, "ssm_d": 1, "ssm_b_glu": 1, "ffn_conv_w": 2}
SMALL = [n for n in WEIGHTS if n not in BIG]
SMALL_PAD = N_DEV * SUBLANES * LANES
N_MIXERS = 3
REDUCE_LAG = 2


def _pack(arrays):
    flat = jnp.concatenate([a.reshape(-1).astype(F32) for a in arrays])
    total = -(-flat.shape[0] // SMALL_PAD) * SMALL_PAD
    flat = jnp.pad(flat, (0, total - flat.shape[0]))
    return flat.reshape(N_DEV, -1, LANES)


def _unpack(packed, like):
    flat = packed.reshape(-1)
    out, off = [], 0
    for a in like:
        out.append(flat[off:off + a.size].reshape(a.shape))
        off += a.size
    return out


def kernel(x, norm_mix_g, norm_ffn_g, pool_w, pool_b, pool_scale, sb_w_qkv, sb_q_gain, sb_k_gain, sb_w_o, ssm_lam_re, ssm_lam_im, ssm_log_step, ssm_b_re, ssm_b_im, ssm_c_re, ssm_c_im, ssm_d, ssm_w_glu, ssm_b_glu, ffn_w_up, ffn_conv_w, ffn_conv_b, ffn_w_down, loss_target, m_norm_mix_g, m_norm_ffn_g, m_pool_w, m_pool_b, m_pool_scale, m_sb_w_qkv, m_sb_q_gain, m_sb_k_gain, m_sb_w_o, m_ssm_lam_re, m_ssm_lam_im, m_ssm_log_step, m_ssm_b_re, m_ssm_b_im, m_ssm_c_re, m_ssm_c_im, m_ssm_d, m_ssm_w_glu, m_ssm_b_glu, m_ffn_w_up, m_ffn_conv_w, m_ffn_conv_b, m_ffn_w_down, v_norm_mix_g, v_norm_ffn_g, v_pool_w, v_pool_b, v_pool_scale, v_sb_w_qkv, v_sb_q_gain, v_sb_k_gain, v_sb_w_o, v_ssm_lam_re, v_ssm_lam_im, v_ssm_log_step, v_ssm_b_re, v_ssm_b_im, v_ssm_c_re, v_ssm_c_im, v_ssm_d, v_ssm_w_glu, v_ssm_b_glu, v_ffn_w_up, v_ffn_conv_w, v_ffn_conv_b, v_ffn_w_down):
    given = dict(locals())
    w = {n: given[n] for n in WEIGHTS}
    mom = {n: given["m_" + n] for n in WEIGHTS}
    var = {n: given["v_" + n] for n in WEIGHTS}
    pool_shape = pool_w.shape
    for group in (w, mom, var):
        group["pool_w"] = group["pool_w"].reshape(pool_shape[0], pool_shape[1] * pool_shape[2], pool_shape[3])
    xi, yi, ci = _place()
    me = 2 * xi + yi
    depth = norm_mix_g.shape[0]
    x_in = x[0]
    t, d = x_in.shape

    def placed(a, ax):
        shp = list(a.shape)
        shp[ax] *= N_CHIPS
        full = lax.dynamic_update_slice_in_dim(jnp.zeros(shp, F32), a, me * a.shape[ax], ax)
        return jnp.where(ci == 0, full, 0.0)

    sharded_full = [placed(w[n], ax) for n, ax in SMALL_SHARDED.items()]

    big = list(BIG)
    me1 = _me_scalar()
    shard = {n: tuple(w[n].shape[1:]) for n in big}
    vec = lambda a, i: a[i:i + 1]
    tie = lambda v, token: v + token[0, 0]

    def layer_weights(i):
        kind, j = i % N_MIXERS, i // N_MIXERS
        mixer = {0: [("pool_w", j)], 1: [("sb_w_qkv", j), ("sb_w_o", j)], 2: [("ssm_w_glu", j)]}[kind]
        return mixer + [("ffn_w_up", i), ("ffn_w_down", i)]

    def pool_matrices(j):
        return gathered["pool_w", j].reshape((1, N_CHIPS) + tuple(pool_shape[1:]))

    started, token = [], x_in
    for i in range(depth):
        keys = layer_weights(i)
        bufs = [cast_into_gathered(w[n], l, BIG[n], me1, f"cast_{n}{l}") for n, l in keys]
        send_sem, recv_sem, bufs, token = gather_start(bufs, [shard[n] for n, _ in keys], [BIG[n] for n, _ in keys],
                                                       token, f"gather_start{i}")
        started.append((keys, send_sem, recv_sem, bufs))
        if i == 0:
            token = small_allreduce(tie(_pack(sharded_full), token), "gather_vectors")
            whole = dict(zip(SMALL_SHARDED, _unpack(token, sharded_full)))
    gathered = {}

    saved = []
    xc = x_in
    for i in range(depth):
        kind, j = i % N_MIXERS, i // N_MIXERS
        keys, send_sem, recv_sem, bufs = started[i]
        bufs = gather_wait(send_sem, recv_sem, bufs, [shard[n] for n, _ in keys], [BIG[n] for n, _ in keys],
                           token if i == 0 else xc, f"gather_wait{i}")
        for (n, l), b in zip(keys, bufs):
            gathered[n, l] = gather_forward(b, BIG[n], *shard[n], me1, f"gather_forward_{n}{l}")
        s = {"x_in": xc}
        g_mix = vec(norm_mix_g, i)
        if kind == 0:
            (h,) = rmsnorm_fwd(xc, g_mix, [F32], f"norm_mix{i}")
            x_mid = pool_fwd(h, xc, pool_matrices(j), vec(whole["pool_b"], j), vec(whole["pool_scale"], j),
                             f"pool_fwd{i}", lead=0)
        elif kind == 1:
            (h,) = rmsnorm_fwd(xc, g_mix, [BF16], f"norm_mix{i}")
            s["qkv"] = mm_cols(h, gathered["sb_w_qkv", j], out_dtype=BF16, name=f"sb_qkv{i}", lead=0)
            s["o"], s["ltot"] = sb_attn_fwd(s["qkv"], vec(sb_q_gain, j), vec(sb_k_gain, j), f"sb_attn_fwd{i}")
            x_mid = mm_cols(s["o"], gathered["sb_w_o", j], out_dtype=F32, name=f"sb_out{i}", resid=xc, lead=0)
        else:
            (h,) = rmsnorm_fwd(xc, g_mix, [F32], f"norm_mix{i}")
            prm = tuple(w[n][j] for n in ("ssm_lam_re", "ssm_lam_im", "ssm_log_step", "ssm_b_re", "ssm_b_im",
                                          "ssm_c_re", "ssm_c_im"))
            prep, s["prep_vjp"] = jax.vjp(ssm_prepare, *prm)
            s["prep"] = tuple(a.astype(BF16) for a in prep[:4]) + tuple(prep[4:])
            s["y"], s["yg"] = ssm_core_fwd(h, *s["prep"], vec(whole["ssm_d"], j), f"ssm_fwd{i}")
            s["val"], s["gate"], x_mid = glu_fwd(s["yg"], gathered["ssm_w_glu", j], vec(whole["ssm_b_glu"], j), xc,
                                                 f"ssm_glu{i}", lead=0)
        s["x_mid"], s["h"] = x_mid, h
        (h2,) = rmsnorm_fwd(x_mid, vec(norm_ffn_g, i), [BF16], f"norm_ffn{i}")
        s["h2"] = h2
        s["up_val"], s["up_gate"], s["act"] = ffn_up_fused(h2, gathered["ffn_w_up", i], whole["ffn_conv_w"][i],
                                                           vec(ffn_conv_b, i), f"ffn_up{i}", lead=0)
        xc = mm_k([s["act"]], gathered["ffn_w_down", i], b_nt=False, name=f"ffn_down{i}", resid=x_mid, lead=0)
        saved.append(s)

    dx, dxb, loss_part = loss_head(xc, loss_target[0], "loss_head")
    loss = lax.psum(loss_part[0, 0], ("x", "y", "c"))

    small = {n: [None] * w[n].shape[0] for n in SMALL}
    big_g, updated = {}, {}

    def finish_reduction(pending, after):
        layer, keys, send_sem, recv_sem, halves, landed, _ = pending
        halves, landed = chip_wait(send_sem, recv_sem, halves, landed, after, f"grads_chip_wait{layer}")
        for (n, l), h, lb in zip(keys, halves, landed):
            g = reduce_share(h, lb, me1, f"grads_share_{n}{l}")
            updated[n] = adamw_layer(w[n], g, mom[n], var[n], l, f"adamw_{n}{l}", into=updated.get(n))

    pending, travelling = None, []
    for i in reversed(range(depth)):
        kind, j = i % N_MIXERS, i // N_MIXERS
        s = saved[i]
        g_ffn, g_mix = vec(norm_ffn_g, i), vec(norm_mix_g, i)
        cw, cb = whole["ffn_conv_w"][i], vec(ffn_conv_b, i)
        if pending is not None:
            cb = tie(cb, pending[-1])
        h, h2 = s["h"], s["h2"]
        dupv, dupg, dcwv, dcwg, dcbv, dcbg = ffn_bwd_fused(dxb, gathered["ffn_w_down", i], s["up_val"], s["up_gate"], cw, cb,
                                                           f"ffn_bwd{i}", lead=0)
        big_g["ffn_w_down", i] = [mm_rows(s["act"], dxb, out_dtype=BF16, name=f"ffn_dwdown{i}")]
        big_g["ffn_w_up", i] = [mm_cols(h2, dupv, a_contract=0, out_dtype=BF16, name=f"ffn_dwup_val{i}"),
                                mm_cols(h2, dupg, a_contract=0, out_dtype=BF16, name=f"ffn_dwup_gate{i}")]
        dh2 = mm_k([dupv, dupg], gathered["ffn_w_up", i], b_nt=True, name=f"ffn_dh{i}", lead=0)
        dx_mid, dxb_mid, small["norm_ffn_g"][i] = rmsnorm_bwd(s["x_mid"], g_ffn, dh2, dx, f"norm_ffn_bwd{i}")
        small["ffn_conv_w"][i] = jnp.concatenate([dcwv, dcwg], axis=1)[None]
        small["ffn_conv_b"][i] = jnp.concatenate([dcbv, dcbg], axis=1)

        if kind == 0:
            dh, dwp, small["pool_b"][j], small["pool_scale"][j] = pool_bwd(
                dx_mid, h, pool_matrices(j), vec(whole["pool_b"], j), vec(whole["pool_scale"], j), f"pool_bwd{i}", lead=0)
            big_g["pool_w", j] = [dwp.reshape(-1, dwp.shape[-1])]
        elif kind == 1:
            do = mm_cols(dxb_mid, gathered["sb_w_o", j], b_nt=True, out_dtype=BF16, name=f"sb_do{i}", lead=0)
            big_g["sb_w_o", j] = [mm_cols(s["o"], dxb_mid, a_contract=0, out_dtype=BF16, name=f"sb_dwo{i}")]
            dq, dk, dv, small["sb_q_gain"][j], small["sb_k_gain"][j] = sb_attn_bwd(
                s["qkv"], s["ltot"], do, vec(sb_q_gain, j), vec(sb_k_gain, j), f"sb_attn_bwd{i}")
            dqkv = jnp.concatenate([dq, dk, dv], axis=1)
            big_g["sb_w_qkv", j] = [mm_cols(h, dqkv, a_contract=0, out_dtype=BF16, name=f"sb_dwqkv{i}")]
            dh = mm_k([dqkv], gathered["sb_w_qkv", j], b_nt=True, name=f"sb_dh{i}", lead=0)
        else:
            dval, dgate, dbv, dbg = glu_bwd(dx_mid, s["val"], s["gate"], f"ssm_glu_bwd{i}")
            small["ssm_b_glu"][j] = jnp.concatenate([dbv, dbg], axis=1)
            big_g["ssm_w_glu", j] = [mm_cols(s["yg"], dval, a_contract=0, out_dtype=BF16, name=f"ssm_dwglu_val{i}"),
                                     mm_cols(s["yg"], dgate, a_contract=0, out_dtype=BF16, name=f"ssm_dwglu_gate{i}")]
            dyg = mm_k([dval, dgate], gathered["ssm_w_glu", j], b_nt=True, name=f"ssm_dyg{i}", lead=0)
            dh, small["ssm_d"][j], *dprep = ssm_core_bwd(h, s["y"], dyg, *s["prep"], vec(whole["ssm_d"], j), f"ssm_bwd{i}")
            dprm = s["prep_vjp"](tuple(dprep))
            for n, g in zip(("ssm_lam_re", "ssm_lam_im", "ssm_log_step", "ssm_b_re", "ssm_b_im", "ssm_c_re", "ssm_c_im"),
                            dprm):
                small[n][j] = g[None]
        dx, dxb, small["norm_mix_g"][i] = rmsnorm_bwd(s["x_in"], g_mix, dh, dx_mid, f"norm_mix_bwd{i}")

        keys, halves = layer_weights(i), []
        for n, l in keys:
            h = None
            for a, g in enumerate(big_g[n, l]):
                h = pair_reduce(g, BIG[n], *shard[n], f"grads_pair_{n}{l}_{a}", into=h,
                                first_slot=a * (N_CHIPS // len(big_g[n, l])))
            halves.append(h)
        pending = (i, keys) + tuple(chip_start(halves, f"grads_chip_start{i}"))
        travelling.append(pending)
        if len(travelling) > REDUCE_LAG:
            finish_reduction(travelling.pop(0), dx)

    small_full = [jnp.concatenate(small[n], axis=0) for n in SMALL]
    small_sum = dict(zip(SMALL, _unpack(small_allreduce(_pack(small_full), "reduce_vectors"), small_full)))
    grads = {}
    for n in SMALL:
        g = small_sum[n]
        if n in SMALL_SHARDED:
            ax = SMALL_SHARDED[n]
            g = lax.dynamic_slice_in_dim(g, me * w[n].shape[ax], w[n].shape[ax], ax)
        grads[n] = g
    delta, new_m, new_v = {}, {}, {}
    like = [w[n] for n in SMALL]
    packed = [_pack([src[n] for n in SMALL]).reshape(-1, LANES) for src in (w, grads, mom, var)]
    vector_updates = adamw(*packed, "adamw_vectors")
    for dst, out in zip((delta, new_m, new_v), vector_updates):
        dst.update(zip(SMALL, _unpack(out, like)))

    after = vector_updates[0]
    for pending in travelling:
        finish_reduction(pending, after)
        after = updated[pending[1][0][0]][0]
    for n in big:
        grads[n], delta[n], new_m[n], new_v[n] = updated[n]
    for group in (grads, delta, new_m, new_v):
        group["pool_w"] = group["pool_w"].reshape(pool_shape)

    return (loss, dx[None], *[grads[n] for n in WEIGHTS], *[delta[n] for n in WEIGHTS],
            *[new_m[n] for n in WEIGHTS], *[new_v[n] for n in WEIGHTS])
```

```python
import functools
import math

import jax
import jax.numpy as jnp
from jax import lax
from jax.experimental import pallas as pl
from jax.experimental.pallas import tpu as pltpu

F32 = jnp.float32
BF16 = jnp.bfloat16

RMS_EPS = 1e-6
POOL_WINDOWS = (2, 4, 8, 16)
SB_HEAD_DIM = 128
SSM_GROUP_CH = 16
SSM_STATE = 64
SSM_BLOCK_GROUPS = 8
ADAM_LR = 0.001
ADAM_B1 = 0.9
ADAM_B2 = 0.999
ADAM_EPS = 1e-08
ADAM_WD = 0.01
ADAM_STEP = 10

V7X_VMEM_BYTES = 64 * 1024 * 1024
VMEM_LIMIT = V7X_VMEM_BYTES - 8 * 1024 * 1024
SUBLANES = 8
LANES = 128
NORM_ROWS = 256
MESH = pl.DeviceIdType.MESH
N_CHIPS = 4
N_DEV = 8


def _params(*sem):
    return pltpu.CompilerParams(dimension_semantics=tuple(sem) if sem else None, vmem_limit_bytes=VMEM_LIMIT)


def _tile(n, want):
    if n <= want:
        return n
    t = (want // LANES) * LANES
    while t > LANES and n % t:
        t -= LANES
    assert n % t == 0, (n, want)
    return t


def _spec(shape, imap, lead=None):
    if lead is None:
        return pl.BlockSpec(tuple(shape), imap)
    return pl.BlockSpec((None,) + tuple(shape), lambda *a: (lead,) + tuple(imap(*a)))


def _sigmoid(v):
    return 1.0 / (1.0 + jnp.exp(-v))


def _shift_down(v, k):
    return pltpu.roll(v, k, 0)


def _shift_up(v, k):
    return pltpu.roll(v, v.shape[0] - k, 0)


def rmsnorm_fwd(x, g, out_dtypes, name):
    t, d = x.shape
    tr = min(t, NORM_ROWS)

    def body(x_ref, g_ref, *o_refs):
        xv = x_ref[...]
        r = lax.rsqrt(jnp.mean(xv * xv, axis=-1, keepdims=True) + RMS_EPS)
        h = xv * r * g_ref[...]
        for o in o_refs:
            o[...] = h.astype(o.dtype)

    outs = pl.pallas_call(
        body, name=name, grid=(t // tr,),
        in_specs=[pl.BlockSpec((tr, d), lambda i: (i, 0)), pl.BlockSpec((1, d), lambda i: (0, 0))],
        out_specs=[pl.BlockSpec((tr, d), lambda i: (i, 0)) for _ in out_dtypes],
        out_shape=[jax.ShapeDtypeStruct((t, d), dt) for dt in out_dtypes],
        compiler_params=_params("parallel"),
    )(x, g)
    return outs


def rmsnorm_bwd(x, g, dh, dres, name):
    t, d = x.shape
    tr = min(t, NORM_ROWS)

    def body(x_ref, g_ref, dh_ref, dres_ref, dx_ref, dxb_ref, dg_ref):
        xv = x_ref[...]
        r = lax.rsqrt(jnp.mean(xv * xv, axis=-1, keepdims=True) + RMS_EPS)
        xhat = xv * r
        dhv = dh_ref[...]
        dxhat = dhv * g_ref[...]
        dx = dres_ref[...] + r * (dxhat - xhat * jnp.mean(dxhat * xhat, axis=-1, keepdims=True))
        dx_ref[...] = dx
        dxb_ref[...] = dx.astype(BF16)
        part = jnp.sum(dhv * xhat, axis=0, keepdims=True)

        @pl.when(pl.program_id(0) == 0)
        def _():
            dg_ref[...] = part

        @pl.when(pl.program_id(0) != 0)
        def _():
            dg_ref[...] += part

    row = pl.BlockSpec((tr, d), lambda i: (i, 0))
    vec = pl.BlockSpec((1, d), lambda i: (0, 0))
    return pl.pallas_call(
        body, name=name, grid=(t // tr,),
        in_specs=[row, vec, row, row],
        out_specs=[row, row, vec],
        out_shape=[jax.ShapeDtypeStruct((t, d), F32), jax.ShapeDtypeStruct((t, d), BF16),
                   jax.ShapeDtypeStruct((1, d), F32)],
        compiler_params=_params("arbitrary"),
    )(x, g, dh, dres)


def mm_cols(a, b, *, a_contract=1, b_nt=False, out_dtype, name, resid=None, tn=512, lead=None):
    m = a.shape[1 - a_contract]
    k = a.shape[a_contract]
    n = b.shape[-2] if b_nt else b.shape[-1]
    assert (b.shape[-1] if b_nt else b.shape[-2]) == k
    tn = _tile(n, tn)

    def body(a_ref, b_ref, *rest):
        o_ref = rest[-1]
        dn = (((a_contract,), (1 if b_nt else 0,)), ((), ()))
        acc = lax.dot_general(a_ref[...], b_ref[...], dn, preferred_element_type=F32)
        if resid is not None:
            acc = acc + rest[0][...]
        o_ref[...] = acc.astype(o_ref.dtype)

    in_specs = [pl.BlockSpec(a.shape, lambda j: (0, 0)),
                _spec((tn, k), lambda j: (j, 0), lead) if b_nt else _spec((k, tn), lambda j: (0, j), lead)]
    args = [a, b]
    if resid is not None:
        in_specs.append(pl.BlockSpec((m, tn), lambda j: (0, j)))
        args.append(resid)
    return pl.pallas_call(
        body, name=name, grid=(n // tn,), in_specs=in_specs,
        out_specs=pl.BlockSpec((m, tn), lambda j: (0, j)),
        out_shape=jax.ShapeDtypeStruct((m, n), out_dtype),
        compiler_params=_params("parallel"),
    )(*args)


def mm_rows(st, res, *, out_dtype, name, tn=512):
    k, m = st.shape
    n = res.shape[1]
    assert res.shape[0] == k
    tn = _tile(m, tn)

    def body(st_ref, res_ref, o_ref):
        o_ref[...] = lax.dot_general(st_ref[...], res_ref[...], (((0,), (0,)), ((), ())),
                                     preferred_element_type=F32).astype(o_ref.dtype)

    return pl.pallas_call(
        body, name=name, grid=(m // tn,),
        in_specs=[pl.BlockSpec((k, tn), lambda j: (0, j)), pl.BlockSpec((k, n), lambda j: (0, 0))],
        out_specs=pl.BlockSpec((tn, n), lambda j: (j, 0)),
        out_shape=jax.ShapeDtypeStruct((m, n), out_dtype),
        compiler_params=_params("parallel"),
    )(st, res)


def mm_k(a_list, b, *, b_nt, name, resid=None, tk=512, tnn=1024, lead=None):
    m = a_list[0].shape[0]
    ks = [a.shape[1] for a in a_list]
    ktot = sum(ks)
    n = b.shape[-2] if b_nt else b.shape[-1]
    assert (b.shape[-1] if b_nt else b.shape[-2]) == ktot
    tk = _tile(ks[0], tk)
    assert all(kk % tk == 0 for kk in ks)
    tnn = _tile(n, tnn)
    nks = [kk // tk for kk in ks]
    starts = [sum(nks[:i]) for i in range(len(nks))]
    nk = sum(nks)

    def body(*refs):
        a_refs = refs[:len(a_list)]
        b_ref = refs[len(a_list)]
        o_ref = refs[-1]
        kk = pl.program_id(1)

        @pl.when(kk == 0)
        def _():
            if resid is not None:
                o_ref[...] = refs[len(a_list) + 1][...]
            else:
                o_ref[...] = jnp.zeros_like(o_ref)

        dn = (((1,), (1 if b_nt else 0,)), ((), ()))
        for i, a_ref in enumerate(a_refs):
            @pl.when(jnp.logical_and(kk >= starts[i], kk < starts[i] + nks[i]))
            def _(a_ref=a_ref):
                o_ref[...] += lax.dot_general(a_ref[...], b_ref[...], dn, preferred_element_type=F32)

    def a_spec(i):
        return pl.BlockSpec((m, tk), lambda nn, kk: (0, jnp.clip(kk - starts[i], 0, nks[i] - 1)))

    in_specs = [a_spec(i) for i in range(len(a_list))]
    in_specs.append(_spec((tnn, tk), lambda nn, kk: (nn, kk), lead) if b_nt
                    else _spec((tk, tnn), lambda nn, kk: (kk, nn), lead))
    args = list(a_list) + [b]
    if resid is not None:
        in_specs.append(pl.BlockSpec((m, tnn), lambda nn, kk: (0, nn)))
        args.append(resid)
    return pl.pallas_call(
        body, name=name, grid=(n // tnn, nk), in_specs=in_specs,
        out_specs=pl.BlockSpec((m, tnn), lambda nn, kk: (0, nn)),
        out_shape=jax.ShapeDtypeStruct((m, n), F32),
        compiler_params=_params("parallel", "arbitrary"),
    )(*args)


HALO = SUBLANES
CHUNK_ROWS = 64


def _conv_taps(ext, r):
    return ext[HALO:], _shift_down(ext, 1)[HALO:], _shift_down(ext, 2)[HALO:]


def ffn_up_fused(h, w_up, conv_w, conv_b, name, lead=None):
    t, d = h.shape
    f = w_up.shape[-1] // 2
    tn = _tile(f, 256)
    nf = f // tn
    r = min(CHUNK_ROWS, t)

    def body(h_ref, wv_ref, wg_ref, cwv_ref, cwg_ref, cbv_ref, cbg_ref, uv_ref, ug_ref, act_ref, sv, sg):
        zero = jnp.zeros((HALO, tn), F32)
        sv[0:HALO, :] = zero
        sg[0:HALO, :] = zero
        hv = h_ref[...]
        sv[HALO:, :] = jnp.dot(hv, wv_ref[...], preferred_element_type=F32).astype(BF16).astype(F32)
        sg[HALO:, :] = jnp.dot(hv, wg_ref[...], preferred_element_type=F32).astype(BF16).astype(F32)
        cwv, cwg = cwv_ref[...], cwg_ref[...]
        cbv, cbg = cbv_ref[...], cbg_ref[...]

        def chunk(i, carry):
            r0 = pl.multiple_of(i * r, r)
            v0, v1, v2 = _conv_taps(sv[pl.ds(r0, r + HALO), :], r)
            g0, g1, g2 = _conv_taps(sg[pl.ds(r0, r + HALO), :], r)
            cval = cbv + cwv[2:3] * v0 + cwv[1:2] * v1 + cwv[0:1] * v2
            cgate = cbg + cwg[2:3] * g0 + cwg[1:2] * g1 + cwg[0:1] * g2
            uv_ref[pl.ds(r0, r), :] = v0.astype(BF16)
            ug_ref[pl.ds(r0, r), :] = g0.astype(BF16)
            act_ref[pl.ds(r0, r), :] = (cgate * _sigmoid(cgate) * cval).astype(BF16)
            return carry

        lax.fori_loop(0, t // r, chunk, 0)

    col = lambda off: _spec((d, tn), lambda j: (0, j + off), lead)
    cw = lambda off: pl.BlockSpec((3, tn), lambda j: (0, j + off))
    cb = lambda off: pl.BlockSpec((1, tn), lambda j: (0, j + off))
    out = pl.BlockSpec((t, tn), lambda j: (0, j))
    return pl.pallas_call(
        body, name=name, grid=(nf,),
        in_specs=[pl.BlockSpec((t, d), lambda j: (0, 0)), col(0), col(nf), cw(0), cw(nf), cb(0), cb(nf)],
        out_specs=[out, out, out],
        out_shape=[jax.ShapeDtypeStruct((t, f), BF16)] * 3,
        scratch_shapes=[pltpu.VMEM((t + HALO, tn), F32), pltpu.VMEM((t + HALO, tn), F32)],
        compiler_params=_params("parallel"),
    )(h, w_up, w_up, conv_w, conv_w, conv_b, conv_b)


def ffn_bwd_fused(dout, w_down, up_val, up_gate, conv_w, conv_b, name, lead=None):
    t, d = dout.shape
    f = w_down.shape[-2]
    tn = _tile(f, 256)
    nf = f // tn
    r = min(CHUNK_ROWS, t)

    def body(do_ref, wd_ref, uv_ref, ug_ref, cwv_ref, cwg_ref, cbv_ref, cbg_ref,
             dv_ref, dg_ref, dcwv_ref, dcwg_ref, dcbv_ref, dcbg_ref, da, sv, sg, ev, eg):
        zero = jnp.zeros((HALO, tn), F32)
        sv[0:HALO, :] = zero
        sg[0:HALO, :] = zero
        ev[t:, :] = zero
        eg[t:, :] = zero
        da[...] = lax.dot_general(do_ref[...], wd_ref[...], (((1,), (1,)), ((), ())), preferred_element_type=F32)
        sv[HALO:, :] = uv_ref[...].astype(F32)
        sg[HALO:, :] = ug_ref[...].astype(F32)
        cwv, cwg = cwv_ref[...], cwg_ref[...]
        cbv, cbg = cbv_ref[...], cbg_ref[...]

        def chunk(i, acc):
            r0 = pl.multiple_of(i * r, r)
            v = _conv_taps(sv[pl.ds(r0, r + HALO), :], r)
            g = _conv_taps(sg[pl.ds(r0, r + HALO), :], r)
            cval = cbv + cwv[2:3] * v[0] + cwv[1:2] * v[1] + cwv[0:1] * v[2]
            cgate = cbg + cwg[2:3] * g[0] + cwg[1:2] * g[1] + cwg[0:1] * g[2]
            s = _sigmoid(cgate)
            dav = da[pl.ds(r0, r), :]
            dval = dav * (cgate * s)
            dgate = dav * cval * (s * (1.0 + cgate * (1.0 - s)))
            ev[pl.ds(r0, r), :] = dval
            eg[pl.ds(r0, r), :] = dgate
            col = lambda z: jnp.sum(z, axis=0, keepdims=True)
            new = [acc[0] + col(dval), acc[1] + col(dgate)]
            new += [acc[2 + j] + col(dval * v[2 - j]) for j in range(3)]
            new += [acc[5 + j] + col(dgate * g[2 - j]) for j in range(3)]
            return tuple(new)

        z1 = jnp.zeros((1, tn), F32)
        acc = lax.fori_loop(0, t // r, chunk, (z1,) * 8)
        dcbv_ref[...] = acc[0]
        dcbg_ref[...] = acc[1]
        for j in range(3):
            dcwv_ref[j:j + 1, :] = acc[2 + j]
            dcwg_ref[j:j + 1, :] = acc[5 + j]

        def chunk2(i, carry):
            r0 = pl.multiple_of(i * r, r)
            for e_ref, cw_, o_ref in ((ev, cwv, dv_ref), (eg, cwg, dg_ref)):
                ext = e_ref[pl.ds(r0, r + HALO), :]
                d0, d1, d2 = ext[:r], _shift_up(ext, 1)[:r], _shift_up(ext, 2)[:r]
                o_ref[pl.ds(r0, r), :] = (cw_[2:3] * d0 + cw_[1:2] * d1 + cw_[0:1] * d2).astype(BF16)
            return carry

        lax.fori_loop(0, t // r, chunk2, 0)

    cw = lambda off: pl.BlockSpec((3, tn), lambda j: (0, j + off))
    cb = lambda off: pl.BlockSpec((1, tn), lambda j: (0, j + off))
    tile = pl.BlockSpec((t, tn), lambda j: (0, j))
    s = lambda rows, dt: jax.ShapeDtypeStruct((rows, f), dt)
    halo = pltpu.VMEM((t + HALO, tn), F32)
    return pl.pallas_call(
        body, name=name, grid=(nf,),
        in_specs=[pl.BlockSpec((t, d), lambda j: (0, 0)), _spec((tn, d), lambda j: (j, 0), lead),
                  tile, tile, cw(0), cw(nf), cb(0), cb(nf)],
        out_specs=[tile, tile, pl.BlockSpec((3, tn), lambda j: (0, j)), pl.BlockSpec((3, tn), lambda j: (0, j)),
                   pl.BlockSpec((1, tn), lambda j: (0, j)), pl.BlockSpec((1, tn), lambda j: (0, j))],
        out_shape=[s(t, BF16), s(t, BF16), s(3, F32), s(3, F32), s(1, F32), s(1, F32)],
        scratch_shapes=[pltpu.VMEM((t, tn), F32), halo, halo, halo, halo],
        compiler_params=_params("parallel"),
    )(dout, w_down, up_val, up_gate, conv_w, conv_w, conv_b, conv_b)


POOL_PAD = max(POOL_WINDOWS)


def _window_sum(ext, win, shift):
    assert POOL_WINDOWS == (2, 4, 8, 16)
    s2 = ext + shift(ext, 1)
    s4 = s2 + shift(s2, 2)
    s8 = s4 + shift(s4, 4)
    s16 = s8 + shift(s8, 8)
    return jnp.where(win == 2, s2, jnp.where(win == 4, s4, jnp.where(win == 8, s8, s16)))


def _pool_win_scalar(g):
    win = jnp.int32(POOL_WINDOWS[-1])
    for k in range(len(POOL_WINDOWS) - 2, -1, -1):
        win = jnp.where(g == k, jnp.int32(POOL_WINDOWS[k]), win)
    return win


def _pool_count(r0, r, win):
    rows = r0 + lax.broadcasted_iota(jnp.int32, (r, 1), 0)
    return jnp.minimum(rows + 1, win).astype(F32)


def _pooled_into(hp, pooled, h_ref, t, r, win):
    hp[0:POOL_PAD, :] = jnp.zeros((POOL_PAD, hp.shape[1]), F32)
    hp[POOL_PAD:, :] = h_ref[...]

    def chunk(i, carry):
        r0 = pl.multiple_of(i * r, r)
        ext = hp[pl.ds(r0, r + POOL_PAD), :]
        s = _window_sum(ext, win, _shift_down)[POOL_PAD:]
        pooled[pl.ds(r0, r), :] = (s / _pool_count(r0, r, win) - ext[POOL_PAD:]).astype(BF16)
        return carry

    lax.fori_loop(0, t // r, chunk, 0)


def pool_fwd(h, x, w, b, scale, name, lead=None):
    t, d = h.shape
    ng, rows, dg = w.shape[-3:]
    r = min(CHUNK_ROWS, t)

    def body(h_ref, x_ref, w_ref, b_ref, s_ref, o_ref, hp, pooled):
        win = _pool_win_scalar(pl.program_id(0))
        _pooled_into(hp, pooled, h_ref, t, r, win)
        y = jnp.dot(pooled[...], w_ref[...].reshape(dg, dg), preferred_element_type=F32)
        o_ref[...] = x_ref[...] + (y + b_ref[...]) * s_ref[...]

    col = pl.BlockSpec((t, dg), lambda g: (0, g))
    vec = pl.BlockSpec((1, dg), lambda g: (0, g))
    return pl.pallas_call(
        body, name=name, grid=(ng,),
        in_specs=[col, col, _spec((N_CHIPS, None, rows, dg), lambda g: (0, g, 0, 0), lead), vec, vec],
        out_specs=col, out_shape=jax.ShapeDtypeStruct((t, d), F32),
        scratch_shapes=[pltpu.VMEM((t + POOL_PAD, dg), F32), pltpu.VMEM((t, dg), BF16)],
        compiler_params=_params("parallel"),
    )(h, x, w, b, scale)


def pool_bwd(dm, h, w, b, scale, name, lead=None):
    t, d = h.shape
    ng, rows, dg = w.shape[-3:]
    r = min(CHUNK_ROWS, t)

    def body(dm_ref, h_ref, w_ref, b_ref, s_ref, dh_ref, dw_ref, db_ref, ds_ref, hp, pooled, q):
        win = _pool_win_scalar(pl.program_id(0))
        _pooled_into(hp, pooled, h_ref, t, r, win)
        wv = w_ref[...].reshape(dg, dg)
        y = jnp.dot(pooled[...], wv, preferred_element_type=F32)
        dmv = dm_ref[...]
        ds_ref[...] = jnp.sum(dmv * (y + b_ref[...]), axis=0, keepdims=True)
        dy = dmv * s_ref[...]
        db_ref[...] = jnp.sum(dy, axis=0, keepdims=True)
        dyb = dy.astype(BF16)
        dw_ref[...] = lax.dot_general(pooled[...], dyb, (((0,), (0,)), ((), ())),
                                      preferred_element_type=F32).astype(BF16).reshape(N_CHIPS, rows, dg)
        dp = lax.dot_general(dyb, wv, (((1,), (1,)), ((), ())), preferred_element_type=F32)
        q[t:, :] = jnp.zeros((POOL_PAD, dg), F32)
        q[0:t, :] = dp / _pool_count(0, t, win)
        dh_ref[...] = -dp

        def chunk(i, carry):
            r0 = pl.multiple_of(i * r, r)
            ext = q[pl.ds(r0, r + POOL_PAD), :]
            dh_ref[pl.ds(r0, r), :] += _window_sum(ext, win, _shift_up)[:r]
            return carry

        lax.fori_loop(0, t // r, chunk, 0)

    col = pl.BlockSpec((t, dg), lambda g: (0, g))
    vec = pl.BlockSpec((1, dg), lambda g: (0, g))
    wshape = (N_CHIPS, None, rows, dg)
    wmap = lambda g: (0, g, 0, 0)
    return pl.pallas_call(
        body, name=name, grid=(ng,),
        in_specs=[col, col, _spec(wshape, wmap, lead), vec, vec],
        out_specs=[col, pl.BlockSpec(wshape, wmap), vec, vec],
        out_shape=[jax.ShapeDtypeStruct((t, d), F32), jax.ShapeDtypeStruct((N_CHIPS, ng, rows, dg), BF16),
                   jax.ShapeDtypeStruct((1, d), F32), jax.ShapeDtypeStruct((1, d), F32)],
        scratch_shapes=[pltpu.VMEM((t + POOL_PAD, dg), F32), pltpu.VMEM((t, dg), BF16),
                        pltpu.VMEM((t + POOL_PAD, dg), F32)],
        compiler_params=_params("parallel"),
    )(dm, h, w, b, scale)


SB_BLOCK = 256


def _tri_sum(v, tri):
    hi = v.astype(BF16)
    lo = (v - hi.astype(F32)).astype(BF16)
    dot = lambda p: jnp.dot(p, tri, preferred_element_type=F32)
    return dot(hi) + dot(lo)


def _tri(bk, cmp):
    return cmp(lax.broadcasted_iota(jnp.int32, (bk, bk), 0), lax.broadcasted_iota(jnp.int32, (bk, bk), 1)).astype(BF16)


def _sb_logits(qblk, kblk, inv, diagonal):
    bq, bk = qblk.shape[0], kblk.shape[0]
    z = lax.dot_general(qblk, kblk, (((1,), (1,)), ((), ())), preferred_element_type=F32) * inv
    lb = jnp.minimum(z, 0.0) - jnp.log(1.0 + jnp.exp(-jnp.abs(z)))
    if not diagonal:
        return lb, lb - z, None
    mask = lax.broadcasted_iota(jnp.int32, (bq, bk), 1) < lax.broadcasted_iota(jnp.int32, (bq, bk), 0)
    return lb, jnp.where(mask, lb - z, 0.0), mask


def _keep(mask, v):
    return v if mask is None else jnp.where(mask, v, 0.0)


def _head_norm(ref, gain):
    xv = ref[...].astype(F32)
    r = lax.rsqrt(jnp.mean(xv * xv, axis=-1, keepdims=True) + RMS_EPS)
    xhat = xv * r
    return xhat, r, (xhat * gain).astype(BF16)


def sb_attn_fwd(qkv, q_gain, k_gain, name):
    t = qkv.shape[0]
    d = qkv.shape[1] // 3
    dh = SB_HEAD_DIM
    nh = d // dh
    blk = min(SB_BLOCK, t)
    inv = 1.0 / math.sqrt(dh)

    def body(q_ref, k_ref, v_ref, qg_ref, kg_ref, o_ref, lt_ref, qn, kn):
        qn[...] = _head_norm(q_ref, qg_ref[...])[2]
        kn[...] = _head_norm(k_ref, kg_ref[...])[2]
        later = _tri(blk, lambda j, s: j > s)

        def q_loop(qb, carry):
            q0 = pl.multiple_of(qb * blk, blk)
            qblk = qn[pl.ds(q0, blk), :]

            def pair(k0, st, diagonal):
                c, acc = st
                lb, lm, mask = _sb_logits(qblk, kn[pl.ds(k0, blk), :], inv, diagonal)
                a = _keep(mask, jnp.exp(lb + _tri_sum(lm, later) + c))
                acc = acc + jnp.dot(a.astype(BF16), v_ref[pl.ds(k0, blk), :], preferred_element_type=F32)
                return c + jnp.sum(lm, axis=1, keepdims=True), acc

            st = pair(q0, (jnp.zeros((blk, 1), F32), jnp.zeros((blk, dh), F32)), True)
            c, acc = lax.fori_loop(1, qb + 1, lambda i, st: pair(pl.multiple_of((qb - i) * blk, blk), st, False), st)
            o_ref[pl.ds(q0, blk), :] = acc.astype(BF16)
            lt_ref[pl.ds(q0, blk), :] = c
            return carry

        lax.fori_loop(0, t // blk, q_loop, 0)

    head = lambda off: pl.BlockSpec((t, dh), lambda h: (0, h + off))
    gain = pl.BlockSpec((1, dh), lambda h: (0, 0))
    return pl.pallas_call(
        body, name=name, grid=(nh,),
        in_specs=[head(0), head(nh), head(2 * nh), gain, gain],
        out_specs=[head(0), pl.BlockSpec((None, t, 1), lambda h: (h, 0, 0))],
        out_shape=[jax.ShapeDtypeStruct((t, d), BF16), jax.ShapeDtypeStruct((nh, t, 1), F32)],
        scratch_shapes=[pltpu.VMEM((t, dh), BF16), pltpu.VMEM((t, dh), BF16)],
        compiler_params=_params("parallel"),
    )(qkv, qkv, qkv, q_gain, k_gain)


def sb_attn_bwd(qkv, ltot, do, q_gain, k_gain, name):
    t = qkv.shape[0]
    d = qkv.shape[1] // 3
    dh = SB_HEAD_DIM
    nh = d // dh
    blk = min(SB_BLOCK, t)
    inv = 1.0 / math.sqrt(dh)
    tn_dims = (((0,), (0,)), ((), ()))

    def body(q_ref, k_ref, v_ref, lt_ref, do_ref, qg_ref, kg_ref, dq_ref, dk_ref, dv_ref, dqg_ref, dkg_ref,
             qn, kn, dqn, dkn, dvn):
        qg, kg = qg_ref[...], kg_ref[...]
        qhat, rq, qnb = _head_norm(q_ref, qg)
        khat, rk, knb = _head_norm(k_ref, kg)
        qn[...] = qnb
        kn[...] = knb
        dkn[...] = jnp.zeros_like(dkn)
        dvn[...] = jnp.zeros_like(dvn)
        upto = _tri(blk, lambda j, s: j <= s)
        before = _tri(blk, lambda j, s: j < s)

        def q_loop(qb, carry):
            q0 = pl.multiple_of(qb * blk, blk)
            qblk = qn[pl.ds(q0, blk), :]
            doblk = do_ref[pl.ds(q0, blk), :]
            ltv = lt_ref[pl.ds(q0, blk), :]

            def pair(k0, st, diagonal):
                pl_, pg, dq = st
                kblk = kn[pl.ds(k0, blk), :]
                lb, lm, mask = _sb_logits(qblk, kblk, inv, diagonal)
                a = _keep(mask, jnp.exp(lb + (ltv - pl_ - _tri_sum(lm, upto))))
                da = lax.dot_general(doblk, v_ref[pl.ds(k0, blk), :], (((1,), (1,)), ((), ())),
                                     preferred_element_type=F32)
                g = da * a
                g_before = pg + _tri_sum(g, before)
                beta = jnp.exp(lb)
                dz = (_keep(mask, g * (1.0 - beta) - beta * g_before) * inv).astype(BF16)
                dq = dq + jnp.dot(dz, kblk, preferred_element_type=F32)
                dkn[pl.ds(k0, blk), :] += lax.dot_general(dz, qblk, tn_dims, preferred_element_type=F32)
                dvn[pl.ds(k0, blk), :] += lax.dot_general(a.astype(BF16), doblk, tn_dims, preferred_element_type=F32)
                return pl_ + jnp.sum(lm, axis=1, keepdims=True), pg + jnp.sum(g, axis=1, keepdims=True), dq

            z1 = jnp.zeros((blk, 1), F32)
            st = lax.fori_loop(0, qb, lambda kb, st: pair(pl.multiple_of(kb * blk, blk), st, False),
                               (z1, z1, jnp.zeros((blk, dh), F32)))
            dqn[pl.ds(q0, blk), :] = pair(q0, st, True)[2]
            return carry

        lax.fori_loop(0, t // blk, q_loop, 0)

        first = pl.program_id(0) == 0
        for dn, xhat, r, gain, out_ref, dgain_ref in ((dqn, qhat, rq, qg, dq_ref, dqg_ref),
                                                      (dkn, khat, rk, kg, dk_ref, dkg_ref)):
            dnv = dn[...]
            dxhat = dnv * gain
            out_ref[...] = (r * (dxhat - xhat * jnp.mean(dxhat * xhat, axis=-1, keepdims=True))).astype(BF16)
            part = jnp.sum(dnv * xhat, axis=0, keepdims=True)

            @pl.when(first)
            def _(dgain_ref=dgain_ref, part=part):
                dgain_ref[...] = part

            @pl.when(jnp.logical_not(first))
            def _(dgain_ref=dgain_ref, part=part):
                dgain_ref[...] += part

        dv_ref[...] = dvn[...].astype(BF16)

    head = lambda off: pl.BlockSpec((t, dh), lambda h: (0, h + off))
    gain = pl.BlockSpec((1, dh), lambda h: (0, 0))
    big = jax.ShapeDtypeStruct((t, d), BF16)
    small = jax.ShapeDtypeStruct((1, dh), F32)
    return pl.pallas_call(
        body, name=name, grid=(nh,),
        in_specs=[head(0), head(nh), head(2 * nh), pl.BlockSpec((None, t, 1), lambda h: (h, 0, 0)), head(0),
                  gain, gain],
        out_specs=[head(0), head(0), head(0), gain, gain],
        out_shape=[big, big, big, small, small],
        scratch_shapes=[pltpu.VMEM((t, dh), BF16), pltpu.VMEM((t, dh), BF16),
                        pltpu.VMEM((t, dh), F32), pltpu.VMEM((t, dh), F32), pltpu.VMEM((t, dh), F32)],
        compiler_params=_params("arbitrary"),
    )(qkv, qkv, qkv, ltot, do, q_gain, k_gain)


GELU_C = math.sqrt(2.0 / math.pi)
GELU_A = 0.044715
SCAN_ROWS = SUBLANES


def _gelu(y):
    return 0.5 * y * (1.0 + jnp.tanh(GELU_C * (y + GELU_A * y * y * y)))


def _gelu_grad(y):
    th = jnp.tanh(GELU_C * (y + GELU_A * y * y * y))
    return 0.5 * (1.0 + th) + 0.5 * y * (1.0 - th * th) * GELU_C * (1.0 + 3.0 * GELU_A * y * y)


def _powers(ar, ai):
    out = [(ar, ai)]
    for _ in range(SCAN_ROWS - 1):
        pr, pi = out[-1]
        out.append((pr * ar - pi * ai, pr * ai + pi * ar))
    return out


def _rows(vals):
    c = vals[0].shape[1]
    row = lax.broadcasted_iota(jnp.int32, (SCAN_ROWS, c), 0)
    out = jnp.broadcast_to(vals[SCAN_ROWS - 1], (SCAN_ROWS, c))
    for j in range(SCAN_ROWS - 2, -1, -1):
        out = jnp.where(row == j, vals[j], out)
    return out


def _scan_forward(sr, si, off, t, ar, ai):
    c = ar.shape[1]
    p = _powers(ar, ai)
    pwr = _rows([q[0] for q in p])
    pwi = _rows([q[1] for q in p])
    row = lax.broadcasted_iota(jnp.int32, (SCAN_ROWS, c), 0)

    def tile(i, carry):
        cr, ci = carry
        r0 = pl.multiple_of(off + i * SCAN_ROWS, SCAN_ROWS)
        xr = sr[pl.ds(r0, SCAN_ROWS), :]
        xi = si[pl.ds(r0, SCAN_ROWS), :]
        for k in (1, 2, 4):
            pr, pi = p[k - 1]
            shr = jnp.where(row >= k, _shift_down(xr, k), 0.0)
            shi = jnp.where(row >= k, _shift_down(xi, k), 0.0)
            xr, xi = xr + pr * shr - pi * shi, xi + pr * shi + pi * shr
        xr, xi = xr + pwr * cr - pwi * ci, xi + pwr * ci + pwi * cr
        sr[pl.ds(r0, SCAN_ROWS), :] = xr
        si[pl.ds(r0, SCAN_ROWS), :] = xi
        return xr[SCAN_ROWS - 1:SCAN_ROWS], xi[SCAN_ROWS - 1:SCAN_ROWS]

    z = jnp.zeros((1, c), F32)
    lax.fori_loop(0, t // SCAN_ROWS, tile, (z, z))


def _scan_reverse(gr, gi, t, ar, ai, xr_ref, xi_ref):
    c = ar.shape[1]
    p = _powers(ar, ai)
    pwr = _rows([p[SCAN_ROWS - 1 - j][0] for j in range(SCAN_ROWS)])
    pwi = _rows([p[SCAN_ROWS - 1 - j][1] for j in range(SCAN_ROWS)])
    row = lax.broadcasted_iota(jnp.int32, (SCAN_ROWS, c), 0)
    n = t // SCAN_ROWS

    def tile(ii, carry):
        cr, ci, dar, dai = carry
        r0 = pl.multiple_of((n - 1 - ii) * SCAN_ROWS, SCAN_ROWS)
        xr = gr[pl.ds(r0, SCAN_ROWS), :]
        xi = gi[pl.ds(r0, SCAN_ROWS), :]
        for k in (1, 2, 4):
            pr, pi = p[k - 1]
            shr = jnp.where(row < SCAN_ROWS - k, _shift_up(xr, k), 0.0)
            shi = jnp.where(row < SCAN_ROWS - k, _shift_up(xi, k), 0.0)
            xr, xi = xr + pr * shr + pi * shi, xi + pr * shi - pi * shr
        xr, xi = xr + pwr * cr + pwi * ci, xi + pwr * ci - pwi * cr
        gr[pl.ds(r0, SCAN_ROWS), :] = xr
        gi[pl.ds(r0, SCAN_ROWS), :] = xi
        xpr = _shift_down(xr_ref[pl.ds(r0, 2 * SCAN_ROWS), :], 1)[SCAN_ROWS:]
        xpi = _shift_down(xi_ref[pl.ds(r0, 2 * SCAN_ROWS), :], 1)[SCAN_ROWS:]
        return xr[0:1], xi[0:1], dar + xr * xpr + xi * xpi, dai + xi * xpr - xr * xpi

    z = jnp.zeros((1, c), F32)
    z8 = jnp.zeros((SCAN_ROWS, c), F32)
    _, _, dar, dai = lax.fori_loop(0, n, tile, (z, z, z8, z8))
    return jnp.sum(dar, axis=0, keepdims=True), jnp.sum(dai, axis=0, keepdims=True)


def _ssm_specs(t, nb, ch, st):
    col = pl.BlockSpec((t, ch), lambda b: (0, b))
    vec = pl.BlockSpec((1, ch), lambda b: (0, b))
    bspec = pl.BlockSpec((None, ch, st), lambda b: (b, 0, 0))
    cspec = pl.BlockSpec((None, st, ch), lambda b: (b, 0, 0))
    aspec = pl.BlockSpec((None, 1, st), lambda b: (b, 0, 0))
    return col, vec, bspec, cspec, aspec


def ssm_core_fwd(u, bre, bim, cre, cim, a_re, a_im, dskip, name):
    t, d = u.shape
    nb, ch, st = bre.shape

    def body(u_ref, bre_ref, bim_ref, cre_ref, cim_ref, ar_ref, ai_ref, d_ref, y_ref, yg_ref, sr, si):
        uv = u_ref[...]
        ub = uv.astype(BF16)
        sr[...] = jnp.dot(ub, bre_ref[...], preferred_element_type=F32)
        si[...] = jnp.dot(ub, bim_ref[...], preferred_element_type=F32)
        _scan_forward(sr, si, 0, t, ar_ref[...], ai_ref[...])
        y = (jnp.dot(sr[...].astype(BF16), cre_ref[...], preferred_element_type=F32)
             - jnp.dot(si[...].astype(BF16), cim_ref[...], preferred_element_type=F32) + d_ref[...] * uv)
        y_ref[...] = y
        yg_ref[...] = _gelu(y).astype(BF16)

    col, vec, bspec, cspec, aspec = _ssm_specs(t, nb, ch, st)
    return pl.pallas_call(
        body, name=name, grid=(nb,),
        in_specs=[col, bspec, bspec, cspec, cspec, aspec, aspec, vec],
        out_specs=[col, col],
        out_shape=[jax.ShapeDtypeStruct((t, d), F32), jax.ShapeDtypeStruct((t, d), BF16)],
        scratch_shapes=[pltpu.VMEM((t, st), F32), pltpu.VMEM((t, st), F32)],
        compiler_params=_params("parallel"),
    )(u, bre, bim, cre, cim, a_re, a_im, dskip)


def ssm_core_bwd(u, y, dyg, bre, bim, cre, cim, a_re, a_im, dskip, name):
    t, d = u.shape
    nb, ch, st = bre.shape
    tn_dims = (((0,), (0,)), ((), ()))
    nt_dims = (((1,), (1,)), ((), ()))

    def body(u_ref, y_ref, dyg_ref, bre_ref, bim_ref, cre_ref, cim_ref, ar_ref, ai_ref, d_ref,
             du_ref, dd_ref, dbre_ref, dbim_ref, dcre_ref, dcim_ref, dar_ref, dai_ref, xr, xi, gr, gi):
        uv = u_ref[...]
        ub = uv.astype(BF16)
        ar, ai = ar_ref[...], ai_ref[...]
        dy = dyg_ref[...] * _gelu_grad(y_ref[...])
        dd_ref[...] = jnp.sum(dy * uv, axis=0, keepdims=True)
        zero = jnp.zeros((HALO, st), F32)
        xr[0:HALO, :] = zero
        xi[0:HALO, :] = zero
        xr[HALO:, :] = jnp.dot(ub, bre_ref[...], preferred_element_type=F32)
        xi[HALO:, :] = jnp.dot(ub, bim_ref[...], preferred_element_type=F32)
        _scan_forward(xr, xi, HALO, t, ar, ai)
        dyb = dy.astype(BF16)
        dcre_ref[...] = lax.dot_general(xr[HALO:, :].astype(BF16), dyb, tn_dims, preferred_element_type=F32)
        dcim_ref[...] = -lax.dot_general(xi[HALO:, :].astype(BF16), dyb, tn_dims, preferred_element_type=F32)
        gr[...] = lax.dot_general(dyb, cre_ref[...], nt_dims, preferred_element_type=F32)
        gi[...] = -lax.dot_general(dyb, cim_ref[...], nt_dims, preferred_element_type=F32)
        dar, dai = _scan_reverse(gr, gi, t, ar, ai, xr, xi)
        dar_ref[...] = dar
        dai_ref[...] = dai
        grb = gr[...].astype(BF16)
        gib = gi[...].astype(BF16)
        dbre_ref[...] = lax.dot_general(ub, grb, tn_dims, preferred_element_type=F32)
        dbim_ref[...] = lax.dot_general(ub, gib, tn_dims, preferred_element_type=F32)
        du_ref[...] = (d_ref[...] * dy + lax.dot_general(grb, bre_ref[...], nt_dims, preferred_element_type=F32)
                       + lax.dot_general(gib, bim_ref[...], nt_dims, preferred_element_type=F32))

    col, vec, bspec, cspec, aspec = _ssm_specs(t, nb, ch, st)
    sh = jax.ShapeDtypeStruct
    return pl.pallas_call(
        body, name=name, grid=(nb,),
        in_specs=[col, col, col, bspec, bspec, cspec, cspec, aspec, aspec, vec],
        out_specs=[col, vec, bspec, bspec, cspec, cspec, aspec, aspec],
        out_shape=[sh((t, d), F32), sh((1, d), F32), sh((nb, ch, st), F32), sh((nb, ch, st), F32),
                   sh((nb, st, ch), F32), sh((nb, st, ch), F32), sh((nb, 1, st), F32), sh((nb, 1, st), F32)],
        scratch_shapes=[pltpu.VMEM((t + HALO, st), F32), pltpu.VMEM((t + HALO, st), F32),
                        pltpu.VMEM((t, st), F32), pltpu.VMEM((t, st), F32)],
        compiler_params=_params("parallel"),
    )(u, y, dyg, bre, bim, cre, cim, a_re, a_im, dskip)


def glu_fwd(yg, w_glu, b_glu, x, name, lead=None):
    t, d = yg.shape
    tn = _tile(d, 256)
    nd = d // tn

    def body(yg_ref, wv_ref, wg_ref, bv_ref, bg_ref, x_ref, val_ref, gate_ref, o_ref):
        ygv = yg_ref[...]
        vb = (jnp.dot(ygv, wv_ref[...], preferred_element_type=F32) + bv_ref[...]).astype(BF16)
        gb = (jnp.dot(ygv, wg_ref[...], preferred_element_type=F32) + bg_ref[...]).astype(BF16)
        val_ref[...] = vb
        gate_ref[...] = gb
        o_ref[...] = x_ref[...] + vb.astype(F32) * _sigmoid(gb.astype(F32))

    col = lambda off: _spec((d, tn), lambda j: (0, j + off), lead)
    vec = lambda off: pl.BlockSpec((1, tn), lambda j: (0, j + off))
    tile = pl.BlockSpec((t, tn), lambda j: (0, j))
    return pl.pallas_call(
        body, name=name, grid=(nd,),
        in_specs=[pl.BlockSpec((t, d), lambda j: (0, 0)), col(0), col(nd), vec(0), vec(nd), tile],
        out_specs=[tile, tile, tile],
        out_shape=[jax.ShapeDtypeStruct((t, d), BF16), jax.ShapeDtypeStruct((t, d), BF16),
                   jax.ShapeDtypeStruct((t, d), F32)],
        compiler_params=_params("parallel"),
    )(yg, w_glu, w_glu, b_glu, b_glu, x)


def glu_bwd(dm, val, gate, name):
    t, d = dm.shape
    tn = _tile(d, 256)

    def body(dm_ref, val_ref, gate_ref, dv_ref, dg_ref, dbv_ref, dbg_ref):
        dmv = dm_ref[...]
        s = _sigmoid(gate_ref[...].astype(F32))
        dval = dmv * s
        dgate = dmv * val_ref[...].astype(F32) * s * (1.0 - s)
        dv_ref[...] = dval.astype(BF16)
        dg_ref[...] = dgate.astype(BF16)
        dbv_ref[...] = jnp.sum(dval, axis=0, keepdims=True)
        dbg_ref[...] = jnp.sum(dgate, axis=0, keepdims=True)

    tile = pl.BlockSpec((t, tn), lambda j: (0, j))
    vec = pl.BlockSpec((1, tn), lambda j: (0, j))
    return pl.pallas_call(
        body, name=name, grid=(d // tn,),
        in_specs=[tile, tile, tile], out_specs=[tile, tile, vec, vec],
        out_shape=[jax.ShapeDtypeStruct((t, d), BF16), jax.ShapeDtypeStruct((t, d), BF16),
                   jax.ShapeDtypeStruct((1, d), F32), jax.ShapeDtypeStruct((1, d), F32)],
        compiler_params=_params("parallel"),
    )(dm, val, gate)


def _block_diag(m, gb):
    g, a, b = m.shape
    eye = jnp.eye(gb, dtype=m.dtype)
    return jnp.einsum("ngab,gk->ngakb", m.reshape(g // gb, gb, a, b), eye).reshape(g // gb, gb * a, gb * b)


def ssm_prepare(lam_re, lam_im, log_step, b_re, b_im, c_re, c_im):
    gb = SSM_BLOCK_GROUPS
    g, p = lam_re.shape
    step = jnp.exp(log_step)[:, None]
    mag = jnp.exp(lam_re * step)
    lb_re = mag * jnp.cos(lam_im * step)
    lb_im = mag * jnp.sin(lam_im * step)
    den = lam_re * lam_re + lam_im * lam_im
    f_re = ((lb_re - 1.0) * lam_re + lb_im * lam_im) / den
    f_im = (lb_im * lam_re - (lb_re - 1.0) * lam_im) / den
    bb_re = f_re[..., None] * b_re - f_im[..., None] * b_im
    bb_im = f_re[..., None] * b_im + f_im[..., None] * b_re
    tr = lambda m: jnp.transpose(m, (0, 2, 1))
    return (_block_diag(tr(bb_re), gb), _block_diag(tr(bb_im), gb), _block_diag(tr(c_re), gb), _block_diag(tr(c_im), gb),
            lb_re.reshape(g // gb, 1, gb * p), lb_im.reshape(g // gb, 1, gb * p))


EW_BLOCK_BYTES = 2 * 1024 * 1024
BF16_ROWS = 16


def _row_tile(rows, cols, block_bytes=EW_BLOCK_BYTES):
    limit = max(BF16_ROWS, block_bytes // (cols * 4))
    best = None
    for tr in range(BF16_ROWS, min(rows, limit) + 1, BF16_ROWS):
        if rows % tr == 0:
            best = tr
    return best if best is not None else rows


def _as2d(a):
    return a.reshape(-1, a.shape[-1])


def ew(fn, ins, out_dtypes, name):
    rows, cols = ins[0].shape
    tr = _row_tile(rows, cols)
    n_in = len(ins)

    def body(*refs):
        outs = fn(*[r[...] for r in refs[:n_in]])
        for o_ref, v in zip(refs[n_in:], outs):
            o_ref[...] = v.astype(o_ref.dtype)

    spec = pl.BlockSpec((tr, cols), lambda i: (i, 0))
    return pl.pallas_call(
        body, name=name, grid=(rows // tr,), in_specs=[spec] * n_in, out_specs=[spec] * len(out_dtypes),
        out_shape=[jax.ShapeDtypeStruct((rows, cols), dt) for dt in out_dtypes],
        compiler_params=_params("parallel"),
    )(*ins)


def _adamw(w, g, m, v):
    m = ADAM_B1 * m + (1.0 - ADAM_B1) * g
    v = ADAM_B2 * v + (1.0 - ADAM_B2) * (g * g)
    m_hat = m / (1.0 - ADAM_B1 ** ADAM_STEP)
    v_hat = v / (1.0 - ADAM_B2 ** ADAM_STEP)
    delta = -ADAM_LR * (m_hat / (jnp.sqrt(v_hat) + ADAM_EPS) + ADAM_WD * w)
    return delta, m, v


def adamw(w, g, m, v, name):
    outs = ew(_adamw, [_as2d(w), _as2d(g), _as2d(m), _as2d(v)], [F32, F32, F32], name)
    return [o.reshape(w.shape) for o in outs]


def adamw_layer(w, g, m, v, layer, name, into=None):
    nl, r, cw = w.shape
    tr = _row_tile(r, cw)

    def body(w_ref, g_ref, m_ref, v_ref, *rest):
        g_out, d_out, m_out, v_out = rest[-4:]
        gv = g_ref[...]
        d_out[...], m_out[...], v_out[...] = _adamw(w_ref[...], gv, m_ref[...], v_ref[...])
        g_out[...] = gv

    lay = pl.BlockSpec((None, tr, cw), lambda i: (layer, i, 0))
    args = [w, g, m, v]
    in_specs = [lay, pl.BlockSpec((None, tr, cw), lambda i: (0, i, 0)), lay, lay]
    aliases = {}
    if into is not None:
        args += list(into)
        in_specs += [pl.BlockSpec(memory_space=pl.ANY)] * 4
        aliases = {4 + k: k for k in range(4)}
    return pl.pallas_call(
        body, name=name, grid=(r // tr,), in_specs=in_specs, out_specs=[lay] * 4,
        out_shape=[jax.ShapeDtypeStruct((nl, r, cw), F32)] * 4,
        input_output_aliases=aliases,
        compiler_params=_params("parallel"),
    )(*args)


def loss_head(y, target, name):
    t, d = y.shape
    tr = min(t, NORM_ROWS)
    n = t // tr

    def body(y_ref, t_ref, dy_ref, dyb_ref, loss_ref, acc):
        i = pl.program_id(0)
        err = y_ref[...] - t_ref[...]
        dy = err * (1.0 / d)
        dy_ref[...] = dy
        dyb_ref[...] = dy.astype(BF16)
        part = jnp.sum(err * err, axis=0, keepdims=True)

        @pl.when(i == 0)
        def _():
            acc[...] = part

        @pl.when(i != 0)
        def _():
            acc[...] += part

        @pl.when(i == n - 1)
        def _():
            loss_ref[...] = jnp.full((1, LANES), 0.5 / d, F32) * jnp.sum(acc[...])

    row = pl.BlockSpec((tr, d), lambda i: (i, 0))
    return pl.pallas_call(
        body, name=name, grid=(n,), in_specs=[row, row],
        out_specs=[row, row, pl.BlockSpec((1, LANES), lambda i: (0, 0))],
        out_shape=[jax.ShapeDtypeStruct((t, d), F32), jax.ShapeDtypeStruct((t, d), BF16),
                   jax.ShapeDtypeStruct((1, LANES), F32)],
        scratch_shapes=[pltpu.VMEM((1, d), F32)],
        compiler_params=_params("arbitrary"),
    )(y, target)


HBM_SPEC = pl.BlockSpec(memory_space=pltpu.HBM)
VMEM_SPEC = pl.BlockSpec(memory_space=pltpu.VMEM)


def _place():
    return lax.axis_index("x"), lax.axis_index("y"), lax.axis_index("c")


def _other_chips(x, y):
    return [(1 - x, y), (x, 1 - y), (1 - x, 1 - y)]


def _remote(src, dst, send_sem, recv_sem, dev):
    return pltpu.make_async_remote_copy(src_ref=src, dst_ref=dst, send_sem=send_sem, recv_sem=recv_sem,
                                        device_id=dev, device_id_type=MESH)


def _piece(refs, shard_shape, ax, j, half):
    w = shard_shape[ax]
    a, off = divmod(j * w, refs[0].shape[ax]) if isinstance(j, int) else (0, j * w)
    idx = [pl.ds(0, s) for s in shard_shape]
    idx[ax] = pl.ds(off, w)
    if half is not None:
        h0 = shard_shape[0] // 2
        idx[0] = pl.ds((off if ax == 0 else 0) + half * h0, h0)
    return refs[a].at[tuple(idx)]


def small_allreduce(v, name):
    n, r, l = v.shape
    assert n == N_DEV

    def body(v_ref, o_ref, recv, red, send1, recv1, send2, recv2):
        x, y, c = _place()
        me = 4 * x + 2 * y + c
        dev = lambda k: (k // 4, (k // 2) % 2, k % 2)
        firsts = []
        for o in range(1, N_DEV):
            tgt = (me + o) % N_DEV
            cp = _remote(v_ref.at[tgt], recv.at[me], send1.at[o], recv1.at[me], dev(tgt))
            cp.start()
            firsts.append(cp)
        recv[me] = v_ref[me]
        for o in range(1, N_DEV):
            src = (me + o) % N_DEV
            _remote(v_ref.at[src], recv.at[src], send1.at[o], recv1.at[src], dev(src)).wait_recv()
        acc = recv[0]
        for s in range(1, N_DEV):
            acc = acc + recv[s]
        red[...] = acc
        o_ref[me] = acc
        seconds = []
        for o in range(1, N_DEV):
            tgt = (me + o) % N_DEV
            cp = _remote(red, o_ref.at[me], send2.at[o], recv2.at[me], dev(tgt))
            cp.start()
            seconds.append(cp)
        for o in range(1, N_DEV):
            src = (me + o) % N_DEV
            _remote(red, o_ref.at[src], send2.at[o], recv2.at[src], dev(src)).wait_recv()
        for cp in firsts + seconds:
            cp.wait_send()

    sems = pltpu.SemaphoreType.DMA((N_DEV,))
    return pl.pallas_call(
        body, name=name, in_specs=[VMEM_SPEC], out_specs=VMEM_SPEC,
        out_shape=jax.ShapeDtypeStruct(v.shape, F32),
        scratch_shapes=[pltpu.VMEM((N_DEV, r, l), F32), pltpu.VMEM((r, l), F32), sems, sems, sems, sems],
        compiler_params=pltpu.CompilerParams(vmem_limit_bytes=VMEM_LIMIT),
    )(v)


def _me_scalar():
    return (2 * lax.axis_index("x") + lax.axis_index("y")).astype(jnp.int32).reshape(1)


def cast_into_gathered(wf, layer, axis, me1, name):
    _, r, cw = wf.shape
    tr = _row_tile(r, cw)
    nrb = r // tr
    full = (1, r * N_CHIPS, cw) if axis == 0 else (1, r, cw * N_CHIPS)
    omap = (lambda i, me: (0, me[0] * nrb + i, 0)) if axis == 0 else (lambda i, me: (0, i, me[0]))

    def body(me_ref, w_ref, o_ref):
        o_ref[...] = w_ref[...].astype(BF16)

    return pl.pallas_call(
        body, name=name,
        grid_spec=pltpu.PrefetchScalarGridSpec(
            num_scalar_prefetch=1, grid=(nrb,),
            in_specs=[pl.BlockSpec((None, tr, cw), lambda i, me: (layer, i, 0))],
            out_specs=pl.BlockSpec((None, tr, cw), omap)),
        out_shape=jax.ShapeDtypeStruct(full, BF16),
        compiler_params=_params("parallel"),
    )(me1, wf)


SEM_SPEC = pl.BlockSpec(memory_space=pltpu.SEMAPHORE)
SPLIT_COPY_PARAMS = pltpu.CompilerParams(has_side_effects=pltpu.SideEffectType.DATAFLOW_SIDE_EFFECTING)
TOKEN_SHAPE = (SUBLANES, LANES)


def _hbm(a):
    return pltpu.with_memory_space_constraint(a, pltpu.HBM)


def _gather_copies(refs, shapes, axes, send_sem, recv_sem):
    x, y, c = _place()
    me = 2 * x + y
    out = []
    for p, ref in enumerate(refs):
        place = lambda j: _piece([ref.at[0]], shapes[p], axes[p], j, c)
        for q, chip in enumerate(_other_chips(x, y)):
            dev = (chip[0], chip[1], c)
            sems = (send_sem.at[3 * p + q], recv_sem.at[3 * p + q])
            theirs = place(2 * chip[0] + chip[1])
            out.append((_remote(place(me), place(me), *sems, dev), _remote(theirs, theirs, *sems, dev)))
    return out


def gather_start(bufs, shapes, axes, after, name):
    n = len(bufs)

    def body(*refs):
        send_sem, recv_sem = refs[n + 1:n + 3]
        o_refs = refs[n + 3:2 * n + 3]
        token = refs[-1]
        for mine, _ in _gather_copies(o_refs, shapes, axes, send_sem, recv_sem):
            mine.start()
        token[...] = jnp.zeros(TOKEN_SHAPE, F32)

    sems = pltpu.SemaphoreType.DMA((3 * n,))
    outs = pl.pallas_call(
        body, name=name,
        in_specs=[HBM_SPEC] * n + [pl.BlockSpec(memory_space=pl.ANY)],
        out_specs=[SEM_SPEC, SEM_SPEC] + [HBM_SPEC] * n + [VMEM_SPEC],
        out_shape=[sems, sems] + [pltpu.HBM(b.shape, b.dtype) for b in bufs] + [jax.ShapeDtypeStruct(TOKEN_SHAPE, F32)],
        input_output_aliases={p: p + 2 for p in range(n)},
        compiler_params=SPLIT_COPY_PARAMS,
    )(*[_hbm(b) for b in bufs], after)
    return outs[0], outs[1], list(outs[2:2 + n]), outs[-1]


def gather_wait(send_sem, recv_sem, bufs, shapes, axes, after, name):
    n = len(bufs)

    def body(*refs):
        s_sem, r_sem = refs[n:n + 2]
        o_refs = refs[n + 3:]
        for mine, theirs in _gather_copies(o_refs, shapes, axes, s_sem, r_sem):
            mine.wait_send()
            theirs.wait_recv()

    return pl.pallas_call(
        body, name=name,
        in_specs=[HBM_SPEC] * n + [SEM_SPEC, SEM_SPEC, pl.BlockSpec(memory_space=pl.ANY)],
        out_specs=[HBM_SPEC] * n,
        out_shape=[pltpu.HBM(b.shape, b.dtype) for b in bufs],
        input_output_aliases={p: p for p in range(n)},
        compiler_params=SPLIT_COPY_PARAMS,
    )(*bufs, send_sem, recv_sem, after)


SIBLING_SLOTS = 2
SIBLING_BLOCK_BYTES = 8 * 1024 * 1024
SHARE_BLOCK_BYTES = 4 * 1024 * 1024


def _row_step(nrb):
    s = pl.program_id(0)
    for ax in range(1, len(nrb)):
        s = s * nrb[ax] + pl.program_id(ax)
    return s


def gather_forward(buf, axis, r, cw, me1, name):
    nl = buf.shape[0]
    h0 = r // 2
    tr = _row_tile(h0, cw, SIBLING_BLOCK_BYTES)
    nrb = h0 // tr
    peer = lambda q, me: (me[0] + q + 1) % N_CHIPS
    if axis == 1:
        view = buf.reshape(nl, 1, 2, h0, N_CHIPS * cw)
        spec = pl.BlockSpec((1, 1, 2, tr, cw), lambda l, q, i, me: (l, 0, 0, i, peer(q, me)))
    else:
        view = buf.reshape(nl, N_CHIPS, 2, h0, cw)
        spec = pl.BlockSpec((1, 1, 2, tr, cw), lambda l, q, i, me: (l, peer(q, me), 0, i, 0))

    def body(me_ref, in_ref, o_ref, rbuf, send_sem, recv_sem):
        x, y, c = _place()
        slot = _row_step((nl, N_CHIPS - 1, nrb)) % SIBLING_SLOTS
        cp = _remote(in_ref.at[0, 0, c], rbuf.at[slot], send_sem.at[slot], recv_sem.at[slot], (x, y, 1 - c))
        cp.start()
        o_ref[0, 0, c] = in_ref[0, 0, c]
        cp.wait_recv()
        o_ref[0, 0, 1 - c] = rbuf[slot]
        cp.wait_send()

    out = pl.pallas_call(
        body, name=name,
        grid_spec=pltpu.PrefetchScalarGridSpec(
            num_scalar_prefetch=1, grid=(nl, N_CHIPS - 1, nrb), in_specs=[spec], out_specs=spec,
            scratch_shapes=[pltpu.VMEM((SIBLING_SLOTS, tr, cw), buf.dtype),
                            pltpu.SemaphoreType.DMA((SIBLING_SLOTS,)), pltpu.SemaphoreType.DMA((SIBLING_SLOTS,))]),
        out_shape=jax.ShapeDtypeStruct(view.shape, view.dtype),
        input_output_aliases={1: 0},
        compiler_params=_params("arbitrary", "arbitrary", "arbitrary"),
    )(me1, view)
    return out.reshape(buf.shape)


def pair_reduce(g, axis, r, cw, name, into=None, first_slot=0):
    h0 = r // 2
    tr = _row_tile(h0, cw, SIBLING_BLOCK_BYTES)
    nrb = h0 // tr
    if axis == 1:
        n_sh = g.shape[1] // cw
        view = g.reshape(1, 2, h0, n_sh * cw)
        spec = pl.BlockSpec((1, 2, tr, cw), lambda j, i: (0, 0, i, j))
    else:
        n_sh = g.shape[0] // r
        view = g.reshape(n_sh, 2, h0, cw)
        spec = pl.BlockSpec((1, 2, tr, cw), lambda j, i: (j, 0, i, 0))

    def body(g_ref, *rest):
        o_ref, rbuf, send_sem, recv_sem = rest[-4:]
        x, y, c = _place()
        slot = _row_step((n_sh, nrb)) % SIBLING_SLOTS
        cp = _remote(g_ref.at[0, 1 - c], rbuf.at[slot], send_sem.at[slot], recv_sem.at[slot], (x, y, 1 - c))
        cp.start()
        mine = g_ref[0, c].astype(F32)
        cp.wait_recv()
        o_ref[0] = (mine + rbuf[slot].astype(F32)).astype(BF16)
        cp.wait_send()

    args, in_specs, aliases = [view], [spec], {}
    if into is not None:
        args.append(into)
        in_specs.append(pl.BlockSpec(memory_space=pl.ANY))
        aliases = {1: 0}
    return pl.pallas_call(
        body, name=name, grid=(n_sh, nrb), in_specs=in_specs,
        out_specs=pl.BlockSpec((1, tr, cw), lambda j, i: (j + first_slot, i, 0)),
        out_shape=jax.ShapeDtypeStruct((N_CHIPS, h0, cw), BF16),
        input_output_aliases=aliases,
        scratch_shapes=[pltpu.VMEM((SIBLING_SLOTS, tr, cw), BF16),
                        pltpu.SemaphoreType.DMA((SIBLING_SLOTS,)), pltpu.SemaphoreType.DMA((SIBLING_SLOTS,))],
        compiler_params=_params("arbitrary", "arbitrary"),
    )(*args)


def _chip_copies(h_refs, lb_refs, send_sem, recv_sem):
    x, y, c = _place()
    out = []
    for k, (h, lb) in enumerate(zip(h_refs, lb_refs)):
        for q, chip in enumerate(_other_chips(x, y)):
            out.append(_remote(h.at[2 * chip[0] + chip[1]], lb.at[q], send_sem.at[3 * k + q], recv_sem.at[3 * k + q],
                               (chip[0], chip[1], c)))
    return out


def chip_start(halves, name):
    n = len(halves)
    landed = [lax.empty((N_CHIPS - 1,) + h.shape[1:], h.dtype) for h in halves]

    def body(*refs):
        send_sem, recv_sem = refs[2 * n:2 * n + 2]
        h_refs, lb_refs = refs[2 * n + 2:3 * n + 2], refs[3 * n + 2:4 * n + 2]
        for cp in _chip_copies(h_refs, lb_refs, send_sem, recv_sem):
            cp.start()
        refs[-1][...] = jnp.zeros(TOKEN_SHAPE, F32)

    sems = pltpu.SemaphoreType.DMA((3 * n,))
    outs = pl.pallas_call(
        body, name=name,
        in_specs=[HBM_SPEC] * (2 * n),
        out_specs=[SEM_SPEC, SEM_SPEC] + [HBM_SPEC] * (2 * n) + [VMEM_SPEC],
        out_shape=[sems, sems] + [pltpu.HBM(a.shape, a.dtype) for a in halves + landed]
        + [jax.ShapeDtypeStruct(TOKEN_SHAPE, F32)],
        input_output_aliases={p: p + 2 for p in range(2 * n)},
        compiler_params=SPLIT_COPY_PARAMS,
    )(*[_hbm(a) for a in halves + landed])
    return outs[0], outs[1], list(outs[2:2 + n]), list(outs[2 + n:2 + 2 * n]), outs[-1]


def chip_wait(send_sem, recv_sem, halves, landed, after, name):
    n = len(halves)

    def body(*refs):
        s_sem, r_sem = refs[2 * n:2 * n + 2]
        h_refs, lb_refs = refs[2 * n + 3:3 * n + 3], refs[3 * n + 3:]
        for cp in _chip_copies(h_refs, lb_refs, s_sem, r_sem):
            cp.wait_send()
            cp.wait_recv()

    outs = pl.pallas_call(
        body, name=name,
        in_specs=[HBM_SPEC] * (2 * n) + [SEM_SPEC, SEM_SPEC, pl.BlockSpec(memory_space=pl.ANY)],
        out_specs=[HBM_SPEC] * (2 * n),
        out_shape=[pltpu.HBM(a.shape, a.dtype) for a in halves + landed],
        input_output_aliases={p: p for p in range(2 * n)},
        compiler_params=SPLIT_COPY_PARAMS,
    )(*halves, *landed, send_sem, recv_sem, after)
    return list(outs[:n]), list(outs[n:])


def reduce_share(half, landed, me1, name):
    _, h0, cw = half.shape
    tr = _row_tile(h0, cw, SHARE_BLOCK_BYTES)
    nrb = h0 // tr

    def body(me_ref, h_ref, l0, l1, l2, o_ref, sbuf, rbuf, send_sem, recv_sem):
        x, y, c = _place()
        slot = pl.program_id(0) % SIBLING_SLOTS
        total = ((h_ref[...].astype(F32) + l0[...].astype(F32)) + l1[...].astype(F32)) + l2[...].astype(F32)
        sbuf[slot] = total
        cp = _remote(sbuf.at[slot], rbuf.at[slot], send_sem.at[slot], recv_sem.at[slot], (x, y, 1 - c))
        cp.start()
        o_ref[0, c] = total
        cp.wait_recv()
        o_ref[0, 1 - c] = rbuf[slot]
        cp.wait_send()

    landed_spec = lambda q: pl.BlockSpec((None, tr, cw), lambda i, me: (q, i, 0))
    args = [me1, half, landed, landed, landed]
    in_specs = [pl.BlockSpec((None, tr, cw), lambda i, me: (me[0], i, 0))] + [landed_spec(q) for q in range(N_CHIPS - 1)]
    out = pl.pallas_call(
        body, name=name,
        grid_spec=pltpu.PrefetchScalarGridSpec(
            num_scalar_prefetch=1, grid=(nrb,), in_specs=in_specs,
            out_specs=pl.BlockSpec((1, 2, tr, cw), lambda i, me: (0, 0, i, 0)),
            scratch_shapes=[pltpu.VMEM((SIBLING_SLOTS, tr, cw), F32), pltpu.VMEM((SIBLING_SLOTS, tr, cw), F32),
                            pltpu.SemaphoreType.DMA((SIBLING_SLOTS,)), pltpu.SemaphoreType.DMA((SIBLING_SLOTS,))]),
        out_shape=jax.ShapeDtypeStruct((1, 2, h0, cw), F32),
        compiler_params=_params("arbitrary"),
    )(*args)
    return out.reshape(1, 2 * h0, cw)


WEIGHTS = ["norm_mix_g", "norm_ffn_g", "pool_w", "pool_b", "pool_scale", "sb_w_qkv", "sb_q_gain", "sb_k_gain",
           "sb_w_o", "ssm_lam_re", "ssm_lam_im", "ssm_log_step", "ssm_b_re", "ssm_b_im", "ssm_c_re", "ssm_c_im",
           "ssm_d", "ssm_w_glu", "ssm_b_glu", "ffn_w_up", "ffn_conv_w", "ffn_conv_b", "ffn_w_down"]
BIG = {"pool_w": 0, "sb_w_qkv": 1, "sb_w_o": 0, "ssm_w_glu": 1, "ffn_w_up": 1, "ffn_w_down": 0}
SMALL_SHARDED = {"pool_b": 1, "pool_scale": 1, "ssm_d": 1, "ssm_b_glu": 1, "ffn_conv_w": 2}
SMALL = [n for n in WEIGHTS if n not in BIG]
SMALL_PAD = N_DEV * SUBLANES * LANES
N_MIXERS = 3
REDUCE_LAG = 3


def _pack(arrays):
    flat = jnp.concatenate([a.reshape(-1).astype(F32) for a in arrays])
    total = -(-flat.shape[0] // SMALL_PAD) * SMALL_PAD
    flat = jnp.pad(flat, (0, total - flat.shape[0]))
    return flat.reshape(N_DEV, -1, LANES)


def _unpack(packed, like):
    flat = packed.reshape(-1)
    out, off = [], 0
    for a in like:
        out.append(flat[off:off + a.size].reshape(a.shape))
        off += a.size
    return out


def kernel(x, norm_mix_g, norm_ffn_g, pool_w, pool_b, pool_scale, sb_w_qkv, sb_q_gain, sb_k_gain, sb_w_o, ssm_lam_re, ssm_lam_im, ssm_log_step, ssm_b_re, ssm_b_im, ssm_c_re, ssm_c_im, ssm_d, ssm_w_glu, ssm_b_glu, ffn_w_up, ffn_conv_w, ffn_conv_b, ffn_w_down, loss_target, m_norm_mix_g, m_norm_ffn_g, m_pool_w, m_pool_b, m_pool_scale, m_sb_w_qkv, m_sb_q_gain, m_sb_k_gain, m_sb_w_o, m_ssm_lam_re, m_ssm_lam_im, m_ssm_log_step, m_ssm_b_re, m_ssm_b_im, m_ssm_c_re, m_ssm_c_im, m_ssm_d, m_ssm_w_glu, m_ssm_b_glu, m_ffn_w_up, m_ffn_conv_w, m_ffn_conv_b, m_ffn_w_down, v_norm_mix_g, v_norm_ffn_g, v_pool_w, v_pool_b, v_pool_scale, v_sb_w_qkv, v_sb_q_gain, v_sb_k_gain, v_sb_w_o, v_ssm_lam_re, v_ssm_lam_im, v_ssm_log_step, v_ssm_b_re, v_ssm_b_im, v_ssm_c_re, v_ssm_c_im, v_ssm_d, v_ssm_w_glu, v_ssm_b_glu, v_ffn_w_up, v_ffn_conv_w, v_ffn_conv_b, v_ffn_w_down):
    given = dict(locals())
    w = {n: given[n] for n in WEIGHTS}
    mom = {n: given["m_" + n] for n in WEIGHTS}
    var = {n: given["v_" + n] for n in WEIGHTS}
    pool_shape = pool_w.shape
    for group in (w, mom, var):
        group["pool_w"] = group["pool_w"].reshape(pool_shape[0], pool_shape[1] * pool_shape[2], pool_shape[3])
    xi, yi, ci = _place()
    me = 2 * xi + yi
    depth = norm_mix_g.shape[0]
    x_in = x[0]
    t, d = x_in.shape

    def placed(a, ax):
        shp = list(a.shape)
        shp[ax] *= N_CHIPS
        full = lax.dynamic_update_slice_in_dim(jnp.zeros(shp, F32), a, me * a.shape[ax], ax)
        return jnp.where(ci == 0, full, 0.0)

    sharded_full = [placed(w[n], ax) for n, ax in SMALL_SHARDED.items()]

    big = list(BIG)
    me1 = _me_scalar()
    shard = {n: tuple(w[n].shape[1:]) for n in big}
    vec = lambda a, i: a[i:i + 1]
    tie = lambda v, token: v + token[0, 0]

    def layer_weights(i):
        kind, j = i % N_MIXERS, i // N_MIXERS
        mixer = {0: [("pool_w", j)], 1: [("sb_w_qkv", j), ("sb_w_o", j)], 2: [("ssm_w_glu", j)]}[kind]
        return mixer + [("ffn_w_up", i), ("ffn_w_down", i)]

    def pool_matrices(j):
        return gathered["pool_w", j].reshape((1, N_CHIPS) + tuple(pool_shape[1:]))

    started, token = [], x_in
    for i in range(depth):
        keys = layer_weights(i)
        bufs = [cast_into_gathered(w[n], l, BIG[n], me1, f"cast_{n}{l}") for n, l in keys]
        send_sem, recv_sem, bufs, token = gather_start(bufs, [shard[n] for n, _ in keys], [BIG[n] for n, _ in keys],
                                                       token, f"gather_start{i}")
        started.append((keys, send_sem, recv_sem, bufs))
        if i == 0:
            token = small_allreduce(tie(_pack(sharded_full), token), "gather_vectors")
            whole = dict(zip(SMALL_SHARDED, _unpack(token, sharded_full)))
    gathered = {}

    saved = []
    xc = x_in
    for i in range(depth):
        kind, j = i % N_MIXERS, i // N_MIXERS
        keys, send_sem, recv_sem, bufs = started[i]
        bufs = gather_wait(send_sem, recv_sem, bufs, [shard[n] for n, _ in keys], [BIG[n] for n, _ in keys],
                           token if i == 0 else xc, f"gather_wait{i}")
        for (n, l), b in zip(keys, bufs):
            gathered[n, l] = gather_forward(b, BIG[n], *shard[n], me1, f"gather_forward_{n}{l}")
        s = {"x_in": xc}
        g_mix = vec(norm_mix_g, i)
        if kind == 0:
            (h,) = rmsnorm_fwd(xc, g_mix, [F32], f"norm_mix{i}")
            x_mid = pool_fwd(h, xc, pool_matrices(j), vec(whole["pool_b"], j), vec(whole["pool_scale"], j),
                             f"pool_fwd{i}", lead=0)
        elif kind == 1:
            (h,) = rmsnorm_fwd(xc, g_mix, [BF16], f"norm_mix{i}")
            s["qkv"] = mm_cols(h, gathered["sb_w_qkv", j], out_dtype=BF16, name=f"sb_qkv{i}", lead=0)
            s["o"], s["ltot"] = sb_attn_fwd(s["qkv"], vec(sb_q_gain, j), vec(sb_k_gain, j), f"sb_attn_fwd{i}")
            x_mid = mm_cols(s["o"], gathered["sb_w_o", j], out_dtype=F32, name=f"sb_out{i}", resid=xc, lead=0)
        else:
            (h,) = rmsnorm_fwd(xc, g_mix, [F32], f"norm_mix{i}")
            prm = tuple(w[n][j] for n in ("ssm_lam_re", "ssm_lam_im", "ssm_log_step", "ssm_b_re", "ssm_b_im",
                                          "ssm_c_re", "ssm_c_im"))
            prep, s["prep_vjp"] = jax.vjp(ssm_prepare, *prm)
            s["prep"] = tuple(a.astype(BF16) for a in prep[:4]) + tuple(prep[4:])
            s["y"], s["yg"] = ssm_core_fwd(h, *s["prep"], vec(whole["ssm_d"], j), f"ssm_fwd{i}")
            s["val"], s["gate"], x_mid = glu_fwd(s["yg"], gathered["ssm_w_glu", j], vec(whole["ssm_b_glu"], j), xc,
                                                 f"ssm_glu{i}", lead=0)
        s["x_mid"], s["h"] = x_mid, h
        (h2,) = rmsnorm_fwd(x_mid, vec(norm_ffn_g, i), [BF16], f"norm_ffn{i}")
        s["h2"] = h2
        s["up_val"], s["up_gate"], s["act"] = ffn_up_fused(h2, gathered["ffn_w_up", i], whole["ffn_conv_w"][i],
                                                           vec(ffn_conv_b, i), f"ffn_up{i}", lead=0)
        xc = mm_k([s["act"]], gathered["ffn_w_down", i], b_nt=False, name=f"ffn_down{i}", resid=x_mid, lead=0)
        saved.append(s)

    dx, dxb, loss_part = loss_head(xc, loss_target[0], "loss_head")
    loss = lax.psum(loss_part[0, 0], ("x", "y", "c"))

    small = {n: [None] * w[n].shape[0] for n in SMALL}
    big_g, updated = {}, {}

    def finish_reduction(pending, after):
        layer, keys, send_sem, recv_sem, halves, landed, _ = pending
        halves, landed = chip_wait(send_sem, recv_sem, halves, landed, after, f"grads_chip_wait{layer}")
        for (n, l), h, lb in zip(keys, halves, landed):
            g = reduce_share(h, lb, me1, f"grads_share_{n}{l}")
            updated[n] = adamw_layer(w[n], g, mom[n], var[n], l, f"adamw_{n}{l}", into=updated.get(n))

    pending, travelling = None, []
    for i in reversed(range(depth)):
        kind, j = i % N_MIXERS, i // N_MIXERS
        s = saved[i]
        g_ffn, g_mix = vec(norm_ffn_g, i), vec(norm_mix_g, i)
        cw, cb = whole["ffn_conv_w"][i], vec(ffn_conv_b, i)
        if pending is not None:
            cb = tie(cb, pending[-1])
        h, h2 = s["h"], s["h2"]
        dupv, dupg, dcwv, dcwg, dcbv, dcbg = ffn_bwd_fused(dxb, gathered["ffn_w_down", i], s["up_val"], s["up_gate"], cw, cb,
                                                           f"ffn_bwd{i}", lead=0)
        big_g["ffn_w_down", i] = [mm_rows(s["act"], dxb, out_dtype=BF16, name=f"ffn_dwdown{i}")]
        big_g["ffn_w_up", i] = [mm_cols(h2, dupv, a_contract=0, out_dtype=BF16, name=f"ffn_dwup_val{i}"),
                                mm_cols(h2, dupg, a_contract=0, out_dtype=BF16, name=f"ffn_dwup_gate{i}")]
        dh2 = mm_k([dupv, dupg], gathered["ffn_w_up", i], b_nt=True, name=f"ffn_dh{i}", lead=0)
        dx_mid, dxb_mid, small["norm_ffn_g"][i] = rmsnorm_bwd(s["x_mid"], g_ffn, dh2, dx, f"norm_ffn_bwd{i}")
        small["ffn_conv_w"][i] = jnp.concatenate([dcwv, dcwg], axis=1)[None]
        small["ffn_conv_b"][i] = jnp.concatenate([dcbv, dcbg], axis=1)

        if kind == 0:
            dh, dwp, small["pool_b"][j], small["pool_scale"][j] = pool_bwd(
                dx_mid, h, pool_matrices(j), vec(whole["pool_b"], j), vec(whole["pool_scale"], j), f"pool_bwd{i}", lead=0)
            big_g["pool_w", j] = [dwp.reshape(-1, dwp.shape[-1])]
        elif kind == 1:
            do = mm_cols(dxb_mid, gathered["sb_w_o", j], b_nt=True, out_dtype=BF16, name=f"sb_do{i}", lead=0)
            big_g["sb_w_o", j] = [mm_cols(s["o"], dxb_mid, a_contract=0, out_dtype=BF16, name=f"sb_dwo{i}")]
            dq, dk, dv, small["sb_q_gain"][j], small["sb_k_gain"][j] = sb_attn_bwd(
                s["qkv"], s["ltot"], do, vec(sb_q_gain, j), vec(sb_k_gain, j), f"sb_attn_bwd{i}")
            dqkv = jnp.concatenate([dq, dk, dv], axis=1)
            big_g["sb_w_qkv", j] = [mm_cols(h, dqkv, a_contract=0, out_dtype=BF16, name=f"sb_dwqkv{i}")]
            dh = mm_k([dqkv], gathered["sb_w_qkv", j], b_nt=True, name=f"sb_dh{i}", lead=0)
        else:
            dval, dgate, dbv, dbg = glu_bwd(dx_mid, s["val"], s["gate"], f"ssm_glu_bwd{i}")
            small["ssm_b_glu"][j] = jnp.concatenate([dbv, dbg], axis=1)
            big_g["ssm_w_glu", j] = [mm_cols(s["yg"], dval, a_contract=0, out_dtype=BF16, name=f"ssm_dwglu_val{i}"),
                                     mm_cols(s["yg"], dgate, a_contract=0, out_dtype=BF16, name=f"ssm_dwglu_gate{i}")]
            dyg = mm_k([dval, dgate], gathered["ssm_w_glu", j], b_nt=True, name=f"ssm_dyg{i}", lead=0)
            dh, small["ssm_d"][j], *dprep = ssm_core_bwd(h, s["y"], dyg, *s["prep"], vec(whole["ssm_d"], j), f"ssm_bwd{i}")
            dprm = s["prep_vjp"](tuple(dprep))
            for n, g in zip(("ssm_lam_re", "ssm_lam_im", "ssm_log_step", "ssm_b_re", "ssm_b_im", "ssm_c_re", "ssm_c_im"),
                            dprm):
                small[n][j] = g[None]
        dx, dxb, small["norm_mix_g"][i] = rmsnorm_bwd(s["x_in"], g_mix, dh, dx_mid, f"norm_mix_bwd{i}")

        keys, halves = layer_weights(i), []
        for n, l in keys:
            h = None
            for a, g in enumerate(big_g[n, l]):
                h = pair_reduce(g, BIG[n], *shard[n], f"grads_pair_{n}{l}_{a}", into=h,
                                first_slot=a * (N_CHIPS // len(big_g[n, l])))
            halves.append(h)
        pending = (i, keys) + tuple(chip_start(halves, f"grads_chip_start{i}"))
        travelling.append(pending)
        if len(travelling) > REDUCE_LAG:
            finish_reduction(travelling.pop(0), dx)

    small_full = [jnp.concatenate(small[n], axis=0) for n in SMALL]
    small_sum = dict(zip(SMALL, _unpack(small_allreduce(_pack(small_full), "reduce_vectors"), small_full)))
    grads = {}
    for n in SMALL:
        g = small_sum[n]
        if n in SMALL_SHARDED:
            ax = SMALL_SHARDED[n]
            g = lax.dynamic_slice_in_dim(g, me * w[n].shape[ax], w[n].shape[ax], ax)
        grads[n] = g
    delta, new_m, new_v = {}, {}, {}
    like = [w[n] for n in SMALL]
    packed = [_pack([src[n] for n in SMALL]).reshape(-1, LANES) for src in (w, grads, mom, var)]
    vector_updates = adamw(*packed, "adamw_vectors")
    for dst, out in zip((delta, new_m, new_v), vector_updates):
        dst.update(zip(SMALL, _unpack(out, like)))

    after = vector_updates[0]
    for pending in travelling:
        finish_reduction(pending, after)
        after = updated[pending[1][0][0]][0]
    for n in big:
        grads[n], delta[n], new_m[n], new_v[n] = updated[n]
    for group in (grads, delta, new_m, new_v):
        group["pool_w"] = group["pool_w"].reshape(pool_shape)

    return (loss, dx[None], *[grads[n] for n in WEIGHTS], *[delta[n] for n in WEIGHTS],
            *[new_m[n] for n in WEIGHTS], *[new_v[n] for n in WEIGHTS])
```

```python
import functools
import math

import jax
import jax.numpy as jnp
from jax import lax
from jax.experimental import pallas as pl
from jax.experimental.pallas import tpu as pltpu

F32 = jnp.float32
BF16 = jnp.bfloat16

RMS_EPS = 1e-6
POOL_WINDOWS = (2, 4, 8, 16)
SB_HEAD_DIM = 128
SSM_GROUP_CH = 16
SSM_STATE = 64
SSM_BLOCK_GROUPS = 8
ADAM_LR = 0.001
ADAM_B1 = 0.9
ADAM_B2 = 0.999
ADAM_EPS = 1e-08
ADAM_WD = 0.01
ADAM_STEP = 10

V7X_VMEM_BYTES = 64 * 1024 * 1024
VMEM_LIMIT = V7X_VMEM_BYTES - 8 * 1024 * 1024
SUBLANES = 8
LANES = 128
NORM_ROWS = 256
MESH = pl.DeviceIdType.MESH
N_CHIPS = 4
N_DEV = 8


def _params(*sem):
    return pltpu.CompilerParams(dimension_semantics=tuple(sem) if sem else None, vmem_limit_bytes=VMEM_LIMIT)


def _tile(n, want):
    if n <= want:
        return n
    t = (want // LANES) * LANES
    while t > LANES and n % t:
        t -= LANES
    assert n % t == 0, (n, want)
    return t


def _spec(shape, imap, lead=None):
    if lead is None:
        return pl.BlockSpec(tuple(shape), imap)
    return pl.BlockSpec((None,) + tuple(shape), lambda *a: (lead,) + tuple(imap(*a)))


def _sigmoid(v):
    return 1.0 / (1.0 + jnp.exp(-v))


def _shift_down(v, k):
    return pltpu.roll(v, k, 0)


def _shift_up(v, k):
    return pltpu.roll(v, v.shape[0] - k, 0)


def rmsnorm_fwd(x, g, out_dtypes, name):
    t, d = x.shape
    tr = min(t, NORM_ROWS)

    def body(x_ref, g_ref, *o_refs):
        xv = x_ref[...]
        r = lax.rsqrt(jnp.mean(xv * xv, axis=-1, keepdims=True) + RMS_EPS)
        h = xv * r * g_ref[...]
        for o in o_refs:
            o[...] = h.astype(o.dtype)

    outs = pl.pallas_call(
        body, name=name, grid=(t // tr,),
        in_specs=[pl.BlockSpec((tr, d), lambda i: (i, 0)), pl.BlockSpec((1, d), lambda i: (0, 0))],
        out_specs=[pl.BlockSpec((tr, d), lambda i: (i, 0)) for _ in out_dtypes],
        out_shape=[jax.ShapeDtypeStruct((t, d), dt) for dt in out_dtypes],
        compiler_params=_params("parallel"),
    )(x, g)
    return outs


def rmsnorm_bwd(x, g, dh, dres, name):
    t, d = x.shape
    tr = min(t, NORM_ROWS)

    def body(x_ref, g_ref, dh_ref, dres_ref, dx_ref, dxb_ref, dg_ref):
        xv = x_ref[...]
        r = lax.rsqrt(jnp.mean(xv * xv, axis=-1, keepdims=True) + RMS_EPS)
        xhat = xv * r
        dhv = dh_ref[...]
        dxhat = dhv * g_ref[...]
        dx = dres_ref[...] + r * (dxhat - xhat * jnp.mean(dxhat * xhat, axis=-1, keepdims=True))
        dx_ref[...] = dx
        dxb_ref[...] = dx.astype(BF16)
        part = jnp.sum(dhv * xhat, axis=0, keepdims=True)

        @pl.when(pl.program_id(0) == 0)
        def _():
            dg_ref[...] = part

        @pl.when(pl.program_id(0) != 0)
        def _():
            dg_ref[...] += part

    row = pl.BlockSpec((tr, d), lambda i: (i, 0))
    vec = pl.BlockSpec((1, d), lambda i: (0, 0))
    return pl.pallas_call(
        body, name=name, grid=(t // tr,),
        in_specs=[row, vec, row, row],
        out_specs=[row, row, vec],
        out_shape=[jax.ShapeDtypeStruct((t, d), F32), jax.ShapeDtypeStruct((t, d), BF16),
                   jax.ShapeDtypeStruct((1, d), F32)],
        compiler_params=_params("arbitrary"),
    )(x, g, dh, dres)


def mm_cols(a, b, *, a_contract=1, b_nt=False, out_dtype, name, resid=None, tn=512, lead=None):
    m = a.shape[1 - a_contract]
    k = a.shape[a_contract]
    n = b.shape[-2] if b_nt else b.shape[-1]
    assert (b.shape[-1] if b_nt else b.shape[-2]) == k
    tn = _tile(n, tn)

    def body(a_ref, b_ref, *rest):
        o_ref = rest[-1]
        dn = (((a_contract,), (1 if b_nt else 0,)), ((), ()))
        acc = lax.dot_general(a_ref[...], b_ref[...], dn, preferred_element_type=F32)
        if resid is not None:
            acc = acc + rest[0][...]
        o_ref[...] = acc.astype(o_ref.dtype)

    in_specs = [pl.BlockSpec(a.shape, lambda j: (0, 0)),
                _spec((tn, k), lambda j: (j, 0), lead) if b_nt else _spec((k, tn), lambda j: (0, j), lead)]
    args = [a, b]
    if resid is not None:
        in_specs.append(pl.BlockSpec((m, tn), lambda j: (0, j)))
        args.append(resid)
    return pl.pallas_call(
        body, name=name, grid=(n // tn,), in_specs=in_specs,
        out_specs=pl.BlockSpec((m, tn), lambda j: (0, j)),
        out_shape=jax.ShapeDtypeStruct((m, n), out_dtype),
        compiler_params=_params("parallel"),
    )(*args)


def mm_rows(st, res, *, out_dtype, name, tn=512):
    k, m = st.shape
    n = res.shape[1]
    assert res.shape[0] == k
    tn = _tile(m, tn)

    def body(st_ref, res_ref, o_ref):
        o_ref[...] = lax.dot_general(st_ref[...], res_ref[...], (((0,), (0,)), ((), ())),
                                     preferred_element_type=F32).astype(o_ref.dtype)

    return pl.pallas_call(
        body, name=name, grid=(m // tn,),
        in_specs=[pl.BlockSpec((k, tn), lambda j: (0, j)), pl.BlockSpec((k, n), lambda j: (0, 0))],
        out_specs=pl.BlockSpec((tn, n), lambda j: (j, 0)),
        out_shape=jax.ShapeDtypeStruct((m, n), out_dtype),
        compiler_params=_params("parallel"),
    )(st, res)


def mm_k(a_list, b, *, b_nt, name, resid=None, tk=512, tnn=1024, lead=None):
    m = a_list[0].shape[0]
    ks = [a.shape[1] for a in a_list]
    ktot = sum(ks)
    n = b.shape[-2] if b_nt else b.shape[-1]
    assert (b.shape[-1] if b_nt else b.shape[-2]) == ktot
    tk = _tile(ks[0], tk)
    assert all(kk % tk == 0 for kk in ks)
    tnn = _tile(n, tnn)
    nks = [kk // tk for kk in ks]
    starts = [sum(nks[:i]) for i in range(len(nks))]
    nk = sum(nks)

    def body(*refs):
        a_refs = refs[:len(a_list)]
        b_ref = refs[len(a_list)]
        o_ref = refs[-1]
        kk = pl.program_id(1)

        @pl.when(kk == 0)
        def _():
            if resid is not None:
                o_ref[...] = refs[len(a_list) + 1][...]
            else:
                o_ref[...] = jnp.zeros_like(o_ref)

        dn = (((1,), (1 if b_nt else 0,)), ((), ()))
        for i, a_ref in enumerate(a_refs):
            @pl.when(jnp.logical_and(kk >= starts[i], kk < starts[i] + nks[i]))
            def _(a_ref=a_ref):
                o_ref[...] += lax.dot_general(a_ref[...], b_ref[...], dn, preferred_element_type=F32)

    def a_spec(i):
        return pl.BlockSpec((m, tk), lambda nn, kk: (0, jnp.clip(kk - starts[i], 0, nks[i] - 1)))

    in_specs = [a_spec(i) for i in range(len(a_list))]
    in_specs.append(_spec((tnn, tk), lambda nn, kk: (nn, kk), lead) if b_nt
                    else _spec((tk, tnn), lambda nn, kk: (kk, nn), lead))
    args = list(a_list) + [b]
    if resid is not None:
        in_specs.append(pl.BlockSpec((m, tnn), lambda nn, kk: (0, nn)))
        args.append(resid)
    return pl.pallas_call(
        body, name=name, grid=(n // tnn, nk), in_specs=in_specs,
        out_specs=pl.BlockSpec((m, tnn), lambda nn, kk: (0, nn)),
        out_shape=jax.ShapeDtypeStruct((m, n), F32),
        compiler_params=_params("parallel", "arbitrary"),
    )(*args)


HALO = SUBLANES
CHUNK_ROWS = 64


def _conv_taps(ext, r):
    return ext[HALO:], _shift_down(ext, 1)[HALO:], _shift_down(ext, 2)[HALO:]


def ffn_up_fused(h, w_up, conv_w, conv_b, name, lead=None):
    t, d = h.shape
    f = w_up.shape[-1] // 2
    tn = _tile(f, 256)
    nf = f // tn
    r = min(CHUNK_ROWS, t)

    def body(h_ref, wv_ref, wg_ref, cwv_ref, cwg_ref, cbv_ref, cbg_ref, uv_ref, ug_ref, act_ref, sv, sg):
        zero = jnp.zeros((HALO, tn), F32)
        sv[0:HALO, :] = zero
        sg[0:HALO, :] = zero
        hv = h_ref[...]
        sv[HALO:, :] = jnp.dot(hv, wv_ref[...], preferred_element_type=F32).astype(BF16).astype(F32)
        sg[HALO:, :] = jnp.dot(hv, wg_ref[...], preferred_element_type=F32).astype(BF16).astype(F32)
        cwv, cwg = cwv_ref[...], cwg_ref[...]
        cbv, cbg = cbv_ref[...], cbg_ref[...]

        def chunk(i, carry):
            r0 = pl.multiple_of(i * r, r)
            v0, v1, v2 = _conv_taps(sv[pl.ds(r0, r + HALO), :], r)
            g0, g1, g2 = _conv_taps(sg[pl.ds(r0, r + HALO), :], r)
            cval = cbv + cwv[2:3] * v0 + cwv[1:2] * v1 + cwv[0:1] * v2
            cgate = cbg + cwg[2:3] * g0 + cwg[1:2] * g1 + cwg[0:1] * g2
            uv_ref[pl.ds(r0, r), :] = v0.astype(BF16)
            ug_ref[pl.ds(r0, r), :] = g0.astype(BF16)
            act_ref[pl.ds(r0, r), :] = (cgate * _sigmoid(cgate) * cval).astype(BF16)
            return carry

        lax.fori_loop(0, t // r, chunk, 0)

    col = lambda off: _spec((d, tn), lambda j: (0, j + off), lead)
    cw = lambda off: pl.BlockSpec((3, tn), lambda j: (0, j + off))
    cb = lambda off: pl.BlockSpec((1, tn), lambda j: (0, j + off))
    out = pl.BlockSpec((t, tn), lambda j: (0, j))
    return pl.pallas_call(
        body, name=name, grid=(nf,),
        in_specs=[pl.BlockSpec((t, d), lambda j: (0, 0)), col(0), col(nf), cw(0), cw(nf), cb(0), cb(nf)],
        out_specs=[out, out, out],
        out_shape=[jax.ShapeDtypeStruct((t, f), BF16)] * 3,
        scratch_shapes=[pltpu.VMEM((t + HALO, tn), F32), pltpu.VMEM((t + HALO, tn), F32)],
        compiler_params=_params("parallel"),
    )(h, w_up, w_up, conv_w, conv_w, conv_b, conv_b)


def ffn_bwd_fused(dout, w_down, up_val, up_gate, conv_w, conv_b, name, lead=None):
    t, d = dout.shape
    f = w_down.shape[-2]
    tn = _tile(f, 256)
    nf = f // tn
    r = min(CHUNK_ROWS, t)

    def body(do_ref, wd_ref, uv_ref, ug_ref, cwv_ref, cwg_ref, cbv_ref, cbg_ref,
             dv_ref, dg_ref, dcwv_ref, dcwg_ref, dcbv_ref, dcbg_ref, da, sv, sg, ev, eg):
        zero = jnp.zeros((HALO, tn), F32)
        sv[0:HALO, :] = zero
        sg[0:HALO, :] = zero
        ev[t:, :] = zero
        eg[t:, :] = zero
        da[...] = lax.dot_general(do_ref[...], wd_ref[...], (((1,), (1,)), ((), ())), preferred_element_type=F32)
        sv[HALO:, :] = uv_ref[...].astype(F32)
        sg[HALO:, :] = ug_ref[...].astype(F32)
        cwv, cwg = cwv_ref[...], cwg_ref[...]
        cbv, cbg = cbv_ref[...], cbg_ref[...]

        def chunk(i, acc):
            r0 = pl.multiple_of(i * r, r)
            v = _conv_taps(sv[pl.ds(r0, r + HALO), :], r)
            g = _conv_taps(sg[pl.ds(r0, r + HALO), :], r)
            cval = cbv + cwv[2:3] * v[0] + cwv[1:2] * v[1] + cwv[0:1] * v[2]
            cgate = cbg + cwg[2:3] * g[0] + cwg[1:2] * g[1] + cwg[0:1] * g[2]
            s = _sigmoid(cgate)
            dav = da[pl.ds(r0, r), :]
            dval = dav * (cgate * s)
            dgate = dav * cval * (s * (1.0 + cgate * (1.0 - s)))
            ev[pl.ds(r0, r), :] = dval
            eg[pl.ds(r0, r), :] = dgate
            col = lambda z: jnp.sum(z, axis=0, keepdims=True)
            new = [acc[0] + col(dval), acc[1] + col(dgate)]
            new += [acc[2 + j] + col(dval * v[2 - j]) for j in range(3)]
            new += [acc[5 + j] + col(dgate * g[2 - j]) for j in range(3)]
            return tuple(new)

        z1 = jnp.zeros((1, tn), F32)
        acc = lax.fori_loop(0, t // r, chunk, (z1,) * 8)
        dcbv_ref[...] = acc[0]
        dcbg_ref[...] = acc[1]
        for j in range(3):
            dcwv_ref[j:j + 1, :] = acc[2 + j]
            dcwg_ref[j:j + 1, :] = acc[5 + j]

        def chunk2(i, carry):
            r0 = pl.multiple_of(i * r, r)
            for e_ref, cw_, o_ref in ((ev, cwv, dv_ref), (eg, cwg, dg_ref)):
                ext = e_ref[pl.ds(r0, r + HALO), :]
                d0, d1, d2 = ext[:r], _shift_up(ext, 1)[:r], _shift_up(ext, 2)[:r]
                o_ref[pl.ds(r0, r), :] = (cw_[2:3] * d0 + cw_[1:2] * d1 + cw_[0:1] * d2).astype(BF16)
            return carry

        lax.fori_loop(0, t // r, chunk2, 0)

    cw = lambda off: pl.BlockSpec((3, tn), lambda j: (0, j + off))
    cb = lambda off: pl.BlockSpec((1, tn), lambda j: (0, j + off))
    tile = pl.BlockSpec((t, tn), lambda j: (0, j))
    s = lambda rows, dt: jax.ShapeDtypeStruct((rows, f), dt)
    halo = pltpu.VMEM((t + HALO, tn), F32)
    return pl.pallas_call(
        body, name=name, grid=(nf,),
        in_specs=[pl.BlockSpec((t, d), lambda j: (0, 0)), _spec((tn, d), lambda j: (j, 0), lead),
                  tile, tile, cw(0), cw(nf), cb(0), cb(nf)],
        out_specs=[tile, tile, pl.BlockSpec((3, tn), lambda j: (0, j)), pl.BlockSpec((3, tn), lambda j: (0, j)),
                   pl.BlockSpec((1, tn), lambda j: (0, j)), pl.BlockSpec((1, tn), lambda j: (0, j))],
        out_shape=[s(t, BF16), s(t, BF16), s(3, F32), s(3, F32), s(1, F32), s(1, F32)],
        scratch_shapes=[pltpu.VMEM((t, tn), F32), halo, halo, halo, halo],
        compiler_params=_params("parallel"),
    )(dout, w_down, up_val, up_gate, conv_w, conv_w, conv_b, conv_b)


POOL_PAD = max(POOL_WINDOWS)


def _window_sum(ext, win, shift):
    assert POOL_WINDOWS == (2, 4, 8, 16)
    s2 = ext + shift(ext, 1)
    s4 = s2 + shift(s2, 2)
    s8 = s4 + shift(s4, 4)
    s16 = s8 + shift(s8, 8)
    return jnp.where(win == 2, s2, jnp.where(win == 4, s4, jnp.where(win == 8, s8, s16)))


def _pool_win_scalar(g):
    win = jnp.int32(POOL_WINDOWS[-1])
    for k in range(len(POOL_WINDOWS) - 2, -1, -1):
        win = jnp.where(g == k, jnp.int32(POOL_WINDOWS[k]), win)
    return win


def _pool_count(r0, r, win):
    rows = r0 + lax.broadcasted_iota(jnp.int32, (r, 1), 0)
    return jnp.minimum(rows + 1, win).astype(F32)


def _pooled_into(hp, pooled, h_ref, t, r, win):
    hp[0:POOL_PAD, :] = jnp.zeros((POOL_PAD, hp.shape[1]), F32)
    hp[POOL_PAD:, :] = h_ref[...]

    def chunk(i, carry):
        r0 = pl.multiple_of(i * r, r)
        ext = hp[pl.ds(r0, r + POOL_PAD), :]
        s = _window_sum(ext, win, _shift_down)[POOL_PAD:]
        pooled[pl.ds(r0, r), :] = (s / _pool_count(r0, r, win) - ext[POOL_PAD:]).astype(BF16)
        return carry

    lax.fori_loop(0, t // r, chunk, 0)


def pool_fwd(h, x, w, b, scale, name, lead=None):
    t, d = h.shape
    ng, rows, dg = w.shape[-3:]
    r = min(CHUNK_ROWS, t)

    def body(h_ref, x_ref, w_ref, b_ref, s_ref, o_ref, hp, pooled):
        win = _pool_win_scalar(pl.program_id(0))
        _pooled_into(hp, pooled, h_ref, t, r, win)
        y = jnp.dot(pooled[...], w_ref[...].reshape(dg, dg), preferred_element_type=F32)
        o_ref[...] = x_ref[...] + (y + b_ref[...]) * s_ref[...]

    col = pl.BlockSpec((t, dg), lambda g: (0, g))
    vec = pl.BlockSpec((1, dg), lambda g: (0, g))
    return pl.pallas_call(
        body, name=name, grid=(ng,),
        in_specs=[col, col, _spec((N_CHIPS, None, rows, dg), lambda g: (0, g, 0, 0), lead), vec, vec],
        out_specs=col, out_shape=jax.ShapeDtypeStruct((t, d), F32),
        scratch_shapes=[pltpu.VMEM((t + POOL_PAD, dg), F32), pltpu.VMEM((t, dg), BF16)],
        compiler_params=_params("parallel"),
    )(h, x, w, b, scale)


def pool_bwd(dm, h, w, b, scale, name, lead=None):
    t, d = h.shape
    ng, rows, dg = w.shape[-3:]
    r = min(CHUNK_ROWS, t)

    def body(dm_ref, h_ref, w_ref, b_ref, s_ref, dh_ref, dw_ref, db_ref, ds_ref, hp, pooled, q):
        win = _pool_win_scalar(pl.program_id(0))
        _pooled_into(hp, pooled, h_ref, t, r, win)
        wv = w_ref[...].reshape(dg, dg)
        y = jnp.dot(pooled[...], wv, preferred_element_type=F32)
        dmv = dm_ref[...]
        ds_ref[...] = jnp.sum(dmv * (y + b_ref[...]), axis=0, keepdims=True)
        dy = dmv * s_ref[...]
        db_ref[...] = jnp.sum(dy, axis=0, keepdims=True)
        dyb = dy.astype(BF16)
        dw_ref[...] = lax.dot_general(pooled[...], dyb, (((0,), (0,)), ((), ())),
                                      preferred_element_type=F32).astype(BF16).reshape(N_CHIPS, rows, dg)
        dp = lax.dot_general(dyb, wv, (((1,), (1,)), ((), ())), preferred_element_type=F32)
        q[t:, :] = jnp.zeros((POOL_PAD, dg), F32)
        q[0:t, :] = dp / _pool_count(0, t, win)
        dh_ref[...] = -dp

        def chunk(i, carry):
            r0 = pl.multiple_of(i * r, r)
            ext = q[pl.ds(r0, r + POOL_PAD), :]
            dh_ref[pl.ds(r0, r), :] += _window_sum(ext, win, _shift_up)[:r]
            return carry

        lax.fori_loop(0, t // r, chunk, 0)

    col = pl.BlockSpec((t, dg), lambda g: (0, g))
    vec = pl.BlockSpec((1, dg), lambda g: (0, g))
    wshape = (N_CHIPS, None, rows, dg)
    wmap = lambda g: (0, g, 0, 0)
    return pl.pallas_call(
        body, name=name, grid=(ng,),
        in_specs=[col, col, _spec(wshape, wmap, lead), vec, vec],
        out_specs=[col, pl.BlockSpec(wshape, wmap), vec, vec],
        out_shape=[jax.ShapeDtypeStruct((t, d), F32), jax.ShapeDtypeStruct((N_CHIPS, ng, rows, dg), BF16),
                   jax.ShapeDtypeStruct((1, d), F32), jax.ShapeDtypeStruct((1, d), F32)],
        scratch_shapes=[pltpu.VMEM((t + POOL_PAD, dg), F32), pltpu.VMEM((t, dg), BF16),
                        pltpu.VMEM((t + POOL_PAD, dg), F32)],
        compiler_params=_params("parallel"),
    )(dm, h, w, b, scale)


SB_BLOCK = 256
SB_QUERY_ROWS = 128


def _tri_sum(v, tri):
    hi = v.astype(BF16)
    lo = (v - hi.astype(F32)).astype(BF16)
    dot = lambda p: jnp.dot(p, tri, preferred_element_type=F32)
    return dot(hi) + dot(lo)


def _tri(bk, cmp):
    return cmp(lax.broadcasted_iota(jnp.int32, (bk, bk), 0), lax.broadcasted_iota(jnp.int32, (bk, bk), 1)).astype(BF16)


def _sb_logits(qblk, kblk, inv, row0):
    bq, bk = qblk.shape[0], kblk.shape[0]
    z = lax.dot_general(qblk, kblk, (((1,), (1,)), ((), ())), preferred_element_type=F32) * inv
    lb = jnp.minimum(z, 0.0) - jnp.log(1.0 + jnp.exp(-jnp.abs(z)))
    if row0 is None:
        return lb, lb - z, None
    mask = lax.broadcasted_iota(jnp.int32, (bq, bk), 1) < lax.broadcasted_iota(jnp.int32, (bq, bk), 0) + row0
    return lb, jnp.where(mask, lb - z, 0.0), mask


def _keep(mask, v):
    return v if mask is None else jnp.where(mask, v, 0.0)


def _head_norm(ref, gain):
    xv = ref[...].astype(F32)
    r = lax.rsqrt(jnp.mean(xv * xv, axis=-1, keepdims=True) + RMS_EPS)
    xhat = xv * r
    return xhat, r, (xhat * gain).astype(BF16)


def sb_attn_fwd(qkv, q_gain, k_gain, name):
    t = qkv.shape[0]
    d = qkv.shape[1] // 3
    dh = SB_HEAD_DIM
    nh = d // dh
    blk = min(SB_BLOCK, t)
    bq = min(SB_QUERY_ROWS, blk)
    inv = 1.0 / math.sqrt(dh)

    def body(q_ref, k_ref, v_ref, qg_ref, kg_ref, o_ref, lt_ref, qn, kn):
        qn[...] = _head_norm(q_ref, qg_ref[...])[2]
        kn[...] = _head_norm(k_ref, kg_ref[...])[2]
        later = _tri(blk, lambda j, s: j > s)

        def q_loop(qb, carry):
            q0 = pl.multiple_of(qb * bq, bq)
            qblk = qn[pl.ds(q0, bq), :]
            kd = q0 // blk

            def pair(kb, st, row0):
                c, acc = st
                k0 = pl.multiple_of(kb * blk, blk)
                lb, lm, mask = _sb_logits(qblk, kn[pl.ds(k0, blk), :], inv, row0)
                a = _keep(mask, jnp.exp(lb + _tri_sum(lm, later) + c))
                acc = acc + jnp.dot(a.astype(BF16), v_ref[pl.ds(k0, blk), :], preferred_element_type=F32)
                return c + jnp.sum(lm, axis=1, keepdims=True), acc

            st = pair(kd, (jnp.zeros((bq, 1), F32), jnp.zeros((bq, dh), F32)), q0 - kd * blk)
            c, acc = lax.fori_loop(1, kd + 1, lambda i, st: pair(kd - i, st, None), st)
            o_ref[pl.ds(q0, bq), :] = acc.astype(BF16)
            lt_ref[pl.ds(q0, bq), :] = c
            return carry

        lax.fori_loop(0, t // bq, q_loop, 0)

    head = lambda off: pl.BlockSpec((t, dh), lambda h: (0, h + off))
    gain = pl.BlockSpec((1, dh), lambda h: (0, 0))
    return pl.pallas_call(
        body, name=name, grid=(nh,),
        in_specs=[head(0), head(nh), head(2 * nh), gain, gain],
        out_specs=[head(0), pl.BlockSpec((None, t, 1), lambda h: (h, 0, 0))],
        out_shape=[jax.ShapeDtypeStruct((t, d), BF16), jax.ShapeDtypeStruct((nh, t, 1), F32)],
        scratch_shapes=[pltpu.VMEM((t, dh), BF16), pltpu.VMEM((t, dh), BF16)],
        compiler_params=_params("parallel"),
    )(qkv, qkv, qkv, q_gain, k_gain)


def sb_attn_bwd(qkv, ltot, do, q_gain, k_gain, name):
    t = qkv.shape[0]
    d = qkv.shape[1] // 3
    dh = SB_HEAD_DIM
    nh = d // dh
    blk = min(SB_BLOCK, t)
    bq = min(SB_QUERY_ROWS, blk)
    inv = 1.0 / math.sqrt(dh)
    tn_dims = (((0,), (0,)), ((), ()))

    def body(q_ref, k_ref, v_ref, lt_ref, do_ref, qg_ref, kg_ref, dq_ref, dk_ref, dv_ref, dqg_ref, dkg_ref,
             qn, kn, dqn, dkn, dvn):
        qg, kg = qg_ref[...], kg_ref[...]
        qhat, rq, qnb = _head_norm(q_ref, qg)
        khat, rk, knb = _head_norm(k_ref, kg)
        qn[...] = qnb
        kn[...] = knb
        dkn[...] = jnp.zeros_like(dkn)
        dvn[...] = jnp.zeros_like(dvn)
        upto = _tri(blk, lambda j, s: j <= s)
        before = _tri(blk, lambda j, s: j < s)

        def q_loop(qb, carry):
            q0 = pl.multiple_of(qb * bq, bq)
            qblk = qn[pl.ds(q0, bq), :]
            doblk = do_ref[pl.ds(q0, bq), :]
            ltv = lt_ref[pl.ds(q0, bq), :]
            kd = q0 // blk

            def pair(kb, st, row0):
                pl_, pg, dq = st
                k0 = pl.multiple_of(kb * blk, blk)
                kblk = kn[pl.ds(k0, blk), :]
                lb, lm, mask = _sb_logits(qblk, kblk, inv, row0)
                a = _keep(mask, jnp.exp(lb + (ltv - pl_ - _tri_sum(lm, upto))))
                da = lax.dot_general(doblk, v_ref[pl.ds(k0, blk), :], (((1,), (1,)), ((), ())),
                                     preferred_element_type=F32)
                g = da * a
                g_before = pg + _tri_sum(g, before)
                beta = jnp.exp(lb)
                dz = (_keep(mask, g * (1.0 - beta) - beta * g_before) * inv).astype(BF16)
                dq = dq + jnp.dot(dz, kblk, preferred_element_type=F32)
                dkn[pl.ds(k0, blk), :] += lax.dot_general(dz, qblk, tn_dims, preferred_element_type=F32)
                dvn[pl.ds(k0, blk), :] += lax.dot_general(a.astype(BF16), doblk, tn_dims, preferred_element_type=F32)
                return pl_ + jnp.sum(lm, axis=1, keepdims=True), pg + jnp.sum(g, axis=1, keepdims=True), dq

            z1 = jnp.zeros((bq, 1), F32)
            st = lax.fori_loop(0, kd, lambda kb, st: pair(kb, st, None), (z1, z1, jnp.zeros((bq, dh), F32)))
            dqn[pl.ds(q0, bq), :] = pair(kd, st, q0 - kd * blk)[2]
            return carry

        lax.fori_loop(0, t // bq, q_loop, 0)

        first = pl.program_id(0) == 0
        for dn, xhat, r, gain, out_ref, dgain_ref in ((dqn, qhat, rq, qg, dq_ref, dqg_ref),
                                                      (dkn, khat, rk, kg, dk_ref, dkg_ref)):
            dnv = dn[...]
            dxhat = dnv * gain
            out_ref[...] = (r * (dxhat - xhat * jnp.mean(dxhat * xhat, axis=-1, keepdims=True))).astype(BF16)
            part = jnp.sum(dnv * xhat, axis=0, keepdims=True)

            @pl.when(first)
            def _(dgain_ref=dgain_ref, part=part):
                dgain_ref[...] = part

            @pl.when(jnp.logical_not(first))
            def _(dgain_ref=dgain_ref, part=part):
                dgain_ref[...] += part

        dv_ref[...] = dvn[...].astype(BF16)

    head = lambda off: pl.BlockSpec((t, dh), lambda h: (0, h + off))
    gain = pl.BlockSpec((1, dh), lambda h: (0, 0))
    big = jax.ShapeDtypeStruct((t, d), BF16)
    small = jax.ShapeDtypeStruct((1, dh), F32)
    return pl.pallas_call(
        body, name=name, grid=(nh,),
        in_specs=[head(0), head(nh), head(2 * nh), pl.BlockSpec((None, t, 1), lambda h: (h, 0, 0)), head(0),
                  gain, gain],
        out_specs=[head(0), head(0), head(0), gain, gain],
        out_shape=[big, big, big, small, small],
        scratch_shapes=[pltpu.VMEM((t, dh), BF16), pltpu.VMEM((t, dh), BF16),
                        pltpu.VMEM((t, dh), F32), pltpu.VMEM((t, dh), F32), pltpu.VMEM((t, dh), F32)],
        compiler_params=_params("arbitrary"),
    )(qkv, qkv, qkv, ltot, do, q_gain, k_gain)


GELU_C = math.sqrt(2.0 / math.pi)
GELU_A = 0.044715
SCAN_ROWS = SUBLANES


def _gelu(y):
    return 0.5 * y * (1.0 + jnp.tanh(GELU_C * (y + GELU_A * y * y * y)))


def _gelu_grad(y):
    th = jnp.tanh(GELU_C * (y + GELU_A * y * y * y))
    return 0.5 * (1.0 + th) + 0.5 * y * (1.0 - th * th) * GELU_C * (1.0 + 3.0 * GELU_A * y * y)


def _powers(ar, ai):
    out = [(ar, ai)]
    for _ in range(SCAN_ROWS - 1):
        pr, pi = out[-1]
        out.append((pr * ar - pi * ai, pr * ai + pi * ar))
    return out


def _rows(vals):
    c = vals[0].shape[1]
    row = lax.broadcasted_iota(jnp.int32, (SCAN_ROWS, c), 0)
    out = jnp.broadcast_to(vals[SCAN_ROWS - 1], (SCAN_ROWS, c))
    for j in range(SCAN_ROWS - 2, -1, -1):
        out = jnp.where(row == j, vals[j], out)
    return out


def _scan_forward(sr, si, off, t, ar, ai):
    c = ar.shape[1]
    p = _powers(ar, ai)
    pwr = _rows([q[0] for q in p])
    pwi = _rows([q[1] for q in p])
    row = lax.broadcasted_iota(jnp.int32, (SCAN_ROWS, c), 0)

    def tile(i, carry):
        cr, ci = carry
        r0 = pl.multiple_of(off + i * SCAN_ROWS, SCAN_ROWS)
        xr = sr[pl.ds(r0, SCAN_ROWS), :]
        xi = si[pl.ds(r0, SCAN_ROWS), :]
        for k in (1, 2, 4):
            pr, pi = p[k - 1]
            shr = jnp.where(row >= k, _shift_down(xr, k), 0.0)
            shi = jnp.where(row >= k, _shift_down(xi, k), 0.0)
            xr, xi = xr + pr * shr - pi * shi, xi + pr * shi + pi * shr
        xr, xi = xr + pwr * cr - pwi * ci, xi + pwr * ci + pwi * cr
        sr[pl.ds(r0, SCAN_ROWS), :] = xr
        si[pl.ds(r0, SCAN_ROWS), :] = xi
        return xr[SCAN_ROWS - 1:SCAN_ROWS], xi[SCAN_ROWS - 1:SCAN_ROWS]

    z = jnp.zeros((1, c), F32)
    lax.fori_loop(0, t // SCAN_ROWS, tile, (z, z))


def _scan_reverse(gr, gi, t, ar, ai, xr_ref, xi_ref):
    c = ar.shape[1]
    p = _powers(ar, ai)
    pwr = _rows([p[SCAN_ROWS - 1 - j][0] for j in range(SCAN_ROWS)])
    pwi = _rows([p[SCAN_ROWS - 1 - j][1] for j in range(SCAN_ROWS)])
    row = lax.broadcasted_iota(jnp.int32, (SCAN_ROWS, c), 0)
    n = t // SCAN_ROWS

    def tile(ii, carry):
        cr, ci, dar, dai = carry
        r0 = pl.multiple_of((n - 1 - ii) * SCAN_ROWS, SCAN_ROWS)
        xr = gr[pl.ds(r0, SCAN_ROWS), :]
        xi = gi[pl.ds(r0, SCAN_ROWS), :]
        for k in (1, 2, 4):
            pr, pi = p[k - 1]
            shr = jnp.where(row < SCAN_ROWS - k, _shift_up(xr, k), 0.0)
            shi = jnp.where(row < SCAN_ROWS - k, _shift_up(xi, k), 0.0)
            xr, xi = xr + pr * shr + pi * shi, xi + pr * shi - pi * shr
        xr, xi = xr + pwr * cr + pwi * ci, xi + pwr * ci - pwi * cr
        gr[pl.ds(r0, SCAN_ROWS), :] = xr
        gi[pl.ds(r0, SCAN_ROWS), :] = xi
        xpr = _shift_down(xr_ref[pl.ds(r0, 2 * SCAN_ROWS), :], 1)[SCAN_ROWS:]
        xpi = _shift_down(xi_ref[pl.ds(r0, 2 * SCAN_ROWS), :], 1)[SCAN_ROWS:]
        return xr[0:1], xi[0:1], dar + xr * xpr + xi * xpi, dai + xi * xpr - xr * xpi

    z = jnp.zeros((1, c), F32)
    z8 = jnp.zeros((SCAN_ROWS, c), F32)
    _, _, dar, dai = lax.fori_loop(0, n, tile, (z, z, z8, z8))
    return jnp.sum(dar, axis=0, keepdims=True), jnp.sum(dai, axis=0, keepdims=True)


def _ssm_specs(t, nb, ch, st):
    col = pl.BlockSpec((t, ch), lambda b: (0, b))
    vec = pl.BlockSpec((1, ch), lambda b: (0, b))
    bspec = pl.BlockSpec((None, ch, st), lambda b: (b, 0, 0))
    cspec = pl.BlockSpec((None, st, ch), lambda b: (b, 0, 0))
    aspec = pl.BlockSpec((None, 1, st), lambda b: (b, 0, 0))
    return col, vec, bspec, cspec, aspec


def ssm_core_fwd(u, bre, bim, cre, cim, a_re, a_im, dskip, name):
    t, d = u.shape
    nb, ch, st = bre.shape

    def body(u_ref, bre_ref, bim_ref, cre_ref, cim_ref, ar_ref, ai_ref, d_ref, y_ref, yg_ref, sr, si):
        uv = u_ref[...]
        ub = uv.astype(BF16)
        sr[...] = jnp.dot(ub, bre_ref[...], preferred_element_type=F32)
        si[...] = jnp.dot(ub, bim_ref[...], preferred_element_type=F32)
        _scan_forward(sr, si, 0, t, ar_ref[...], ai_ref[...])
        y = (jnp.dot(sr[...].astype(BF16), cre_ref[...], preferred_element_type=F32)
             - jnp.dot(si[...].astype(BF16), cim_ref[...], preferred_element_type=F32) + d_ref[...] * uv)
        y_ref[...] = y
        yg_ref[...] = _gelu(y).astype(BF16)

    col, vec, bspec, cspec, aspec = _ssm_specs(t, nb, ch, st)
    return pl.pallas_call(
        body, name=name, grid=(nb,),
        in_specs=[col, bspec, bspec, cspec, cspec, aspec, aspec, vec],
        out_specs=[col, col],
        out_shape=[jax.ShapeDtypeStruct((t, d), F32), jax.ShapeDtypeStruct((t, d), BF16)],
        scratch_shapes=[pltpu.VMEM((t, st), F32), pltpu.VMEM((t, st), F32)],
        compiler_params=_params("parallel"),
    )(u, bre, bim, cre, cim, a_re, a_im, dskip)


def ssm_core_bwd(u, y, dyg, bre, bim, cre, cim, a_re, a_im, dskip, name):
    t, d = u.shape
    nb, ch, st = bre.shape
    tn_dims = (((0,), (0,)), ((), ()))
    nt_dims = (((1,), (1,)), ((), ()))

    def body(u_ref, y_ref, dyg_ref, bre_ref, bim_ref, cre_ref, cim_ref, ar_ref, ai_ref, d_ref,
             du_ref, dd_ref, dbre_ref, dbim_ref, dcre_ref, dcim_ref, dar_ref, dai_ref, xr, xi, gr, gi):
        uv = u_ref[...]
        ub = uv.astype(BF16)
        ar, ai = ar_ref[...], ai_ref[...]
        dy = dyg_ref[...] * _gelu_grad(y_ref[...])
        dd_ref[...] = jnp.sum(dy * uv, axis=0, keepdims=True)
        zero = jnp.zeros((HALO, st), F32)
        xr[0:HALO, :] = zero
        xi[0:HALO, :] = zero
        xr[HALO:, :] = jnp.dot(ub, bre_ref[...], preferred_element_type=F32)
        xi[HALO:, :] = jnp.dot(ub, bim_ref[...], preferred_element_type=F32)
        _scan_forward(xr, xi, HALO, t, ar, ai)
        dyb = dy.astype(BF16)
        dcre_ref[...] = lax.dot_general(xr[HALO:, :].astype(BF16), dyb, tn_dims, preferred_element_type=F32)
        dcim_ref[...] = -lax.dot_general(xi[HALO:, :].astype(BF16), dyb, tn_dims, preferred_element_type=F32)
        gr[...] = lax.dot_general(dyb, cre_ref[...], nt_dims, preferred_element_type=F32)
        gi[...] = -lax.dot_general(dyb, cim_ref[...], nt_dims, preferred_element_type=F32)
        dar, dai = _scan_reverse(gr, gi, t, ar, ai, xr, xi)
        dar_ref[...] = dar
        dai_ref[...] = dai
        grb = gr[...].astype(BF16)
        gib = gi[...].astype(BF16)
        dbre_ref[...] = lax.dot_general(ub, grb, tn_dims, preferred_element_type=F32)
        dbim_ref[...] = lax.dot_general(ub, gib, tn_dims, preferred_element_type=F32)
        du_ref[...] = (d_ref[...] * dy + lax.dot_general(grb, bre_ref[...], nt_dims, preferred_element_type=F32)
                       + lax.dot_general(gib, bim_ref[...], nt_dims, preferred_element_type=F32))

    col, vec, bspec, cspec, aspec = _ssm_specs(t, nb, ch, st)
    sh = jax.ShapeDtypeStruct
    return pl.pallas_call(
        body, name=name, grid=(nb,),
        in_specs=[col, col, col, bspec, bspec, cspec, cspec, aspec, aspec, vec],
        out_specs=[col, vec, bspec, bspec, cspec, cspec, aspec, aspec],
        out_shape=[sh((t, d), F32), sh((1, d), F32), sh((nb, ch, st), F32), sh((nb, ch, st), F32),
                   sh((nb, st, ch), F32), sh((nb, st, ch), F32), sh((nb, 1, st), F32), sh((nb, 1, st), F32)],
        scratch_shapes=[pltpu.VMEM((t + HALO, st), F32), pltpu.VMEM((t + HALO, st), F32),
                        pltpu.VMEM((t, st), F32), pltpu.VMEM((t, st), F32)],
        compiler_params=_params("parallel"),
    )(u, y, dyg, bre, bim, cre, cim, a_re, a_im, dskip)


def glu_fwd(yg, w_glu, b_glu, x, name, lead=None):
    t, d = yg.shape
    tn = _tile(d, 256)
    nd = d // tn

    def body(yg_ref, wv_ref, wg_ref, bv_ref, bg_ref, x_ref, val_ref, gate_ref, o_ref):
        ygv = yg_ref[...]
        vb = (jnp.dot(ygv, wv_ref[...], preferred_element_type=F32) + bv_ref[...]).astype(BF16)
        gb = (jnp.dot(ygv, wg_ref[...], preferred_element_type=F32) + bg_ref[...]).astype(BF16)
        val_ref[...] = vb
        gate_ref[...] = gb
        o_ref[...] = x_ref[...] + vb.astype(F32) * _sigmoid(gb.astype(F32))

    col = lambda off: _spec((d, tn), lambda j: (0, j + off), lead)
    vec = lambda off: pl.BlockSpec((1, tn), lambda j: (0, j + off))
    tile = pl.BlockSpec((t, tn), lambda j: (0, j))
    return pl.pallas_call(
        body, name=name, grid=(nd,),
        in_specs=[pl.BlockSpec((t, d), lambda j: (0, 0)), col(0), col(nd), vec(0), vec(nd), tile],
        out_specs=[tile, tile, tile],
        out_shape=[jax.ShapeDtypeStruct((t, d), BF16), jax.ShapeDtypeStruct((t, d), BF16),
                   jax.ShapeDtypeStruct((t, d), F32)],
        compiler_params=_params("parallel"),
    )(yg, w_glu, w_glu, b_glu, b_glu, x)


def glu_bwd(dm, val, gate, name):
    t, d = dm.shape
    tn = _tile(d, 256)

    def body(dm_ref, val_ref, gate_ref, dv_ref, dg_ref, dbv_ref, dbg_ref):
        dmv = dm_ref[...]
        s = _sigmoid(gate_ref[...].astype(F32))
        dval = dmv * s
        dgate = dmv * val_ref[...].astype(F32) * s * (1.0 - s)
        dv_ref[...] = dval.astype(BF16)
        dg_ref[...] = dgate.astype(BF16)
        dbv_ref[...] = jnp.sum(dval, axis=0, keepdims=True)
        dbg_ref[...] = jnp.sum(dgate, axis=0, keepdims=True)

    tile = pl.BlockSpec((t, tn), lambda j: (0, j))
    vec = pl.BlockSpec((1, tn), lambda j: (0, j))
    return pl.pallas_call(
        body, name=name, grid=(d // tn,),
        in_specs=[tile, tile, tile], out_specs=[tile, tile, vec, vec],
        out_shape=[jax.ShapeDtypeStruct((t, d), BF16), jax.ShapeDtypeStruct((t, d), BF16),
                   jax.ShapeDtypeStruct((1, d), F32), jax.ShapeDtypeStruct((1, d), F32)],
        compiler_params=_params("parallel"),
    )(dm, val, gate)


def _block_diag(m, gb):
    g, a, b = m.shape
    eye = jnp.eye(gb, dtype=m.dtype)
    return jnp.einsum("ngab,gk->ngakb", m.reshape(g // gb, gb, a, b), eye).reshape(g // gb, gb * a, gb * b)


def ssm_prepare(lam_re, lam_im, log_step, b_re, b_im, c_re, c_im):
    gb = SSM_BLOCK_GROUPS
    g, p = lam_re.shape
    step = jnp.exp(log_step)[:, None]
    mag = jnp.exp(lam_re * step)
    lb_re = mag * jnp.cos(lam_im * step)
    lb_im = mag * jnp.sin(lam_im * step)
    den = lam_re * lam_re + lam_im * lam_im
    f_re = ((lb_re - 1.0) * lam_re + lb_im * lam_im) / den
    f_im = (lb_im * lam_re - (lb_re - 1.0) * lam_im) / den
    bb_re = f_re[..., None] * b_re - f_im[..., None] * b_im
    bb_im = f_re[..., None] * b_im + f_im[..., None] * b_re
    tr = lambda m: jnp.transpose(m, (0, 2, 1))
    return (_block_diag(tr(bb_re), gb), _block_diag(tr(bb_im), gb), _block_diag(tr(c_re), gb), _block_diag(tr(c_im), gb),
            lb_re.reshape(g // gb, 1, gb * p), lb_im.reshape(g // gb, 1, gb * p))


EW_BLOCK_BYTES = 2 * 1024 * 1024
BF16_ROWS = 16


def _row_tile(rows, cols, block_bytes=EW_BLOCK_BYTES):
    limit = max(BF16_ROWS, block_bytes // (cols * 4))
    best = None
    for tr in range(BF16_ROWS, min(rows, limit) + 1, BF16_ROWS):
        if rows % tr == 0:
            best = tr
    return best if best is not None else rows


def _as2d(a):
    return a.reshape(-1, a.shape[-1])


def ew(fn, ins, out_dtypes, name):
    rows, cols = ins[0].shape
    tr = _row_tile(rows, cols)
    n_in = len(ins)

    def body(*refs):
        outs = fn(*[r[...] for r in refs[:n_in]])
        for o_ref, v in zip(refs[n_in:], outs):
            o_ref[...] = v.astype(o_ref.dtype)

    spec = pl.BlockSpec((tr, cols), lambda i: (i, 0))
    return pl.pallas_call(
        body, name=name, grid=(rows // tr,), in_specs=[spec] * n_in, out_specs=[spec] * len(out_dtypes),
        out_shape=[jax.ShapeDtypeStruct((rows, cols), dt) for dt in out_dtypes],
        compiler_params=_params("parallel"),
    )(*ins)


def _adamw(w, g, m, v):
    m = ADAM_B1 * m + (1.0 - ADAM_B1) * g
    v = ADAM_B2 * v + (1.0 - ADAM_B2) * (g * g)
    m_hat = m / (1.0 - ADAM_B1 ** ADAM_STEP)
    v_hat = v / (1.0 - ADAM_B2 ** ADAM_STEP)
    delta = -ADAM_LR * (m_hat / (jnp.sqrt(v_hat) + ADAM_EPS) + ADAM_WD * w)
    return delta, m, v


def adamw(w, g, m, v, name):
    outs = ew(_adamw, [_as2d(w), _as2d(g), _as2d(m), _as2d(v)], [F32, F32, F32], name)
    return [o.reshape(w.shape) for o in outs]


def adamw_layer(w, g, m, v, layer, name, into=None):
    nl, r, cw = w.shape
    tr = _row_tile(r, cw)

    def body(w_ref, g_ref, m_ref, v_ref, *rest):
        g_out, d_out, m_out, v_out = rest[-4:]
        gv = g_ref[...]
        d_out[...], m_out[...], v_out[...] = _adamw(w_ref[...], gv, m_ref[...], v_ref[...])
        g_out[...] = gv

    lay = pl.BlockSpec((None, tr, cw), lambda i: (layer, i, 0))
    args = [w, g, m, v]
    in_specs = [lay, pl.BlockSpec((None, tr, cw), lambda i: (0, i, 0)), lay, lay]
    aliases = {}
    if into is not None:
        args += list(into)
        in_specs += [pl.BlockSpec(memory_space=pl.ANY)] * 4
        aliases = {4 + k: k for k in range(4)}
    return pl.pallas_call(
        body, name=name, grid=(r // tr,), in_specs=in_specs, out_specs=[lay] * 4,
        out_shape=[jax.ShapeDtypeStruct((nl, r, cw), F32)] * 4,
        input_output_aliases=aliases,
        compiler_params=_params("parallel"),
    )(*args)


def loss_head(y, target, name):
    t, d = y.shape
    tr = min(t, NORM_ROWS)
    n = t // tr

    def body(y_ref, t_ref, dy_ref, dyb_ref, loss_ref, acc):
        i = pl.program_id(0)
        err = y_ref[...] - t_ref[...]
        dy = err * (1.0 / d)
        dy_ref[...] = dy
        dyb_ref[...] = dy.astype(BF16)
        part = jnp.sum(err * err, axis=0, keepdims=True)

        @pl.when(i == 0)
        def _():
            acc[...] = part

        @pl.when(i != 0)
        def _():
            acc[...] += part

        @pl.when(i == n - 1)
        def _():
            loss_ref[...] = jnp.full((1, LANES), 0.5 / d, F32) * jnp.sum(acc[...])

    row = pl.BlockSpec((tr, d), lambda i: (i, 0))
    return pl.pallas_call(
        body, name=name, grid=(n,), in_specs=[row, row],
        out_specs=[row, row, pl.BlockSpec((1, LANES), lambda i: (0, 0))],
        out_shape=[jax.ShapeDtypeStruct((t, d), F32), jax.ShapeDtypeStruct((t, d), BF16),
                   jax.ShapeDtypeStruct((1, LANES), F32)],
        scratch_shapes=[pltpu.VMEM((1, d), F32)],
        compiler_params=_params("arbitrary"),
    )(y, target)


HBM_SPEC = pl.BlockSpec(memory_space=pltpu.HBM)
VMEM_SPEC = pl.BlockSpec(memory_space=pltpu.VMEM)


def _place():
    return lax.axis_index("x"), lax.axis_index("y"), lax.axis_index("c")


def _other_chips(x, y):
    return [(1 - x, y), (x, 1 - y), (1 - x, 1 - y)]


def _remote(src, dst, send_sem, recv_sem, dev):
    return pltpu.make_async_remote_copy(src_ref=src, dst_ref=dst, send_sem=send_sem, recv_sem=recv_sem,
                                        device_id=dev, device_id_type=MESH)


def _piece(refs, shard_shape, ax, j, half):
    w = shard_shape[ax]
    a, off = divmod(j * w, refs[0].shape[ax]) if isinstance(j, int) else (0, j * w)
    idx = [pl.ds(0, s) for s in shard_shape]
    idx[ax] = pl.ds(off, w)
    if half is not None:
        h0 = shard_shape[0] // 2
        idx[0] = pl.ds((off if ax == 0 else 0) + half * h0, h0)
    return refs[a].at[tuple(idx)]


def small_allreduce(v, name):
    n, r, l = v.shape
    assert n == N_DEV

    def body(v_ref, o_ref, recv, red, send1, recv1, send2, recv2):
        x, y, c = _place()
        me = 4 * x + 2 * y + c
        dev = lambda k: (k // 4, (k // 2) % 2, k % 2)
        firsts = []
        for o in range(1, N_DEV):
            tgt = (me + o) % N_DEV
            cp = _remote(v_ref.at[tgt], recv.at[me], send1.at[o], recv1.at[me], dev(tgt))
            cp.start()
            firsts.append(cp)
        recv[me] = v_ref[me]
        for o in range(1, N_DEV):
            src = (me + o) % N_DEV
            _remote(v_ref.at[src], recv.at[src], send1.at[o], recv1.at[src], dev(src)).wait_recv()
        acc = recv[0]
        for s in range(1, N_DEV):
            acc = acc + recv[s]
        red[...] = acc
        o_ref[me] = acc
        seconds = []
        for o in range(1, N_DEV):
            tgt = (me + o) % N_DEV
            cp = _remote(red, o_ref.at[me], send2.at[o], recv2.at[me], dev(tgt))
            cp.start()
            seconds.append(cp)
        for o in range(1, N_DEV):
            src = (me + o) % N_DEV
            _remote(red, o_ref.at[src], send2.at[o], recv2.at[src], dev(src)).wait_recv()
        for cp in firsts + seconds:
            cp.wait_send()

    sems = pltpu.SemaphoreType.DMA((N_DEV,))
    return pl.pallas_call(
        body, name=name, in_specs=[VMEM_SPEC], out_specs=VMEM_SPEC,
        out_shape=jax.ShapeDtypeStruct(v.shape, F32),
        scratch_shapes=[pltpu.VMEM((N_DEV, r, l), F32), pltpu.VMEM((r, l), F32), sems, sems, sems, sems],
        compiler_params=pltpu.CompilerParams(vmem_limit_bytes=VMEM_LIMIT),
    )(v)


def _me_scalar():
    return (2 * lax.axis_index("x") + lax.axis_index("y")).astype(jnp.int32).reshape(1)


def cast_into_gathered(wf, layer, axis, me1, name):
    _, r, cw = wf.shape
    tr = _row_tile(r, cw)
    nrb = r // tr
    full = (1, r * N_CHIPS, cw) if axis == 0 else (1, r, cw * N_CHIPS)
    omap = (lambda i, me: (0, me[0] * nrb + i, 0)) if axis == 0 else (lambda i, me: (0, i, me[0]))

    def body(me_ref, w_ref, o_ref):
        o_ref[...] = w_ref[...].astype(BF16)

    return pl.pallas_call(
        body, name=name,
        grid_spec=pltpu.PrefetchScalarGridSpec(
            num_scalar_prefetch=1, grid=(nrb,),
            in_specs=[pl.BlockSpec((None, tr, cw), lambda i, me: (layer, i, 0))],
            out_specs=pl.BlockSpec((None, tr, cw), omap)),
        out_shape=jax.ShapeDtypeStruct(full, BF16),
        compiler_params=_params("parallel"),
    )(me1, wf)


SEM_SPEC = pl.BlockSpec(memory_space=pltpu.SEMAPHORE)
SPLIT_COPY_PARAMS = pltpu.CompilerParams(has_side_effects=pltpu.SideEffectType.DATAFLOW_SIDE_EFFECTING)
TOKEN_SHAPE = (SUBLANES, LANES)


def _hbm(a):
    return pltpu.with_memory_space_constraint(a, pltpu.HBM)


def _gather_copies(refs, shapes, axes, send_sem, recv_sem):
    x, y, c = _place()
    me = 2 * x + y
    out = []
    for p, ref in enumerate(refs):
        place = lambda j: _piece([ref.at[0]], shapes[p], axes[p], j, c)
        for q, chip in enumerate(_other_chips(x, y)):
            dev = (chip[0], chip[1], c)
            sems = (send_sem.at[3 * p + q], recv_sem.at[3 * p + q])
            theirs = place(2 * chip[0] + chip[1])
            out.append((_remote(place(me), place(me), *sems, dev), _remote(theirs, theirs, *sems, dev)))
    return out


def gather_start(bufs, shapes, axes, after, name):
    n = len(bufs)

    def body(*refs):
        send_sem, recv_sem = refs[n + 1:n + 3]
        o_refs = refs[n + 3:2 * n + 3]
        token = refs[-1]
        for mine, _ in _gather_copies(o_refs, shapes, axes, send_sem, recv_sem):
            mine.start()
        token[...] = jnp.zeros(TOKEN_SHAPE, F32)

    sems = pltpu.SemaphoreType.DMA((3 * n,))
    outs = pl.pallas_call(
        body, name=name,
        in_specs=[HBM_SPEC] * n + [pl.BlockSpec(memory_space=pl.ANY)],
        out_specs=[SEM_SPEC, SEM_SPEC] + [HBM_SPEC] * n + [VMEM_SPEC],
        out_shape=[sems, sems] + [pltpu.HBM(b.shape, b.dtype) for b in bufs] + [jax.ShapeDtypeStruct(TOKEN_SHAPE, F32)],
        input_output_aliases={p: p + 2 for p in range(n)},
        compiler_params=SPLIT_COPY_PARAMS,
    )(*[_hbm(b) for b in bufs], after)
    return outs[0], outs[1], list(outs[2:2 + n]), outs[-1]


def gather_wait(send_sem, recv_sem, bufs, shapes, axes, after, name):
    n = len(bufs)

    def body(*refs):
        s_sem, r_sem = refs[n:n + 2]
        o_refs = refs[n + 3:]
        for mine, theirs in _gather_copies(o_refs, shapes, axes, s_sem, r_sem):
            mine.wait_send()
            theirs.wait_recv()

    return pl.pallas_call(
        body, name=name,
        in_specs=[HBM_SPEC] * n + [SEM_SPEC, SEM_SPEC, pl.BlockSpec(memory_space=pl.ANY)],
        out_specs=[HBM_SPEC] * n,
        out_shape=[pltpu.HBM(b.shape, b.dtype) for b in bufs],
        input_output_aliases={p: p for p in range(n)},
        compiler_params=SPLIT_COPY_PARAMS,
    )(*bufs, send_sem, recv_sem, after)


SIBLING_SLOTS = 2
SIBLING_BLOCK_BYTES = 8 * 1024 * 1024
SHARE_BLOCK_BYTES = 4 * 1024 * 1024


def _row_step(nrb):
    s = pl.program_id(0)
    for ax in range(1, len(nrb)):
        s = s * nrb[ax] + pl.program_id(ax)
    return s


def gather_forward(buf, axis, r, cw, me1, name):
    nl = buf.shape[0]
    h0 = r // 2
    tr = _row_tile(h0, cw, SIBLING_BLOCK_BYTES)
    nrb = h0 // tr
    peer = lambda q, me: (me[0] + q + 1) % N_CHIPS
    if axis == 1:
        view = buf.reshape(nl, 1, 2, h0, N_CHIPS * cw)
        spec = pl.BlockSpec((1, 1, 2, tr, cw), lambda l, q, i, me: (l, 0, 0, i, peer(q, me)))
    else:
        view = buf.reshape(nl, N_CHIPS, 2, h0, cw)
        spec = pl.BlockSpec((1, 1, 2, tr, cw), lambda l, q, i, me: (l, peer(q, me), 0, i, 0))

    def body(me_ref, in_ref, o_ref, rbuf, send_sem, recv_sem):
        x, y, c = _place()
        slot = _row_step((nl, N_CHIPS - 1, nrb)) % SIBLING_SLOTS
        cp = _remote(in_ref.at[0, 0, c], rbuf.at[slot], send_sem.at[slot], recv_sem.at[slot], (x, y, 1 - c))
        cp.start()
        o_ref[0, 0, c] = in_ref[0, 0, c]
        cp.wait_recv()
        o_ref[0, 0, 1 - c] = rbuf[slot]
        cp.wait_send()

    out = pl.pallas_call(
        body, name=name,
        grid_spec=pltpu.PrefetchScalarGridSpec(
            num_scalar_prefetch=1, grid=(nl, N_CHIPS - 1, nrb), in_specs=[spec], out_specs=spec,
            scratch_shapes=[pltpu.VMEM((SIBLING_SLOTS, tr, cw), buf.dtype),
                            pltpu.SemaphoreType.DMA((SIBLING_SLOTS,)), pltpu.SemaphoreType.DMA((SIBLING_SLOTS,))]),
        out_shape=jax.ShapeDtypeStruct(view.shape, view.dtype),
        input_output_aliases={1: 0},
        compiler_params=_params("arbitrary", "arbitrary", "arbitrary"),
    )(me1, view)
    return out.reshape(buf.shape)


def pair_reduce(g, axis, r, cw, name, into=None, first_slot=0):
    h0 = r // 2
    tr = _row_tile(h0, cw, SIBLING_BLOCK_BYTES)
    nrb = h0 // tr
    if axis == 1:
        n_sh = g.shape[1] // cw
        view = g.reshape(1, 2, h0, n_sh * cw)
        spec = pl.BlockSpec((1, 2, tr, cw), lambda j, i: (0, 0, i, j))
    else:
        n_sh = g.shape[0] // r
        view = g.reshape(n_sh, 2, h0, cw)
        spec = pl.BlockSpec((1, 2, tr, cw), lambda j, i: (j, 0, i, 0))

    def body(g_ref, *rest):
        o_ref, rbuf, send_sem, recv_sem = rest[-4:]
        x, y, c = _place()
        slot = _row_step((n_sh, nrb)) % SIBLING_SLOTS
        cp = _remote(g_ref.at[0, 1 - c], rbuf.at[slot], send_sem.at[slot], recv_sem.at[slot], (x, y, 1 - c))
        cp.start()
        mine = g_ref[0, c].astype(F32)
        cp.wait_recv()
        o_ref[0] = (mine + rbuf[slot].astype(F32)).astype(BF16)
        cp.wait_send()

    args, in_specs, aliases = [view], [spec], {}
    if into is not None:
        args.append(into)
        in_specs.append(pl.BlockSpec(memory_space=pl.ANY))
        aliases = {1: 0}
    return pl.pallas_call(
        body, name=name, grid=(n_sh, nrb), in_specs=in_specs,
        out_specs=pl.BlockSpec((1, tr, cw), lambda j, i: (j + first_slot, i, 0)),
        out_shape=jax.ShapeDtypeStruct((N_CHIPS, h0, cw), BF16),
        input_output_aliases=aliases,
        scratch_shapes=[pltpu.VMEM((SIBLING_SLOTS, tr, cw), BF16),
                        pltpu.SemaphoreType.DMA((SIBLING_SLOTS,)), pltpu.SemaphoreType.DMA((SIBLING_SLOTS,))],
        compiler_params=_params("arbitrary", "arbitrary"),
    )(*args)


def _chip_copies(h_refs, lb_refs, send_sem, recv_sem):
    x, y, c = _place()
    out = []
    for k, (h, lb) in enumerate(zip(h_refs, lb_refs)):
        for q, chip in enumerate(_other_chips(x, y)):
            out.append(_remote(h.at[2 * chip[0] + chip[1]], lb.at[q], send_sem.at[3 * k + q], recv_sem.at[3 * k + q],
                               (chip[0], chip[1], c)))
    return out


def chip_start(halves, name):
    n = len(halves)
    landed = [lax.empty((N_CHIPS - 1,) + h.shape[1:], h.dtype) for h in halves]

    def body(*refs):
        send_sem, recv_sem = refs[2 * n:2 * n + 2]
        h_refs, lb_refs = refs[2 * n + 2:3 * n + 2], refs[3 * n + 2:4 * n + 2]
        for cp in _chip_copies(h_refs, lb_refs, send_sem, recv_sem):
            cp.start()
        refs[-1][...] = jnp.zeros(TOKEN_SHAPE, F32)

    sems = pltpu.SemaphoreType.DMA((3 * n,))
    outs = pl.pallas_call(
        body, name=name,
        in_specs=[HBM_SPEC] * (2 * n),
        out_specs=[SEM_SPEC, SEM_SPEC] + [HBM_SPEC] * (2 * n) + [VMEM_SPEC],
        out_shape=[sems, sems] + [pltpu.HBM(a.shape, a.dtype) for a in halves + landed]
        + [jax.ShapeDtypeStruct(TOKEN_SHAPE, F32)],
        input_output_aliases={p: p + 2 for p in range(2 * n)},
        compiler_params=SPLIT_COPY_PARAMS,
    )(*[_hbm(a) for a in halves + landed])
    return outs[0], outs[1], list(outs[2:2 + n]), list(outs[2 + n:2 + 2 * n]), outs[-1]


def chip_wait(send_sem, recv_sem, halves, landed, after, name):
    n = len(halves)

    def body(*refs):
        s_sem, r_sem = refs[2 * n:2 * n + 2]
        h_refs, lb_refs = refs[2 * n + 3:3 * n + 3], refs[3 * n + 3:]
        for cp in _chip_copies(h_refs, lb_refs, s_sem, r_sem):
            cp.wait_send()
            cp.wait_recv()

    outs = pl.pallas_call(
        body, name=name,
        in_specs=[HBM_SPEC] * (2 * n) + [SEM_SPEC, SEM_SPEC, pl.BlockSpec(memory_space=pl.ANY)],
        out_specs=[HBM_SPEC] * (2 * n),
        out_shape=[pltpu.HBM(a.shape, a.dtype) for a in halves + landed],
        input_output_aliases={p: p for p in range(2 * n)},
        compiler_params=SPLIT_COPY_PARAMS,
    )(*halves, *landed, send_sem, recv_sem, after)
    return list(outs[:n]), list(outs[n:])


def reduce_share(half, landed, me1, name):
    _, h0, cw = half.shape
    tr = _row_tile(h0, cw, SHARE_BLOCK_BYTES)
    nrb = h0 // tr

    def body(me_ref, h_ref, l0, l1, l2, o_ref, sbuf, rbuf, send_sem, recv_sem):
        x, y, c = _place()
        slot = pl.program_id(0) % SIBLING_SLOTS
        total = ((h_ref[...].astype(F32) + l0[...].astype(F32)) + l1[...].astype(F32)) + l2[...].astype(F32)
        sbuf[slot] = total
        cp = _remote(sbuf.at[slot], rbuf.at[slot], send_sem.at[slot], recv_sem.at[slot], (x, y, 1 - c))
        cp.start()
        o_ref[0, c] = total
        cp.wait_recv()
        o_ref[0, 1 - c] = rbuf[slot]
        cp.wait_send()

    landed_spec = lambda q: pl.BlockSpec((None, tr, cw), lambda i, me: (q, i, 0))
    args = [me1, half, landed, landed, landed]
    in_specs = [pl.BlockSpec((None, tr, cw), lambda i, me: (me[0], i, 0))] + [landed_spec(q) for q in range(N_CHIPS - 1)]
    out = pl.pallas_call(
        body, name=name,
        grid_spec=pltpu.PrefetchScalarGridSpec(
            num_scalar_prefetch=1, grid=(nrb,), in_specs=in_specs,
            out_specs=pl.BlockSpec((1, 2, tr, cw), lambda i, me: (0, 0, i, 0)),
            scratch_shapes=[pltpu.VMEM((SIBLING_SLOTS, tr, cw), F32), pltpu.VMEM((SIBLING_SLOTS, tr, cw), F32),
                            pltpu.SemaphoreType.DMA((SIBLING_SLOTS,)), pltpu.SemaphoreType.DMA((SIBLING_SLOTS,))]),
        out_shape=jax.ShapeDtypeStruct((1, 2, h0, cw), F32),
        compiler_params=_params("arbitrary"),
    )(*args)
    return out.reshape(1, 2 * h0, cw)


WEIGHTS = ["norm_mix_g", "norm_ffn_g", "pool_w", "pool_b", "pool_scale", "sb_w_qkv", "sb_q_gain", "sb_k_gain",
           "sb_w_o", "ssm_lam_re", "ssm_lam_im", "ssm_log_step", "ssm_b_re", "ssm_b_im", "ssm_c_re", "ssm_c_im",
           "ssm_d", "ssm_w_glu", "ssm_b_glu", "ffn_w_up", "ffn_conv_w", "ffn_conv_b", "ffn_w_down"]
BIG = {"pool_w": 0, "sb_w_qkv": 1, "sb_w_o": 0, "ssm_w_glu": 1, "ffn_w_up": 1, "ffn_w_down": 0}
SMALL_SHARDED = {"pool_b": 1, "pool_scale": 1, "ssm_d": 1, "ssm_b_glu": 1, "ffn_conv_w": 2}
SMALL = [n for n in WEIGHTS if n not in BIG]
SMALL_PAD = N_DEV * SUBLANES * LANES
N_MIXERS = 3
REDUCE_LAG = 3


def _pack(arrays):
    flat = jnp.concatenate([a.reshape(-1).astype(F32) for a in arrays])
    total = -(-flat.shape[0] // SMALL_PAD) * SMALL_PAD
    flat = jnp.pad(flat, (0, total - flat.shape[0]))
    return flat.reshape(N_DEV, -1, LANES)


def _unpack(packed, like):
    flat = packed.reshape(-1)
    out, off = [], 0
    for a in like:
        out.append(flat[off:off + a.size].reshape(a.shape))
        off += a.size
    return out


def kernel(x, norm_mix_g, norm_ffn_g, pool_w, pool_b, pool_scale, sb_w_qkv, sb_q_gain, sb_k_gain, sb_w_o, ssm_lam_re, ssm_lam_im, ssm_log_step, ssm_b_re, ssm_b_im, ssm_c_re, ssm_c_im, ssm_d, ssm_w_glu, ssm_b_glu, ffn_w_up, ffn_conv_w, ffn_conv_b, ffn_w_down, loss_target, m_norm_mix_g, m_norm_ffn_g, m_pool_w, m_pool_b, m_pool_scale, m_sb_w_qkv, m_sb_q_gain, m_sb_k_gain, m_sb_w_o, m_ssm_lam_re, m_ssm_lam_im, m_ssm_log_step, m_ssm_b_re, m_ssm_b_im, m_ssm_c_re, m_ssm_c_im, m_ssm_d, m_ssm_w_glu, m_ssm_b_glu, m_ffn_w_up, m_ffn_conv_w, m_ffn_conv_b, m_ffn_w_down, v_norm_mix_g, v_norm_ffn_g, v_pool_w, v_pool_b, v_pool_scale, v_sb_w_qkv, v_sb_q_gain, v_sb_k_gain, v_sb_w_o, v_ssm_lam_re, v_ssm_lam_im, v_ssm_log_step, v_ssm_b_re, v_ssm_b_im, v_ssm_c_re, v_ssm_c_im, v_ssm_d, v_ssm_w_glu, v_ssm_b_glu, v_ffn_w_up, v_ffn_conv_w, v_ffn_conv_b, v_ffn_w_down):
    given = dict(locals())
    w = {n: given[n] for n in WEIGHTS}
    mom = {n: given["m_" + n] for n in WEIGHTS}
    var = {n: given["v_" + n] for n in WEIGHTS}
    pool_shape = pool_w.shape
    for group in (w, mom, var):
        group["pool_w"] = group["pool_w"].reshape(pool_shape[0], pool_shape[1] * pool_shape[2], pool_shape[3])
    xi, yi, ci = _place()
    me = 2 * xi + yi
    depth = norm_mix_g.shape[0]
    x_in = x[0]
    t, d = x_in.shape

    def placed(a, ax):
        shp = list(a.shape)
        shp[ax] *= N_CHIPS
        full = lax.dynamic_update_slice_in_dim(jnp.zeros(shp, F32), a, me * a.shape[ax], ax)
        return jnp.where(ci == 0, full, 0.0)

    sharded_full = [placed(w[n], ax) for n, ax in SMALL_SHARDED.items()]

    big = list(BIG)
    me1 = _me_scalar()
    shard = {n: tuple(w[n].shape[1:]) for n in big}
    vec = lambda a, i: a[i:i + 1]
    tie = lambda v, token: v + token[0, 0]

    def layer_weights(i):
        kind, j = i % N_MIXERS, i // N_MIXERS
        mixer = {0: [("pool_w", j)], 1: [("sb_w_qkv", j), ("sb_w_o", j)], 2: [("ssm_w_glu", j)]}[kind]
        return mixer + [("ffn_w_up", i), ("ffn_w_down", i)]

    def pool_matrices(j):
        return gathered["pool_w", j].reshape((1, N_CHIPS) + tuple(pool_shape[1:]))

    started, token = [], x_in
    for i in range(depth):
        keys = layer_weights(i)
        bufs = [cast_into_gathered(w[n], l, BIG[n], me1, f"cast_{n}{l}") for n, l in keys]
        send_sem, recv_sem, bufs, token = gather_start(bufs, [shard[n] for n, _ in keys], [BIG[n] for n, _ in keys],
                                                       token, f"gather_start{i}")
        started.append((keys, send_sem, recv_sem, bufs))
        if i == 0:
            token = small_allreduce(tie(_pack(sharded_full), token), "gather_vectors")
            whole = dict(zip(SMALL_SHARDED, _unpack(token, sharded_full)))
    gathered = {}

    saved = []
    xc = x_in
    for i in range(depth):
        kind, j = i % N_MIXERS, i // N_MIXERS
        keys, send_sem, recv_sem, bufs = started[i]
        bufs = gather_wait(send_sem, recv_sem, bufs, [shard[n] for n, _ in keys], [BIG[n] for n, _ in keys],
                           token if i == 0 else xc, f"gather_wait{i}")
        for (n, l), b in zip(keys, bufs):
            gathered[n, l] = gather_forward(b, BIG[n], *shard[n], me1, f"gather_forward_{n}{l}")
        s = {"x_in": xc}
        g_mix = vec(norm_mix_g, i)
        if kind == 0:
            (h,) = rmsnorm_fwd(xc, g_mix, [F32], f"norm_mix{i}")
            x_mid = pool_fwd(h, xc, pool_matrices(j), vec(whole["pool_b"], j), vec(whole["pool_scale"], j),
                             f"pool_fwd{i}", lead=0)
        elif kind == 1:
            (h,) = rmsnorm_fwd(xc, g_mix, [BF16], f"norm_mix{i}")
            s["qkv"] = mm_cols(h, gathered["sb_w_qkv", j], out_dtype=BF16, name=f"sb_qkv{i}", lead=0)
            s["o"], s["ltot"] = sb_attn_fwd(s["qkv"], vec(sb_q_gain, j), vec(sb_k_gain, j), f"sb_attn_fwd{i}")
            x_mid = mm_cols(s["o"], gathered["sb_w_o", j], out_dtype=F32, name=f"sb_out{i}", resid=xc, lead=0)
        else:
            (h,) = rmsnorm_fwd(xc, g_mix, [F32], f"norm_mix{i}")
            prm = tuple(w[n][j] for n in ("ssm_lam_re", "ssm_lam_im", "ssm_log_step", "ssm_b_re", "ssm_b_im",
                                          "ssm_c_re", "ssm_c_im"))
            prep, s["prep_vjp"] = jax.vjp(ssm_prepare, *prm)
            s["prep"] = tuple(a.astype(BF16) for a in prep[:4]) + tuple(prep[4:])
            s["y"], s["yg"] = ssm_core_fwd(h, *s["prep"], vec(whole["ssm_d"], j), f"ssm_fwd{i}")
            s["val"], s["gate"], x_mid = glu_fwd(s["yg"], gathered["ssm_w_glu", j], vec(whole["ssm_b_glu"], j), xc,
                                                 f"ssm_glu{i}", lead=0)
        s["x_mid"], s["h"] = x_mid, h
        (h2,) = rmsnorm_fwd(x_mid, vec(norm_ffn_g, i), [BF16], f"norm_ffn{i}")
        s["h2"] = h2
        s["up_val"], s["up_gate"], s["act"] = ffn_up_fused(h2, gathered["ffn_w_up", i], whole["ffn_conv_w"][i],
                                                           vec(ffn_conv_b, i), f"ffn_up{i}", lead=0)
        xc = mm_k([s["act"]], gathered["ffn_w_down", i], b_nt=False, name=f"ffn_down{i}", resid=x_mid, lead=0)
        saved.append(s)

    dx, dxb, loss_part = loss_head(xc, loss_target[0], "loss_head")
    loss = lax.psum(loss_part[0, 0], ("x", "y", "c"))

    small = {n: [None] * w[n].shape[0] for n in SMALL}
    big_g, updated = {}, {}

    def finish_reduction(pending, after):
        layer, keys, send_sem, recv_sem, halves, landed, _ = pending
        halves, landed = chip_wait(send_sem, recv_sem, halves, landed, after, f"grads_chip_wait{layer}")
        for (n, l), h, lb in zip(keys, halves, landed):
            g = reduce_share(h, lb, me1, f"grads_share_{n}{l}")
            updated[n] = adamw_layer(w[n], g, mom[n], var[n], l, f"adamw_{n}{l}", into=updated.get(n))

    pending, travelling = None, []
    for i in reversed(range(depth)):
        kind, j = i % N_MIXERS, i // N_MIXERS
        s = saved[i]
        g_ffn, g_mix = vec(norm_ffn_g, i), vec(norm_mix_g, i)
        cw, cb = whole["ffn_conv_w"][i], vec(ffn_conv_b, i)
        if pending is not None:
            cb = tie(cb, pending[-1])
        h, h2 = s["h"], s["h2"]
        dupv, dupg, dcwv, dcwg, dcbv, dcbg = ffn_bwd_fused(dxb, gathered["ffn_w_down", i], s["up_val"], s["up_gate"], cw, cb,
                                                           f"ffn_bwd{i}", lead=0)
        big_g["ffn_w_down", i] = [mm_rows(s["act"], dxb, out_dtype=BF16, name=f"ffn_dwdown{i}")]
        big_g["ffn_w_up", i] = [mm_cols(h2, dupv, a_contract=0, out_dtype=BF16, name=f"ffn_dwup_val{i}"),
                                mm_cols(h2, dupg, a_contract=0, out_dtype=BF16, name=f"ffn_dwup_gate{i}")]
        dh2 = mm_k([dupv, dupg], gathered["ffn_w_up", i], b_nt=True, name=f"ffn_dh{i}", lead=0)
        dx_mid, dxb_mid, small["norm_ffn_g"][i] = rmsnorm_bwd(s["x_mid"], g_ffn, dh2, dx, f"norm_ffn_bwd{i}")
        small["ffn_conv_w"][i] = jnp.concatenate([dcwv, dcwg], axis=1)[None]
        small["ffn_conv_b"][i] = jnp.concatenate([dcbv, dcbg], axis=1)

        if kind == 0:
            dh, dwp, small["pool_b"][j], small["pool_scale"][j] = pool_bwd(
                dx_mid, h, pool_matrices(j), vec(whole["pool_b"], j), vec(whole["pool_scale"], j), f"pool_bwd{i}", lead=0)
            big_g["pool_w", j] = [dwp.reshape(-1, dwp.shape[-1])]
        elif kind == 1:
            do = mm_cols(dxb_mid, gathered["sb_w_o", j], b_nt=True, out_dtype=BF16, name=f"sb_do{i}", lead=0)
            big_g["sb_w_o", j] = [mm_cols(s["o"], dxb_mid, a_contract=0, out_dtype=BF16, name=f"sb_dwo{i}")]
            dq, dk, dv, small["sb_q_gain"][j], small["sb_k_gain"][j] = sb_attn_bwd(
                s["qkv"], s["ltot"], do, vec(sb_q_gain, j), vec(sb_k_gain, j), f"sb_attn_bwd{i}")
            dqkv = jnp.concatenate([dq, dk, dv], axis=1)
            big_g["sb_w_qkv", j] = [mm_cols(h, dqkv, a_contract=0, out_dtype=BF16, name=f"sb_dwqkv{i}")]
            dh = mm_k([dqkv], gathered["sb_w_qkv", j], b_nt=True, name=f"sb_dh{i}", lead=0)
        else:
            dval, dgate, dbv, dbg = glu_bwd(dx_mid, s["val"], s["gate"], f"ssm_glu_bwd{i}")
            small["ssm_b_glu"][j] = jnp.concatenate([dbv, dbg], axis=1)
            big_g["ssm_w_glu", j] = [mm_cols(s["yg"], dval, a_contract=0, out_dtype=BF16, name=f"ssm_dwglu_val{i}"),
                                     mm_cols(s["yg"], dgate, a_contract=0, out_dtype=BF16, name=f"ssm_dwglu_gate{i}")]
            dyg = mm_k([dval, dgate], gathered["ssm_w_glu", j], b_nt=True, name=f"ssm_dyg{i}", lead=0)
            dh, small["ssm_d"][j], *dprep = ssm_core_bwd(h, s["y"], dyg, *s["prep"], vec(whole["ssm_d"], j), f"ssm_bwd{i}")
            dprm = s["prep_vjp"](tuple(dprep))
            for n, g in zip(("ssm_lam_re", "ssm_lam_im", "ssm_log_step", "ssm_b_re", "ssm_b_im", "ssm_c_re", "ssm_c_im"),
                            dprm):
                small[n][j] = g[None]
        dx, dxb, small["norm_mix_g"][i] = rmsnorm_bwd(s["x_in"], g_mix, dh, dx_mid, f"norm_mix_bwd{i}")

        keys, halves = layer_weights(i), []
        for n, l in keys:
            h = None
            for a, g in enumerate(big_g[n, l]):
                h = pair_reduce(g, BIG[n], *shard[n], f"grads_pair_{n}{l}_{a}", into=h,
                                first_slot=a * (N_CHIPS // len(big_g[n, l])))
            halves.append(h)
        pending = (i, keys) + tuple(chip_start(halves, f"grads_chip_start{i}"))
        travelling.append(pending)
        if len(travelling) > REDUCE_LAG:
            finish_reduction(travelling.pop(0), dx)

    small_full = [jnp.concatenate(small[n], axis=0) for n in SMALL]
    small_sum = dict(zip(SMALL, _unpack(small_allreduce(_pack(small_full), "reduce_vectors"), small_full)))
    grads = {}
    for n in SMALL:
        g = small_sum[n]
        if n in SMALL_SHARDED:
            ax = SMALL_SHARDED[n]
            g = lax.dynamic_slice_in_dim(g, me * w[n].shape[ax], w[n].shape[ax], ax)
        grads[n] = g
    delta, new_m, new_v = {}, {}, {}
    like = [w[n] for n in SMALL]
    packed = [_pack([src[n] for n in SMALL]).reshape(-1, LANES) for src in (w, grads, mom, var)]
    vector_updates = adamw(*packed, "adamw_vectors")
    for dst, out in zip((delta, new_m, new_v), vector_updates):
        dst.update(zip(SMALL, _unpack(out, like)))

    after = vector_updates[0]
    for pending in travelling:
        finish_reduction(pending, after)
        after = updated[pending[1][0][0]][0]
    for n in big:
        grads[n], delta[n], new_m[n], new_v[n] = updated[n]
    for group in (grads, delta, new_m, new_v):
        group["pool_w"] = group["pool_w"].reshape(pool_shape)

    return (loss, dx[None], *[grads[n] for n in WEIGHTS], *[delta[n] for n in WEIGHTS],
            *[new_m[n] for n in WEIGHTS], *[new_v[n] for n in WEIGHTS])
```
